```python
import math
import jax, jax.numpy as jnp
from jax import lax
import numpy as np

D_MODEL = 1024
BATCH = 2
SEQ = 16384
DEPTH = 2
DEC_BATCH = 32
DEC_SEQ = 32
PAST_LEN = 4096

CHUNK = 64
QBLK = 128

HA_HEADS = 4
HA_DK = 128
HA_DV = 128

HB_HEADS = 4
Q_LORA = 384
KV_LORA = 256
NOPE_DIM = 128
ROPE_DIM = 64
V_DIM = 128
ROPE_THETA = 10000.0
MLA_SCALE = (NOPE_DIM + ROPE_DIM) ** -0.5

D_MIX = HA_HEADS * HA_DV + HB_HEADS * V_DIM
SPLIT_Q = HA_HEADS * HA_DK
SPLIT_F = SPLIT_Q + HA_HEADS * HA_DK
SPLIT_I = SPLIT_F + HA_HEADS * HA_DV
SPLIT_G = SPLIT_I + HA_HEADS * HA_DV
SPLIT_CQ = SPLIT_G + Q_LORA
SPLIT_CKV = SPLIT_CQ + KV_LORA
D_IN = SPLIT_CKV + ROPE_DIM

N_EXPERTS = 16
N_GROUPS = 4
EXPERTS_PER_GROUP = N_EXPERTS // N_GROUPS
TOP_K = 2
D_EXPERT = 512

PLE_DIM = 256

ALPHA = (2 * DEPTH) ** 0.25
BETA = (8 * DEPTH) ** -0.25

NEG_INF = -1e30

kernel_name = 'hgrn2_mla_moe_stream_step'


def rms_norm(x, g, eps=1e-6):
    xf = x.astype(jnp.float32)
    y = xf * lax.rsqrt(jnp.mean(xf * xf, axis=-1, keepdims=True) + eps)
    return (y * g.astype(jnp.float32)).astype(x.dtype)


def layer_norm(x, g, b, eps=1e-5):
    xf = x.astype(jnp.float32)
    mu = jnp.mean(xf, axis=-1, keepdims=True)
    var = jnp.mean(jnp.square(xf - mu), axis=-1, keepdims=True)
    y = (xf - mu) * lax.rsqrt(var + eps) * g.astype(jnp.float32) + b.astype(jnp.float32)
    return y.astype(x.dtype)


def rope_angles(pos):
    inv = ROPE_THETA ** (-jnp.arange(0, ROPE_DIM, 2, dtype=jnp.float32) / ROPE_DIM)
    ang = pos.astype(jnp.float32)[:, None] * inv[None, :]
    return jnp.cos(ang), jnp.sin(ang)


def apply_rope(x, cos, sin):
    half = ROPE_DIM // 2
    x1 = x[..., :half].astype(jnp.float32)
    x2 = x[..., half:].astype(jnp.float32)
    out = jnp.concatenate([x1 * cos - x2 * sin, x2 * cos + x1 * sin], axis=-1)
    return out.astype(x.dtype)


def hgrn2_scan(q, k, logf, v, s0):
    B, T, H, _ = q.shape
    cb = min(CHUNK, T)
    nb = -(-T // cb)
    pad = nb * cb - T

    def blocks(a):
        a = jnp.pad(a.astype(jnp.float32), ((0, 0), (0, pad), (0, 0), (0, 0)))
        return a.reshape(B, nb, cb, H, a.shape[-1]).transpose(1, 0, 3, 2, 4)

    causal = jnp.tril(jnp.ones((cb, cb), dtype=bool))

    def step(S, blk):
        qc, kc, gc, vc = blk
        b = jnp.cumsum(gc, axis=2)
        diff = b[:, :, :, None, :] - b[:, :, None, :, :]
        decay = jnp.exp(jnp.where(causal[:, :, None], diff, -jnp.inf))
        attn = jnp.einsum('bhtk,bhtsk,bhsk->bhts', qc, decay, kc)
        o = (jnp.einsum('bhts,bhsv->bhtv', attn, vc)
             + jnp.einsum('bhtk,bhkv->bhtv', qc * jnp.exp(b), S))
        b_last = b[:, :, -1, :]
        S_new = (jnp.exp(b_last)[..., None] * S
                 + jnp.einsum('bhsk,bhsv->bhkv', kc * jnp.exp(b_last[:, :, None, :] - b), vc))
        return S_new, o

    S_fin, o = lax.scan(step, s0.astype(jnp.float32), (blocks(q), blocks(k), blocks(logf), blocks(v)))
    o = o.transpose(1, 0, 3, 2, 4).reshape(B, nb * cb, H, v.shape[-1])[:, :T]
    return o, S_fin


def mla_core(q_abs, q_pe, ckv, kpe, mask):
    s = (jnp.einsum('bqhc,bkc->bhqk', q_abs, ckv)
         + jnp.einsum('bqhr,bkr->bhqk', q_pe, kpe)).astype(jnp.float32) * MLA_SCALE
    if mask is not None:
        s = jnp.where(mask, s, NEG_INF)
    p = jax.nn.softmax(s, axis=-1).astype(ckv.dtype)
    return jnp.einsum('bhqk,bkc->bqhc', p, ckv)


def mla_prompt(q_abs, q_pe, ckv, kpe):
    B, T, H, C = q_abs.shape
    nq = T // QBLK
    kchunk = jnp.arange(T) // CHUNK

    def blk(args):
        qa, qp, bi = args
        qchunk = (bi * QBLK + jnp.arange(QBLK)) // CHUNK
        mask = kchunk[None, :] <= qchunk[:, None]
        return mla_core(qa, qp, ckv, kpe, mask)

    qa_b = q_abs.reshape(B, nq, QBLK, H, C).transpose(1, 0, 2, 3, 4)
    qp_b = q_pe.reshape(B, nq, QBLK, H, ROPE_DIM).transpose(1, 0, 2, 3, 4)
    out = lax.map(blk, (qa_b, qp_b, jnp.arange(nq)))
    return out.transpose(1, 0, 2, 3, 4).reshape(B, T, H, C)


def moe(x, w_router, router_bias, w_gate, w_up, w_down):
    N = x.shape[0]
    scores = jax.nn.sigmoid((x @ w_router).astype(jnp.float32))
    sel = (scores + router_bias.astype(jnp.float32)).reshape(N, N_GROUPS, EXPERTS_PER_GROUP)
    group_score = lax.top_k(sel, 2)[0].sum(-1)
    gidx = jnp.argmax(group_score, axis=-1)
    in_group = jnp.take_along_axis(sel, gidx[:, None, None], axis=1)[:, 0]
    _, loc = lax.top_k(in_group, TOP_K)
    eidx = gidx[:, None] * EXPERTS_PER_GROUP + loc
    w = jnp.take_along_axis(scores, eidx, axis=-1)
    w = w / jnp.sum(w, axis=-1, keepdims=True)
    gates = jnp.einsum('nk,nke->ne', w, jax.nn.one_hot(eidx, N_EXPERTS, dtype=jnp.float32)).astype(x.dtype)
    out = jnp.zeros_like(x)
    for e in range(N_EXPERTS):
        h = jax.nn.silu(x @ w_gate[e]) * (x @ w_up[e])
        out = out + gates[:, e:e + 1] * (h @ w_down[e])
    return out


def trunk_layer(x, p, pos, s0, past_ckv, past_kpe, lb,
                w_in, hgrn_norm_g, q_norm_g, w_qb, kv_norm_g, w_kvb, mla_norm_g, w_o,
                ln1_g, ln1_b, ln2_g, ln2_b, w_router, router_bias,
                w_gate, w_up, w_down, w_ple, w_ple_gate):
    B, T, D = x.shape
    h = x @ w_in
    q_a, f_a, i_a, g_a, cq, ckv_raw, kpe_raw = jnp.split(
        h, [SPLIT_Q, SPLIT_F, SPLIT_I, SPLIT_G, SPLIT_CQ, SPLIT_CKV], axis=-1)

    lbf = lb.astype(jnp.float32)
    fa = f_a.astype(jnp.float32)
    logf = jnp.logaddexp(jnp.log(lbf), jnp.log1p(-lbf) + jax.nn.log_sigmoid(fa))
    k_in = (1.0 - lbf) * jax.nn.sigmoid(-fa)
    hd = (B, T, HA_HEADS, HA_DK)
    o_a, S_new = hgrn2_scan(jax.nn.silu(q_a).reshape(hd), k_in.reshape(hd), logf.reshape(hd),
                            i_a.reshape(B, T, HA_HEADS, HA_DV), s0)
    o_a = rms_norm(o_a.astype(x.dtype), hgrn_norm_g) * jax.nn.silu(g_a.reshape(B, T, HA_HEADS, HA_DV))
    o_a = o_a.reshape(B, T, HA_HEADS * HA_DV)

    cos, sin = rope_angles(pos)
    q = (rms_norm(cq, q_norm_g) @ w_qb).reshape(B, T, HB_HEADS, NOPE_DIM + ROPE_DIM)
    q_nope = q[..., :NOPE_DIM]
    q_pe = apply_rope(q[..., NOPE_DIM:], cos[None, :, None, :], sin[None, :, None, :])
    ckv = rms_norm(ckv_raw, kv_norm_g)
    kpe = apply_rope(kpe_raw, cos[None], sin[None])
    w_kvb_h = w_kvb.reshape(KV_LORA, HB_HEADS, NOPE_DIM + V_DIM)
    w_uk = w_kvb_h[..., :NOPE_DIM]
    w_uv = w_kvb_h[..., NOPE_DIM:]
    q_abs = jnp.einsum('bthd,chd->bthc', q_nope, w_uk)
    if past_ckv is None:
        o_lat = mla_prompt(q_abs, q_pe, ckv, kpe)
    else:
        o_lat = mla_core(q_abs, q_pe, jnp.concatenate([past_ckv, ckv], axis=1),
                         jnp.concatenate([past_kpe, kpe], axis=1), None)
    o_b = jnp.einsum('bthc,chv->bthv', o_lat, w_uv).reshape(B, T, HB_HEADS * V_DIM)
    o_b = rms_norm(o_b, mla_norm_g)

    mix = jnp.concatenate([o_a, o_b], axis=-1) @ w_o
    x = layer_norm(ALPHA * x + mix, ln1_g, ln1_b)
    ffn = moe(x.reshape(B * T, D), w_router, router_bias, w_gate, w_up, w_down).reshape(B, T, D)
    x = layer_norm(ALPHA * x + ffn, ln2_g, ln2_b)

    x = x + jax.nn.sigmoid(x @ w_ple_gate) * (p @ w_ple)
    return x, S_new.astype(x.dtype), ckv, kpe


def setup_inputs(seed: int = 0) -> dict:
    key = jax.random.key(seed)
    ks = jax.random.split(key, 27)

    def nrm(k, shape, scale=1.0):
        return jax.random.normal(k, shape, jnp.float32) * scale

    def gain(k, shape):
        return 1.0 + nrm(k, shape, 0.01)

    return dict(
        x_prompt=nrm(ks[0], (BATCH, SEQ, D_MODEL)),
        x_sample=nrm(ks[1], (DEC_BATCH, DEC_SEQ, D_MODEL)),
        p_prompt=nrm(ks[2], (DEPTH, BATCH, SEQ, PLE_DIM)),
        p_sample=nrm(ks[3], (DEPTH, DEC_BATCH, DEC_SEQ, PLE_DIM)),
        state_hgrn=nrm(ks[4], (DEPTH, DEC_BATCH, HA_HEADS, HA_DK, HA_DV), 0.5),
        cache_ckv=nrm(ks[5], (DEPTH, DEC_BATCH, PAST_LEN, KV_LORA)),
        cache_kpe=nrm(ks[6], (DEPTH, DEC_BATCH, PAST_LEN, ROPE_DIM)),
        w_in=nrm(ks[7], (DEPTH, D_MODEL, D_IN), D_MODEL ** -0.5),
        lb_logits=nrm(ks[8], (DEPTH, HA_HEADS * HA_DK), 0.5),
        hgrn_norm_g=gain(ks[9], (DEPTH, HA_DV)),
        q_norm_g=gain(ks[10], (DEPTH, Q_LORA)),
        w_qb=nrm(ks[11], (DEPTH, Q_LORA, HB_HEADS * (NOPE_DIM + ROPE_DIM)), Q_LORA ** -0.5),
        kv_norm_g=gain(ks[12], (DEPTH, KV_LORA)),
        w_kvb=nrm(ks[13], (DEPTH, KV_LORA, HB_HEADS * (NOPE_DIM + V_DIM)), KV_LORA ** -0.5),
        mla_norm_g=gain(ks[14], (DEPTH, HB_HEADS * V_DIM)),
        w_o=nrm(ks[15], (DEPTH, D_MIX, D_MODEL), BETA * D_MIX ** -0.5),
        ln1_g=gain(ks[16], (DEPTH, D_MODEL)),
        ln1_b=nrm(ks[17], (DEPTH, D_MODEL), 0.01),
        ln2_g=gain(ks[18], (DEPTH, D_MODEL)),
        ln2_b=nrm(ks[19], (DEPTH, D_MODEL), 0.01),
        w_router=nrm(ks[20], (D_MODEL, N_EXPERTS), D_MODEL ** -0.5),
        router_bias=nrm(ks[21], (N_EXPERTS,), 0.01),
        w_gate=nrm(ks[22], (DEPTH, N_EXPERTS, D_MODEL, D_EXPERT), D_MODEL ** -0.5),
        w_up=nrm(ks[23], (DEPTH, N_EXPERTS, D_MODEL, D_EXPERT), D_MODEL ** -0.5),
        w_down=nrm(ks[24], (DEPTH, N_EXPERTS, D_EXPERT, D_MODEL), BETA * D_EXPERT ** -0.5),
        w_ple=nrm(ks[25], (DEPTH, PLE_DIM, D_MODEL), PLE_DIM ** -0.5),
        w_ple_gate=nrm(ks[26], (DEPTH, D_MODEL, D_MODEL), D_MODEL ** -0.5),
    )


def reference(x_prompt, x_sample, p_prompt, p_sample, state_hgrn, cache_ckv, cache_kpe,
              w_in, lb_logits, hgrn_norm_g, q_norm_g, w_qb, kv_norm_g, w_kvb, mla_norm_g, w_o,
              ln1_g, ln1_b, ln2_g, ln2_b, w_router, router_bias, w_gate, w_up, w_down,
              w_ple, w_ple_gate):
    sm = jax.nn.softmax(lb_logits.astype(jnp.float32), axis=0)
    lb_all = jnp.maximum(jnp.cumsum(sm, axis=0) - sm[0:1], 0.0)

    t_p = x_prompt.shape[1]
    t_s = x_sample.shape[1]
    past = cache_ckv.shape[2]
    pos_p = jnp.arange(t_p)
    pos_s = past + jnp.arange(t_s)
    s0_p = jnp.zeros((x_prompt.shape[0], HA_HEADS, HA_DK, HA_DV), jnp.float32)

    yp, ys = x_prompt, x_sample
    st_p, ck_p, kp_p, st_s, ck_s, kp_s = [], [], [], [], [], []
    for l in range(DEPTH):
        wl = (w_in[l], hgrn_norm_g[l], q_norm_g[l], w_qb[l], kv_norm_g[l], w_kvb[l], mla_norm_g[l], w_o[l],
              ln1_g[l], ln1_b[l], ln2_g[l], ln2_b[l], w_router, router_bias,
              w_gate[l], w_up[l], w_down[l], w_ple[l], w_ple_gate[l])
        yp, sp, cp, kp = trunk_layer(yp, p_prompt[l], pos_p, s0_p, None, None, lb_all[l], *wl)
        ys, ss, cs, kss = trunk_layer(ys, p_sample[l], pos_s, state_hgrn[l], cache_ckv[l], cache_kpe[l],
                                      lb_all[l], *wl)
        st_p.append(sp); ck_p.append(cp); kp_p.append(kp)
        st_s.append(ss); ck_s.append(cs); kp_s.append(kss)

    state_hgrn_prompt = jnp.stack(st_p)
    ckv_prompt = jnp.stack(ck_p)
    kpe_prompt = jnp.stack(kp_p)
    state_hgrn_sample = jnp.stack(st_s)
    ckv_sample = jnp.stack(ck_s)
    kpe_sample = jnp.stack(kp_s)
    return (yp, ys, state_hgrn_prompt, ckv_prompt, kpe_prompt, state_hgrn_sample, ckv_sample, kpe_sample)
```

```python
import functools

import jax
import jax.numpy as jnp
from jax import lax
from jax.experimental import pallas as pl
from jax.experimental.pallas import tpu as pltpu

F32 = jnp.float32
BF16 = jnp.bfloat16

D_MODEL = 1024
HA_HEADS = 4
HA_DK = 128
HA_DV = 128
HB_HEADS = 4
Q_LORA = 384
KV_LORA = 256
NOPE_DIM = 128
ROPE_DIM = 64
V_DIM = 128
ROPE_THETA = 10000.0
MLA_SCALE = (NOPE_DIM + ROPE_DIM) ** -0.5
CHUNK = 64
N_EXPERTS = 16
N_GROUPS = 4
EXPERTS_PER_GROUP = N_EXPERTS // N_GROUPS
D_EXPERT = 512
PLE_DIM = 256
NEG_INF = -1e30

HA_W = HA_HEADS * HA_DK
COL_CQ = 4 * HA_W
COL_CKV = COL_CQ + Q_LORA
COL_KPE = COL_CKV + KV_LORA
D_IN_AUG = COL_KPE + 2 * ROPE_DIM
Q_HEAD_AUG = NOPE_DIM + 2 * ROPE_DIM

LANES = 128
SUB_BLOCK = 16
VMEM_LIMIT = 56 * 1024 * 1024


def _cparams(sem, vmem=VMEM_LIMIT):
    return pltpu.CompilerParams(dimension_semantics=sem, vmem_limit_bytes=vmem)


def _dot(a, b):
    return jnp.dot(a, b, preferred_element_type=F32)


def _dot_nt(a, b):
    return lax.dot_general(a, b, (((1,), (1,)), ((), ())), preferred_element_type=F32)


def _rms(x, g, eps=1e-6):
    return x * lax.rsqrt(jnp.mean(x * x, axis=-1, keepdims=True) + eps) * g


def _layer_norm(x, g, b, eps=1e-5):
    mu = jnp.mean(x, axis=-1, keepdims=True)
    xc = x - mu
    var = jnp.mean(xc * xc, axis=-1, keepdims=True)
    return xc * lax.rsqrt(var + eps) * g + b


def _silu(x):
    return x * jax.nn.sigmoid(x)


def _full(shape):
    nd = len(shape)
    return pl.BlockSpec(shape, lambda *_: (0,) * nd)


def _inproj_kernel(x_ref, cs_ref, win_ref, lb_ref, qg_ref, wqb_ref, kvg_ref, wuk_ref,
                   qs_ref, kin_ref, logf_ref, v_ref, gs_ref, qabs_ref, qpe_ref,
                   ckv_ref, kpe_ref, *rest, tq, cb, prompt):
    tm = x_ref.shape[0]
    x = x_ref[...].astype(BF16)

    def sect(lo, hi):
        return _dot(x, win_ref[:, lo:hi])

    def put_heads(ref, val):
        for h in range(HA_HEADS):
            ref[h] = val[:, h * HA_DK:(h + 1) * HA_DK]

    put_heads(qs_ref, _silu(sect(0, HA_W)))
    fa = sect(HA_W, 2 * HA_W)
    log_lb = lb_ref[0:1, :]
    log1m_lb = lb_ref[1:2, :]
    one_m_lb = lb_ref[2:3, :]
    log_sig = jnp.minimum(fa, 0.0) - jnp.log1p(jnp.exp(-jnp.abs(fa)))
    c = log1m_lb + log_sig
    hi_ = jnp.maximum(log_lb, c)
    lo_ = jnp.minimum(log_lb, c)
    put_heads(logf_ref, hi_ + jnp.log1p(jnp.exp(lo_ - hi_)))
    put_heads(kin_ref, one_m_lb * jax.nn.sigmoid(-fa))
    v = sect(2 * HA_W, 3 * HA_W)
    put_heads(v_ref, v)
    gs_ref[...] = _silu(sect(3 * HA_W, 4 * HA_W))

    cs = cs_ref[...]

    def rope(t):
        prod = t * cs
        return prod + pltpu.roll(prod, ROPE_DIM, 1)

    ckv = _rms(sect(COL_CKV, COL_KPE), kvg_ref[...])
    ckv_ref[...] = ckv
    kpe2 = rope(sect(COL_KPE, D_IN_AUG))
    kpe_ref[...] = kpe2[:, :ROPE_DIM]
    if prompt:
        vt_ref, ckvb_ref, ckvt_ref, kpet_ref = rest
        for h in range(HA_HEADS):
            for ci in range(tm // cb):
                vt_ref[0, h, ci] = v[ci * cb:(ci + 1) * cb, h * HA_DV:(h + 1) * HA_DV].T.astype(BF16)
        ckvb_ref[...] = ckv.astype(BF16)
        ckvt_ref[0] = ckv.T.astype(BF16)
        kpet_ref[0] = kpe2.T.astype(BF16)

    cqn = _rms(sect(COL_CQ, COL_CKV), qg_ref[...]).astype(BF16)
    lane = lax.broadcasted_iota(jnp.int32, (tm, LANES), 1)
    for h in range(HB_HEADS):
        qh = _dot(cqn, wqb_ref[:, h * Q_HEAD_AUG:(h + 1) * Q_HEAD_AUG])
        qabs = (_dot(qh[:, :NOPE_DIM].astype(BF16), wuk_ref[h]) * MLA_SCALE).astype(BF16)
        qpe = rope(qh[:, NOPE_DIM:]) * MLA_SCALE
        qpe = jnp.where(lane < ROPE_DIM, qpe, 0.0).astype(BF16)
        for s in range(tm // tq):
            qabs_ref[s, h] = qabs[s * tq:(s + 1) * tq]
            qpe_ref[s, h] = qpe[s * tq:(s + 1) * tq]


def _inproj(x, cs, w_in, lbp, qg, wqb, kvg, wuk, *, tm, tq, cb, prompt):
    n = x.shape[0]
    nt = n // tm
    row = lambda w: pl.BlockSpec((tm, w), lambda i: (i, 0))
    heads = pl.BlockSpec((HA_HEADS, tm, HA_DK), lambda i: (0, i, 0))
    qblk = lambda w: pl.BlockSpec((tm // tq, HB_HEADS, tq, w), lambda i: (i, 0, 0, 0))
    head_major = jax.ShapeDtypeStruct((HA_HEADS, n, HA_DK), F32)
    out_shape = [
        head_major,
        head_major,
        head_major,
        head_major,
        jax.ShapeDtypeStruct((n, HA_W), F32),
        jax.ShapeDtypeStruct((n // tq, HB_HEADS, tq, KV_LORA), BF16),
        jax.ShapeDtypeStruct((n // tq, HB_HEADS, tq, LANES), BF16),
        jax.ShapeDtypeStruct((n, KV_LORA), F32),
        jax.ShapeDtypeStruct((n, ROPE_DIM), F32),
    ]
    out_specs = [heads, heads, heads, heads, row(HA_W), qblk(KV_LORA), qblk(LANES),
                 row(KV_LORA), row(ROPE_DIM)]
    if prompt:
        out_shape += [jax.ShapeDtypeStruct((nt, HA_HEADS, tm // cb, HA_DV, cb), BF16),
                      jax.ShapeDtypeStruct((n, KV_LORA), BF16),
                      jax.ShapeDtypeStruct((nt, KV_LORA, tm), BF16),
                      jax.ShapeDtypeStruct((nt, LANES, tm), BF16)]
        out_specs += [pl.BlockSpec((1, HA_HEADS, tm // cb, HA_DV, cb), lambda i: (i, 0, 0, 0, 0)),
                      row(KV_LORA),
                      pl.BlockSpec((1, KV_LORA, tm), lambda i: (i, 0, 0)),
                      pl.BlockSpec((1, LANES, tm), lambda i: (i, 0, 0))]
    return pl.pallas_call(
        functools.partial(_inproj_kernel, tq=tq, cb=cb, prompt=prompt),
        grid=(nt,),
        in_specs=[row(D_MODEL), row(LANES), _full(w_in.shape), _full(lbp.shape), _full(qg.shape),
                  _full(wqb.shape), _full(kvg.shape), _full(wuk.shape)],
        out_specs=out_specs,
        out_shape=out_shape,
        compiler_params=_cparams(("parallel",)),
        name="inproj",
    )(x, cs, w_in, lbp, qg, wqb, kvg, wuk)


def _hgrn_kernel(q_ref, k_ref, g_ref, v_ref, vt_ref, s0_ref, o_ref, sfin_ref, st_scr, *, cb):
    _, bb, tb, _ = q_ref.shape
    nsub = cb // SUB_BLOCK
    ti = pl.program_id(1)

    @pl.when(ti == 0)
    def _():
        for b in range(bb):
            for h in range(HA_HEADS):
                st_scr[b, h] = s0_ref[b, h].T

    r_i = lax.broadcasted_iota(jnp.int32, (cb, cb), 0)
    c_i = lax.broadcasted_iota(jnp.int32, (cb, cb), 1)
    tril = (r_i >= c_i).astype(F32)
    row16 = lax.broadcasted_iota(jnp.int32, (SUB_BLOCK, 1), 0)

    def chunk_head(idx, carry):
        ci = idx // (bb * HA_HEADS)
        rem = idx % (bb * HA_HEADS)
        b = rem // HA_HEADS
        h = rem % HA_HEADS
        r0 = pl.multiple_of(ci * cb, cb)
        rows = pl.ds(r0, cb)
        q = q_ref[h, b, rows, :]
        k = k_ref[h, b, rows, :]
        g = g_ref[h, b, rows, :]
        v = v_ref[h, b, rows, :]
        vt = vt_ref[b, h, ci]
        st = st_scr[b, h]

        bcum = jnp.dot(tril, g, precision=lax.Precision.HIGHEST, preferred_element_type=F32)
        b_last = bcum[cb - 1:cb, :]

        o = _dot_nt((q * jnp.exp(bcum)).astype(BF16), st.astype(BF16))
        o_parts = [o[j * SUB_BLOCK:(j + 1) * SUB_BLOCK] for j in range(nsub)]

        for j in range(nsub - 1):
            lo, hi = j * SUB_BLOCK, (j + 1) * SUB_BLOCK
            bnd = bcum[hi - 1:hi, :]
            kd = (k[lo:hi] * jnp.exp(bnd - bcum[lo:hi])).astype(BF16)
            qe = (q[hi:] * jnp.exp(bcum[hi:] - bnd)).astype(BF16)
            a = _dot_nt(qe, kd).astype(BF16)
            upd = _dot(a, v[lo:hi].astype(BF16))
            for jj in range(j + 1, nsub):
                o_parts[jj] = o_parts[jj] + upd[(jj - j - 1) * SUB_BLOCK:(jj - j) * SUB_BLOCK]

        for j in range(nsub):
            lo, hi = j * SUB_BLOCK, (j + 1) * SUB_BLOCK
            qj, kj, bj, vj = q[lo:hi], k[lo:hi], bcum[lo:hi], v[lo:hi]
            acc = o_parts[j]
            for s in range(SUB_BLOCK):
                arg = jnp.where(row16 >= s, bj - bj[s:s + 1, :], NEG_INF)
                w = qj * jnp.exp(arg) * kj[s:s + 1, :]
                acc = acc + jnp.sum(w, axis=-1, keepdims=True) * vj[s:s + 1, :]
            o_ref[h, b, pl.ds(pl.multiple_of(r0 + lo, SUB_BLOCK), SUB_BLOCK), :] = acc

        kd_all = (k * jnp.exp(b_last - bcum)).astype(BF16)
        st_scr[b, h] = st * jnp.exp(b_last) + _dot(vt, kd_all)
        return carry

    lax.fori_loop(0, (tb // cb) * bb * HA_HEADS, chunk_head, 0)

    @pl.when(ti == pl.num_programs(1) - 1)
    def _():
        for b in range(bb):
            for h in range(HA_HEADS):
                sfin_ref[b, h] = st_scr[b, h].T


def _hgrn(qs, kin, logf, v, vt, s0, *, bb, tb, cb):
    _, bsz, t, _ = qs.shape
    blk = pl.BlockSpec((HA_HEADS, bb, tb, HA_DK), lambda i, j: (0, i, j, 0))
    st_blk = pl.BlockSpec((bb, HA_HEADS, HA_DK, HA_DV), lambda i, j: (i, 0, 0, 0))
    vt_blk = pl.BlockSpec((bb, None, HA_HEADS, tb // cb, HA_DV, cb),
                          lambda i, j: (i, j, 0, 0, 0, 0))
    return pl.pallas_call(
        functools.partial(_hgrn_kernel, cb=cb),
        grid=(bsz // bb, t // tb),
        in_specs=[blk, blk, blk, blk, vt_blk, st_blk],
        out_specs=[blk, st_blk],
        out_shape=[jax.ShapeDtypeStruct(qs.shape, F32),
                   jax.ShapeDtypeStruct(s0.shape, F32)],
        scratch_shapes=[pltpu.VMEM((bb, HA_HEADS, HA_DV, HA_DK), F32)],
        compiler_params=_cparams(("parallel", "arbitrary")),
        name="hgrn",
    )(qs, kin, logf, v, vt, s0)


def _attn_prompt_kernel(qa_ref, qp_ref, kt_ref, pt_ref, kv_ref, o_ref, m_scr, l_scr, acc_scr,
                        *, tq, tk):
    rows = HB_HEADS * tq
    qi = pl.program_id(1)
    qa = qa_ref[0].reshape(rows, KV_LORA)
    qp = qp_ref[0].reshape(rows, LANES)
    m_scr[...] = jnp.full(m_scr.shape, NEG_INF, F32)
    l_scr[...] = jnp.zeros(l_scr.shape, F32)
    acc_scr[...] = jnp.zeros(acc_scr.shape, F32)

    def step(j, masked):
        s = _dot(qa, kt_ref[j]) + _dot(qp, pt_ref[j])
        if masked:
            shift = CHUNK.bit_length() - 1
            qpos = qi * tq + (lax.broadcasted_iota(jnp.int32, (rows, 1), 0) & (tq - 1))
            kpos = j * tk + lax.broadcasted_iota(jnp.int32, (1, tk), 1)
            s = jnp.where((kpos >> shift) <= (qpos >> shift), s, NEG_INF)
        m_prev = m_scr[...]
        m_new = jnp.maximum(m_prev, jnp.max(s, axis=-1, keepdims=True))
        alpha = jnp.exp(m_prev - m_new)
        p = jnp.exp(s - m_new)
        l_scr[...] = alpha * l_scr[...] + jnp.sum(p, axis=-1, keepdims=True)
        kv = kv_ref[pl.ds(pl.multiple_of(j * tk, tk), tk), :]
        acc_scr[...] = alpha * acc_scr[...] + _dot(p.astype(BF16), kv)
        m_scr[...] = m_new

    jd = (qi * tq) // tk

    def body(j, carry):
        step(j, False)
        return carry

    lax.fori_loop(0, jd, body, 0)
    step(jd, True)
    o = acc_scr[...] / l_scr[...]
    o_ref[0] = o.astype(BF16).reshape(HB_HEADS, tq, KV_LORA)


def _attn_prompt(qabs, qpe, ckvt, kpet, ckvb, *, bsz, t, tq, tk):
    nq = t // tq
    nk = t // tk
    rows = HB_HEADS * tq
    single = pl.Buffered(1)
    return pl.pallas_call(
        functools.partial(_attn_prompt_kernel, tq=tq, tk=tk),
        grid=(bsz, nq),
        in_specs=[
            pl.BlockSpec((1, HB_HEADS, tq, KV_LORA), lambda b, i: (b * nq + i, 0, 0, 0)),
            pl.BlockSpec((1, HB_HEADS, tq, LANES), lambda b, i: (b * nq + i, 0, 0, 0)),
            pl.BlockSpec((nk, KV_LORA, tk), lambda b, i: (b, 0, 0), pipeline_mode=single),
            pl.BlockSpec((nk, LANES, tk), lambda b, i: (b, 0, 0), pipeline_mode=single),
            pl.BlockSpec((t, KV_LORA), lambda b, i: (b, 0), pipeline_mode=single),
        ],
        out_specs=pl.BlockSpec((1, HB_HEADS, tq, KV_LORA), lambda b, i: (b * nq + i, 0, 0, 0)),
        out_shape=jax.ShapeDtypeStruct(qabs.shape, BF16),
        scratch_shapes=[pltpu.VMEM((rows, 1), F32), pltpu.VMEM((rows, 1), F32),
                        pltpu.VMEM((rows, KV_LORA), F32)],
        compiler_params=_cparams(("parallel", "arbitrary")),
        name="attn_prompt",
    )(qabs, qpe, ckvt, kpet, ckvb)


def _attn_sample_kernel(qa_ref, qp_ref, pckv_ref, pkpe_ref, nckv_ref, nkpe_ref, o_ref):
    _, _, ts, _ = qa_ref.shape
    rows = HB_HEADS * ts
    qa = qa_ref[0].reshape(rows, KV_LORA)
    qp = qp_ref[0].reshape(rows, LANES)[:, :ROPE_DIM]
    pckv = pckv_ref[0].astype(BF16)
    nckv = nckv_ref[...].astype(BF16)
    s_past = _dot_nt(qa, pckv) + _dot_nt(qp, pkpe_ref[0].astype(BF16))
    s_new = _dot_nt(qa, nckv) + _dot_nt(qp, nkpe_ref[...].astype(BF16))
    m = jnp.maximum(jnp.max(s_past, axis=-1, keepdims=True), jnp.max(s_new, axis=-1, keepdims=True))
    p_past = jnp.exp(s_past - m)
    p_new = jnp.exp(s_new - m)
    l = jnp.sum(p_past, axis=-1, keepdims=True) + jnp.sum(p_new, axis=-1, keepdims=True)
    o = (_dot(p_past.astype(BF16), pckv) + _dot(p_new.astype(BF16), nckv)) / l
    o_ref[0] = o.astype(BF16).reshape(HB_HEADS, ts, KV_LORA)


def _attn_sample(qabs, qpe, past_ckv, past_kpe, ckv, kpe, *, ts):
    bsz, past, _ = past_ckv.shape
    return pl.pallas_call(
        _attn_sample_kernel,
        grid=(bsz,),
        in_specs=[
            pl.BlockSpec((1, HB_HEADS, ts, KV_LORA), lambda b: (b, 0, 0, 0)),
            pl.BlockSpec((1, HB_HEADS, ts, LANES), lambda b: (b, 0, 0, 0)),
            pl.BlockSpec((1, past, KV_LORA), lambda b: (b, 0, 0)),
            pl.BlockSpec((1, past, ROPE_DIM), lambda b: (b, 0, 0)),
            pl.BlockSpec((ts, KV_LORA), lambda b: (b, 0)),
            pl.BlockSpec((ts, ROPE_DIM), lambda b: (b, 0)),
        ],
        out_specs=pl.BlockSpec((1, HB_HEADS, ts, KV_LORA), lambda b: (b, 0, 0, 0)),
        out_shape=jax.ShapeDtypeStruct(qabs.shape, BF16),
        compiler_params=_cparams(("parallel",)),
        name="attn_sample",
    )(qabs, qpe, past_ckv, past_kpe, ckv, kpe)


def _gates(logits, bias):
    tm = logits.shape[0]
    lane = lax.broadcasted_iota(jnp.int32, (tm, LANES), 1)
    pos = lane % EXPERTS_PER_GROUP
    valid = lane < N_EXPERTS
    scores = jax.nn.sigmoid(logits)
    sel = jnp.where(valid, scores + bias, -jnp.inf)

    others = []
    for r in range(1, EXPERTS_PER_GROUP):
        others.append(jnp.where(pos >= r, pltpu.roll(sel, r, 1),
                                pltpu.roll(sel, LANES - (EXPERTS_PER_GROUP - r), 1)))
    a, b, c, d = sel, others[0], others[1], others[2]
    hi1, lo1 = jnp.maximum(a, b), jnp.minimum(a, b)
    hi2, lo2 = jnp.maximum(c, d), jnp.minimum(c, d)
    gscore = jnp.maximum(hi1, hi2) + jnp.maximum(jnp.minimum(hi1, hi2), jnp.maximum(lo1, lo2))
    gmax = jnp.max(gscore, axis=-1, keepdims=True)
    group = (lane // EXPERTS_PER_GROUP).astype(F32)
    gidx = jnp.min(jnp.where(gscore == gmax, group, float(LANES)), axis=-1, keepdims=True)

    rank = jnp.zeros((tm, LANES), jnp.int32)
    for r, o in enumerate(others, start=1):
        ahead = (o > sel) | ((o == sel) & (pos >= r))
        rank = rank + ahead.astype(jnp.int32)
    chosen = (group == gidx) & (rank < 2) & valid
    w = jnp.where(chosen, scores, 0.0)
    return w / jnp.sum(w, axis=-1, keepdims=True)


def _merge_kernel(x_ref, o_ref, gs_ref, olat_ref, wuv_ref, wo_ref, hg_ref, mg_ref,
                  g1_ref, b1_ref, wr_ref, rb_ref, x1_ref, x1b_ref, gate_ref, *, alpha):
    nsb, _, tq, _ = olat_ref.shape
    gs = gs_ref[...]
    hg = hg_ref[...]
    oa = jnp.concatenate(
        [_rms(o_ref[h], hg) * gs[:, h * HA_DV:(h + 1) * HA_DV] for h in range(HA_HEADS)],
        axis=-1)
    ob = []
    for h in range(HB_HEADS):
        lat = jnp.concatenate([olat_ref[s, h] for s in range(nsb)], axis=0)
        ob.append(_dot(lat, wuv_ref[h]))
    ob = _rms(jnp.concatenate(ob, axis=-1), mg_ref[...])
    d_a = HA_HEADS * HA_DV
    mix = _dot(oa.astype(BF16), wo_ref[:d_a, :]) + _dot(ob.astype(BF16), wo_ref[d_a:, :])
    x1 = _layer_norm(alpha * x_ref[...] + mix, g1_ref[...], b1_ref[...])
    x1_ref[...] = x1
    x1b = x1.astype(BF16)
    x1b_ref[...] = x1b
    gate_ref[...] = _gates(_dot(x1b, wr_ref[...]), rb_ref[...])


def _merge(x, o_raw, gs, olat, wuv, wo, hg, mg, g1, b1, wr, rb, *, tm, tq, alpha):
    n = x.shape[0]
    row = lambda w: pl.BlockSpec((tm, w), lambda i: (i, 0))
    return pl.pallas_call(
        functools.partial(_merge_kernel, alpha=alpha),
        grid=(n // tm,),
        in_specs=[row(D_MODEL), pl.BlockSpec((HA_HEADS, tm, HA_DV), lambda i: (0, i, 0)), row(HA_W),
                  pl.BlockSpec((tm // tq, HB_HEADS, tq, KV_LORA), lambda i: (i, 0, 0, 0)),
                  _full(wuv.shape), _full(wo.shape), _full(hg.shape), _full(mg.shape),
                  _full(g1.shape), _full(b1.shape), _full(wr.shape), _full(rb.shape)],
        out_specs=[row(D_MODEL), row(D_MODEL), row(LANES)],
        out_shape=[jax.ShapeDtypeStruct((n, D_MODEL), F32),
                   jax.ShapeDtypeStruct((n, D_MODEL), BF16),
                   jax.ShapeDtypeStruct((n, LANES), F32)],
        compiler_params=_cparams(("parallel",)),
        name="merge",
    )(x, o_raw, gs, olat, wuv, wo, hg, mg, g1, b1, wr, rb)


def _moe_kernel(x1b_ref, x1_ref, gate_ref, p_ref, wg_ref, wu_ref, wd_ref, g2_ref, b2_ref,
                wpg_ref, wp_ref, y_ref, acc_scr, *, alpha):
    e = pl.program_id(1)
    tm = x1b_ref.shape[0]

    @pl.when(e == 0)
    def _():
        acc_scr[...] = jnp.zeros(acc_scr.shape, F32)

    xb = x1b_ref[...]
    hmid = _silu(_dot(xb, wg_ref[0])) * _dot(xb, wu_ref[0])
    lane = lax.broadcasted_iota(jnp.int32, (tm, LANES), 1)
    gcol = jnp.sum(jnp.where(lane == e, gate_ref[...], 0.0), axis=-1, keepdims=True)
    acc_scr[...] += gcol * _dot(hmid.astype(BF16), wd_ref[0])

    @pl.when(e == pl.num_programs(1) - 1)
    def _():
        x2 = _layer_norm(alpha * x1_ref[...] + acc_scr[...], g2_ref[...], b2_ref[...])
        gate = jax.nn.sigmoid(_dot(x2.astype(BF16), wpg_ref[...]))
        y_ref[...] = x2 + gate * _dot(p_ref[...].astype(BF16), wp_ref[...])


def _moe(x1b, x1, gates, p, wg, wu, wd, g2, b2, wpg, wp, *, tm, alpha):
    n = x1.shape[0]
    row = lambda w: pl.BlockSpec((tm, w), lambda i, e: (i, 0))
    return pl.pallas_call(
        functools.partial(_moe_kernel, alpha=alpha),
        grid=(n // tm, N_EXPERTS),
        in_specs=[row(D_MODEL), row(D_MODEL), row(LANES), row(PLE_DIM),
                  pl.BlockSpec((1, D_MODEL, D_EXPERT), lambda i, e: (e, 0, 0)),
                  pl.BlockSpec((1, D_MODEL, D_EXPERT), lambda i, e: (e, 0, 0)),
                  pl.BlockSpec((1, D_EXPERT, D_MODEL), lambda i, e: (e, 0, 0)),
                  pl.BlockSpec(g2.shape, lambda i, e: (0, 0)),
                  pl.BlockSpec(b2.shape, lambda i, e: (0, 0)),
                  pl.BlockSpec(wpg.shape, lambda i, e: (0, 0)),
                  pl.BlockSpec(wp.shape, lambda i, e: (0, 0))],
        out_specs=row(D_MODEL),
        out_shape=jax.ShapeDtypeStruct((n, D_MODEL), F32),
        scratch_shapes=[pltpu.VMEM((tm, D_MODEL), F32)],
        compiler_params=_cparams(("parallel", "arbitrary")),
        name="moe",
    )(x1b, x1, gates, p, wg, wu, wd, g2, b2, wpg, wp)


def _rot_cols(w):
    half = ROPE_DIM // 2
    return jnp.concatenate([-w[..., half:], w[..., :half]], axis=-1)


def _prep_layer(l, w_in, w_qb, w_kvb, w_o, w_gate, w_up, w_down, w_ple, w_ple_gate):
    win = w_in[l]
    win_aug = jnp.concatenate([win, _rot_cols(win[:, COL_KPE:])], axis=-1).astype(BF16)
    wqb = w_qb[l].reshape(Q_LORA, HB_HEADS, NOPE_DIM + ROPE_DIM)
    wqb_aug = jnp.concatenate([wqb, _rot_cols(wqb[..., NOPE_DIM:])], axis=-1)
    wqb_aug = wqb_aug.reshape(Q_LORA, HB_HEADS * Q_HEAD_AUG).astype(BF16)
    wkvb = w_kvb[l].reshape(KV_LORA, HB_HEADS, NOPE_DIM + V_DIM)
    wuk_t = jnp.transpose(wkvb[..., :NOPE_DIM], (1, 2, 0)).astype(BF16)
    wuv = jnp.transpose(wkvb[..., NOPE_DIM:], (1, 0, 2)).astype(BF16)
    return dict(win=win_aug, wqb=wqb_aug, wuk=wuk_t, wuv=wuv, wo=w_o[l].astype(BF16),
                wg=w_gate[l].astype(BF16), wu=w_up[l].astype(BF16), wd=w_down[l].astype(BF16),
                wp=w_ple[l].astype(BF16), wpg=w_ple_gate[l].astype(BF16))


def _rope_table(pos):
    inv = ROPE_THETA ** (-jnp.arange(0, ROPE_DIM, 2, dtype=F32) / ROPE_DIM)
    ang = pos.astype(F32)[:, None] * inv[None, :]
    cos, sin = jnp.cos(ang), jnp.sin(ang)
    return jnp.concatenate([cos, cos, sin, sin], axis=-1)


def _tiles(n, t, prompt):
    if prompt:
        tm = min(512, t)
        return dict(tm=tm, tq=min(256, t), tk=tm, tb=tm, cb=min(128, t), bb=2, tmoe=min(1024, n))
    return dict(tm=min(512, n), tq=t, tk=None, tb=t, cb=t, bb=2, tmoe=min(1024, n))


def _layer(x, p, cs, s0, past, lbp, prm, small, *, alpha, prompt):
    bsz, t, _ = x.shape
    n = bsz * t
    tl = _tiles(n, t, prompt)
    x2d = x.reshape(n, D_MODEL)
    tm, tb, cb = tl["tm"], tl["tb"], tl["cb"]
    outs = _inproj(x2d, cs, prm["win"], lbp, small["qg"], prm["wqb"], small["kvg"], prm["wuk"],
                   tm=tm, tq=tl["tq"], cb=cb, prompt=prompt)
    qs, kin, logf, v, gs, qabs, qpe, ckv, kpe = outs[:9]
    r4 = lambda a: a.reshape(HA_HEADS, bsz, t, HA_DK)
    if prompt:
        vt, ckvb, ckvt, kpet = outs[9:]
        vt = vt.reshape(bsz, t // tb, HA_HEADS, tb // cb, HA_DV, cb)
    else:
        vt = jnp.transpose(r4(v), (1, 0, 3, 2)).astype(BF16)
        vt = vt.reshape(bsz, 1, HA_HEADS, 1, HA_DV, t)
    o_raw, s_new = _hgrn(r4(qs), r4(kin), r4(logf), r4(v), vt, s0, bb=tl["bb"], tb=tb, cb=cb)
    if prompt:
        olat = _attn_prompt(qabs, qpe, ckvt, kpet, ckvb, bsz=bsz, t=t, tq=tl["tq"], tk=tl["tk"])
    else:
        olat = _attn_sample(qabs, qpe, past[0], past[1], ckv, kpe, ts=t)
    x1, x1b, gates = _merge(x2d, o_raw.reshape(HA_HEADS, n, HA_DV), gs, olat, prm["wuv"], prm["wo"],
                            small["hg"], small["mg"], small["g1"], small["b1"],
                            small["wr"], small["rb"], tm=tl["tm"], tq=tl["tq"], alpha=alpha)
    y = _moe(x1b, x1, gates, p.reshape(n, PLE_DIM), prm["wg"], prm["wu"], prm["wd"],
             small["g2"], small["b2"], prm["wpg"], prm["wp"], tm=tl["tmoe"], alpha=alpha)
    return (y.reshape(bsz, t, D_MODEL), s_new, ckv.reshape(bsz, t, KV_LORA),
            kpe.reshape(bsz, t, ROPE_DIM))


def kernel(x_prompt, x_sample, p_prompt, p_sample, state_hgrn, cache_ckv, cache_kpe, w_in,
           lb_logits, hgrn_norm_g, q_norm_g, w_qb, kv_norm_g, w_kvb, mla_norm_g, w_o,
           ln1_g, ln1_b, ln2_g, ln2_b, w_router, router_bias, w_gate, w_up, w_down,
           w_ple, w_ple_gate):
    depth = w_in.shape[0]
    alpha = (2 * depth) ** 0.25
    bp, tp, _ = x_prompt.shape
    bs, ts, _ = x_sample.shape
    past = cache_ckv.shape[2]

    sm = jax.nn.softmax(lb_logits.astype(F32), axis=0)
    lb_all = jnp.maximum(jnp.cumsum(sm, axis=0) - sm[0:1], 0.0)
    lbp_all = jnp.stack([jnp.log(lb_all), jnp.log1p(-lb_all), 1.0 - lb_all], axis=1)

    cs_p = jnp.tile(_rope_table(jnp.arange(tp)), (bp, 1))
    cs_s = jnp.tile(_rope_table(past + jnp.arange(ts)), (bs, 1))
    wr = jnp.pad(w_router, ((0, 0), (0, LANES - N_EXPERTS))).astype(BF16)
    rb = jnp.pad(router_bias.astype(F32), (0, LANES - N_EXPERTS)).reshape(1, LANES)
    s0_p = jnp.zeros((bp, HA_HEADS, HA_DK, HA_DV), F32)

    yp, ys = x_prompt, x_sample
    res = [[] for _ in range(6)]
    for l in range(depth):
        prm = _prep_layer(l, w_in, w_qb, w_kvb, w_o, w_gate, w_up, w_down, w_ple, w_ple_gate)
        row = lambda a: a[l].reshape(1, -1).astype(F32)
        small = dict(qg=row(q_norm_g), kvg=row(kv_norm_g), hg=row(hgrn_norm_g), mg=row(mla_norm_g),
                     g1=row(ln1_g), b1=row(ln1_b), g2=row(ln2_g), b2=row(ln2_b), wr=wr, rb=rb)
        yp, sp, cp, kp = _layer(yp, p_prompt[l], cs_p, s0_p, None, lbp_all[l], prm, small,
                                alpha=alpha, prompt=True)
        ys, ss, cs_, ks = _layer(ys, p_sample[l], cs_s, state_hgrn[l],
                                 (cache_ckv[l], cache_kpe[l]), lbp_all[l], prm, small,
                                 alpha=alpha, prompt=False)
        for lst, a in zip(res, (sp, cp, kp, ss, cs_, ks)):
            lst.append(a)
    sp, cp, kp, ss, cs_, ks = (jnp.stack(a) for a in res)
    return (yp, ys, sp, cp, kp, ss, cs_, ks)
```

```python
import functools

import jax
import jax.numpy as jnp
from jax import lax
from jax.experimental import pallas as pl
from jax.experimental.pallas import tpu as pltpu

F32 = jnp.float32
BF16 = jnp.bfloat16

D_MODEL = 1024
HA_HEADS = 4
HA_DK = 128
HA_DV = 128
HB_HEADS = 4
Q_LORA = 384
KV_LORA = 256
NOPE_DIM = 128
ROPE_DIM = 64
V_DIM = 128
ROPE_THETA = 10000.0
MLA_SCALE = (NOPE_DIM + ROPE_DIM) ** -0.5
Q_SCALE = MLA_SCALE * 1.4426950408889634
CHUNK = 64
N_EXPERTS = 16
N_GROUPS = 4
EXPERTS_PER_GROUP = N_EXPERTS // N_GROUPS
D_EXPERT = 512
PLE_DIM = 256
NEG_INF = -1e30

HA_W = HA_HEADS * HA_DK
COL_CQ = 4 * HA_W
COL_CKV = COL_CQ + Q_LORA
COL_KPE = COL_CKV + KV_LORA
D_IN_AUG = COL_KPE + 2 * ROPE_DIM
Q_HEAD_AUG = NOPE_DIM + 2 * ROPE_DIM

LANES = 128
SCORE_W = 512
SUB_BLOCK = 16
VMEM_LIMIT = 56 * 1024 * 1024


def _cparams(sem, vmem=VMEM_LIMIT):
    return pltpu.CompilerParams(dimension_semantics=sem, vmem_limit_bytes=vmem)


def _dot(a, b):
    return jnp.dot(a, b, preferred_element_type=F32)


def _dot_nt(a, b):
    return lax.dot_general(a, b, (((1,), (1,)), ((), ())), preferred_element_type=F32)


def _rms(x, g, eps=1e-6):
    return x * lax.rsqrt(jnp.mean(x * x, axis=-1, keepdims=True) + eps) * g


def _layer_norm(x, g, b, eps=1e-5):
    mu = jnp.mean(x, axis=-1, keepdims=True)
    xc = x - mu
    var = jnp.mean(xc * xc, axis=-1, keepdims=True)
    return xc * lax.rsqrt(var + eps) * g + b


def _silu(x):
    return x * jax.nn.sigmoid(x)


def _full(shape):
    nd = len(shape)
    return pl.BlockSpec(shape, lambda *_: (0,) * nd)


def _inproj_kernel(x_ref, cs_ref, win_ref, lb_ref, qg_ref, wqb_ref, kvg_ref, wuk_ref,
                   qs_ref, kin_ref, logf_ref, v_ref, gs_ref, qabs_ref, qpe_ref,
                   ckv_ref, kpe_ref, *rest, tq, cb, prompt):
    tm = x_ref.shape[0]
    x = x_ref[...].astype(BF16)

    def sect(lo, hi):
        return _dot(x, win_ref[:, lo:hi])

    def put_heads(ref, val):
        for h in range(HA_HEADS):
            ref[h] = val[:, h * HA_DK:(h + 1) * HA_DK]

    put_heads(qs_ref, _silu(sect(0, HA_W)))
    fa = sect(HA_W, 2 * HA_W)
    log_lb = lb_ref[0:1, :]
    log1m_lb = lb_ref[1:2, :]
    one_m_lb = lb_ref[2:3, :]
    log_sig = jnp.minimum(fa, 0.0) - jnp.log1p(jnp.exp(-jnp.abs(fa)))
    c = log1m_lb + log_sig
    hi_ = jnp.maximum(log_lb, c)
    lo_ = jnp.minimum(log_lb, c)
    put_heads(logf_ref, hi_ + jnp.log1p(jnp.exp(lo_ - hi_)))
    put_heads(kin_ref, one_m_lb * jax.nn.sigmoid(-fa))
    v = sect(2 * HA_W, 3 * HA_W)
    put_heads(v_ref, v)
    gs_ref[...] = _silu(sect(3 * HA_W, 4 * HA_W))

    cs = cs_ref[...]

    def rope(t):
        prod = t * cs
        return prod + pltpu.roll(prod, ROPE_DIM, 1)

    ckv = _rms(sect(COL_CKV, COL_KPE), kvg_ref[...])
    ckv_ref[...] = ckv
    kpe2 = rope(sect(COL_KPE, D_IN_AUG))
    kpe_ref[...] = kpe2[:, :ROPE_DIM]
    if prompt:
        vt_ref, ckvb_ref, ckvt_ref, kpet_ref = rest
        for h in range(HA_HEADS):
            for ci in range(tm // cb):
                vt_ref[0, h, ci] = v[ci * cb:(ci + 1) * cb, h * HA_DV:(h + 1) * HA_DV].T.astype(BF16)
        ckvb_ref[...] = ckv.astype(BF16)
        ckvt_ref[0] = ckv.T.astype(BF16)
        kpet_ref[0] = kpe2.T.astype(BF16)

    cqn = _rms(sect(COL_CQ, COL_CKV), qg_ref[...]).astype(BF16)
    lane = lax.broadcasted_iota(jnp.int32, (tm, LANES), 1)
    for h in range(HB_HEADS):
        qh = _dot(cqn, wqb_ref[:, h * Q_HEAD_AUG:(h + 1) * Q_HEAD_AUG])
        qabs = (_dot(qh[:, :NOPE_DIM].astype(BF16), wuk_ref[h]) * Q_SCALE).astype(BF16)
        qpe = rope(qh[:, NOPE_DIM:]) * Q_SCALE
        qpe = jnp.where(lane < ROPE_DIM, qpe, 0.0).astype(BF16)
        for s in range(tm // tq):
            qabs_ref[s, h] = qabs[s * tq:(s + 1) * tq]
            qpe_ref[s, h] = qpe[s * tq:(s + 1) * tq]


def _inproj(x, cs, w_in, lbp, qg, wqb, kvg, wuk, *, tm, tq, cb, prompt):
    n = x.shape[0]
    nt = n // tm
    row = lambda w: pl.BlockSpec((tm, w), lambda i: (i, 0))
    heads = pl.BlockSpec((HA_HEADS, tm, HA_DK), lambda i: (0, i, 0))
    qblk = lambda w: pl.BlockSpec((tm // tq, HB_HEADS, tq, w), lambda i: (i, 0, 0, 0))
    head_major = jax.ShapeDtypeStruct((HA_HEADS, n, HA_DK), F32)
    out_shape = [
        head_major,
        head_major,
        head_major,
        head_major,
        jax.ShapeDtypeStruct((n, HA_W), F32),
        jax.ShapeDtypeStruct((n // tq, HB_HEADS, tq, KV_LORA), BF16),
        jax.ShapeDtypeStruct((n // tq, HB_HEADS, tq, LANES), BF16),
        jax.ShapeDtypeStruct((n, KV_LORA), F32),
        jax.ShapeDtypeStruct((n, ROPE_DIM), F32),
    ]
    out_specs = [heads, heads, heads, heads, row(HA_W), qblk(KV_LORA), qblk(LANES),
                 row(KV_LORA), row(ROPE_DIM)]
    if prompt:
        out_shape += [jax.ShapeDtypeStruct((nt, HA_HEADS, tm // cb, HA_DV, cb), BF16),
                      jax.ShapeDtypeStruct((n, KV_LORA), BF16),
                      jax.ShapeDtypeStruct((nt, KV_LORA, tm), BF16),
                      jax.ShapeDtypeStruct((nt, LANES, tm), BF16)]
        out_specs += [pl.BlockSpec((1, HA_HEADS, tm // cb, HA_DV, cb), lambda i: (i, 0, 0, 0, 0)),
                      row(KV_LORA),
                      pl.BlockSpec((1, KV_LORA, tm), lambda i: (i, 0, 0)),
                      pl.BlockSpec((1, LANES, tm), lambda i: (i, 0, 0))]
    return pl.pallas_call(
        functools.partial(_inproj_kernel, tq=tq, cb=cb, prompt=prompt),
        grid=(nt,),
        in_specs=[row(D_MODEL), row(LANES), _full(w_in.shape), _full(lbp.shape), _full(qg.shape),
                  _full(wqb.shape), _full(kvg.shape), _full(wuk.shape)],
        out_specs=out_specs,
        out_shape=out_shape,
        compiler_params=_cparams(("parallel",)),
        name="inproj",
    )(x, cs, w_in, lbp, qg, wqb, kvg, wuk)


def _hgrn_kernel(q_ref, k_ref, g_ref, v_ref, vt_ref, s0_ref, o_ref, sfin_ref, st_scr, *, cb):
    _, bb, tb, _ = q_ref.shape
    nsub = cb // SUB_BLOCK
    ti = pl.program_id(1)

    @pl.when(ti == 0)
    def _():
        for b in range(bb):
            for h in range(HA_HEADS):
                st_scr[b, h] = s0_ref[b, h].T

    r_i = lax.broadcasted_iota(jnp.int32, (cb, cb), 0)
    c_i = lax.broadcasted_iota(jnp.int32, (cb, cb), 1)
    tril = (r_i >= c_i).astype(F32)
    row16 = lax.broadcasted_iota(jnp.int32, (SUB_BLOCK, 1), 0)

    def chunk_head(idx, carry):
        ci = idx // (bb * HA_HEADS)
        rem = idx % (bb * HA_HEADS)
        b = rem // HA_HEADS
        h = rem % HA_HEADS
        r0 = pl.multiple_of(ci * cb, cb)
        rows = pl.ds(r0, cb)
        q = q_ref[h, b, rows, :]
        k = k_ref[h, b, rows, :]
        g = g_ref[h, b, rows, :]
        v = v_ref[h, b, rows, :]
        vt = vt_ref[b, h, ci]
        st = st_scr[b, h]

        bcum = jnp.dot(tril, g, precision=lax.Precision.HIGHEST, preferred_element_type=F32)
        b_last = bcum[cb - 1:cb, :]

        o = _dot_nt((q * jnp.exp(bcum)).astype(BF16), st.astype(BF16))
        o_parts = [o[j * SUB_BLOCK:(j + 1) * SUB_BLOCK] for j in range(nsub)]

        for j in range(nsub - 1):
            lo, hi = j * SUB_BLOCK, (j + 1) * SUB_BLOCK
            bnd = bcum[hi - 1:hi, :]
            kd = (k[lo:hi] * jnp.exp(bnd - bcum[lo:hi])).astype(BF16)
            qe = (q[hi:] * jnp.exp(bcum[hi:] - bnd)).astype(BF16)
            a = _dot_nt(qe, kd).astype(BF16)
            upd = _dot(a, v[lo:hi].astype(BF16))
            for jj in range(j + 1, nsub):
                o_parts[jj] = o_parts[jj] + upd[(jj - j - 1) * SUB_BLOCK:(jj - j) * SUB_BLOCK]

        for j in range(nsub):
            lo, hi = j * SUB_BLOCK, (j + 1) * SUB_BLOCK
            qj, kj, bj, vj = q[lo:hi], k[lo:hi], bcum[lo:hi], v[lo:hi]
            acc = o_parts[j]
            for s in range(SUB_BLOCK):
                arg = jnp.where(row16 >= s, bj - bj[s:s + 1, :], NEG_INF)
                w = qj * jnp.exp(arg) * kj[s:s + 1, :]
                acc = acc + jnp.sum(w, axis=-1, keepdims=True) * vj[s:s + 1, :]
            o_ref[h, b, pl.ds(pl.multiple_of(r0 + lo, SUB_BLOCK), SUB_BLOCK), :] = acc

        kd_all = (k * jnp.exp(b_last - bcum)).astype(BF16)
        st_scr[b, h] = st * jnp.exp(b_last) + _dot(vt, kd_all)
        return carry

    lax.fori_loop(0, (tb // cb) * bb * HA_HEADS, chunk_head, 0)

    @pl.when(ti == pl.num_programs(1) - 1)
    def _():
        for b in range(bb):
            for h in range(HA_HEADS):
                sfin_ref[b, h] = st_scr[b, h].T


def _hgrn(qs, kin, logf, v, vt, s0, *, bb, tb, cb):
    _, bsz, t, _ = qs.shape
    blk = pl.BlockSpec((HA_HEADS, bb, tb, HA_DK), lambda i, j: (0, i, j, 0))
    st_blk = pl.BlockSpec((bb, HA_HEADS, HA_DK, HA_DV), lambda i, j: (i, 0, 0, 0))
    vt_blk = pl.BlockSpec((bb, None, HA_HEADS, tb // cb, HA_DV, cb),
                          lambda i, j: (i, j, 0, 0, 0, 0))
    return pl.pallas_call(
        functools.partial(_hgrn_kernel, cb=cb),
        grid=(bsz // bb, t // tb),
        in_specs=[blk, blk, blk, blk, vt_blk, st_blk],
        out_specs=[blk, st_blk],
        out_shape=[jax.ShapeDtypeStruct(qs.shape, F32),
                   jax.ShapeDtypeStruct(s0.shape, F32)],
        scratch_shapes=[pltpu.VMEM((bb, HA_HEADS, HA_DV, HA_DK), F32)],
        compiler_params=_cparams(("parallel", "arbitrary")),
        name="hgrn",
    )(qs, kin, logf, v, vt, s0)


def _attn_prompt_kernel(qa_ref, qp_ref, kt_ref, pt_ref, kv_ref, o_ref, m_scr, l_scr, acc_scr,
                        *, tq, tk):
    qi = pl.program_id(1)
    m_scr[...] = jnp.full(m_scr.shape, NEG_INF, F32)
    l_scr[...] = jnp.zeros(l_scr.shape, F32)
    acc_scr[...] = jnp.zeros(acc_scr.shape, F32)

    def chain(j, c, masked):
        k0 = j * tk + c * SCORE_W
        kt = kt_ref[j, :, c * SCORE_W:(c + 1) * SCORE_W]
        pt = pt_ref[j, :, c * SCORE_W:(c + 1) * SCORE_W]
        kv = kv_ref[pl.ds(pl.multiple_of(k0, SCORE_W), SCORE_W), :]
        if masked:
            shift = CHUNK.bit_length() - 1
            qpos = qi * tq + lax.broadcasted_iota(jnp.int32, (tq, 1), 0)
            kpos = k0 + lax.broadcasted_iota(jnp.int32, (1, SCORE_W), 1)
            visible = (kpos >> shift) <= (qpos >> shift)
        for h in range(HB_HEADS):
            s = _dot(qa_ref[0, h], kt) + _dot(qp_ref[0, h], pt)
            if masked:
                s = jnp.where(visible, s, NEG_INF)
            m_prev = m_scr[h]
            m_new = jnp.maximum(m_prev, jnp.max(s, axis=-1, keepdims=True))
            alpha = jnp.exp2(m_prev - m_new)
            p = jnp.exp2(s - pltpu.repeat(m_new, SCORE_W // LANES, axis=1))
            l_scr[h] = alpha * l_scr[h] + jnp.sum(p, axis=-1, keepdims=True)
            acc_scr[h] = (pltpu.repeat(alpha, KV_LORA // LANES, axis=1) * acc_scr[h]
                          + _dot(p.astype(BF16), kv))
            m_scr[h] = m_new

    jd = (qi * tq) // tk

    def body(j, carry):
        for c in range(tk // SCORE_W):
            chain(j, c, False)
        return carry

    lax.fori_loop(0, jd, body, 0)
    for c in range(tk // SCORE_W):
        pl.when(jd * tk + c * SCORE_W < (qi + 1) * tq)(functools.partial(chain, jd, c, True))
    for h in range(HB_HEADS):
        inv_l = pltpu.repeat(1.0 / l_scr[h], KV_LORA // LANES, axis=1)
        o_ref[0, h] = (acc_scr[h] * inv_l).astype(BF16)


def _attn_prompt(qabs, qpe, ckvt, kpet, ckvb, *, bsz, t, tq, tk):
    nq = t // tq
    nk = t // tk
    rows = HB_HEADS * tq
    single = pl.Buffered(1)
    return pl.pallas_call(
        functools.partial(_attn_prompt_kernel, tq=tq, tk=tk),
        grid=(bsz, nq),
        in_specs=[
            pl.BlockSpec((1, HB_HEADS, tq, KV_LORA), lambda b, i: (b * nq + i, 0, 0, 0)),
            pl.BlockSpec((1, HB_HEADS, tq, LANES), lambda b, i: (b * nq + i, 0, 0, 0)),
            pl.BlockSpec((nk, KV_LORA, tk), lambda b, i: (b, 0, 0), pipeline_mode=single),
            pl.BlockSpec((nk, LANES, tk), lambda b, i: (b, 0, 0), pipeline_mode=single),
            pl.BlockSpec((t, KV_LORA), lambda b, i: (b, 0), pipeline_mode=single),
        ],
        out_specs=pl.BlockSpec((1, HB_HEADS, tq, KV_LORA), lambda b, i: (b * nq + i, 0, 0, 0)),
        out_shape=jax.ShapeDtypeStruct(qabs.shape, BF16),
        scratch_shapes=[pltpu.VMEM((HB_HEADS, tq, LANES), F32), pltpu.VMEM((HB_HEADS, tq, LANES), F32),
                        pltpu.VMEM((HB_HEADS, tq, KV_LORA), F32)],
        compiler_params=_cparams(("parallel", "arbitrary")),
        name="attn_prompt",
    )(qabs, qpe, ckvt, kpet, ckvb)


def _attn_sample_kernel(qa_ref, qp_ref, pckv_ref, pkpe_ref, nckv_ref, nkpe_ref, o_ref):
    _, _, ts, _ = qa_ref.shape
    rows = HB_HEADS * ts
    qa = qa_ref[0].reshape(rows, KV_LORA)
    qp = qp_ref[0].reshape(rows, LANES)[:, :ROPE_DIM]
    pckv = pckv_ref[0].astype(BF16)
    nckv = nckv_ref[...].astype(BF16)
    s_past = _dot_nt(qa, pckv) + _dot_nt(qp, pkpe_ref[0].astype(BF16))
    s_new = _dot_nt(qa, nckv) + _dot_nt(qp, nkpe_ref[...].astype(BF16))
    m = jnp.maximum(jnp.max(s_past, axis=-1, keepdims=True), jnp.max(s_new, axis=-1, keepdims=True))
    p_past = jnp.exp2(s_past - m)
    p_new = jnp.exp2(s_new - m)
    l = jnp.sum(p_past, axis=-1, keepdims=True) + jnp.sum(p_new, axis=-1, keepdims=True)
    o = (_dot(p_past.astype(BF16), pckv) + _dot(p_new.astype(BF16), nckv)) / l
    o_ref[0] = o.astype(BF16).reshape(HB_HEADS, ts, KV_LORA)


def _attn_sample(qabs, qpe, past_ckv, past_kpe, ckv, kpe, *, ts):
    bsz, past, _ = past_ckv.shape
    return pl.pallas_call(
        _attn_sample_kernel,
        grid=(bsz,),
        in_specs=[
            pl.BlockSpec((1, HB_HEADS, ts, KV_LORA), lambda b: (b, 0, 0, 0)),
            pl.BlockSpec((1, HB_HEADS, ts, LANES), lambda b: (b, 0, 0, 0)),
            pl.BlockSpec((1, past, KV_LORA), lambda b: (b, 0, 0)),
            pl.BlockSpec((1, past, ROPE_DIM), lambda b: (b, 0, 0)),
            pl.BlockSpec((ts, KV_LORA), lambda b: (b, 0)),
            pl.BlockSpec((ts, ROPE_DIM), lambda b: (b, 0)),
        ],
        out_specs=pl.BlockSpec((1, HB_HEADS, ts, KV_LORA), lambda b: (b, 0, 0, 0)),
        out_shape=jax.ShapeDtypeStruct(qabs.shape, BF16),
        compiler_params=_cparams(("parallel",)),
        name="attn_sample",
    )(qabs, qpe, past_ckv, past_kpe, ckv, kpe)


def _gates(logits, bias):
    tm = logits.shape[0]
    lane = lax.broadcasted_iota(jnp.int32, (tm, LANES), 1)
    pos = lane % EXPERTS_PER_GROUP
    valid = lane < N_EXPERTS
    scores = jax.nn.sigmoid(logits)
    sel = jnp.where(valid, scores + bias, -jnp.inf)

    others = []
    for r in range(1, EXPERTS_PER_GROUP):
        others.append(jnp.where(pos >= r, pltpu.roll(sel, r, 1),
                                pltpu.roll(sel, LANES - (EXPERTS_PER_GROUP - r), 1)))
    a, b, c, d = sel, others[0], others[1], others[2]
    hi1, lo1 = jnp.maximum(a, b), jnp.minimum(a, b)
    hi2, lo2 = jnp.maximum(c, d), jnp.minimum(c, d)
    gscore = jnp.maximum(hi1, hi2) + jnp.maximum(jnp.minimum(hi1, hi2), jnp.maximum(lo1, lo2))
    gmax = jnp.max(gscore, axis=-1, keepdims=True)
    group = (lane // EXPERTS_PER_GROUP).astype(F32)
    gidx = jnp.min(jnp.where(gscore == gmax, group, float(LANES)), axis=-1, keepdims=True)

    rank = jnp.zeros((tm, LANES), jnp.int32)
    for r, o in enumerate(others, start=1):
        ahead = (o > sel) | ((o == sel) & (pos >= r))
        rank = rank + ahead.astype(jnp.int32)
    chosen = (group == gidx) & (rank < 2) & valid
    w = jnp.where(chosen, scores, 0.0)
    return w / jnp.sum(w, axis=-1, keepdims=True)


def _merge_kernel(x_ref, o_ref, gs_ref, olat_ref, wuv_ref, wo_ref, hg_ref, mg_ref,
                  g1_ref, b1_ref, wr_ref, rb_ref, x1_ref, x1b_ref, gate_ref, *, alpha):
    nsb, _, tq, _ = olat_ref.shape
    gs = gs_ref[...]
    hg = hg_ref[...]
    oa = jnp.concatenate(
        [_rms(o_ref[h], hg) * gs[:, h * HA_DV:(h + 1) * HA_DV] for h in range(HA_HEADS)],
        axis=-1)
    ob = []
    for h in range(HB_HEADS):
        lat = jnp.concatenate([olat_ref[s, h] for s in range(nsb)], axis=0)
        ob.append(_dot(lat, wuv_ref[h]))
    ob = _rms(jnp.concatenate(ob, axis=-1), mg_ref[...])
    d_a = HA_HEADS * HA_DV
    mix = _dot(oa.astype(BF16), wo_ref[:d_a, :]) + _dot(ob.astype(BF16), wo_ref[d_a:, :])
    x1 = _layer_norm(alpha * x_ref[...] + mix, g1_ref[...], b1_ref[...])
    x1_ref[...] = x1
    x1b = x1.astype(BF16)
    x1b_ref[...] = x1b
    gate_ref[...] = _gates(_dot(x1b, wr_ref[...]), rb_ref[...])


def _merge(x, o_raw, gs, olat, wuv, wo, hg, mg, g1, b1, wr, rb, *, tm, tq, alpha):
    n = x.shape[0]
    row = lambda w: pl.BlockSpec((tm, w), lambda i: (i, 0))
    return pl.pallas_call(
        functools.partial(_merge_kernel, alpha=alpha),
        grid=(n // tm,),
        in_specs=[row(D_MODEL), pl.BlockSpec((HA_HEADS, tm, HA_DV), lambda i: (0, i, 0)), row(HA_W),
                  pl.BlockSpec((tm // tq, HB_HEADS, tq, KV_LORA), lambda i: (i, 0, 0, 0)),
                  _full(wuv.shape), _full(wo.shape), _full(hg.shape), _full(mg.shape),
                  _full(g1.shape), _full(b1.shape), _full(wr.shape), _full(rb.shape)],
        out_specs=[row(D_MODEL), row(D_MODEL), row(LANES)],
        out_shape=[jax.ShapeDtypeStruct((n, D_MODEL), F32),
                   jax.ShapeDtypeStruct((n, D_MODEL), BF16),
                   jax.ShapeDtypeStruct((n, LANES), F32)],
        compiler_params=_cparams(("parallel",)),
        name="merge",
    )(x, o_raw, gs, olat, wuv, wo, hg, mg, g1, b1, wr, rb)


def _moe_kernel(x1b_ref, x1_ref, gate_ref, p_ref, wg_ref, wu_ref, wd_ref, g2_ref, b2_ref,
                wpg_ref, wp_ref, y_ref, acc_scr, *, alpha):
    e = pl.program_id(1)
    tm = x1b_ref.shape[0]

    @pl.when(e == 0)
    def _():
        acc_scr[...] = jnp.zeros(acc_scr.shape, F32)

    xb = x1b_ref[...]
    hmid = _silu(_dot(xb, wg_ref[0])) * _dot(xb, wu_ref[0])
    lane = lax.broadcasted_iota(jnp.int32, (tm, LANES), 1)
    gcol = jnp.sum(jnp.where(lane == e, gate_ref[...], 0.0), axis=-1, keepdims=True)
    acc_scr[...] += gcol * _dot(hmid.astype(BF16), wd_ref[0])

    @pl.when(e == pl.num_programs(1) - 1)
    def _():
        x2 = _layer_norm(alpha * x1_ref[...] + acc_scr[...], g2_ref[...], b2_ref[...])
        gate = jax.nn.sigmoid(_dot(x2.astype(BF16), wpg_ref[...]))
        y_ref[...] = x2 + gate * _dot(p_ref[...].astype(BF16), wp_ref[...])


def _moe(x1b, x1, gates, p, wg, wu, wd, g2, b2, wpg, wp, *, tm, alpha):
    n = x1.shape[0]
    row = lambda w: pl.BlockSpec((tm, w), lambda i, e: (i, 0))
    return pl.pallas_call(
        functools.partial(_moe_kernel, alpha=alpha),
        grid=(n // tm, N_EXPERTS),
        in_specs=[row(D_MODEL), row(D_MODEL), row(LANES), row(PLE_DIM),
                  pl.BlockSpec((1, D_MODEL, D_EXPERT), lambda i, e: (e, 0, 0)),
                  pl.BlockSpec((1, D_MODEL, D_EXPERT), lambda i, e: (e, 0, 0)),
                  pl.BlockSpec((1, D_EXPERT, D_MODEL), lambda i, e: (e, 0, 0)),
                  pl.BlockSpec(g2.shape, lambda i, e: (0, 0)),
                  pl.BlockSpec(b2.shape, lambda i, e: (0, 0)),
                  pl.BlockSpec(wpg.shape, lambda i, e: (0, 0)),
                  pl.BlockSpec(wp.shape, lambda i, e: (0, 0))],
        out_specs=row(D_MODEL),
        out_shape=jax.ShapeDtypeStruct((n, D_MODEL), F32),
        scratch_shapes=[pltpu.VMEM((tm, D_MODEL), F32)],
        compiler_params=_cparams(("parallel", "arbitrary")),
        name="moe",
    )(x1b, x1, gates, p, wg, wu, wd, g2, b2, wpg, wp)


def _rot_cols(w):
    half = ROPE_DIM // 2
    return jnp.concatenate([-w[..., half:], w[..., :half]], axis=-1)


def _prep_layer(l, w_in, w_qb, w_kvb, w_o, w_gate, w_up, w_down, w_ple, w_ple_gate):
    win = w_in[l]
    win_aug = jnp.concatenate([win, _rot_cols(win[:, COL_KPE:])], axis=-1).astype(BF16)
    wqb = w_qb[l].reshape(Q_LORA, HB_HEADS, NOPE_DIM + ROPE_DIM)
    wqb_aug = jnp.concatenate([wqb, _rot_cols(wqb[..., NOPE_DIM:])], axis=-1)
    wqb_aug = wqb_aug.reshape(Q_LORA, HB_HEADS * Q_HEAD_AUG).astype(BF16)
    wkvb = w_kvb[l].reshape(KV_LORA, HB_HEADS, NOPE_DIM + V_DIM)
    wuk_t = jnp.transpose(wkvb[..., :NOPE_DIM], (1, 2, 0)).astype(BF16)
    wuv = jnp.transpose(wkvb[..., NOPE_DIM:], (1, 0, 2)).astype(BF16)
    return dict(win=win_aug, wqb=wqb_aug, wuk=wuk_t, wuv=wuv, wo=w_o[l].astype(BF16),
                wg=w_gate[l].astype(BF16), wu=w_up[l].astype(BF16), wd=w_down[l].astype(BF16),
                wp=w_ple[l].astype(BF16), wpg=w_ple_gate[l].astype(BF16))


def _rope_table(pos):
    inv = ROPE_THETA ** (-jnp.arange(0, ROPE_DIM, 2, dtype=F32) / ROPE_DIM)
    ang = pos.astype(F32)[:, None] * inv[None, :]
    cos, sin = jnp.cos(ang), jnp.sin(ang)
    return jnp.concatenate([cos, cos, sin, sin], axis=-1)


def _tiles(n, t, prompt):
    if prompt:
        tm = min(512, t)
        return dict(tm=tm, tq=min(256, t), tk=tm, tb=tm, cb=min(128, t), bb=2, tmoe=min(1024, n))
    return dict(tm=min(512, n), tq=t, tk=None, tb=t, cb=t, bb=2, tmoe=min(1024, n))


def _layer(x, p, cs, s0, past, lbp, prm, small, *, alpha, prompt):
    bsz, t, _ = x.shape
    n = bsz * t
    tl = _tiles(n, t, prompt)
    x2d = x.reshape(n, D_MODEL)
    tm, tb, cb = tl["tm"], tl["tb"], tl["cb"]
    outs = _inproj(x2d, cs, prm["win"], lbp, small["qg"], prm["wqb"], small["kvg"], prm["wuk"],
                   tm=tm, tq=tl["tq"], cb=cb, prompt=prompt)
    qs, kin, logf, v, gs, qabs, qpe, ckv, kpe = outs[:9]
    r4 = lambda a: a.reshape(HA_HEADS, bsz, t, HA_DK)
    if prompt:
        vt, ckvb, ckvt, kpet = outs[9:]
        vt = vt.reshape(bsz, t // tb, HA_HEADS, tb // cb, HA_DV, cb)
    else:
        vt = jnp.transpose(r4(v), (1, 0, 3, 2)).astype(BF16)
        vt = vt.reshape(bsz, 1, HA_HEADS, 1, HA_DV, t)
    o_raw, s_new = _hgrn(r4(qs), r4(kin), r4(logf), r4(v), vt, s0, bb=tl["bb"], tb=tb, cb=cb)
    if prompt:
        olat = _attn_prompt(qabs, qpe, ckvt, kpet, ckvb, bsz=bsz, t=t, tq=tl["tq"], tk=tl["tk"])
    else:
        olat = _attn_sample(qabs, qpe, past[0], past[1], ckv, kpe, ts=t)
    x1, x1b, gates = _merge(x2d, o_raw.reshape(HA_HEADS, n, HA_DV), gs, olat, prm["wuv"], prm["wo"],
                            small["hg"], small["mg"], small["g1"], small["b1"],
                            small["wr"], small["rb"], tm=tl["tm"], tq=tl["tq"], alpha=alpha)
    y = _moe(x1b, x1, gates, p.reshape(n, PLE_DIM), prm["wg"], prm["wu"], prm["wd"],
             small["g2"], small["b2"], prm["wpg"], prm["wp"], tm=tl["tmoe"], alpha=alpha)
    return (y.reshape(bsz, t, D_MODEL), s_new, ckv.reshape(bsz, t, KV_LORA),
            kpe.reshape(bsz, t, ROPE_DIM))


def kernel(x_prompt, x_sample, p_prompt, p_sample, state_hgrn, cache_ckv, cache_kpe, w_in,
           lb_logits, hgrn_norm_g, q_norm_g, w_qb, kv_norm_g, w_kvb, mla_norm_g, w_o,
           ln1_g, ln1_b, ln2_g, ln2_b, w_router, router_bias, w_gate, w_up, w_down,
           w_ple, w_ple_gate):
    depth = w_in.shape[0]
    alpha = (2 * depth) ** 0.25
    bp, tp, _ = x_prompt.shape
    bs, ts, _ = x_sample.shape
    past = cache_ckv.shape[2]

    sm = jax.nn.softmax(lb_logits.astype(F32), axis=0)
    lb_all = jnp.maximum(jnp.cumsum(sm, axis=0) - sm[0:1], 0.0)
    lbp_all = jnp.stack([jnp.log(lb_all), jnp.log1p(-lb_all), 1.0 - lb_all], axis=1)

    cs_p = jnp.tile(_rope_table(jnp.arange(tp)), (bp, 1))
    cs_s = jnp.tile(_rope_table(past + jnp.arange(ts)), (bs, 1))
    wr = jnp.pad(w_router, ((0, 0), (0, LANES - N_EXPERTS))).astype(BF16)
    rb = jnp.pad(router_bias.astype(F32), (0, LANES - N_EXPERTS)).reshape(1, LANES)
    s0_p = jnp.zeros((bp, HA_HEADS, HA_DK, HA_DV), F32)

    yp, ys = x_prompt, x_sample
    res = [[] for _ in range(6)]
    for l in range(depth):
        prm = _prep_layer(l, w_in, w_qb, w_kvb, w_o, w_gate, w_up, w_down, w_ple, w_ple_gate)
        row = lambda a: a[l].reshape(1, -1).astype(F32)
        small = dict(qg=row(q_norm_g), kvg=row(kv_norm_g), hg=row(hgrn_norm_g), mg=row(mla_norm_g),
                     g1=row(ln1_g), b1=row(ln1_b), g2=row(ln2_g), b2=row(ln2_b), wr=wr, rb=rb)
        yp, sp, cp, kp = _layer(yp, p_prompt[l], cs_p, s0_p, None, lbp_all[l], prm, small,
                                alpha=alpha, prompt=True)
        ys, ss, cs_, ks = _layer(ys, p_sample[l], cs_s, state_hgrn[l],
                                 (cache_ckv[l], cache_kpe[l]), lbp_all[l], prm, small,
                                 alpha=alpha, prompt=False)
        for lst, a in zip(res, (sp, cp, kp, ss, cs_, ks)):
            lst.append(a)
    sp, cp, kp, ss, cs_, ks = (jnp.stack(a) for a in res)
    return (yp, ys, sp, cp, kp, ss, cs_, ks)
```

```python
import functools

import jax
import jax.numpy as jnp
from jax import lax
from jax.experimental import pallas as pl
from jax.experimental.pallas import tpu as pltpu

F32 = jnp.float32
BF16 = jnp.bfloat16

D_MODEL = 1024
HA_HEADS = 4
HA_DK = 128
HA_DV = 128
HB_HEADS = 4
Q_LORA = 384
KV_LORA = 256
NOPE_DIM = 128
ROPE_DIM = 64
V_DIM = 128
ROPE_THETA = 10000.0
MLA_SCALE = (NOPE_DIM + ROPE_DIM) ** -0.5
LOG2E = 1.4426950408889634
Q_SCALE = MLA_SCALE * LOG2E
CHUNK = 64
N_EXPERTS = 16
N_GROUPS = 4
EXPERTS_PER_GROUP = N_EXPERTS // N_GROUPS
D_EXPERT = 512
PLE_DIM = 256
NEG_INF = -1e30

HA_W = HA_HEADS * HA_DK
COL_CQ = 4 * HA_W
COL_CKV = COL_CQ + Q_LORA
COL_KPE = COL_CKV + KV_LORA
D_IN_AUG = COL_KPE + 2 * ROPE_DIM
Q_HEAD_AUG = NOPE_DIM + 2 * ROPE_DIM

LANES = 128
SUBLANES = 8
ROW_W = D_MODEL + LANES
GROUP_LANE = N_EXPERTS
GATHER_UNROLL = 8
SCORE_W = 512
SUB_BLOCK = 16
VMEM_LIMIT = 56 * 1024 * 1024


def _cparams(sem, vmem=VMEM_LIMIT):
    return pltpu.CompilerParams(dimension_semantics=sem, vmem_limit_bytes=vmem)


def _dot(a, b):
    return jnp.dot(a, b, preferred_element_type=F32)


def _dot_nt(a, b):
    return lax.dot_general(a, b, (((1,), (1,)), ((), ())), preferred_element_type=F32)


def _rms(x, g, eps=1e-6):
    return x * lax.rsqrt(jnp.mean(x * x, axis=-1, keepdims=True) + eps) * g


def _layer_norm(x, g, b, eps=1e-5):
    mu = jnp.mean(x, axis=-1, keepdims=True)
    xc = x - mu
    var = jnp.mean(xc * xc, axis=-1, keepdims=True)
    return xc * lax.rsqrt(var + eps) * g + b


def _lane_tile(x, n):
    return jnp.concatenate([x] * n, axis=1)


def _silu(x):
    return x * jax.nn.sigmoid(x)


def _full(shape):
    nd = len(shape)
    return pl.BlockSpec(shape, lambda *_: (0,) * nd)


def _inproj_kernel(x_ref, cs_ref, win_ref, lb_ref, qg_ref, wqb_ref, kvg_ref, wuk_ref,
                   qs_ref, kin_ref, logf_ref, v_ref, gs_ref, qabs_ref, qpe_ref,
                   ckv_ref, kpe_ref, *rest, tq, cb, prompt):
    tm = x_ref.shape[0]
    x = x_ref[...].astype(BF16)

    def sect(lo, hi):
        return _dot(x, win_ref[:, lo:hi])

    def put_heads(ref, val):
        for h in range(HA_HEADS):
            ref[h] = val[:, h * HA_DK:(h + 1) * HA_DK]

    put_heads(qs_ref, _silu(sect(0, HA_W)))
    fa = sect(HA_W, 2 * HA_W)
    log_lb = lb_ref[0:1, :]
    log1m_lb = lb_ref[1:2, :]
    one_m_lb = lb_ref[2:3, :]
    log_sig = jnp.minimum(fa, 0.0) - jnp.log1p(jnp.exp(-jnp.abs(fa)))
    c = log1m_lb + log_sig
    hi_ = jnp.maximum(log_lb, c)
    lo_ = jnp.minimum(log_lb, c)
    put_heads(logf_ref, hi_ + jnp.log1p(jnp.exp(lo_ - hi_)))
    put_heads(kin_ref, one_m_lb * jax.nn.sigmoid(-fa))
    v = sect(2 * HA_W, 3 * HA_W)
    put_heads(v_ref, v)
    gs_ref[...] = _silu(sect(3 * HA_W, 4 * HA_W))

    cs = cs_ref[...]

    def rope(t):
        prod = t * cs
        return prod + pltpu.roll(prod, ROPE_DIM, 1)

    ckv = _rms(sect(COL_CKV, COL_KPE), kvg_ref[...])
    ckv_ref[...] = ckv
    kpe2 = rope(sect(COL_KPE, D_IN_AUG))
    kpe_ref[...] = kpe2[:, :ROPE_DIM]
    if prompt:
        vt_ref, ckvb_ref, ckvt_ref, kpet_ref = rest
        for h in range(HA_HEADS):
            for ci in range(tm // cb):
                vt_ref[0, h, ci] = v[ci * cb:(ci + 1) * cb, h * HA_DV:(h + 1) * HA_DV].T.astype(BF16)
        ckvb_ref[...] = ckv.astype(BF16)
        ckvt_ref[0] = ckv.T.astype(BF16)
        kpet_ref[0] = kpe2.T.astype(BF16)

    cqn = _rms(sect(COL_CQ, COL_CKV), qg_ref[...]).astype(BF16)
    lane = lax.broadcasted_iota(jnp.int32, (tm, LANES), 1)
    for h in range(HB_HEADS):
        qh = _dot(cqn, wqb_ref[:, h * Q_HEAD_AUG:(h + 1) * Q_HEAD_AUG])
        qabs = (_dot(qh[:, :NOPE_DIM].astype(BF16), wuk_ref[h]) * Q_SCALE).astype(BF16)
        qpe = rope(qh[:, NOPE_DIM:]) * Q_SCALE
        qpe = jnp.where(lane < ROPE_DIM, qpe, 0.0).astype(BF16)
        for s in range(tm // tq):
            qabs_ref[s, h] = qabs[s * tq:(s + 1) * tq]
            qpe_ref[s, h] = qpe[s * tq:(s + 1) * tq]


def _inproj(x, cs, w_in, lbp, qg, wqb, kvg, wuk, *, tm, tq, cb, prompt):
    n = x.shape[0]
    nt = n // tm
    row = lambda w: pl.BlockSpec((tm, w), lambda i: (i, 0))
    heads = pl.BlockSpec((HA_HEADS, tm, HA_DK), lambda i: (0, i, 0))
    qblk = lambda w: pl.BlockSpec((tm // tq, HB_HEADS, tq, w), lambda i: (i, 0, 0, 0))
    head_major = jax.ShapeDtypeStruct((HA_HEADS, n, HA_DK), F32)
    out_shape = [
        head_major,
        head_major,
        head_major,
        head_major,
        jax.ShapeDtypeStruct((n, HA_W), F32),
        jax.ShapeDtypeStruct((n // tq, HB_HEADS, tq, KV_LORA), BF16),
        jax.ShapeDtypeStruct((n // tq, HB_HEADS, tq, LANES), BF16),
        jax.ShapeDtypeStruct((n, KV_LORA), F32),
        jax.ShapeDtypeStruct((n, ROPE_DIM), F32),
    ]
    out_specs = [heads, heads, heads, heads, row(HA_W), qblk(KV_LORA), qblk(LANES),
                 row(KV_LORA), row(ROPE_DIM)]
    if prompt:
        out_shape += [jax.ShapeDtypeStruct((nt, HA_HEADS, tm // cb, HA_DV, cb), BF16),
                      jax.ShapeDtypeStruct((n, KV_LORA), BF16),
                      jax.ShapeDtypeStruct((nt, KV_LORA, tm), BF16),
                      jax.ShapeDtypeStruct((nt, LANES, tm), BF16)]
        out_specs += [pl.BlockSpec((1, HA_HEADS, tm // cb, HA_DV, cb), lambda i: (i, 0, 0, 0, 0)),
                      row(KV_LORA),
                      pl.BlockSpec((1, KV_LORA, tm), lambda i: (i, 0, 0)),
                      pl.BlockSpec((1, LANES, tm), lambda i: (i, 0, 0))]
    return pl.pallas_call(
        functools.partial(_inproj_kernel, tq=tq, cb=cb, prompt=prompt),
        grid=(nt,),
        in_specs=[row(D_MODEL), row(LANES), _full(w_in.shape), _full(lbp.shape), _full(qg.shape),
                  _full(wqb.shape), _full(kvg.shape), _full(wuk.shape)],
        out_specs=out_specs,
        out_shape=out_shape,
        compiler_params=_cparams(("parallel",)),
        name="inproj",
    )(x, cs, w_in, lbp, qg, wqb, kvg, wuk)


def _hgrn_kernel(q_ref, k_ref, g_ref, v_ref, vt_ref, s0_ref, o_ref, sfin_ref, st_scr, *, cb):
    _, bb, tb, _ = q_ref.shape
    nsub = cb // SUB_BLOCK
    ti = pl.program_id(1)

    @pl.when(ti == 0)
    def _():
        for b in range(bb):
            for h in range(HA_HEADS):
                st_scr[b, h] = s0_ref[b, h].T

    r_i = lax.broadcasted_iota(jnp.int32, (cb, cb), 0)
    c_i = lax.broadcasted_iota(jnp.int32, (cb, cb), 1)
    tril = (r_i >= c_i).astype(F32)
    row8 = lax.broadcasted_iota(jnp.int32, (SUBLANES, 1), 0)

    def chunk_head(ci, b, h):
        r0 = pl.multiple_of(ci * cb, cb)
        rows = pl.ds(r0, cb)
        q = q_ref[h, b, rows, :]
        k = k_ref[h, b, rows, :]
        g = g_ref[h, b, rows, :]
        v = v_ref[h, b, rows, :]
        vt = vt_ref[b, h, ci]
        st = st_scr[b, h]

        bcum = jnp.dot(tril, g, precision=lax.Precision.HIGHEST, preferred_element_type=F32)
        b2 = bcum * LOG2E
        b_last = b2[cb - 1:cb, :]

        o = _dot_nt((q * jnp.exp2(b2)).astype(BF16), st.astype(BF16))
        o_parts = [o[j * SUB_BLOCK:(j + 1) * SUB_BLOCK] for j in range(nsub)]

        for j in range(nsub - 1):
            lo, hi = j * SUB_BLOCK, (j + 1) * SUB_BLOCK
            bnd = b2[hi - 1:hi, :]
            kd = (k[lo:hi] * jnp.exp2(bnd - b2[lo:hi])).astype(BF16)
            qe = (q[hi:] * jnp.exp2(b2[hi:] - bnd)).astype(BF16)
            a = _dot_nt(qe, kd).astype(BF16)
            upd = _dot(a, v[lo:hi].astype(BF16))
            for jj in range(j + 1, nsub):
                o_parts[jj] = o_parts[jj] + upd[(jj - j - 1) * SUB_BLOCK:(jj - j) * SUB_BLOCK]

        for j in range(nsub):
            lo, hi = j * SUB_BLOCK, (j + 1) * SUB_BLOCK
            mid = lo + SUBLANES
            kj, bj, vj = k[lo:hi], b2[lo:hi], v[lo:hi]
            q_top, q_bot = q[lo:mid], q[mid:hi]
            b_top, b_bot = b2[lo:mid], b2[mid:hi]
            acc_top, acc_bot = o_parts[j][:SUBLANES], o_parts[j][SUBLANES:]
            for s in range(SUB_BLOCK):
                bs, ks, vs = bj[s:s + 1, :], kj[s:s + 1, :], vj[s:s + 1, :]
                if s < SUBLANES:
                    w = q_top * jnp.exp2(jnp.where(row8 >= s, b_top - bs, NEG_INF)) * ks
                    acc_top = acc_top + jnp.sum(w, axis=-1, keepdims=True) * vs
                    w = q_bot * jnp.exp2(b_bot - bs) * ks
                else:
                    w = q_bot * jnp.exp2(jnp.where(row8 >= s - SUBLANES, b_bot - bs, NEG_INF)) * ks
                acc_bot = acc_bot + jnp.sum(w, axis=-1, keepdims=True) * vs
            row_lo = pl.multiple_of(r0 + lo, SUB_BLOCK)
            o_ref[h, b, pl.ds(row_lo, SUBLANES), :] = acc_top
            o_ref[h, b, pl.ds(row_lo + SUBLANES, SUBLANES), :] = acc_bot

        kd_all = (k * jnp.exp2(b_last - b2)).astype(BF16)
        st_scr[b, h] = st * jnp.exp2(b_last) + _dot(vt, kd_all)

    def chunk(ci, carry):
        for b in range(bb):
            for h in range(HA_HEADS):
                chunk_head(ci, b, h)
        return carry

    lax.fori_loop(0, tb // cb, chunk, 0)

    @pl.when(ti == pl.num_programs(1) - 1)
    def _():
        for b in range(bb):
            for h in range(HA_HEADS):
                sfin_ref[b, h] = st_scr[b, h].T


def _hgrn(qs, kin, logf, v, vt, s0, *, bb, tb, cb):
    _, bsz, t, _ = qs.shape
    blk = pl.BlockSpec((HA_HEADS, bb, tb, HA_DK), lambda i, j: (0, i, j, 0))
    st_blk = pl.BlockSpec((bb, HA_HEADS, HA_DK, HA_DV), lambda i, j: (i, 0, 0, 0))
    vt_blk = pl.BlockSpec((bb, None, HA_HEADS, tb // cb, HA_DV, cb),
                          lambda i, j: (i, j, 0, 0, 0, 0))
    return pl.pallas_call(
        functools.partial(_hgrn_kernel, cb=cb),
        grid=(bsz // bb, t // tb),
        in_specs=[blk, blk, blk, blk, vt_blk, st_blk],
        out_specs=[blk, st_blk],
        out_shape=[jax.ShapeDtypeStruct(qs.shape, F32),
                   jax.ShapeDtypeStruct(s0.shape, F32)],
        scratch_shapes=[pltpu.VMEM((bb, HA_HEADS, HA_DV, HA_DK), F32)],
        compiler_params=_cparams(("parallel", "arbitrary")),
        name="hgrn",
    )(qs, kin, logf, v, vt, s0)


def _attn_prompt_kernel(qa_ref, qp_ref, kt_ref, pt_ref, kv_ref, o_ref, m_scr, l_scr, acc_scr,
                        *, tq, tk):
    qi = pl.program_id(1)
    m_scr[...] = jnp.full(m_scr.shape, NEG_INF, F32)
    l_scr[...] = jnp.zeros(l_scr.shape, F32)
    acc_scr[...] = jnp.zeros(acc_scr.shape, F32)

    def chain(j, c, masked):
        k0 = j * tk + c * SCORE_W
        kt = kt_ref[j, :, c * SCORE_W:(c + 1) * SCORE_W]
        pt = pt_ref[j, :, c * SCORE_W:(c + 1) * SCORE_W]
        kv = kv_ref[pl.ds(pl.multiple_of(k0, SCORE_W), SCORE_W), :]
        if masked:
            shift = CHUNK.bit_length() - 1
            qpos = qi * tq + lax.broadcasted_iota(jnp.int32, (tq, 1), 0)
            kpos = k0 + lax.broadcasted_iota(jnp.int32, (1, SCORE_W), 1)
            visible = (kpos >> shift) <= (qpos >> shift)
        for h in range(HB_HEADS):
            s = _dot(qa_ref[0, h], kt) + _dot(qp_ref[0, h], pt)
            if masked:
                s = jnp.where(visible, s, NEG_INF)
            m_prev = m_scr[h]
            m_new = jnp.maximum(m_prev, jnp.max(s, axis=-1, keepdims=True))
            alpha = jnp.exp2(m_prev - m_new)
            p = jnp.exp2(s - _lane_tile(m_new, SCORE_W // LANES))
            l_scr[h] = alpha * l_scr[h] + jnp.sum(p, axis=-1, keepdims=True)
            acc_scr[h] = (_lane_tile(alpha, KV_LORA // LANES) * acc_scr[h]
                          + _dot(p.astype(BF16), kv))
            m_scr[h] = m_new

    jd = (qi * tq) // tk

    def body(j, carry):
        for c in range(tk // SCORE_W):
            chain(j, c, False)
        return carry

    lax.fori_loop(0, jd, body, 0)
    for c in range(tk // SCORE_W):
        pl.when(jd * tk + c * SCORE_W < (qi + 1) * tq)(functools.partial(chain, jd, c, True))
    for h in range(HB_HEADS):
        inv_l = _lane_tile(1.0 / l_scr[h], KV_LORA // LANES)
        o_ref[0, h] = (acc_scr[h] * inv_l).astype(BF16)


def _attn_prompt(qabs, qpe, ckvt, kpet, ckvb, *, bsz, t, tq, tk):
    nq = t // tq
    nk = t // tk
    rows = HB_HEADS * tq
    single = pl.Buffered(1)
    return pl.pallas_call(
        functools.partial(_attn_prompt_kernel, tq=tq, tk=tk),
        grid=(bsz, nq),
        in_specs=[
            pl.BlockSpec((1, HB_HEADS, tq, KV_LORA), lambda b, i: (b * nq + i, 0, 0, 0)),
            pl.BlockSpec((1, HB_HEADS, tq, LANES), lambda b, i: (b * nq + i, 0, 0, 0)),
            pl.BlockSpec((nk, KV_LORA, tk), lambda b, i: (b, 0, 0), pipeline_mode=single),
            pl.BlockSpec((nk, LANES, tk), lambda b, i: (b, 0, 0), pipeline_mode=single),
            pl.BlockSpec((t, KV_LORA), lambda b, i: (b, 0), pipeline_mode=single),
        ],
        out_specs=pl.BlockSpec((1, HB_HEADS, tq, KV_LORA), lambda b, i: (b * nq + i, 0, 0, 0)),
        out_shape=jax.ShapeDtypeStruct(qabs.shape, BF16),
        scratch_shapes=[pltpu.VMEM((HB_HEADS, tq, LANES), F32), pltpu.VMEM((HB_HEADS, tq, LANES), F32),
                        pltpu.VMEM((HB_HEADS, tq, KV_LORA), F32)],
        compiler_params=_cparams(("parallel", "arbitrary")),
        name="attn_prompt",
    )(qabs, qpe, ckvt, kpet, ckvb)


def _attn_sample_kernel(qa_ref, qp_ref, pckv_ref, pkpe_ref, nckv_ref, nkpe_ref, o_ref):
    _, _, ts, _ = qa_ref.shape
    rows = HB_HEADS * ts
    qa = qa_ref[0].reshape(rows, KV_LORA)
    qp = qp_ref[0].reshape(rows, LANES)[:, :ROPE_DIM]
    pckv = pckv_ref[0].astype(BF16)
    nckv = nckv_ref[...].astype(BF16)
    s_past = _dot_nt(qa, pckv) + _dot_nt(qp, pkpe_ref[0].astype(BF16))
    s_new = _dot_nt(qa, nckv) + _dot_nt(qp, nkpe_ref[...].astype(BF16))
    m = jnp.maximum(jnp.max(s_past, axis=-1, keepdims=True), jnp.max(s_new, axis=-1, keepdims=True))
    p_past = jnp.exp2(s_past - m)
    p_new = jnp.exp2(s_new - m)
    l = jnp.sum(p_past, axis=-1, keepdims=True) + jnp.sum(p_new, axis=-1, keepdims=True)
    o = (_dot(p_past.astype(BF16), pckv) + _dot(p_new.astype(BF16), nckv)) / l
    o_ref[0] = o.astype(BF16).reshape(HB_HEADS, ts, KV_LORA)


def _attn_sample(qabs, qpe, past_ckv, past_kpe, ckv, kpe, *, ts):
    bsz, past, _ = past_ckv.shape
    return pl.pallas_call(
        _attn_sample_kernel,
        grid=(bsz,),
        in_specs=[
            pl.BlockSpec((1, HB_HEADS, ts, KV_LORA), lambda b: (b, 0, 0, 0)),
            pl.BlockSpec((1, HB_HEADS, ts, LANES), lambda b: (b, 0, 0, 0)),
            pl.BlockSpec((1, past, KV_LORA), lambda b: (b, 0, 0)),
            pl.BlockSpec((1, past, ROPE_DIM), lambda b: (b, 0, 0)),
            pl.BlockSpec((ts, KV_LORA), lambda b: (b, 0)),
            pl.BlockSpec((ts, ROPE_DIM), lambda b: (b, 0)),
        ],
        out_specs=pl.BlockSpec((1, HB_HEADS, ts, KV_LORA), lambda b: (b, 0, 0, 0)),
        out_shape=jax.ShapeDtypeStruct(qabs.shape, BF16),
        compiler_params=_cparams(("parallel",)),
        name="attn_sample",
    )(qabs, qpe, past_ckv, past_kpe, ckv, kpe)


def _gates(logits, bias):
    tm = logits.shape[0]
    lane = lax.broadcasted_iota(jnp.int32, (tm, LANES), 1)
    pos = lane % EXPERTS_PER_GROUP
    valid = lane < N_EXPERTS
    scores = jax.nn.sigmoid(logits)
    sel = jnp.where(valid, scores + bias, -jnp.inf)

    others = []
    for r in range(1, EXPERTS_PER_GROUP):
        others.append(jnp.where(pos >= r, pltpu.roll(sel, r, 1),
                                pltpu.roll(sel, LANES - (EXPERTS_PER_GROUP - r), 1)))
    a, b, c, d = sel, others[0], others[1], others[2]
    hi1, lo1 = jnp.maximum(a, b), jnp.minimum(a, b)
    hi2, lo2 = jnp.maximum(c, d), jnp.minimum(c, d)
    gscore = jnp.maximum(hi1, hi2) + jnp.maximum(jnp.minimum(hi1, hi2), jnp.maximum(lo1, lo2))
    gmax = jnp.max(gscore, axis=-1, keepdims=True)
    group = (lane // EXPERTS_PER_GROUP).astype(F32)
    gidx = jnp.min(jnp.where(gscore == gmax, group, float(LANES)), axis=-1, keepdims=True)

    rank = jnp.zeros((tm, LANES), jnp.int32)
    for r, o in enumerate(others, start=1):
        ahead = (o > sel) | ((o == sel) & (pos >= r))
        rank = rank + ahead.astype(jnp.int32)
    chosen = (group == gidx) & (rank < 2) & valid
    w = jnp.where(chosen, scores, 0.0)
    w = w / jnp.sum(w, axis=-1, keepdims=True)
    return jnp.where(lane == GROUP_LANE, gidx, w)


def _merge_kernel(x_ref, o_ref, gs_ref, olat_ref, wuv_ref, wo_ref, hg_ref, mg_ref,
                  g1_ref, b1_ref, wr_ref, rb_ref, x1g_ref, *, alpha):
    nsb, _, tq, _ = olat_ref.shape
    gs = gs_ref[...]
    hg = hg_ref[...]
    oa = jnp.concatenate(
        [_rms(o_ref[h], hg) * gs[:, h * HA_DV:(h + 1) * HA_DV] for h in range(HA_HEADS)],
        axis=-1)
    ob = []
    for h in range(HB_HEADS):
        lat = jnp.concatenate([olat_ref[s, h] for s in range(nsb)], axis=0)
        ob.append(_dot(lat, wuv_ref[h]))
    ob = _rms(jnp.concatenate(ob, axis=-1), mg_ref[...])
    d_a = HA_HEADS * HA_DV
    mix = _dot(oa.astype(BF16), wo_ref[:d_a, :]) + _dot(ob.astype(BF16), wo_ref[d_a:, :])
    x1 = _layer_norm(alpha * x_ref[...] + mix, g1_ref[...], b1_ref[...])
    x1g_ref[:, :D_MODEL] = x1
    x1g_ref[:, D_MODEL:] = _gates(_dot(x1.astype(BF16), wr_ref[...]), rb_ref[...])


def _merge(x, o_raw, gs, olat, wuv, wo, hg, mg, g1, b1, wr, rb, *, tm, tq, alpha):
    n = x.shape[0]
    row = lambda w: pl.BlockSpec((tm, w), lambda i: (i, 0))
    return pl.pallas_call(
        functools.partial(_merge_kernel, alpha=alpha),
        grid=(n // tm,),
        in_specs=[row(D_MODEL), pl.BlockSpec((HA_HEADS, tm, HA_DV), lambda i: (0, i, 0)), row(HA_W),
                  pl.BlockSpec((tm // tq, HB_HEADS, tq, KV_LORA), lambda i: (i, 0, 0, 0)),
                  _full(wuv.shape), _full(wo.shape), _full(hg.shape), _full(mg.shape),
                  _full(g1.shape), _full(b1.shape), _full(wr.shape), _full(rb.shape)],
        out_specs=row(ROW_W),
        out_shape=jax.ShapeDtypeStruct((n, ROW_W), F32),
        compiler_params=_cparams(("parallel",)),
        name="merge",
    )(x, o_raw, gs, olat, wuv, wo, hg, mg, g1, b1, wr, rb)


def _gather_kernel(idx_ref, src_ref, out_ref, sem):
    rows = out_ref.shape[0]

    def issue(i, carry):
        for u in range(GATHER_UNROLL):
            r = i * GATHER_UNROLL + u
            pltpu.make_async_copy(src_ref.at[pl.ds(idx_ref[0, 0, r], 1), :],
                                  out_ref.at[pl.ds(r, 1), :], sem).start(priority=u % 2)
        return carry

    lax.fori_loop(0, rows // GATHER_UNROLL, issue, 0)
    pltpu.make_async_copy(src_ref.at[pl.ds(0, rows), :], out_ref, sem).wait()


def _gather_rows(src, idx, *, rows):
    m = idx.shape[0]
    w = src.shape[1]
    return pl.pallas_call(
        _gather_kernel,
        grid=(m // rows,),
        in_specs=[pl.BlockSpec((1, 1, rows), lambda i: (i, 0, 0), memory_space=pltpu.SMEM),
                  pl.BlockSpec(memory_space=pl.ANY)],
        out_specs=pl.BlockSpec((rows, w), lambda i: (i, 0)),
        out_shape=jax.ShapeDtypeStruct((m, w), F32),
        scratch_shapes=[pltpu.SemaphoreType.DMA(())],
        compiler_params=_cparams(("arbitrary",)),
        name="gather_rows",
    )(idx.reshape(m // rows, 1, rows), src)


def _moe_kernel(tg_ref, xs_ref, wg_ref, wu_ref, wd_ref, y_ref):
    tm = xs_ref.shape[0]
    g = tg_ref[pl.program_id(0)]
    xb = xs_ref[:, :D_MODEL].astype(BF16)
    gates = xs_ref[:, D_MODEL:]
    lane = lax.broadcasted_iota(jnp.int32, (tm, LANES), 1)
    acc = jnp.zeros((tm, D_MODEL), F32)
    for k in range(EXPERTS_PER_GROUP):
        hmid = _silu(_dot(xb, wg_ref[k])) * _dot(xb, wu_ref[k])
        gcol = jnp.sum(jnp.where(lane == g * EXPERTS_PER_GROUP + k, gates, 0.0),
                       axis=-1, keepdims=True)
        acc = acc + gcol * _dot(hmid.astype(BF16), wd_ref[k])
    y_ref[...] = acc


def _moe_sorted(tile_group, xs, wg, wu, wd, *, tm):
    m = xs.shape[0]
    wspec = lambda shape: pl.BlockSpec((EXPERTS_PER_GROUP,) + shape, lambda i, tg: (tg[i], 0, 0))
    return pl.pallas_call(
        _moe_kernel,
        grid_spec=pltpu.PrefetchScalarGridSpec(
            num_scalar_prefetch=1,
            grid=(m // tm,),
            in_specs=[pl.BlockSpec((tm, ROW_W), lambda i, tg: (i, 0)),
                      wspec((D_MODEL, D_EXPERT)), wspec((D_MODEL, D_EXPERT)),
                      wspec((D_EXPERT, D_MODEL))],
            out_specs=pl.BlockSpec((tm, D_MODEL), lambda i, tg: (i, 0)),
        ),
        out_shape=jax.ShapeDtypeStruct((m, D_MODEL), F32),
        compiler_params=_cparams(("arbitrary",)),
        name="moe",
    )(tile_group, xs, wg, wu, wd)


def _route(x1g, *, tm):
    n = x1g.shape[0]
    n_tiles = n // tm + N_GROUPS - 1
    gidx = x1g[:, D_MODEL + GROUP_LANE].astype(jnp.int32)
    onehot = (gidx[:, None] == jnp.arange(N_GROUPS, dtype=jnp.int32)[None, :]).astype(jnp.int32)
    counts = jnp.sum(onehot, axis=0)
    tiles = (counts + tm - 1) // tm
    tile_end = jnp.cumsum(tiles)
    tile_start = tile_end - tiles
    rank = jnp.sum((jnp.cumsum(onehot, axis=0) - onehot) * onehot, axis=1)
    pos = jnp.take(tile_start, gidx) * tm + rank
    src = jnp.zeros((n_tiles * tm,), jnp.int32).at[pos].set(jnp.arange(n, dtype=jnp.int32))
    tile_group = jnp.searchsorted(tile_end, jnp.arange(n_tiles, dtype=jnp.int32), side="right")
    return src, pos, jnp.minimum(tile_group, N_GROUPS - 1).astype(jnp.int32)


def _post_kernel(x1_ref, f_ref, p_ref, g2_ref, b2_ref, wpg_ref, wp_ref, y_ref, *, alpha):
    x2 = _layer_norm(alpha * x1_ref[...] + f_ref[...], g2_ref[...], b2_ref[...])
    gate = jax.nn.sigmoid(_dot(x2.astype(BF16), wpg_ref[...]))
    y_ref[...] = x2 + gate * _dot(p_ref[...].astype(BF16), wp_ref[...])


def _post(x1g, ffn, p, g2, b2, wpg, wp, *, tm, alpha):
    n = ffn.shape[0]
    row = lambda w: pl.BlockSpec((tm, w), lambda i: (i, 0))
    return pl.pallas_call(
        functools.partial(_post_kernel, alpha=alpha),
        grid=(n // tm,),
        in_specs=[row(D_MODEL), row(D_MODEL), row(PLE_DIM), _full(g2.shape), _full(b2.shape),
                  _full(wpg.shape), _full(wp.shape)],
        out_specs=row(D_MODEL),
        out_shape=jax.ShapeDtypeStruct((n, D_MODEL), F32),
        compiler_params=_cparams(("parallel",)),
        name="post",
    )(x1g, ffn, p, g2, b2, wpg, wp)


def _rot_cols(w):
    half = ROPE_DIM // 2
    return jnp.concatenate([-w[..., half:], w[..., :half]], axis=-1)


def _prep_layer(l, w_in, w_qb, w_kvb, w_o, w_gate, w_up, w_down, w_ple, w_ple_gate):
    win = w_in[l]
    win_aug = jnp.concatenate([win, _rot_cols(win[:, COL_KPE:])], axis=-1).astype(BF16)
    wqb = w_qb[l].reshape(Q_LORA, HB_HEADS, NOPE_DIM + ROPE_DIM)
    wqb_aug = jnp.concatenate([wqb, _rot_cols(wqb[..., NOPE_DIM:])], axis=-1)
    wqb_aug = wqb_aug.reshape(Q_LORA, HB_HEADS * Q_HEAD_AUG).astype(BF16)
    wkvb = w_kvb[l].reshape(KV_LORA, HB_HEADS, NOPE_DIM + V_DIM)
    wuk_t = jnp.transpose(wkvb[..., :NOPE_DIM], (1, 2, 0)).astype(BF16)
    wuv = jnp.transpose(wkvb[..., NOPE_DIM:], (1, 0, 2)).astype(BF16)
    return dict(win=win_aug, wqb=wqb_aug, wuk=wuk_t, wuv=wuv, wo=w_o[l].astype(BF16),
                wg=w_gate[l].astype(BF16), wu=w_up[l].astype(BF16), wd=w_down[l].astype(BF16),
                wp=w_ple[l].astype(BF16), wpg=w_ple_gate[l].astype(BF16))


def _rope_table(pos):
    inv = ROPE_THETA ** (-jnp.arange(0, ROPE_DIM, 2, dtype=F32) / ROPE_DIM)
    ang = pos.astype(F32)[:, None] * inv[None, :]
    cos, sin = jnp.cos(ang), jnp.sin(ang)
    return jnp.concatenate([cos, cos, sin, sin], axis=-1)


def _tiles(n, t, prompt):
    if prompt:
        tm = min(512, t)
        return dict(tm=tm, tq=min(256, t), tk=tm, tb=tm, cb=min(128, t), bb=2, tmoe=min(512, n))
    return dict(tm=min(512, n), tq=t, tk=None, tb=t, cb=t, bb=2, tmoe=min(256, n))


def _layer(x, p, cs, s0, past, lbp, prm, small, *, alpha, prompt):
    bsz, t, _ = x.shape
    n = bsz * t
    tl = _tiles(n, t, prompt)
    x2d = x.reshape(n, D_MODEL)
    tm, tb, cb = tl["tm"], tl["tb"], tl["cb"]
    outs = _inproj(x2d, cs, prm["win"], lbp, small["qg"], prm["wqb"], small["kvg"], prm["wuk"],
                   tm=tm, tq=tl["tq"], cb=cb, prompt=prompt)
    qs, kin, logf, v, gs, qabs, qpe, ckv, kpe = outs[:9]
    r4 = lambda a: a.reshape(HA_HEADS, bsz, t, HA_DK)
    if prompt:
        vt, ckvb, ckvt, kpet = outs[9:]
        vt = vt.reshape(bsz, t // tb, HA_HEADS, tb // cb, HA_DV, cb)
    else:
        vt = jnp.transpose(r4(v), (1, 0, 3, 2)).astype(BF16)
        vt = vt.reshape(bsz, 1, HA_HEADS, 1, HA_DV, t)
    o_raw, s_new = _hgrn(r4(qs), r4(kin), r4(logf), r4(v), vt, s0, bb=tl["bb"], tb=tb, cb=cb)
    if prompt:
        olat = _attn_prompt(qabs, qpe, ckvt, kpet, ckvb, bsz=bsz, t=t, tq=tl["tq"], tk=tl["tk"])
    else:
        olat = _attn_sample(qabs, qpe, past[0], past[1], ckv, kpe, ts=t)
    x1g = _merge(x2d, o_raw.reshape(HA_HEADS, n, HA_DV), gs, olat, prm["wuv"], prm["wo"],
                 small["hg"], small["mg"], small["g1"], small["b1"],
                 small["wr"], small["rb"], tm=tm, tq=tl["tq"], alpha=alpha)
    tmoe = tl["tmoe"]
    src, pos, tile_group = _route(x1g, tm=tmoe)
    xs = _gather_rows(x1g, src, rows=tmoe)
    ys = _moe_sorted(tile_group, xs, prm["wg"], prm["wu"], prm["wd"], tm=tmoe)
    ffn = _gather_rows(ys, pos, rows=tmoe)
    y = _post(x1g, ffn, p.reshape(n, PLE_DIM), small["g2"], small["b2"], prm["wpg"], prm["wp"],
              tm=tm, alpha=alpha)
    return (y.reshape(bsz, t, D_MODEL), s_new, ckv.reshape(bsz, t, KV_LORA),
            kpe.reshape(bsz, t, ROPE_DIM))


def kernel(x_prompt, x_sample, p_prompt, p_sample, state_hgrn, cache_ckv, cache_kpe, w_in,
           lb_logits, hgrn_norm_g, q_norm_g, w_qb, kv_norm_g, w_kvb, mla_norm_g, w_o,
           ln1_g, ln1_b, ln2_g, ln2_b, w_router, router_bias, w_gate, w_up, w_down,
           w_ple, w_ple_gate):
    depth = w_in.shape[0]
    alpha = (2 * depth) ** 0.25
    bp, tp, _ = x_prompt.shape
    bs, ts, _ = x_sample.shape
    past = cache_ckv.shape[2]

    sm = jax.nn.softmax(lb_logits.astype(F32), axis=0)
    lb_all = jnp.maximum(jnp.cumsum(sm, axis=0) - sm[0:1], 0.0)
    lbp_all = jnp.stack([jnp.log(lb_all), jnp.log1p(-lb_all), 1.0 - lb_all], axis=1)

    cs_p = jnp.tile(_rope_table(jnp.arange(tp)), (bp, 1))
    cs_s = jnp.tile(_rope_table(past + jnp.arange(ts)), (bs, 1))
    wr = jnp.pad(w_router, ((0, 0), (0, LANES - N_EXPERTS))).astype(BF16)
    rb = jnp.pad(router_bias.astype(F32), (0, LANES - N_EXPERTS)).reshape(1, LANES)
    s0_p = jnp.zeros((bp, HA_HEADS, HA_DK, HA_DV), F32)

    yp, ys = x_prompt, x_sample
    res = [[] for _ in range(6)]
    for l in range(depth):
        prm = _prep_layer(l, w_in, w_qb, w_kvb, w_o, w_gate, w_up, w_down, w_ple, w_ple_gate)
        row = lambda a: a[l].reshape(1, -1).astype(F32)
        small = dict(qg=row(q_norm_g), kvg=row(kv_norm_g), hg=row(hgrn_norm_g), mg=row(mla_norm_g),
                     g1=row(ln1_g), b1=row(ln1_b), g2=row(ln2_g), b2=row(ln2_b), wr=wr, rb=rb)
        yp, sp, cp, kp = _layer(yp, p_prompt[l], cs_p, s0_p, None, lbp_all[l], prm, small,
                                alpha=alpha, prompt=True)
        ys, ss, cs_, ks = _layer(ys, p_sample[l], cs_s, state_hgrn[l],
                                 (cache_ckv[l], cache_kpe[l]), lbp_all[l], prm, small,
                                 alpha=alpha, prompt=False)
        for lst, a in zip(res, (sp, cp, kp, ss, cs_, ks)):
            lst.append(a)
    sp, cp, kp, ss, cs_, ks = (jnp.stack(a) for a in res)
    return (yp, ys, sp, cp, kp, ss, cs_, ks)
```

```python
import functools

import jax
import jax.numpy as jnp
import numpy as np
from jax import lax
from jax.experimental import pallas as pl
from jax.experimental.pallas import tpu as pltpu

F32 = jnp.float32
BF16 = jnp.bfloat16

D_MODEL = 1024
HA_HEADS = 4
HA_DK = 128
HA_DV = 128
HB_HEADS = 4
Q_LORA = 384
KV_LORA = 256
NOPE_DIM = 128
ROPE_DIM = 64
V_DIM = 128
ROPE_THETA = 10000.0
MLA_SCALE = (NOPE_DIM + ROPE_DIM) ** -0.5
LOG2E = 1.4426950408889634
Q_SCALE = MLA_SCALE * LOG2E
CHUNK = 64
N_EXPERTS = 16
N_GROUPS = 4
EXPERTS_PER_GROUP = N_EXPERTS // N_GROUPS
D_EXPERT = 512
PLE_DIM = 256
NEG_INF = -1e30

HA_W = HA_HEADS * HA_DK
COL_CQ = 4 * HA_W
COL_CKV = COL_CQ + Q_LORA
COL_KPE = COL_CKV + KV_LORA
D_IN_AUG = COL_KPE + 2 * ROPE_DIM
Q_HEAD_AUG = NOPE_DIM + 2 * ROPE_DIM

LANES = 128
SUBLANES = 8
QK_DIM = KV_LORA + LANES
ROW_W = D_MODEL + LANES
GROUP_LANE = N_EXPERTS
GATHER_UNROLL = 8
SCORE_W = 512
SUB_BLOCK = 16
VMEM_LIMIT = 56 * 1024 * 1024


def _cparams(sem, vmem=VMEM_LIMIT):
    return pltpu.CompilerParams(dimension_semantics=sem, vmem_limit_bytes=vmem)


def _dot(a, b):
    return jnp.dot(a, b, preferred_element_type=F32)


def _dot_nt(a, b):
    return lax.dot_general(a, b, (((1,), (1,)), ((), ())), preferred_element_type=F32)


def _rms(x, g, eps=1e-6):
    return x * lax.rsqrt(jnp.mean(x * x, axis=-1, keepdims=True) + eps) * g


def _layer_norm(x, g, b, eps=1e-5):
    mu = jnp.mean(x, axis=-1, keepdims=True)
    xc = x - mu
    var = jnp.mean(xc * xc, axis=-1, keepdims=True)
    return xc * lax.rsqrt(var + eps) * g + b


def _lane_tile(x, n):
    return jnp.concatenate([x] * n, axis=1)


def _silu(x):
    return x * jax.nn.sigmoid(x)


def _full(shape):
    nd = len(shape)
    return pl.BlockSpec(shape, lambda *_: (0,) * nd)


def _inproj_kernel(x_ref, cs_ref, win_ref, lb_ref, qg_ref, wqb_ref, kvg_ref, wuk_ref,
                   qs_ref, kin_ref, logf_ref, v_ref, gs_ref, ckv_ref, kpe_ref, *rest,
                   tq, cb, prompt):
    tm = x_ref.shape[0]
    x = x_ref[...].astype(BF16)

    def sect(lo, hi):
        return _dot(x, win_ref[:, lo:hi])

    def put_heads(ref, val):
        for h in range(HA_HEADS):
            ref[h] = val[:, h * HA_DK:(h + 1) * HA_DK]

    put_heads(qs_ref, _silu(sect(0, HA_W)))
    fa = sect(HA_W, 2 * HA_W)
    log_lb = lb_ref[0:1, :]
    log1m_lb = lb_ref[1:2, :]
    one_m_lb = lb_ref[2:3, :]
    log_sig = jnp.minimum(fa, 0.0) - jnp.log1p(jnp.exp(-jnp.abs(fa)))
    c = log1m_lb + log_sig
    hi_ = jnp.maximum(log_lb, c)
    lo_ = jnp.minimum(log_lb, c)
    put_heads(logf_ref, hi_ + jnp.log1p(jnp.exp(lo_ - hi_)))
    put_heads(kin_ref, one_m_lb * jax.nn.sigmoid(-fa))
    v = sect(2 * HA_W, 3 * HA_W)
    put_heads(v_ref, v)
    gs_ref[...] = _silu(sect(3 * HA_W, 4 * HA_W))

    cs = cs_ref[...]

    def rope(t):
        prod = t * cs
        return prod + pltpu.roll(prod, ROPE_DIM, 1)

    ckv = _rms(sect(COL_CKV, COL_KPE), kvg_ref[...])
    ckv_ref[...] = ckv
    kpe2 = rope(sect(COL_KPE, D_IN_AUG))
    kpe_ref[...] = kpe2[:, :ROPE_DIM]
    if prompt:
        qt_ref, vt_ref, kcat_ref, ckvt_ref = rest
        for h in range(HA_HEADS):
            for ci in range(tm // cb):
                vt_ref[0, h, ci] = v[ci * cb:(ci + 1) * cb, h * HA_DV:(h + 1) * HA_DV].T.astype(BF16)
        kcat_ref[:, :KV_LORA] = ckv.astype(BF16)
        kcat_ref[:, KV_LORA:] = kpe2.astype(BF16)
        ckvt_ref[0] = ckv.T.astype(BF16)
    else:
        qabs_ref, qpe_ref = rest

    cqn = _rms(sect(COL_CQ, COL_CKV), qg_ref[...]).astype(BF16)
    lane = lax.broadcasted_iota(jnp.int32, (tm, LANES), 1)
    for h in range(HB_HEADS):
        qh = _dot(cqn, wqb_ref[:, h * Q_HEAD_AUG:(h + 1) * Q_HEAD_AUG])
        qabs = _dot(qh[:, :NOPE_DIM].astype(BF16), wuk_ref[h]) * Q_SCALE
        qpe = jnp.where(lane < ROPE_DIM, rope(qh[:, NOPE_DIM:]) * Q_SCALE, 0.0)
        if prompt:
            qabs_t = qabs.T.astype(BF16)
            qpe_t = qpe.T.astype(BF16)
            for s in range(tm // tq):
                qt_ref[s, :KV_LORA, h * tq:(h + 1) * tq] = qabs_t[:, s * tq:(s + 1) * tq]
                qt_ref[s, KV_LORA:, h * tq:(h + 1) * tq] = qpe_t[:, s * tq:(s + 1) * tq]
        else:
            for s in range(tm // tq):
                qabs_ref[s, h] = qabs[s * tq:(s + 1) * tq].astype(BF16)
                qpe_ref[s, h] = qpe[s * tq:(s + 1) * tq].astype(BF16)


def _inproj(x, cs, w_in, lbp, qg, wqb, kvg, wuk, *, tm, tq, cb, prompt):
    n = x.shape[0]
    nt = n // tm
    row = lambda w: pl.BlockSpec((tm, w), lambda i: (i, 0))
    heads = pl.BlockSpec((HA_HEADS, tm, HA_DK), lambda i: (0, i, 0))
    qblk = lambda w: pl.BlockSpec((tm // tq, HB_HEADS, tq, w), lambda i: (i, 0, 0, 0))
    head_major = jax.ShapeDtypeStruct((HA_HEADS, n, HA_DK), F32)
    out_shape = [
        head_major,
        head_major,
        head_major,
        head_major,
        jax.ShapeDtypeStruct((n, HA_W), F32),
        jax.ShapeDtypeStruct((n, KV_LORA), F32),
        jax.ShapeDtypeStruct((n, ROPE_DIM), F32),
    ]
    out_specs = [heads, heads, heads, heads, row(HA_W), row(KV_LORA), row(ROPE_DIM)]
    if prompt:
        out_shape += [jax.ShapeDtypeStruct((n // tq, QK_DIM, HB_HEADS * tq), BF16),
                      jax.ShapeDtypeStruct((nt, HA_HEADS, tm // cb, HA_DV, cb), BF16),
                      jax.ShapeDtypeStruct((n, QK_DIM), BF16),
                      jax.ShapeDtypeStruct((nt, KV_LORA, tm), BF16)]
        out_specs += [pl.BlockSpec((tm // tq, QK_DIM, HB_HEADS * tq), lambda i: (i, 0, 0)),
                      pl.BlockSpec((1, HA_HEADS, tm // cb, HA_DV, cb), lambda i: (i, 0, 0, 0, 0)),
                      row(QK_DIM),
                      pl.BlockSpec((1, KV_LORA, tm), lambda i: (i, 0, 0))]
    else:
        out_shape += [jax.ShapeDtypeStruct((n // tq, HB_HEADS, tq, KV_LORA), BF16),
                      jax.ShapeDtypeStruct((n // tq, HB_HEADS, tq, LANES), BF16)]
        out_specs += [qblk(KV_LORA), qblk(LANES)]
    return pl.pallas_call(
        functools.partial(_inproj_kernel, tq=tq, cb=cb, prompt=prompt),
        grid=(nt,),
        in_specs=[row(D_MODEL),
                  pl.BlockSpec((tm, LANES), lambda i: (i % (cs.shape[0] // tm), 0)),
                  _full(w_in.shape), _full(lbp.shape), _full(qg.shape),
                  _full(wqb.shape), _full(kvg.shape), _full(wuk.shape)],
        out_specs=out_specs,
        out_shape=out_shape,
        compiler_params=_cparams(("parallel",)),
        name="inproj",
    )(x, cs, w_in, lbp, qg, wqb, kvg, wuk)


def _hgrn_kernel(q_ref, k_ref, g_ref, v_ref, vt_ref, s0_ref, o_ref, sfin_ref, st_scr, *, cb):
    _, bb, tb, _ = q_ref.shape
    nsub = cb // SUB_BLOCK
    ti = pl.program_id(1)

    @pl.when(ti == 0)
    def _():
        for b in range(bb):
            for h in range(HA_HEADS):
                st_scr[b, h] = s0_ref[b, h].T

    r_i = lax.broadcasted_iota(jnp.int32, (cb, cb), 0)
    c_i = lax.broadcasted_iota(jnp.int32, (cb, cb), 1)
    tril = (r_i >= c_i).astype(F32)
    row8 = lax.broadcasted_iota(jnp.int32, (SUBLANES, 1), 0)

    def chunk_head(ci, b, h):
        r0 = pl.multiple_of(ci * cb, cb)
        rows = pl.ds(r0, cb)
        q = q_ref[h, b, rows, :]
        k = k_ref[h, b, rows, :]
        g = g_ref[h, b, rows, :]
        v = v_ref[h, b, rows, :]
        vt = vt_ref[b, h, ci]
        st = st_scr[b, h]

        bcum = jnp.dot(tril, g, precision=lax.Precision.HIGHEST, preferred_element_type=F32)
        b2 = bcum * LOG2E
        b_last = b2[cb - 1:cb, :]

        o = _dot_nt((q * jnp.exp2(b2)).astype(BF16), st.astype(BF16))
        o_parts = [o[j * SUB_BLOCK:(j + 1) * SUB_BLOCK] for j in range(nsub)]

        for j in range(nsub - 1):
            lo, hi = j * SUB_BLOCK, (j + 1) * SUB_BLOCK
            bnd = b2[hi - 1:hi, :]
            kd = (k[lo:hi] * jnp.exp2(bnd - b2[lo:hi])).astype(BF16)
            qe = (q[hi:] * jnp.exp2(b2[hi:] - bnd)).astype(BF16)
            a = _dot_nt(qe, kd).astype(BF16)
            upd = _dot(a, v[lo:hi].astype(BF16))
            for jj in range(j + 1, nsub):
                o_parts[jj] = o_parts[jj] + upd[(jj - j - 1) * SUB_BLOCK:(jj - j) * SUB_BLOCK]

        for j in range(nsub):
            lo, hi = j * SUB_BLOCK, (j + 1) * SUB_BLOCK
            mid = lo + SUBLANES
            kj, bj, vj = k[lo:hi], b2[lo:hi], v[lo:hi]
            q_top, q_bot = q[lo:mid], q[mid:hi]
            b_top, b_bot = b2[lo:mid], b2[mid:hi]
            acc_top, acc_bot = o_parts[j][:SUBLANES], o_parts[j][SUBLANES:]
            for s in range(SUB_BLOCK):
                bs, ks, vs = bj[s:s + 1, :], kj[s:s + 1, :], vj[s:s + 1, :]
                if s < SUBLANES:
                    w = q_top * jnp.exp2(jnp.where(row8 >= s, b_top - bs, NEG_INF)) * ks
                    acc_top = acc_top + jnp.sum(w, axis=-1, keepdims=True) * vs
                    w = q_bot * jnp.exp2(b_bot - bs) * ks
                else:
                    w = q_bot * jnp.exp2(jnp.where(row8 >= s - SUBLANES, b_bot - bs, NEG_INF)) * ks
                acc_bot = acc_bot + jnp.sum(w, axis=-1, keepdims=True) * vs
            row_lo = pl.multiple_of(r0 + lo, SUB_BLOCK)
            o_ref[h, b, pl.ds(row_lo, SUBLANES), :] = acc_top
            o_ref[h, b, pl.ds(row_lo + SUBLANES, SUBLANES), :] = acc_bot

        kd_all = (k * jnp.exp2(b_last - b2)).astype(BF16)
        st_scr[b, h] = st * jnp.exp2(b_last) + _dot(vt, kd_all)

    def chunk(ci, carry):
        for b in range(bb):
            for h in range(HA_HEADS):
                chunk_head(ci, b, h)
        return carry

    lax.fori_loop(0, tb // cb, chunk, 0)

    @pl.when(ti == pl.num_programs(1) - 1)
    def _():
        for b in range(bb):
            for h in range(HA_HEADS):
                sfin_ref[b, h] = st_scr[b, h].T


def _hgrn(qs, kin, logf, v, vt, s0_all, layer, *, bb, tb, cb):
    _, bsz, t, _ = qs.shape
    blk = pl.BlockSpec((HA_HEADS, bb, tb, HA_DK), lambda i, j: (0, i, j, 0))
    st_blk = pl.BlockSpec((bb, HA_HEADS, HA_DK, HA_DV), lambda i, j: (i, 0, 0, 0))
    s0_blk = pl.BlockSpec((None, bb, HA_HEADS, HA_DK, HA_DV), lambda i, j: (layer, i, 0, 0, 0))
    vt_blk = pl.BlockSpec((bb, None, HA_HEADS, tb // cb, HA_DV, cb),
                          lambda i, j: (i, j, 0, 0, 0, 0))
    return pl.pallas_call(
        functools.partial(_hgrn_kernel, cb=cb),
        grid=(bsz // bb, t // tb),
        in_specs=[blk, blk, blk, blk, vt_blk, s0_blk],
        out_specs=[blk, st_blk],
        out_shape=[jax.ShapeDtypeStruct(qs.shape, F32),
                   jax.ShapeDtypeStruct(s0_all.shape[1:], F32)],
        scratch_shapes=[pltpu.VMEM((bb, HA_HEADS, HA_DV, HA_DK), F32)],
        compiler_params=_cparams(("parallel", "arbitrary")),
        name="hgrn",
    )(qs, kin, logf, v, vt, s0_all)


def _attn_prompt_kernel(qt_ref, kc_ref, kt_ref, o_ref, s0, s1, p0, p1, a0, a1, m_scr, l_scr,
                        acc_scr, *, tq, tk):
    qi = pl.program_id(1)
    jd = (qi * tq) // tk
    shift = CHUNK.bit_length() - 1

    def scores(j, s_dst):
        keys = kc_ref[pl.ds(pl.multiple_of(j * tk, tk), tk), :]
        s_dst[...] = _dot(keys, qt_ref[0])

    def softmax(j, s_src, p_dst, a_dst, masked):
        if masked:
            kpos = j * tk + lax.broadcasted_iota(jnp.int32, (tk, 1), 0)
            qpos = qi * tq + lax.broadcasted_iota(jnp.int32, (1, tq), 1)
            visible = (kpos >> shift) <= (qpos >> shift)
        for h in range(HB_HEADS):
            cols = slice(h * tq, (h + 1) * tq)
            s = s_src[:, cols]
            if masked:
                s = jnp.where(visible, s, NEG_INF)
            m_prev = m_scr[:, cols]
            m_new = jnp.maximum(m_prev, jnp.max(s, axis=0, keepdims=True))
            alpha = jnp.exp2(m_prev - m_new)
            p = jnp.exp2(s - m_new)
            l_scr[:, cols] = alpha * l_scr[:, cols] + jnp.sum(p, axis=0, keepdims=True)
            m_scr[:, cols] = m_new
            a_dst[:, cols] = alpha
            p_dst[:, cols] = p.astype(BF16)

    def values(j, p_src, a_src):
        acc_scr[...] = a_src[...] * acc_scr[...] + _dot(kt_ref[j], p_src[...])

    def trip(j, s_cur, s_nxt, p_cur, p_prv, a_cur, a_prv):
        scores(j + 1, s_nxt)
        softmax(j, s_cur, p_cur, a_cur, False)
        values(jnp.maximum(j - 1, 0), p_prv, a_prv)

    m_scr[...] = jnp.full(m_scr.shape, NEG_INF, F32)
    l_scr[...] = jnp.zeros(l_scr.shape, F32)
    acc_scr[...] = jnp.zeros(acc_scr.shape, F32)
    p1[...] = jnp.zeros(p1.shape, BF16)
    a1[...] = jnp.ones(a1.shape, F32)
    scores(0, s0)

    def pair(jj, carry):
        trip(2 * jj, s0, s1, p0, p1, a0, a1)
        trip(2 * jj + 1, s1, s0, p1, p0, a1, a0)
        return carry

    lax.fori_loop(0, jd // 2, pair, 0)

    @pl.when(jd % 2 == 1)
    def _():
        trip(jd - 1, s0, s1, p0, p1, a0, a1)

    def drain(s_cur, p_cur, p_prv, a_cur, a_prv):
        softmax(jd, s_cur, p_cur, a_cur, True)
        values(jnp.maximum(jd - 1, 0), p_prv, a_prv)
        values(jd, p_cur, a_cur)

    pl.when(jd % 2 == 0)(functools.partial(drain, s0, p0, p1, a0, a1))
    pl.when(jd % 2 == 1)(functools.partial(drain, s1, p1, p0, a1, a0))

    for h in range(HB_HEADS):
        cols = slice(h * tq, (h + 1) * tq)
        o_ref[0, h] = (acc_scr[:, cols] * (1.0 / l_scr[:, cols])).T.astype(BF16)


def _attn_prompt(qt, kcat, ckvt, *, bsz, t, tq, tk):
    nq = t // tq
    nk = t // tk
    cols = HB_HEADS * tq
    single = pl.Buffered(1)
    return pl.pallas_call(
        functools.partial(_attn_prompt_kernel, tq=tq, tk=tk),
        grid=(bsz, nq),
        in_specs=[
            pl.BlockSpec((1, QK_DIM, cols), lambda b, i: (b * nq + i, 0, 0)),
            pl.BlockSpec((t, QK_DIM), lambda b, i: (b, 0), pipeline_mode=single),
            pl.BlockSpec((nk, KV_LORA, tk), lambda b, i: (b, 0, 0), pipeline_mode=single),
        ],
        out_specs=pl.BlockSpec((1, HB_HEADS, tq, KV_LORA), lambda b, i: (b * nq + i, 0, 0, 0)),
        out_shape=jax.ShapeDtypeStruct((bsz * nq, HB_HEADS, tq, KV_LORA), BF16),
        scratch_shapes=[pltpu.VMEM((tk, cols), F32), pltpu.VMEM((tk, cols), F32),
                        pltpu.VMEM((tk, cols), BF16), pltpu.VMEM((tk, cols), BF16),
                        pltpu.VMEM((1, cols), F32), pltpu.VMEM((1, cols), F32),
                        pltpu.VMEM((1, cols), F32), pltpu.VMEM((1, cols), F32),
                        pltpu.VMEM((KV_LORA, cols), F32)],
        compiler_params=_cparams(("parallel", "arbitrary")),
        name="attn_prompt",
    )(qt, kcat, ckvt)


def _attn_sample_kernel(qa_ref, qp_ref, pckv_ref, pkpe_ref, nckv_ref, nkpe_ref, o_ref):
    _, _, ts, _ = qa_ref.shape
    rows = HB_HEADS * ts
    qa = qa_ref[0].reshape(rows, KV_LORA)
    qp = qp_ref[0].reshape(rows, LANES)[:, :ROPE_DIM]
    pckv = pckv_ref[0].astype(BF16)
    nckv = nckv_ref[...].astype(BF16)
    s_past = _dot_nt(qa, pckv) + _dot_nt(qp, pkpe_ref[0].astype(BF16))
    s_new = _dot_nt(qa, nckv) + _dot_nt(qp, nkpe_ref[...].astype(BF16))
    m = jnp.maximum(jnp.max(s_past, axis=-1, keepdims=True), jnp.max(s_new, axis=-1, keepdims=True))
    p_past = jnp.exp2(s_past - m)
    p_new = jnp.exp2(s_new - m)
    l = jnp.sum(p_past, axis=-1, keepdims=True) + jnp.sum(p_new, axis=-1, keepdims=True)
    o = (_dot(p_past.astype(BF16), pckv) + _dot(p_new.astype(BF16), nckv)) / l
    o_ref[0] = o.astype(BF16).reshape(HB_HEADS, ts, KV_LORA)


def _attn_sample(qabs, qpe, past_ckv, past_kpe, layer, ckv, kpe, *, ts):
    _, bsz, past, _ = past_ckv.shape
    return pl.pallas_call(
        _attn_sample_kernel,
        grid=(bsz,),
        in_specs=[
            pl.BlockSpec((1, HB_HEADS, ts, KV_LORA), lambda b: (b, 0, 0, 0)),
            pl.BlockSpec((1, HB_HEADS, ts, LANES), lambda b: (b, 0, 0, 0)),
            pl.BlockSpec((None, 1, past, KV_LORA), lambda b: (layer, b, 0, 0)),
            pl.BlockSpec((None, 1, past, ROPE_DIM), lambda b: (layer, b, 0, 0)),
            pl.BlockSpec((ts, KV_LORA), lambda b: (b, 0)),
            pl.BlockSpec((ts, ROPE_DIM), lambda b: (b, 0)),
        ],
        out_specs=pl.BlockSpec((1, HB_HEADS, ts, KV_LORA), lambda b: (b, 0, 0, 0)),
        out_shape=jax.ShapeDtypeStruct(qabs.shape, BF16),
        compiler_params=_cparams(("parallel",)),
        name="attn_sample",
    )(qabs, qpe, past_ckv, past_kpe, ckv, kpe)


def _gates(logits, bias):
    tm = logits.shape[0]
    lane = lax.broadcasted_iota(jnp.int32, (tm, LANES), 1)
    pos = lane % EXPERTS_PER_GROUP
    valid = lane < N_EXPERTS
    scores = jax.nn.sigmoid(logits)
    sel = jnp.where(valid, scores + bias, -jnp.inf)

    others = []
    for r in range(1, EXPERTS_PER_GROUP):
        others.append(jnp.where(pos >= r, pltpu.roll(sel, r, 1),
                                pltpu.roll(sel, LANES - (EXPERTS_PER_GROUP - r), 1)))
    a, b, c, d = sel, others[0], others[1], others[2]
    hi1, lo1 = jnp.maximum(a, b), jnp.minimum(a, b)
    hi2, lo2 = jnp.maximum(c, d), jnp.minimum(c, d)
    gscore = jnp.maximum(hi1, hi2) + jnp.maximum(jnp.minimum(hi1, hi2), jnp.maximum(lo1, lo2))
    gmax = jnp.max(gscore, axis=-1, keepdims=True)
    group = (lane // EXPERTS_PER_GROUP).astype(F32)
    gidx = jnp.min(jnp.where(gscore == gmax, group, float(LANES)), axis=-1, keepdims=True)

    rank = jnp.zeros((tm, LANES), jnp.int32)
    for r, o in enumerate(others, start=1):
        ahead = (o > sel) | ((o == sel) & (pos >= r))
        rank = rank + ahead.astype(jnp.int32)
    chosen = (group == gidx) & (rank < 2) & valid
    w = jnp.where(chosen, scores, 0.0)
    w = w / jnp.sum(w, axis=-1, keepdims=True)
    return jnp.where(lane == GROUP_LANE, gidx, w)


def _merge_kernel(x_ref, o_ref, gs_ref, olat_ref, wuv_ref, wo_ref, hg_ref, mg_ref,
                  g1_ref, b1_ref, wr_ref, rb_ref, x1g_ref, *, alpha):
    nsb, _, tq, _ = olat_ref.shape
    gs = gs_ref[...]
    hg = hg_ref[...]
    oa = jnp.concatenate(
        [_rms(o_ref[h], hg) * gs[:, h * HA_DV:(h + 1) * HA_DV] for h in range(HA_HEADS)],
        axis=-1)
    ob = []
    for h in range(HB_HEADS):
        lat = jnp.concatenate([olat_ref[s, h] for s in range(nsb)], axis=0)
        ob.append(_dot(lat, wuv_ref[h]))
    ob = _rms(jnp.concatenate(ob, axis=-1), mg_ref[...])
    d_a = HA_HEADS * HA_DV
    mix = _dot(oa.astype(BF16), wo_ref[:d_a, :]) + _dot(ob.astype(BF16), wo_ref[d_a:, :])
    x1 = _layer_norm(alpha * x_ref[...] + mix, g1_ref[...], b1_ref[...])
    x1g_ref[:, :D_MODEL] = x1
    x1g_ref[:, D_MODEL:] = _gates(_dot(x1.astype(BF16), wr_ref[...]), rb_ref[...])


def _merge(x, o_raw, gs, olat, wuv, wo, hg, mg, g1, b1, wr, rb, *, tm, tq, alpha):
    n = x.shape[0]
    row = lambda w: pl.BlockSpec((tm, w), lambda i: (i, 0))
    return pl.pallas_call(
        functools.partial(_merge_kernel, alpha=alpha),
        grid=(n // tm,),
        in_specs=[row(D_MODEL), pl.BlockSpec((HA_HEADS, tm, HA_DV), lambda i: (0, i, 0)), row(HA_W),
                  pl.BlockSpec((tm // tq, HB_HEADS, tq, KV_LORA), lambda i: (i, 0, 0, 0)),
                  _full(wuv.shape), _full(wo.shape), _full(hg.shape), _full(mg.shape),
                  _full(g1.shape), _full(b1.shape), _full(wr.shape), _full(rb.shape)],
        out_specs=row(ROW_W),
        out_shape=jax.ShapeDtypeStruct((n, ROW_W), F32),
        compiler_params=_cparams(("parallel",)),
        name="merge",
    )(x, o_raw, gs, olat, wuv, wo, hg, mg, g1, b1, wr, rb)


def _gather_kernel(idx_ref, src_ref, out_ref, sem):
    rows = out_ref.shape[0]

    def issue(i, carry):
        for u in range(GATHER_UNROLL):
            r = i * GATHER_UNROLL + u
            pltpu.make_async_copy(src_ref.at[pl.ds(idx_ref[0, 0, r], 1), :],
                                  out_ref.at[pl.ds(r, 1), :], sem).start(priority=u % 2)
        return carry

    lax.fori_loop(0, rows // GATHER_UNROLL, issue, 0)
    pltpu.make_async_copy(src_ref.at[pl.ds(0, rows), :], out_ref, sem).wait()


def _gather_rows(src, idx, *, rows):
    m = idx.shape[0]
    w = src.shape[1]
    return pl.pallas_call(
        _gather_kernel,
        grid=(m // rows,),
        in_specs=[pl.BlockSpec((1, 1, rows), lambda i: (i, 0, 0), memory_space=pltpu.SMEM),
                  pl.BlockSpec(memory_space=pl.ANY)],
        out_specs=pl.BlockSpec((rows, w), lambda i: (i, 0)),
        out_shape=jax.ShapeDtypeStruct((m, w), F32),
        scratch_shapes=[pltpu.SemaphoreType.DMA(())],
        compiler_params=_cparams(("arbitrary",)),
        name="gather_rows",
    )(idx.reshape(m // rows, 1, rows), src)


def _scatter_kernel(pads_ref, pos_ref, x_ref, out_ref, sem):
    rows = x_ref.shape[0]
    first = pl.program_id(0) == 0

    def issue(i, carry):
        for u in range(GATHER_UNROLL):
            r = i * GATHER_UNROLL + u
            pltpu.make_async_copy(x_ref.at[pl.ds(r, 1), :],
                                  out_ref.at[pl.ds(pos_ref[0, 0, r], 1), :], sem).start(priority=u % 2)
        return carry

    lax.fori_loop(0, rows // GATHER_UNROLL, issue, 0)

    def fill_copy(dst_row):
        return pltpu.make_async_copy(x_ref.at[pl.ds(0, 1), :], out_ref.at[pl.ds(dst_row, 1), :], sem)

    @pl.when(first)
    def _():
        for g in range(N_GROUPS):
            start = pads_ref[g]
            lax.fori_loop(0, pads_ref[N_GROUPS + g],
                          lambda r, c: (fill_copy(start + r).start(), c)[1], 0)

    pltpu.make_async_copy(x_ref, out_ref.at[pl.ds(0, rows), :], sem).wait()

    @pl.when(first)
    def _():
        for g in range(N_GROUPS):
            lax.fori_loop(0, pads_ref[N_GROUPS + g], lambda r, c: (fill_copy(0).wait(), c)[1], 0)


def _scatter_rows(x, pos, pads, *, rows, m):
    n, w = x.shape
    return pl.pallas_call(
        _scatter_kernel,
        grid_spec=pltpu.PrefetchScalarGridSpec(
            num_scalar_prefetch=1,
            grid=(n // rows,),
            in_specs=[pl.BlockSpec((1, 1, rows), lambda i, pads: (i, 0, 0), memory_space=pltpu.SMEM),
                      pl.BlockSpec((rows, w), lambda i, pads: (i, 0))],
            out_specs=pl.BlockSpec(memory_space=pl.ANY),
            scratch_shapes=[pltpu.SemaphoreType.DMA(())],
        ),
        out_shape=jax.ShapeDtypeStruct((m, w), F32),
        compiler_params=_cparams(("arbitrary",)),
        name="scatter_rows",
    )(pads, pos.reshape(n // rows, 1, rows), x)


def _moe_kernel(tg_ref, xs_ref, wg_ref, wu_ref, wd_ref, y_ref):
    tm = xs_ref.shape[0]
    g = tg_ref[pl.program_id(0)]
    xb = xs_ref[:, :D_MODEL].astype(BF16)
    gates = xs_ref[:, D_MODEL:]
    lane = lax.broadcasted_iota(jnp.int32, (tm, LANES), 1)
    acc = jnp.zeros((tm, D_MODEL), F32)
    for k in range(EXPERTS_PER_GROUP):
        hmid = _silu(_dot(xb, wg_ref[k])) * _dot(xb, wu_ref[k])
        gcol = jnp.sum(jnp.where(lane == g * EXPERTS_PER_GROUP + k, gates, 0.0),
                       axis=-1, keepdims=True)
        acc = acc + gcol * _dot(hmid.astype(BF16), wd_ref[k])
    y_ref[...] = acc


def _moe_sorted(tile_group, xs, wg, wu, wd, *, tm):
    m = xs.shape[0]
    wspec = lambda shape: pl.BlockSpec((EXPERTS_PER_GROUP,) + shape, lambda i, tg: (tg[i], 0, 0))
    return pl.pallas_call(
        _moe_kernel,
        grid_spec=pltpu.PrefetchScalarGridSpec(
            num_scalar_prefetch=1,
            grid=(m // tm,),
            in_specs=[pl.BlockSpec((tm, ROW_W), lambda i, tg: (i, 0)),
                      wspec((D_MODEL, D_EXPERT)), wspec((D_MODEL, D_EXPERT)),
                      wspec((D_EXPERT, D_MODEL))],
            out_specs=pl.BlockSpec((tm, D_MODEL), lambda i, tg: (i, 0)),
        ),
        out_shape=jax.ShapeDtypeStruct((m, D_MODEL), F32),
        compiler_params=_cparams(("arbitrary",)),
        name="moe",
    )(tile_group, xs, wg, wu, wd)


def _route(x1g, *, tm):
    n = x1g.shape[0]
    n_tiles = n // tm + N_GROUPS - 1
    gidx = x1g[:, D_MODEL + GROUP_LANE].astype(jnp.int32)
    onehot = (gidx[:, None] == jnp.arange(N_GROUPS, dtype=jnp.int32)[None, :]).astype(jnp.int32)
    counts = jnp.sum(onehot, axis=0)
    tiles = (counts + tm - 1) // tm
    tile_end = jnp.cumsum(tiles)
    tile_start = tile_end - tiles
    rank = jnp.sum((jnp.cumsum(onehot, axis=0) - onehot) * onehot, axis=1)
    pos = jnp.take(tile_start, gidx) * tm + rank
    pad_start = tile_start * tm + counts
    pad_end = (tile_end * tm).at[N_GROUPS - 1].set(n_tiles * tm)
    pads = jnp.concatenate([pad_start, pad_end - pad_start]).astype(jnp.int32)
    tile_group = jnp.searchsorted(tile_end, jnp.arange(n_tiles, dtype=jnp.int32), side="right")
    return pos, pads, jnp.minimum(tile_group, N_GROUPS - 1).astype(jnp.int32), n_tiles


def _post_kernel(x1_ref, f_ref, p_ref, g2_ref, b2_ref, wpg_ref, wp_ref, y_ref, *, alpha):
    x2 = _layer_norm(alpha * x1_ref[...] + f_ref[...], g2_ref[...], b2_ref[...])
    gate = jax.nn.sigmoid(_dot(x2.astype(BF16), wpg_ref[...]))
    y_ref[...] = x2 + gate * _dot(p_ref[...].astype(BF16), wp_ref[...])


def _post(x1g, ffn, p_all, layer, g2, b2, wpg, wp, *, tm, alpha):
    n = ffn.shape[0]
    row = lambda w: pl.BlockSpec((tm, w), lambda i: (i, 0))
    return pl.pallas_call(
        functools.partial(_post_kernel, alpha=alpha),
        grid=(n // tm,),
        in_specs=[row(D_MODEL), row(D_MODEL),
                  pl.BlockSpec((None, tm, PLE_DIM), lambda i: (layer, i, 0)),
                  _full(g2.shape), _full(b2.shape), _full(wpg.shape), _full(wp.shape)],
        out_specs=row(D_MODEL),
        out_shape=jax.ShapeDtypeStruct((n, D_MODEL), F32),
        compiler_params=_cparams(("parallel",)),
        name="post",
    )(x1g, ffn, p_all, g2, b2, wpg, wp)


def _rot_cols(w):
    half = ROPE_DIM // 2
    return jnp.concatenate([-w[..., half:], w[..., :half]], axis=-1)


def _prep_layer(l, w_in, w_qb, w_kvb, w_o, w_gate, w_up, w_down, w_ple, w_ple_gate):
    win = w_in[l]
    win_aug = jnp.concatenate([win, _rot_cols(win[:, COL_KPE:])], axis=-1).astype(BF16)
    wqb = w_qb[l].reshape(Q_LORA, HB_HEADS, NOPE_DIM + ROPE_DIM)
    wqb_aug = jnp.concatenate([wqb, _rot_cols(wqb[..., NOPE_DIM:])], axis=-1)
    wqb_aug = wqb_aug.reshape(Q_LORA, HB_HEADS * Q_HEAD_AUG).astype(BF16)
    wkvb = w_kvb[l].reshape(KV_LORA, HB_HEADS, NOPE_DIM + V_DIM)
    wuk_t = jnp.transpose(wkvb[..., :NOPE_DIM], (1, 2, 0)).astype(BF16)
    wuv = jnp.transpose(wkvb[..., NOPE_DIM:], (1, 0, 2)).astype(BF16)
    return dict(win=win_aug, wqb=wqb_aug, wuk=wuk_t, wuv=wuv, wo=w_o[l].astype(BF16),
                wg=w_gate[l].astype(BF16), wu=w_up[l].astype(BF16), wd=w_down[l].astype(BF16),
                wp=w_ple[l].astype(BF16), wpg=w_ple_gate[l].astype(BF16))


def _rope_table(pos):
    inv = ROPE_THETA ** (-np.arange(0, ROPE_DIM, 2, dtype=np.float64) / ROPE_DIM)
    ang = pos.astype(np.float64)[:, None] * inv[None, :]
    cos, sin = np.cos(ang), np.sin(ang)
    return np.concatenate([cos, cos, sin, sin], axis=-1).astype(np.float32)


def _tiles(n, t, prompt):
    if prompt:
        tm = min(512, t)
        return dict(tm=tm, tq=tm, tk=tm, tb=tm, cb=min(128, t), bb=2, tmoe=min(512, n))
    return dict(tm=min(512, n), tq=t, tk=None, tb=t, cb=t, bb=2, tmoe=min(256, n))


def _layer(x, p_all, layer, cs, s0_all, past, lbp, prm, small, *, alpha, prompt):
    bsz, t, _ = x.shape
    n = bsz * t
    tl = _tiles(n, t, prompt)
    x2d = x.reshape(n, D_MODEL)
    tm, tb, cb = tl["tm"], tl["tb"], tl["cb"]
    outs = _inproj(x2d, cs, prm["win"], lbp, small["qg"], prm["wqb"], small["kvg"], prm["wuk"],
                   tm=tm, tq=tl["tq"], cb=cb, prompt=prompt)
    qs, kin, logf, v, gs, ckv, kpe = outs[:7]
    r4 = lambda a: a.reshape(HA_HEADS, bsz, t, HA_DK)
    if prompt:
        qt, vt, kcat, ckvt = outs[7:]
        vt = vt.reshape(bsz, t // tb, HA_HEADS, tb // cb, HA_DV, cb)
    else:
        qabs, qpe = outs[7:]
        vt = jnp.transpose(r4(v), (1, 0, 3, 2)).astype(BF16)
        vt = vt.reshape(bsz, 1, HA_HEADS, 1, HA_DV, t)
    o_raw, s_new = _hgrn(r4(qs), r4(kin), r4(logf), r4(v), vt, s0_all, layer if not prompt else 0,
                         bb=tl["bb"], tb=tb, cb=cb)
    if prompt:
        olat = _attn_prompt(qt, kcat, ckvt, bsz=bsz, t=t, tq=tl["tq"], tk=tl["tk"])
    else:
        olat = _attn_sample(qabs, qpe, past[0], past[1], layer, ckv, kpe, ts=t)
    x1g = _merge(x2d, o_raw.reshape(HA_HEADS, n, HA_DV), gs, olat, prm["wuv"], prm["wo"],
                 small["hg"], small["mg"], small["g1"], small["b1"],
                 small["wr"], small["rb"], tm=tm, tq=tl["tq"], alpha=alpha)
    tmoe = tl["tmoe"]
    pos, pads, tile_group, n_tiles = _route(x1g, tm=tmoe)
    xs = _scatter_rows(x1g, pos, pads, rows=tmoe, m=n_tiles * tmoe)
    ys = _moe_sorted(tile_group, xs, prm["wg"], prm["wu"], prm["wd"], tm=tmoe)
    ffn = _gather_rows(ys, pos, rows=tmoe)
    y = _post(x1g, ffn, p_all.reshape(-1, n, PLE_DIM), layer, small["g2"], small["b2"],
              prm["wpg"], prm["wp"], tm=tm, alpha=alpha)
    return (y.reshape(bsz, t, D_MODEL), s_new, ckv.reshape(bsz, t, KV_LORA),
            kpe.reshape(bsz, t, ROPE_DIM))


def kernel(x_prompt, x_sample, p_prompt, p_sample, state_hgrn, cache_ckv, cache_kpe, w_in,
           lb_logits, hgrn_norm_g, q_norm_g, w_qb, kv_norm_g, w_kvb, mla_norm_g, w_o,
           ln1_g, ln1_b, ln2_g, ln2_b, w_router, router_bias, w_gate, w_up, w_down,
           w_ple, w_ple_gate):
    depth = w_in.shape[0]
    alpha = (2 * depth) ** 0.25
    bp, tp, _ = x_prompt.shape
    bs, ts, _ = x_sample.shape
    past = cache_ckv.shape[2]

    sm = jax.nn.softmax(lb_logits.astype(F32), axis=0)
    lb_all = jnp.maximum(jnp.cumsum(sm, axis=0) - sm[0:1], 0.0)
    lbp_all = jnp.stack([jnp.log(lb_all), jnp.log1p(-lb_all), 1.0 - lb_all], axis=1)

    cs_p = jnp.asarray(_rope_table(np.arange(tp)))
    tm_s = _tiles(bs * ts, ts, False)["tm"]
    cs_s = jnp.asarray(np.tile(_rope_table(past + np.arange(ts)), (tm_s // ts, 1)))
    wr = jnp.pad(w_router, ((0, 0), (0, LANES - N_EXPERTS))).astype(BF16)
    rb = jnp.pad(router_bias.astype(F32), (0, LANES - N_EXPERTS)).reshape(1, LANES)
    s0_p = jnp.zeros((1, bp, HA_HEADS, HA_DK, HA_DV), F32)

    yp, ys = x_prompt, x_sample
    res = [[] for _ in range(6)]
    for l in range(depth):
        prm = _prep_layer(l, w_in, w_qb, w_kvb, w_o, w_gate, w_up, w_down, w_ple, w_ple_gate)
        row = lambda a: a[l].reshape(1, -1).astype(F32)
        small = dict(qg=row(q_norm_g), kvg=row(kv_norm_g), hg=row(hgrn_norm_g), mg=row(mla_norm_g),
                     g1=row(ln1_g), b1=row(ln1_b), g2=row(ln2_g), b2=row(ln2_b), wr=wr, rb=rb)
        yp, sp, cp, kp = _layer(yp, p_prompt, l, cs_p, s0_p, None, lbp_all[l], prm, small,
                                alpha=alpha, prompt=True)
        ys, ss, cs_, ks = _layer(ys, p_sample, l, cs_s, state_hgrn, (cache_ckv, cache_kpe),
                                 lbp_all[l], prm, small, alpha=alpha, prompt=False)
        for lst, a in zip(res, (sp, cp, kp, ss, cs_, ks)):
            lst.append(a)
    sp, cp, kp, ss, cs_, ks = (jnp.stack(a) for a in res)
    return (yp, ys, sp, cp, kp, ss, cs_, ks)
```

```python
import functools

import jax
import jax.numpy as jnp
import numpy as np
from jax import lax
from jax.experimental import pallas as pl
from jax.experimental.pallas import tpu as pltpu

F32 = jnp.float32
BF16 = jnp.bfloat16

D_MODEL = 1024
HA_HEADS = 4
HA_DK = 128
HA_DV = 128
HB_HEADS = 4
Q_LORA = 384
KV_LORA = 256
NOPE_DIM = 128
ROPE_DIM = 64
V_DIM = 128
ROPE_THETA = 10000.0
MLA_SCALE = (NOPE_DIM + ROPE_DIM) ** -0.5
LOG2E = 1.4426950408889634
Q_SCALE = MLA_SCALE * LOG2E
CHUNK = 64
N_EXPERTS = 16
N_GROUPS = 4
EXPERTS_PER_GROUP = N_EXPERTS // N_GROUPS
D_EXPERT = 512
PLE_DIM = 256
NEG_INF = -1e30

HA_W = HA_HEADS * HA_DK
COL_CQ = 4 * HA_W
COL_CKV = COL_CQ + Q_LORA
COL_KPE = COL_CKV + KV_LORA
D_IN_AUG = COL_KPE + 2 * ROPE_DIM
Q_HEAD_AUG = NOPE_DIM + 2 * ROPE_DIM

LANES = 128
SUBLANES = 8
QK_DIM = KV_LORA + LANES
ROW_W = D_MODEL + LANES
GROUP_LANE = N_EXPERTS
GATHER_UNROLL = 8
SCORE_W = 512
SUB_BLOCK = 16
VMEM_LIMIT = 56 * 1024 * 1024


def _cparams(sem, vmem=VMEM_LIMIT):
    return pltpu.CompilerParams(dimension_semantics=sem, vmem_limit_bytes=vmem)


def _dot(a, b):
    return jnp.dot(a, b, preferred_element_type=F32)


def _dot_nt(a, b):
    return lax.dot_general(a, b, (((1,), (1,)), ((), ())), preferred_element_type=F32)


def _rms(x, g, eps=1e-6):
    return x * lax.rsqrt(jnp.mean(x * x, axis=-1, keepdims=True) + eps) * g


def _layer_norm(x, g, b, eps=1e-5):
    mu = jnp.mean(x, axis=-1, keepdims=True)
    xc = x - mu
    var = jnp.mean(xc * xc, axis=-1, keepdims=True)
    return xc * lax.rsqrt(var + eps) * g + b


def _lane_tile(x, n):
    return jnp.concatenate([x] * n, axis=1)


def _silu(x):
    return x * jax.nn.sigmoid(x)


def _full(shape):
    nd = len(shape)
    return pl.BlockSpec(shape, lambda *_: (0,) * nd)


def _inproj_kernel(x_ref, cs_ref, win_ref, lb_ref, qg_ref, wqb_ref, kvg_ref, wuk_ref,
                   qs_ref, kin_ref, logf_ref, v_ref, gs_ref, ckv_ref, kpe_ref, *rest,
                   tq, cb, prompt):
    tm = x_ref.shape[0]
    x = x_ref[...].astype(BF16)

    def sect(lo, hi):
        return _dot(x, win_ref[:, lo:hi])

    def put_heads(ref, val):
        for h in range(HA_HEADS):
            ref[h] = val[:, h * HA_DK:(h + 1) * HA_DK]

    cqn = _rms(sect(COL_CQ, COL_CKV), qg_ref[...]).astype(BF16)

    put_heads(qs_ref, _silu(sect(0, HA_W)))
    fa = sect(HA_W, 2 * HA_W)
    log_lb = lb_ref[0:1, :]
    log1m_lb = lb_ref[1:2, :]
    one_m_lb = lb_ref[2:3, :]
    e = jnp.exp(-jnp.abs(fa))
    r = 1.0 / (1.0 + e)
    c = log1m_lb + jnp.minimum(fa, 0.0) + jnp.log(r)
    put_heads(logf_ref, jnp.maximum(log_lb, c) + jnp.log(1.0 + jnp.exp(-jnp.abs(log_lb - c))))
    put_heads(kin_ref, one_m_lb * jnp.where(fa >= 0.0, e * r, r))
    v = sect(2 * HA_W, 3 * HA_W)
    put_heads(v_ref, v)
    gs_ref[...] = _silu(sect(3 * HA_W, 4 * HA_W))

    cs = cs_ref[...]

    def rope(t):
        prod = t * cs
        return prod + pltpu.roll(prod, ROPE_DIM, 1)

    ckv = _rms(sect(COL_CKV, COL_KPE), kvg_ref[...])
    ckv_ref[...] = ckv
    kpe2 = rope(sect(COL_KPE, D_IN_AUG))
    kpe_ref[...] = kpe2[:, :ROPE_DIM]
    if prompt:
        qt_ref, vt_ref, kcat_ref, ckvt_ref = rest
        for h in range(HA_HEADS):
            for ci in range(tm // cb):
                vt_ref[0, h, ci] = v[ci * cb:(ci + 1) * cb, h * HA_DV:(h + 1) * HA_DV].T.astype(BF16)
        kcat_ref[:, :KV_LORA] = ckv.astype(BF16)
        kcat_ref[:, KV_LORA:] = kpe2.astype(BF16)
        ckvt_ref[0] = ckv.T.astype(BF16)
    else:
        qabs_ref, qpe_ref = rest

    lane = lax.broadcasted_iota(jnp.int32, (tm, LANES), 1)
    qhs = [_dot(cqn, wqb_ref[:, h * Q_HEAD_AUG:(h + 1) * Q_HEAD_AUG]) for h in range(HB_HEADS)]
    qabss = [_dot(qhs[h][:, :NOPE_DIM].astype(BF16), wuk_ref[h]) for h in range(HB_HEADS)]
    for h in range(HB_HEADS):
        qabs = qabss[h] * Q_SCALE
        qpe = jnp.where(lane < ROPE_DIM, rope(qhs[h][:, NOPE_DIM:]) * Q_SCALE, 0.0)
        if prompt:
            qabs_t = qabs.T.astype(BF16)
            qpe_t = qpe.T.astype(BF16)
            for s in range(tm // tq):
                qt_ref[s, :KV_LORA, h * tq:(h + 1) * tq] = qabs_t[:, s * tq:(s + 1) * tq]
                qt_ref[s, KV_LORA:, h * tq:(h + 1) * tq] = qpe_t[:, s * tq:(s + 1) * tq]
        else:
            for s in range(tm // tq):
                qabs_ref[s, h] = qabs[s * tq:(s + 1) * tq].astype(BF16)
                qpe_ref[s, h] = qpe[s * tq:(s + 1) * tq].astype(BF16)


def _inproj(x, cs, w_in, lbp, qg, wqb, kvg, wuk, *, tm, tq, cb, prompt):
    n = x.shape[0]
    nt = n // tm
    row = lambda w: pl.BlockSpec((tm, w), lambda i: (i, 0))
    heads = pl.BlockSpec((HA_HEADS, tm, HA_DK), lambda i: (0, i, 0))
    qblk = lambda w: pl.BlockSpec((tm // tq, HB_HEADS, tq, w), lambda i: (i, 0, 0, 0))
    head_major = jax.ShapeDtypeStruct((HA_HEADS, n, HA_DK), F32)
    out_shape = [
        head_major,
        head_major,
        head_major,
        head_major,
        jax.ShapeDtypeStruct((n, HA_W), F32),
        jax.ShapeDtypeStruct((n, KV_LORA), F32),
        jax.ShapeDtypeStruct((n, ROPE_DIM), F32),
    ]
    out_specs = [heads, heads, heads, heads, row(HA_W), row(KV_LORA), row(ROPE_DIM)]
    if prompt:
        out_shape += [jax.ShapeDtypeStruct((n // tq, QK_DIM, HB_HEADS * tq), BF16),
                      jax.ShapeDtypeStruct((nt, HA_HEADS, tm // cb, HA_DV, cb), BF16),
                      jax.ShapeDtypeStruct((n, QK_DIM), BF16),
                      jax.ShapeDtypeStruct((nt, KV_LORA, tm), BF16)]
        out_specs += [pl.BlockSpec((tm // tq, QK_DIM, HB_HEADS * tq), lambda i: (i, 0, 0)),
                      pl.BlockSpec((1, HA_HEADS, tm // cb, HA_DV, cb), lambda i: (i, 0, 0, 0, 0)),
                      row(QK_DIM),
                      pl.BlockSpec((1, KV_LORA, tm), lambda i: (i, 0, 0))]
    else:
        out_shape += [jax.ShapeDtypeStruct((n // tq, HB_HEADS, tq, KV_LORA), BF16),
                      jax.ShapeDtypeStruct((n // tq, HB_HEADS, tq, LANES), BF16)]
        out_specs += [qblk(KV_LORA), qblk(LANES)]
    return pl.pallas_call(
        functools.partial(_inproj_kernel, tq=tq, cb=cb, prompt=prompt),
        grid=(nt,),
        in_specs=[row(D_MODEL),
                  pl.BlockSpec((tm, LANES), lambda i: (i % (cs.shape[0] // tm), 0)),
                  _full(w_in.shape), _full(lbp.shape), _full(qg.shape),
                  _full(wqb.shape), _full(kvg.shape), _full(wuk.shape)],
        out_specs=out_specs,
        out_shape=out_shape,
        compiler_params=_cparams(("parallel",)),
        name="inproj",
    )(x, cs, w_in, lbp, qg, wqb, kvg, wuk)


def _hgrn_kernel(q_ref, k_ref, g_ref, v_ref, vt_ref, s0_ref, o_ref, sfin_ref, st_scr, b2_scr,
                 hide_ref, *, cb):
    _, bb, tb, _ = q_ref.shape
    nsub = cb // SUB_BLOCK
    ti = pl.program_id(1)

    @pl.when(ti == 0)
    def _():
        for b in range(bb):
            for h in range(HA_HEADS):
                st_scr[b, h] = s0_ref[b, h].T
        row8 = lax.broadcasted_iota(jnp.int32, (SUBLANES, LANES), 0)
        for s in range(SUBLANES):
            hide_ref[s] = jnp.where(row8 >= s, 0.0, NEG_INF)

    r_i = lax.broadcasted_iota(jnp.int32, (cb, cb), 0)
    c_i = lax.broadcasted_iota(jnp.int32, (cb, cb), 1)
    tril = (r_i >= c_i).astype(F32)

    chains = [(b, h) for b in range(bb) for h in range(HA_HEADS)]

    def matmul_part(ci):
        r0 = pl.multiple_of(ci * cb, cb)
        rows = pl.ds(r0, cb)

        def sub(ref, b, h, lo, size):
            return ref[h, b, pl.ds(pl.multiple_of(r0 + lo, SUB_BLOCK), size), :]

        for b, h in chains:
            bcum = jnp.dot(tril, g_ref[h, b, rows, :], precision=lax.Precision.HIGHEST,
                           preferred_element_type=F32)
            b2_scr[b * HA_HEADS + h] = bcum * LOG2E
        for b, h in chains:
            b2 = b2_scr[b * HA_HEADS + h]
            o_ref[h, b, rows, :] = _dot_nt((q_ref[h, b, rows, :] * jnp.exp2(b2)).astype(BF16),
                                           st_scr[b, h].astype(BF16))
        for j in range(nsub - 1):
            lo, hi = j * SUB_BLOCK, (j + 1) * SUB_BLOCK
            below = pl.ds(pl.multiple_of(r0 + hi, SUB_BLOCK), cb - hi)
            for b, h in chains:
                slot = b * HA_HEADS + h
                bnd = b2_scr[slot, hi - 1:hi, :]
                kd = (sub(k_ref, b, h, lo, SUB_BLOCK)
                      * jnp.exp2(bnd - b2_scr[slot, lo:hi, :])).astype(BF16)
                qe = (sub(q_ref, b, h, hi, cb - hi)
                      * jnp.exp2(b2_scr[slot, hi:, :] - bnd)).astype(BF16)
                a = _dot_nt(qe, kd).astype(BF16)
                o_ref[h, b, below, :] += _dot(a, sub(v_ref, b, h, lo, SUB_BLOCK).astype(BF16))
        for b, h in chains:
            slot = b * HA_HEADS + h
            b_last = b2_scr[slot, cb - 1:cb, :]
            kd_all = (k_ref[h, b, rows, :] * jnp.exp2(b_last - b2_scr[slot])).astype(BF16)
            st_scr[b, h] = st_scr[b, h] * jnp.exp2(b_last) + _dot(vt_ref[b, h, ci], kd_all)

    def pairwise_part(ci, b, h, j):
        slot = b * HA_HEADS + h
        r0 = pl.multiple_of(ci * cb, cb)
        lo = j * SUB_BLOCK
        mid = lo + SUBLANES
        top = pl.ds(pl.multiple_of(r0 + lo, SUBLANES), SUBLANES)
        bot = pl.ds(pl.multiple_of(r0 + mid, SUBLANES), SUBLANES)
        q_top, q_bot = q_ref[h, b, top, :], q_ref[h, b, bot, :]
        b_top, b_bot = b2_scr[slot, lo:mid, :], b2_scr[slot, mid:mid + SUBLANES, :]
        acc_top, acc_bot = o_ref[h, b, top, :], o_ref[h, b, bot, :]
        for s in range(SUB_BLOCK):
            row = pl.ds(r0 + lo + s, 1)
            bs = b2_scr[slot, lo + s:lo + s + 1, :]
            ks = k_ref[h, b, row, :]
            vs = v_ref[h, b, row, :]
            if s < SUBLANES:
                w = q_top * jnp.exp2(b_top - bs + hide_ref[s]) * ks
                acc_top = acc_top + jnp.sum(w, axis=-1, keepdims=True) * vs
                w = q_bot * jnp.exp2(b_bot - bs) * ks
            else:
                w = q_bot * jnp.exp2(b_bot - bs + hide_ref[s - SUBLANES]) * ks
            acc_bot = acc_bot + jnp.sum(w, axis=-1, keepdims=True) * vs
        o_ref[h, b, top, :] = acc_top
        o_ref[h, b, bot, :] = acc_bot

    def chunk(ci, carry):
        matmul_part(ci)
        for j in range(nsub):
            for b, h in chains:
                pairwise_part(ci, b, h, j)
        return carry

    lax.fori_loop(0, tb // cb, chunk, 0)

    @pl.when(ti == pl.num_programs(1) - 1)
    def _():
        for b in range(bb):
            for h in range(HA_HEADS):
                sfin_ref[b, h] = st_scr[b, h].T


def _hgrn(qs, kin, logf, v, vt, s0_all, layer, *, bb, tb, cb):
    _, bsz, t, _ = qs.shape
    blk = pl.BlockSpec((HA_HEADS, bb, tb, HA_DK), lambda i, j: (0, i, j, 0))
    st_blk = pl.BlockSpec((bb, HA_HEADS, HA_DK, HA_DV), lambda i, j: (i, 0, 0, 0))
    s0_blk = pl.BlockSpec((None, bb, HA_HEADS, HA_DK, HA_DV), lambda i, j: (layer, i, 0, 0, 0))
    vt_blk = pl.BlockSpec((bb, None, HA_HEADS, tb // cb, HA_DV, cb),
                          lambda i, j: (i, j, 0, 0, 0, 0))
    return pl.pallas_call(
        functools.partial(_hgrn_kernel, cb=cb),
        grid=(bsz // bb, t // tb),
        in_specs=[blk, blk, blk, blk, vt_blk, s0_blk],
        out_specs=[blk, st_blk],
        out_shape=[jax.ShapeDtypeStruct(qs.shape, F32),
                   jax.ShapeDtypeStruct(s0_all.shape[1:], F32)],
        scratch_shapes=[pltpu.VMEM((bb, HA_HEADS, HA_DV, HA_DK), F32),
                        pltpu.VMEM((bb * HA_HEADS, cb, HA_DK), F32),
                        pltpu.VMEM((SUBLANES, SUBLANES, LANES), F32)],
        compiler_params=_cparams(("parallel", "arbitrary")),
        name="hgrn",
    )(qs, kin, logf, v, vt, s0_all)


def _attn_prompt_kernel(qt_ref, kc_ref, kt_ref, o_ref, s0, s1, p0, p1, a0, a1, m_scr, l_scr,
                        acc_scr, *, tq, tk):
    qi = pl.program_id(1)
    jd = (qi * tq) // tk
    shift = CHUNK.bit_length() - 1

    def scores(j, s_dst):
        keys = kc_ref[pl.ds(pl.multiple_of(j * tk, tk), tk), :]
        s_dst[...] = _dot(keys, qt_ref[0])

    def softmax(j, s_src, p_dst, a_dst, masked):
        if masked:
            kpos = j * tk + lax.broadcasted_iota(jnp.int32, (tk, 1), 0)
            qpos = qi * tq + lax.broadcasted_iota(jnp.int32, (1, tq), 1)
            visible = (kpos >> shift) <= (qpos >> shift)
        for h in range(HB_HEADS):
            cols = slice(h * tq, (h + 1) * tq)
            s = s_src[:, cols]
            if masked:
                s = jnp.where(visible, s, NEG_INF)
            m_prev = m_scr[:, cols]
            m_new = jnp.maximum(m_prev, jnp.max(s, axis=0, keepdims=True))
            alpha = jnp.exp2(m_prev - m_new)
            p = jnp.exp2(s - m_new)
            l_scr[:, cols] = alpha * l_scr[:, cols] + jnp.sum(p, axis=0, keepdims=True)
            m_scr[:, cols] = m_new
            a_dst[:, cols] = alpha
            p_dst[:, cols] = p.astype(BF16)

    def values(j, p_src, a_src):
        acc_scr[...] = a_src[...] * acc_scr[...] + _dot(kt_ref[j], p_src[...])

    def trip(j, s_cur, s_nxt, p_cur, p_prv, a_cur, a_prv):
        scores(j + 1, s_nxt)
        softmax(j, s_cur, p_cur, a_cur, False)
        values(jnp.maximum(j - 1, 0), p_prv, a_prv)

    m_scr[...] = jnp.full(m_scr.shape, NEG_INF, F32)
    l_scr[...] = jnp.zeros(l_scr.shape, F32)
    acc_scr[...] = jnp.zeros(acc_scr.shape, F32)
    p1[...] = jnp.zeros(p1.shape, BF16)
    a1[...] = jnp.ones(a1.shape, F32)
    scores(0, s0)

    def pair(jj, carry):
        trip(2 * jj, s0, s1, p0, p1, a0, a1)
        trip(2 * jj + 1, s1, s0, p1, p0, a1, a0)
        return carry

    lax.fori_loop(0, jd // 2, pair, 0)

    @pl.when(jd % 2 == 1)
    def _():
        trip(jd - 1, s0, s1, p0, p1, a0, a1)

    def drain(s_cur, p_cur, p_prv, a_cur, a_prv):
        softmax(jd, s_cur, p_cur, a_cur, True)
        values(jnp.maximum(jd - 1, 0), p_prv, a_prv)
        values(jd, p_cur, a_cur)

    pl.when(jd % 2 == 0)(functools.partial(drain, s0, p0, p1, a0, a1))
    pl.when(jd % 2 == 1)(functools.partial(drain, s1, p1, p0, a1, a0))

    for h in range(HB_HEADS):
        cols = slice(h * tq, (h + 1) * tq)
        o_ref[0, h] = (acc_scr[:, cols] * (1.0 / l_scr[:, cols])).T.astype(BF16)


def _attn_prompt(qt, kcat, ckvt, *, bsz, t, tq, tk):
    nq = t // tq
    nk = t // tk
    cols = HB_HEADS * tq
    single = pl.Buffered(1)
    return pl.pallas_call(
        functools.partial(_attn_prompt_kernel, tq=tq, tk=tk),
        grid=(bsz, nq),
        in_specs=[
            pl.BlockSpec((1, QK_DIM, cols), lambda b, i: (b * nq + i, 0, 0)),
            pl.BlockSpec((t, QK_DIM), lambda b, i: (b, 0), pipeline_mode=single),
            pl.BlockSpec((nk, KV_LORA, tk), lambda b, i: (b, 0, 0), pipeline_mode=single),
        ],
        out_specs=pl.BlockSpec((1, HB_HEADS, tq, KV_LORA), lambda b, i: (b * nq + i, 0, 0, 0)),
        out_shape=jax.ShapeDtypeStruct((bsz * nq, HB_HEADS, tq, KV_LORA), BF16),
        scratch_shapes=[pltpu.VMEM((tk, cols), F32), pltpu.VMEM((tk, cols), F32),
                        pltpu.VMEM((tk, cols), BF16), pltpu.VMEM((tk, cols), BF16),
                        pltpu.VMEM((1, cols), F32), pltpu.VMEM((1, cols), F32),
                        pltpu.VMEM((1, cols), F32), pltpu.VMEM((1, cols), F32),
                        pltpu.VMEM((KV_LORA, cols), F32)],
        compiler_params=_cparams(("parallel", "arbitrary")),
        name="attn_prompt",
    )(qt, kcat, ckvt)


def _attn_sample_kernel(qa_ref, qp_ref, pckv_ref, pkpe_ref, nckv_ref, nkpe_ref, o_ref):
    _, _, ts, _ = qa_ref.shape
    rows = HB_HEADS * ts
    qa = qa_ref[0].reshape(rows, KV_LORA)
    qp = qp_ref[0].reshape(rows, LANES)[:, :ROPE_DIM]
    pckv = pckv_ref[0].astype(BF16)
    nckv = nckv_ref[...].astype(BF16)
    s_past = _dot_nt(qa, pckv) + _dot(qp, pkpe_ref[0].astype(BF16))
    s_new = _dot_nt(qa, nckv) + _dot_nt(qp, nkpe_ref[...].astype(BF16))
    m = jnp.maximum(jnp.max(s_past, axis=-1, keepdims=True), jnp.max(s_new, axis=-1, keepdims=True))
    p_past = jnp.exp2(s_past - m)
    p_new = jnp.exp2(s_new - m)
    l = jnp.sum(p_past, axis=-1, keepdims=True) + jnp.sum(p_new, axis=-1, keepdims=True)
    o = (_dot(p_past.astype(BF16), pckv) + _dot(p_new.astype(BF16), nckv)) / l
    o_ref[0] = o.astype(BF16).reshape(HB_HEADS, ts, KV_LORA)


def _attn_sample(qabs, qpe, past_ckv, past_kpe, layer, ckv, kpe, *, ts):
    _, bsz, past, _ = past_ckv.shape
    return pl.pallas_call(
        _attn_sample_kernel,
        grid=(bsz,),
        in_specs=[
            pl.BlockSpec((1, HB_HEADS, ts, KV_LORA), lambda b: (b, 0, 0, 0)),
            pl.BlockSpec((1, HB_HEADS, ts, LANES), lambda b: (b, 0, 0, 0)),
            pl.BlockSpec((None, 1, past, KV_LORA), lambda b: (layer, b, 0, 0)),
            pl.BlockSpec((None, 1, ROPE_DIM, past), lambda b: (layer, b, 0, 0)),
            pl.BlockSpec((ts, KV_LORA), lambda b: (b, 0)),
            pl.BlockSpec((ts, ROPE_DIM), lambda b: (b, 0)),
        ],
        out_specs=pl.BlockSpec((1, HB_HEADS, ts, KV_LORA), lambda b: (b, 0, 0, 0)),
        out_shape=jax.ShapeDtypeStruct(qabs.shape, BF16),
        compiler_params=_cparams(("parallel",)),
        name="attn_sample",
    )(qabs, qpe, past_ckv, past_kpe, ckv, kpe)


def _gates(logits, bias):
    tm = logits.shape[0]
    lane = lax.broadcasted_iota(jnp.int32, (tm, LANES), 1)
    pos = lane % EXPERTS_PER_GROUP
    valid = lane < N_EXPERTS
    scores = jax.nn.sigmoid(logits)
    sel = jnp.where(valid, scores + bias, -jnp.inf)

    others = []
    for r in range(1, EXPERTS_PER_GROUP):
        others.append(jnp.where(pos >= r, pltpu.roll(sel, r, 1),
                                pltpu.roll(sel, LANES - (EXPERTS_PER_GROUP - r), 1)))
    a, b, c, d = sel, others[0], others[1], others[2]
    hi1, lo1 = jnp.maximum(a, b), jnp.minimum(a, b)
    hi2, lo2 = jnp.maximum(c, d), jnp.minimum(c, d)
    gscore = jnp.maximum(hi1, hi2) + jnp.maximum(jnp.minimum(hi1, hi2), jnp.maximum(lo1, lo2))
    gmax = jnp.max(gscore, axis=-1, keepdims=True)
    group = (lane // EXPERTS_PER_GROUP).astype(F32)
    gidx = jnp.min(jnp.where(gscore == gmax, group, float(LANES)), axis=-1, keepdims=True)

    rank = jnp.zeros((tm, LANES), jnp.int32)
    for r, o in enumerate(others, start=1):
        ahead = (o > sel) | ((o == sel) & (pos >= r))
        rank = rank + ahead.astype(jnp.int32)
    chosen = (group == gidx) & (rank < 2) & valid
    w = jnp.where(chosen, scores, 0.0)
    w = w / jnp.sum(w, axis=-1, keepdims=True)
    return jnp.where(lane == GROUP_LANE, gidx, w)


def _merge_kernel(x_ref, o_ref, gs_ref, olat_ref, wuv_ref, wo_ref, hg_ref, mg_ref,
                  g1_ref, b1_ref, wr_ref, rb_ref, x1g_ref, *, alpha):
    _, _, tq, _ = olat_ref.shape
    tm = x_ref.shape[0]
    hg = hg_ref[...]
    d_a = HA_HEADS * HA_DV
    halves = [(i * tm // 2, (i + 1) * tm // 2) for i in range(2)]

    def lat(h, lo, hi):
        if tq >= hi - lo:
            return olat_ref[lo // tq, h, lo % tq:lo % tq + hi - lo]
        return jnp.concatenate([olat_ref[s, h] for s in range(lo // tq, hi // tq)], axis=0)

    ob = [[_dot(lat(h, lo, hi), wuv_ref[h]) for h in range(HB_HEADS)] for lo, hi in halves]
    oa = [jnp.concatenate([_rms(o_ref[h, lo:hi, :], hg) * gs_ref[lo:hi, h * HA_DV:(h + 1) * HA_DV]
                           for h in range(HA_HEADS)], axis=-1) for lo, hi in halves]
    ob = [_rms(jnp.concatenate(o, axis=-1), mg_ref[...]) for o in ob]
    mix = [_dot(a.astype(BF16), wo_ref[:d_a, :]) + _dot(o.astype(BF16), wo_ref[d_a:, :])
           for a, o in zip(oa, ob)]
    x1 = [_layer_norm(alpha * x_ref[lo:hi, :] + m, g1_ref[...], b1_ref[...])
          for (lo, hi), m in zip(halves, mix)]
    logits = [_dot(x.astype(BF16), wr_ref[...]) for x in x1]
    for (lo, hi), x, lg in zip(halves, x1, logits):
        x1g_ref[lo:hi, :D_MODEL] = x
        x1g_ref[lo:hi, D_MODEL:] = _gates(lg, rb_ref[...])


def _merge(x, o_raw, gs, olat, wuv, wo, hg, mg, g1, b1, wr, rb, *, tm, tq, alpha):
    n = x.shape[0]
    row = lambda w: pl.BlockSpec((tm, w), lambda i: (i, 0))
    return pl.pallas_call(
        functools.partial(_merge_kernel, alpha=alpha),
        grid=(n // tm,),
        in_specs=[row(D_MODEL), pl.BlockSpec((HA_HEADS, tm, HA_DV), lambda i: (0, i, 0)), row(HA_W),
                  pl.BlockSpec((tm // tq, HB_HEADS, tq, KV_LORA), lambda i: (i, 0, 0, 0)),
                  _full(wuv.shape), _full(wo.shape), _full(hg.shape), _full(mg.shape),
                  _full(g1.shape), _full(b1.shape), _full(wr.shape), _full(rb.shape)],
        out_specs=row(ROW_W),
        out_shape=jax.ShapeDtypeStruct((n, ROW_W), F32),
        compiler_params=_cparams(("parallel",)),
        name="merge",
    )(x, o_raw, gs, olat, wuv, wo, hg, mg, g1, b1, wr, rb)


def _gather_kernel(idx_ref, src_ref, out_ref, sem):
    rows = out_ref.shape[0]

    def issue(i, carry):
        for u in range(GATHER_UNROLL):
            r = i * GATHER_UNROLL + u
            pltpu.make_async_copy(src_ref.at[pl.ds(idx_ref[0, 0, r], 1), :],
                                  out_ref.at[pl.ds(r, 1), :], sem).start(priority=u % 2)
        return carry

    lax.fori_loop(0, rows // GATHER_UNROLL, issue, 0)
    pltpu.make_async_copy(src_ref.at[pl.ds(0, rows), :], out_ref, sem).wait()


def _gather_rows(src, idx, *, rows):
    m = idx.shape[0]
    w = src.shape[1]
    return pl.pallas_call(
        _gather_kernel,
        grid=(m // rows,),
        in_specs=[pl.BlockSpec((1, 1, rows), lambda i: (i, 0, 0), memory_space=pltpu.SMEM),
                  pl.BlockSpec(memory_space=pl.ANY)],
        out_specs=pl.BlockSpec((rows, w), lambda i: (i, 0)),
        out_shape=jax.ShapeDtypeStruct((m, w), F32),
        scratch_shapes=[pltpu.SemaphoreType.DMA(())],
        compiler_params=_cparams(("arbitrary",)),
        name="gather_rows",
    )(idx.reshape(m // rows, 1, rows), src)


def _scatter_kernel(pads_ref, pos_ref, x_ref, out_ref, sem):
    rows = x_ref.shape[0]
    first = pl.program_id(0) == 0

    def issue(i, carry):
        for u in range(GATHER_UNROLL):
            r = i * GATHER_UNROLL + u
            pltpu.make_async_copy(x_ref.at[pl.ds(r, 1), :],
                                  out_ref.at[pl.ds(pos_ref[0, 0, r], 1), :], sem).start(priority=u % 2)
        return carry

    lax.fori_loop(0, rows // GATHER_UNROLL, issue, 0)

    def fill_copy(dst_row):
        return pltpu.make_async_copy(x_ref.at[pl.ds(0, 1), :], out_ref.at[pl.ds(dst_row, 1), :], sem)

    @pl.when(first)
    def _():
        for g in range(N_GROUPS):
            start = pads_ref[g]
            lax.fori_loop(0, pads_ref[N_GROUPS + g],
                          lambda r, c: (fill_copy(start + r).start(), c)[1], 0)

    pltpu.make_async_copy(x_ref, out_ref.at[pl.ds(0, rows), :], sem).wait()

    @pl.when(first)
    def _():
        for g in range(N_GROUPS):
            lax.fori_loop(0, pads_ref[N_GROUPS + g], lambda r, c: (fill_copy(0).wait(), c)[1], 0)


def _scatter_rows(x, pos, pads, *, rows, m):
    n, w = x.shape
    return pl.pallas_call(
        _scatter_kernel,
        grid_spec=pltpu.PrefetchScalarGridSpec(
            num_scalar_prefetch=1,
            grid=(n // rows,),
            in_specs=[pl.BlockSpec((1, 1, rows), lambda i, pads: (i, 0, 0), memory_space=pltpu.SMEM),
                      pl.BlockSpec((rows, w), lambda i, pads: (i, 0))],
            out_specs=pl.BlockSpec(memory_space=pl.ANY),
            scratch_shapes=[pltpu.SemaphoreType.DMA(())],
        ),
        out_shape=jax.ShapeDtypeStruct((m, w), F32),
        compiler_params=_cparams(("arbitrary",)),
        name="scatter_rows",
    )(pads, pos.reshape(n // rows, 1, rows), x)


def _moe_kernel(tg_ref, xs_ref, wg_ref, wu_ref, wd_ref, y_ref):
    tm = xs_ref.shape[0]
    g = tg_ref[pl.program_id(0)]
    xb = xs_ref[:, :D_MODEL].astype(BF16)
    gates = xs_ref[:, D_MODEL:]
    lane = lax.broadcasted_iota(jnp.int32, (tm, LANES), 1)
    acc = jnp.zeros((tm, D_MODEL), F32)
    for k in range(EXPERTS_PER_GROUP):
        hmid = _silu(_dot(xb, wg_ref[k])) * _dot(xb, wu_ref[k])
        gcol = jnp.sum(jnp.where(lane == g * EXPERTS_PER_GROUP + k, gates, 0.0),
                       axis=-1, keepdims=True)
        acc = acc + gcol * _dot(hmid.astype(BF16), wd_ref[k])
    y_ref[...] = acc


def _moe_sorted(tile_group, xs, wg, wu, wd, *, tm):
    m = xs.shape[0]
    wspec = lambda shape: pl.BlockSpec((EXPERTS_PER_GROUP,) + shape, lambda i, tg: (tg[i], 0, 0))
    return pl.pallas_call(
        _moe_kernel,
        grid_spec=pltpu.PrefetchScalarGridSpec(
            num_scalar_prefetch=1,
            grid=(m // tm,),
            in_specs=[pl.BlockSpec((tm, ROW_W), lambda i, tg: (i, 0)),
                      wspec((D_MODEL, D_EXPERT)), wspec((D_MODEL, D_EXPERT)),
                      wspec((D_EXPERT, D_MODEL))],
            out_specs=pl.BlockSpec((tm, D_MODEL), lambda i, tg: (i, 0)),
        ),
        out_shape=jax.ShapeDtypeStruct((m, D_MODEL), F32),
        compiler_params=_cparams(("arbitrary",)),
        name="moe",
    )(tile_group, xs, wg, wu, wd)


def _route(x1g, *, tm):
    n = x1g.shape[0]
    n_tiles = n // tm + N_GROUPS - 1
    gidx = x1g[:, D_MODEL + GROUP_LANE].astype(jnp.int32)
    onehot = (gidx[:, None] == jnp.arange(N_GROUPS, dtype=jnp.int32)[None, :]).astype(jnp.int32)
    counts = jnp.sum(onehot, axis=0)
    tiles = (counts + tm - 1) // tm
    tile_end = jnp.cumsum(tiles)
    tile_start = tile_end - tiles
    rank = jnp.sum((jnp.cumsum(onehot, axis=0) - onehot) * onehot, axis=1)
    pos = jnp.take(tile_start, gidx) * tm + rank
    pad_start = tile_start * tm + counts
    pad_end = (tile_end * tm).at[N_GROUPS - 1].set(n_tiles * tm)
    pads = jnp.concatenate([pad_start, pad_end - pad_start]).astype(jnp.int32)
    tile_group = jnp.searchsorted(tile_end, jnp.arange(n_tiles, dtype=jnp.int32), side="right")
    return pos, pads, jnp.minimum(tile_group, N_GROUPS - 1).astype(jnp.int32), n_tiles


def _post_kernel(x1_ref, f_ref, p_ref, g2_ref, b2_ref, wpg_ref, wp_ref, y_ref, *, alpha):
    tm = x1_ref.shape[0]
    halves = [(i * tm // 2, (i + 1) * tm // 2) for i in range(2)]
    emb = [_dot(p_ref[lo:hi, :].astype(BF16), wp_ref[...]) for lo, hi in halves]
    x2 = [_layer_norm(alpha * x1_ref[lo:hi, :] + f_ref[lo:hi, :], g2_ref[...], b2_ref[...])
          for lo, hi in halves]
    gate = [_dot(x.astype(BF16), wpg_ref[...]) for x in x2]
    for (lo, hi), x, g, e in zip(halves, x2, gate, emb):
        y_ref[lo:hi, :] = x + jax.nn.sigmoid(g) * e


def _post(x1g, ffn, p_all, layer, g2, b2, wpg, wp, *, tm, alpha):
    n = ffn.shape[0]
    row = lambda w: pl.BlockSpec((tm, w), lambda i: (i, 0))
    return pl.pallas_call(
        functools.partial(_post_kernel, alpha=alpha),
        grid=(n // tm,),
        in_specs=[row(D_MODEL), row(D_MODEL),
                  pl.BlockSpec((None, tm, PLE_DIM), lambda i: (layer, i, 0)),
                  _full(g2.shape), _full(b2.shape), _full(wpg.shape), _full(wp.shape)],
        out_specs=row(D_MODEL),
        out_shape=jax.ShapeDtypeStruct((n, D_MODEL), F32),
        compiler_params=_cparams(("parallel",)),
        name="post",
    )(x1g, ffn, p_all, g2, b2, wpg, wp)


def _rot_cols(w):
    half = ROPE_DIM // 2
    return jnp.concatenate([-w[..., half:], w[..., :half]], axis=-1)


def _prep_layer(l, w_in, w_qb, w_kvb, w_o, w_gate, w_up, w_down, w_ple, w_ple_gate):
    win = w_in[l]
    win_aug = jnp.concatenate([win, _rot_cols(win[:, COL_KPE:])], axis=-1).astype(BF16)
    wqb = w_qb[l].reshape(Q_LORA, HB_HEADS, NOPE_DIM + ROPE_DIM)
    wqb_aug = jnp.concatenate([wqb, _rot_cols(wqb[..., NOPE_DIM:])], axis=-1)
    wqb_aug = wqb_aug.reshape(Q_LORA, HB_HEADS * Q_HEAD_AUG).astype(BF16)
    wkvb = w_kvb[l].reshape(KV_LORA, HB_HEADS, NOPE_DIM + V_DIM)
    wuk_t = jnp.transpose(wkvb[..., :NOPE_DIM], (1, 2, 0)).astype(BF16)
    wuv = jnp.transpose(wkvb[..., NOPE_DIM:], (1, 0, 2)).astype(BF16)
    return dict(win=win_aug, wqb=wqb_aug, wuk=wuk_t, wuv=wuv, wo=w_o[l].astype(BF16),
                wg=w_gate[l].astype(BF16), wu=w_up[l].astype(BF16), wd=w_down[l].astype(BF16),
                wp=w_ple[l].astype(BF16), wpg=w_ple_gate[l].astype(BF16))


def _rope_table(pos):
    inv = ROPE_THETA ** (-np.arange(0, ROPE_DIM, 2, dtype=np.float64) / ROPE_DIM)
    ang = pos.astype(np.float64)[:, None] * inv[None, :]
    cos, sin = np.cos(ang), np.sin(ang)
    return np.concatenate([cos, cos, sin, sin], axis=-1).astype(np.float32)


def _tiles(n, t, prompt):
    if prompt:
        tm = min(512, t)
        return dict(tm=tm, tq=tm, tk=tm, tb=tm, cb=min(128, t), bb=2, tmoe=min(512, n))
    return dict(tm=min(512, n), tq=t, tk=None, tb=t, cb=t, bb=2, tmoe=min(256, n))


def _layer(x, p_all, layer, cs, s0_all, past, lbp, prm, small, *, alpha, prompt):
    bsz, t, _ = x.shape
    n = bsz * t
    tl = _tiles(n, t, prompt)
    x2d = x.reshape(n, D_MODEL)
    tm, tb, cb = tl["tm"], tl["tb"], tl["cb"]
    outs = _inproj(x2d, cs, prm["win"], lbp, small["qg"], prm["wqb"], small["kvg"], prm["wuk"],
                   tm=tm, tq=tl["tq"], cb=cb, prompt=prompt)
    qs, kin, logf, v, gs, ckv, kpe = outs[:7]
    r4 = lambda a: a.reshape(HA_HEADS, bsz, t, HA_DK)
    if prompt:
        qt, vt, kcat, ckvt = outs[7:]
        vt = vt.reshape(bsz, t // tb, HA_HEADS, tb // cb, HA_DV, cb)
    else:
        qabs, qpe = outs[7:]
        vt = jnp.transpose(r4(v), (1, 0, 3, 2)).astype(BF16)
        vt = vt.reshape(bsz, 1, HA_HEADS, 1, HA_DV, t)
    o_raw, s_new = _hgrn(r4(qs), r4(kin), r4(logf), r4(v), vt, s0_all, layer if not prompt else 0,
                         bb=tl["bb"], tb=tb, cb=cb)
    if prompt:
        olat = _attn_prompt(qt, kcat, ckvt, bsz=bsz, t=t, tq=tl["tq"], tk=tl["tk"])
    else:
        olat = _attn_sample(qabs, qpe, past[0], past[1], layer, ckv, kpe, ts=t)
    x1g = _merge(x2d, o_raw.reshape(HA_HEADS, n, HA_DV), gs, olat, prm["wuv"], prm["wo"],
                 small["hg"], small["mg"], small["g1"], small["b1"],
                 small["wr"], small["rb"], tm=tm, tq=tl["tq"], alpha=alpha)
    tmoe = tl["tmoe"]
    pos, pads, tile_group, n_tiles = _route(x1g, tm=tmoe)
    xs = _scatter_rows(x1g, pos, pads, rows=tmoe, m=n_tiles * tmoe)
    ys = _moe_sorted(tile_group, xs, prm["wg"], prm["wu"], prm["wd"], tm=tmoe)
    ffn = _gather_rows(ys, pos, rows=tmoe)
    y = _post(x1g, ffn, p_all.reshape(-1, n, PLE_DIM), layer, small["g2"], small["b2"],
              prm["wpg"], prm["wp"], tm=tm, alpha=alpha)
    return (y.reshape(bsz, t, D_MODEL), s_new, ckv.reshape(bsz, t, KV_LORA),
            kpe.reshape(bsz, t, ROPE_DIM))


def kernel(x_prompt, x_sample, p_prompt, p_sample, state_hgrn, cache_ckv, cache_kpe, w_in,
           lb_logits, hgrn_norm_g, q_norm_g, w_qb, kv_norm_g, w_kvb, mla_norm_g, w_o,
           ln1_g, ln1_b, ln2_g, ln2_b, w_router, router_bias, w_gate, w_up, w_down,
           w_ple, w_ple_gate):
    depth = w_in.shape[0]
    alpha = (2 * depth) ** 0.25
    bp, tp, _ = x_prompt.shape
    bs, ts, _ = x_sample.shape
    past = cache_ckv.shape[2]

    sm = jax.nn.softmax(lb_logits.astype(F32), axis=0)
    lb_all = jnp.maximum(jnp.cumsum(sm, axis=0) - sm[0:1], 0.0)
    lbp_all = jnp.stack([jnp.log(lb_all), jnp.log1p(-lb_all), 1.0 - lb_all], axis=1)

    cs_p = jnp.asarray(_rope_table(np.arange(tp)))
    tm_s = _tiles(bs * ts, ts, False)["tm"]
    cs_s = jnp.asarray(np.tile(_rope_table(past + np.arange(ts)), (tm_s // ts, 1)))
    wr = jnp.pad(w_router, ((0, 0), (0, LANES - N_EXPERTS))).astype(BF16)
    rb = jnp.pad(router_bias.astype(F32), (0, LANES - N_EXPERTS)).reshape(1, LANES)
    s0_p = jnp.zeros((1, bp, HA_HEADS, HA_DK, HA_DV), F32)
    cache_kpe_t = jnp.swapaxes(cache_kpe, 2, 3)

    yp, ys = x_prompt, x_sample
    res = [[] for _ in range(6)]
    for l in range(depth):
        prm = _prep_layer(l, w_in, w_qb, w_kvb, w_o, w_gate, w_up, w_down, w_ple, w_ple_gate)
        row = lambda a: a[l].reshape(1, -1).astype(F32)
        small = dict(qg=row(q_norm_g), kvg=row(kv_norm_g), hg=row(hgrn_norm_g), mg=row(mla_norm_g),
                     g1=row(ln1_g), b1=row(ln1_b), g2=row(ln2_g), b2=row(ln2_b), wr=wr, rb=rb)
        yp, sp, cp, kp = _layer(yp, p_prompt, l, cs_p, s0_p, None, lbp_all[l], prm, small,
                                alpha=alpha, prompt=True)
        ys, ss, cs_, ks = _layer(ys, p_sample, l, cs_s, state_hgrn, (cache_ckv, cache_kpe_t),
                                 lbp_all[l], prm, small, alpha=alpha, prompt=False)
        for lst, a in zip(res, (sp, cp, kp, ss, cs_, ks)):
            lst.append(a)
    sp, cp, kp, ss, cs_, ks = (jnp.stack(a) for a in res)
    return (yp, ys, sp, cp, kp, ss, cs_, ks)
```

```python
import functools

import jax
import jax.numpy as jnp
import numpy as np
from jax import lax
from jax.experimental import pallas as pl
from jax.experimental.pallas import tpu as pltpu
from jax.experimental.pallas import tpu_sc as plsc

F32 = jnp.float32
BF16 = jnp.bfloat16

D_MODEL = 1024
HA_HEADS = 4
HA_DK = 128
HA_DV = 128
HB_HEADS = 4
Q_LORA = 384
KV_LORA = 256
NOPE_DIM = 128
ROPE_DIM = 64
V_DIM = 128
ROPE_THETA = 10000.0
MLA_SCALE = (NOPE_DIM + ROPE_DIM) ** -0.5
LOG2E = 1.4426950408889634
Q_SCALE = MLA_SCALE * LOG2E
CHUNK = 64
N_EXPERTS = 16
N_GROUPS = 4
EXPERTS_PER_GROUP = N_EXPERTS // N_GROUPS
D_EXPERT = 512
PLE_DIM = 256
NEG_INF = -1e30

HA_W = HA_HEADS * HA_DK
COL_CQ = 4 * HA_W
COL_CKV = COL_CQ + Q_LORA
COL_KPE = COL_CKV + KV_LORA
D_IN_AUG = COL_KPE + 2 * ROPE_DIM
Q_HEAD_AUG = NOPE_DIM + 2 * ROPE_DIM

LANES = 128
SUBLANES = 8
QK_DIM = KV_LORA + LANES
ROW_W = D_MODEL + LANES
GROUP_LANE = N_EXPERTS
GATHER_UNROLL = 8
SC_WINDOW = 128
SC_ROW = 256
SCORE_W = 512
SUB_BLOCK = 16
VMEM_LIMIT = 56 * 1024 * 1024


def _cparams(sem, vmem=VMEM_LIMIT):
    return pltpu.CompilerParams(dimension_semantics=sem, vmem_limit_bytes=vmem)


def _dot(a, b):
    return jnp.dot(a, b, preferred_element_type=F32)


def _dot_nt(a, b):
    return lax.dot_general(a, b, (((1,), (1,)), ((), ())), preferred_element_type=F32)


def _rms(x, g, eps=1e-6):
    return x * lax.rsqrt(jnp.mean(x * x, axis=-1, keepdims=True) + eps) * g


def _layer_norm(x, g, b, eps=1e-5):
    mu = jnp.mean(x, axis=-1, keepdims=True)
    xc = x - mu
    var = jnp.mean(xc * xc, axis=-1, keepdims=True)
    return xc * lax.rsqrt(var + eps) * g + b


def _lane_tile(x, n):
    return jnp.concatenate([x] * n, axis=1)


def _silu(x):
    return x * jax.nn.sigmoid(x)


def _full(shape):
    nd = len(shape)
    return pl.BlockSpec(shape, lambda *_: (0,) * nd)


def _inproj_kernel(x_ref, cs_ref, win_ref, lb_ref, qg_ref, wqb_ref, kvg_ref, wuk_ref,
                   qs_ref, kin_ref, logf_ref, v_ref, gs_ref, ckv_ref, kpe_ref, *rest,
                   tq, cb, prompt):
    tm = x_ref.shape[0]
    x = x_ref[...].astype(BF16)

    def sect(lo, hi):
        return _dot(x, win_ref[:, lo:hi])

    def put_heads(ref, val):
        for h in range(HA_HEADS):
            ref[h] = val[:, h * HA_DK:(h + 1) * HA_DK]

    cqn = _rms(sect(COL_CQ, COL_CKV), qg_ref[...]).astype(BF16)

    put_heads(qs_ref, _silu(sect(0, HA_W)))
    fa = sect(HA_W, 2 * HA_W)
    log_lb = lb_ref[0:1, :]
    log1m_lb = lb_ref[1:2, :]
    one_m_lb = lb_ref[2:3, :]
    e = jnp.exp(-jnp.abs(fa))
    r = 1.0 / (1.0 + e)
    c = log1m_lb + jnp.minimum(fa, 0.0) + jnp.log(r)
    put_heads(logf_ref, jnp.maximum(log_lb, c) + jnp.log(1.0 + jnp.exp(-jnp.abs(log_lb - c))))
    put_heads(kin_ref, one_m_lb * jnp.where(fa >= 0.0, e * r, r))
    v = sect(2 * HA_W, 3 * HA_W)
    put_heads(v_ref, v)
    gs_ref[...] = _silu(sect(3 * HA_W, 4 * HA_W))

    cs = cs_ref[...]

    def rope(t):
        prod = t * cs
        return prod + pltpu.roll(prod, ROPE_DIM, 1)

    ckv = _rms(sect(COL_CKV, COL_KPE), kvg_ref[...])
    ckv_ref[...] = ckv
    kpe2 = rope(sect(COL_KPE, D_IN_AUG))
    kpe_ref[...] = kpe2[:, :ROPE_DIM]
    if prompt:
        qt_ref, vt_ref, kcat_ref, ckvt_ref = rest
        for h in range(HA_HEADS):
            for ci in range(tm // cb):
                vt_ref[0, h, ci] = v[ci * cb:(ci + 1) * cb, h * HA_DV:(h + 1) * HA_DV].T.astype(BF16)
        kcat_ref[:, :KV_LORA] = ckv.astype(BF16)
        kcat_ref[:, KV_LORA:] = kpe2.astype(BF16)
        ckvt_ref[0] = ckv.T.astype(BF16)
    else:
        qabs_ref, qpe_ref = rest

    lane = lax.broadcasted_iota(jnp.int32, (tm, LANES), 1)
    qhs = [_dot(cqn, wqb_ref[:, h * Q_HEAD_AUG:(h + 1) * Q_HEAD_AUG]) for h in range(HB_HEADS)]
    qabss = [_dot(qhs[h][:, :NOPE_DIM].astype(BF16), wuk_ref[h]) for h in range(HB_HEADS)]
    for h in range(HB_HEADS):
        qabs = qabss[h] * Q_SCALE
        qpe = jnp.where(lane < ROPE_DIM, rope(qhs[h][:, NOPE_DIM:]) * Q_SCALE, 0.0)
        if prompt:
            qabs_t = qabs.T.astype(BF16)
            qpe_t = qpe.T.astype(BF16)
            for s in range(tm // tq):
                qt_ref[s, :KV_LORA, h * tq:(h + 1) * tq] = qabs_t[:, s * tq:(s + 1) * tq]
                qt_ref[s, KV_LORA:, h * tq:(h + 1) * tq] = qpe_t[:, s * tq:(s + 1) * tq]
        else:
            for s in range(tm // tq):
                qabs_ref[s, h] = qabs[s * tq:(s + 1) * tq].astype(BF16)
                qpe_ref[s, h] = qpe[s * tq:(s + 1) * tq].astype(BF16)


def _inproj(x, cs, w_in, lbp, qg, wqb, kvg, wuk, *, tm, tq, cb, prompt):
    n = x.shape[0]
    nt = n // tm
    row = lambda w: pl.BlockSpec((tm, w), lambda i: (i, 0))
    heads = pl.BlockSpec((HA_HEADS, tm, HA_DK), lambda i: (0, i, 0))
    qblk = lambda w: pl.BlockSpec((tm // tq, HB_HEADS, tq, w), lambda i: (i, 0, 0, 0))
    head_major = jax.ShapeDtypeStruct((HA_HEADS, n, HA_DK), F32)
    out_shape = [
        head_major,
        head_major,
        head_major,
        head_major,
        jax.ShapeDtypeStruct((n, HA_W), F32),
        jax.ShapeDtypeStruct((n, KV_LORA), F32),
        jax.ShapeDtypeStruct((n, ROPE_DIM), F32),
    ]
    out_specs = [heads, heads, heads, heads, row(HA_W), row(KV_LORA), row(ROPE_DIM)]
    if prompt:
        out_shape += [jax.ShapeDtypeStruct((n // tq, QK_DIM, HB_HEADS * tq), BF16),
                      jax.ShapeDtypeStruct((nt, HA_HEADS, tm // cb, HA_DV, cb), BF16),
                      jax.ShapeDtypeStruct((n, QK_DIM), BF16),
                      jax.ShapeDtypeStruct((nt, KV_LORA, tm), BF16)]
        out_specs += [pl.BlockSpec((tm // tq, QK_DIM, HB_HEADS * tq), lambda i: (i, 0, 0)),
                      pl.BlockSpec((1, HA_HEADS, tm // cb, HA_DV, cb), lambda i: (i, 0, 0, 0, 0)),
                      row(QK_DIM),
                      pl.BlockSpec((1, KV_LORA, tm), lambda i: (i, 0, 0))]
    else:
        out_shape += [jax.ShapeDtypeStruct((n // tq, HB_HEADS, tq, KV_LORA), BF16),
                      jax.ShapeDtypeStruct((n // tq, HB_HEADS, tq, LANES), BF16)]
        out_specs += [qblk(KV_LORA), qblk(LANES)]
    return pl.pallas_call(
        functools.partial(_inproj_kernel, tq=tq, cb=cb, prompt=prompt),
        grid=(nt,),
        in_specs=[row(D_MODEL),
                  pl.BlockSpec((tm, LANES), lambda i: (i % (cs.shape[0] // tm), 0)),
                  _full(w_in.shape), _full(lbp.shape), _full(qg.shape),
                  _full(wqb.shape), _full(kvg.shape), _full(wuk.shape)],
        out_specs=out_specs,
        out_shape=out_shape,
        compiler_params=_cparams(("parallel",)),
        name="inproj",
    )(x, cs, w_in, lbp, qg, wqb, kvg, wuk)


def _hgrn_kernel(q_ref, k_ref, g_ref, v_ref, vt_ref, s0_ref, o_ref, sfin_ref, st_scr, b2_scr,
                 hide_ref, *, cb):
    _, bb, tb, _ = q_ref.shape
    nsub = cb // SUB_BLOCK
    ti = pl.program_id(1)

    @pl.when(ti == 0)
    def _():
        for b in range(bb):
            for h in range(HA_HEADS):
                st_scr[b, h] = s0_ref[b, h].T
        row8 = lax.broadcasted_iota(jnp.int32, (SUBLANES, LANES), 0)
        for s in range(SUBLANES):
            hide_ref[s] = jnp.where(row8 >= s, 0.0, NEG_INF)

    r_i = lax.broadcasted_iota(jnp.int32, (cb, cb), 0)
    c_i = lax.broadcasted_iota(jnp.int32, (cb, cb), 1)
    tril = (r_i >= c_i).astype(F32)

    chains = [(b, h) for b in range(bb) for h in range(HA_HEADS)]

    def matmul_part(ci):
        r0 = pl.multiple_of(ci * cb, cb)
        rows = pl.ds(r0, cb)

        def sub(ref, b, h, lo, size):
            return ref[h, b, pl.ds(pl.multiple_of(r0 + lo, SUB_BLOCK), size), :]

        for b, h in chains:
            bcum = jnp.dot(tril, g_ref[h, b, rows, :], precision=lax.Precision.HIGHEST,
                           preferred_element_type=F32)
            b2_scr[b * HA_HEADS + h] = bcum * LOG2E
        for b, h in chains:
            b2 = b2_scr[b * HA_HEADS + h]
            o_ref[h, b, rows, :] = _dot_nt((q_ref[h, b, rows, :] * jnp.exp2(b2)).astype(BF16),
                                           st_scr[b, h].astype(BF16))
        for j in range(nsub - 1):
            lo, hi = j * SUB_BLOCK, (j + 1) * SUB_BLOCK
            below = pl.ds(pl.multiple_of(r0 + hi, SUB_BLOCK), cb - hi)
            for b, h in chains:
                slot = b * HA_HEADS + h
                bnd = b2_scr[slot, hi - 1:hi, :]
                kd = (sub(k_ref, b, h, lo, SUB_BLOCK)
                      * jnp.exp2(bnd - b2_scr[slot, lo:hi, :])).astype(BF16)
                qe = (sub(q_ref, b, h, hi, cb - hi)
                      * jnp.exp2(b2_scr[slot, hi:, :] - bnd)).astype(BF16)
                a = _dot_nt(qe, kd).astype(BF16)
                o_ref[h, b, below, :] += _dot(a, sub(v_ref, b, h, lo, SUB_BLOCK).astype(BF16))
        for b, h in chains:
            slot = b * HA_HEADS + h
            b_last = b2_scr[slot, cb - 1:cb, :]
            kd_all = (k_ref[h, b, rows, :] * jnp.exp2(b_last - b2_scr[slot])).astype(BF16)
            st_scr[b, h] = st_scr[b, h] * jnp.exp2(b_last) + _dot(vt_ref[b, h, ci], kd_all)

    def pairwise_part(ci, b, h, j):
        slot = b * HA_HEADS + h
        r0 = pl.multiple_of(ci * cb, cb)
        lo = j * SUB_BLOCK
        mid = lo + SUBLANES
        top = pl.ds(pl.multiple_of(r0 + lo, SUBLANES), SUBLANES)
        bot = pl.ds(pl.multiple_of(r0 + mid, SUBLANES), SUBLANES)
        q_top, q_bot = q_ref[h, b, top, :], q_ref[h, b, bot, :]
        b_top, b_bot = b2_scr[slot, lo:mid, :], b2_scr[slot, mid:mid + SUBLANES, :]
        acc_top, acc_bot = o_ref[h, b, top, :], o_ref[h, b, bot, :]
        for s in range(SUB_BLOCK):
            row = pl.ds(r0 + lo + s, 1)
            bs = b2_scr[slot, lo + s:lo + s + 1, :]
            ks = k_ref[h, b, row, :]
            vs = v_ref[h, b, row, :]
            if s < SUBLANES:
                w = q_top * jnp.exp2(b_top - bs + hide_ref[s]) * ks
                acc_top = acc_top + jnp.sum(w, axis=-1, keepdims=True) * vs
                w = q_bot * jnp.exp2(b_bot - bs) * ks
            else:
                w = q_bot * jnp.exp2(b_bot - bs + hide_ref[s - SUBLANES]) * ks
            acc_bot = acc_bot + jnp.sum(w, axis=-1, keepdims=True) * vs
        o_ref[h, b, top, :] = acc_top
        o_ref[h, b, bot, :] = acc_bot

    def chunk(ci, carry):
        matmul_part(ci)
        for j in range(nsub):
            for b, h in chains:
                pairwise_part(ci, b, h, j)
        return carry

    lax.fori_loop(0, tb // cb, chunk, 0)

    @pl.when(ti == pl.num_programs(1) - 1)
    def _():
        for b in range(bb):
            for h in range(HA_HEADS):
                sfin_ref[b, h] = st_scr[b, h].T


def _hgrn(qs, kin, logf, v, vt, s0_all, layer, *, bb, tb, cb):
    _, bsz, t, _ = qs.shape
    blk = pl.BlockSpec((HA_HEADS, bb, tb, HA_DK), lambda i, j: (0, i, j, 0))
    st_blk = pl.BlockSpec((bb, HA_HEADS, HA_DK, HA_DV), lambda i, j: (i, 0, 0, 0))
    s0_blk = pl.BlockSpec((None, bb, HA_HEADS, HA_DK, HA_DV), lambda i, j: (layer, i, 0, 0, 0))
    vt_blk = pl.BlockSpec((bb, None, HA_HEADS, tb // cb, HA_DV, cb),
                          lambda i, j: (i, j, 0, 0, 0, 0))
    return pl.pallas_call(
        functools.partial(_hgrn_kernel, cb=cb),
        grid=(bsz // bb, t // tb),
        in_specs=[blk, blk, blk, blk, vt_blk, s0_blk],
        out_specs=[blk, st_blk],
        out_shape=[jax.ShapeDtypeStruct(qs.shape, F32),
                   jax.ShapeDtypeStruct(s0_all.shape[1:], F32)],
        scratch_shapes=[pltpu.VMEM((bb, HA_HEADS, HA_DV, HA_DK), F32),
                        pltpu.VMEM((bb * HA_HEADS, cb, HA_DK), F32),
                        pltpu.VMEM((SUBLANES, SUBLANES, LANES), F32)],
        compiler_params=_cparams(("parallel", "arbitrary")),
        name="hgrn",
    )(qs, kin, logf, v, vt, s0_all)


def _attn_prompt_kernel(qt_ref, kc_ref, kt_ref, o_ref, s0, s1, p0, p1, a0, a1, m_scr, l_scr,
                        acc_scr, *, tq, tk):
    qi = pl.program_id(1)
    jd = (qi * tq) // tk
    shift = CHUNK.bit_length() - 1

    heads = range(HB_HEADS)

    def scores(j, s_dst, hs=heads):
        keys = kc_ref[pl.ds(pl.multiple_of(j * tk, tk), tk), :]
        for h in hs:
            cols = slice(h * tq, (h + 1) * tq)
            s_dst[:, cols] = _dot(keys, qt_ref[0, :, cols])

    def softmax(j, s_src, p_dst, a_dst, masked, hs=heads):
        if masked:
            kpos = j * tk + lax.broadcasted_iota(jnp.int32, (tk, 1), 0)
            qpos = qi * tq + lax.broadcasted_iota(jnp.int32, (1, tq), 1)
            visible = (kpos >> shift) <= (qpos >> shift)
        for h in hs:
            cols = slice(h * tq, (h + 1) * tq)
            def load():
                s = s_src[:, cols]
                return jnp.where(visible, s, NEG_INF) if masked else s

            m_prev = m_scr[:, cols]
            m_new = jnp.maximum(m_prev, jnp.max(load(), axis=0, keepdims=True))
            alpha = jnp.exp2(m_prev - m_new)
            p = jnp.exp2(load() - m_new)
            l_scr[:, cols] = alpha * l_scr[:, cols] + jnp.sum(p, axis=0, keepdims=True)
            m_scr[:, cols] = m_new
            a_dst[:, cols] = alpha
            p_dst[:, cols] = p.astype(BF16)

    def values(j, p_src, a_src, hs=heads):
        for h in hs:
            cols = slice(h * tq, (h + 1) * tq)
            acc_scr[:, cols] = (a_src[:, cols] * acc_scr[:, cols]
                                + _dot(kt_ref[j], p_src[:, cols]))

    def trip(j, s_cur, s_nxt, p_cur, p_prv, a_cur, a_prv):
        scores(j + 1, s_nxt)
        softmax(j, s_cur, p_cur, a_cur, False)
        values(jnp.maximum(j - 1, 0), p_prv, a_prv)

    m_scr[...] = jnp.full(m_scr.shape, NEG_INF, F32)
    l_scr[...] = jnp.zeros(l_scr.shape, F32)
    acc_scr[...] = jnp.zeros(acc_scr.shape, F32)
    p1[...] = jnp.zeros(p1.shape, BF16)
    a1[...] = jnp.ones(a1.shape, F32)
    scores(0, s0)

    def pair(jj, carry):
        trip(2 * jj, s0, s1, p0, p1, a0, a1)
        trip(2 * jj + 1, s1, s0, p1, p0, a1, a0)
        return carry

    lax.fori_loop(0, jd // 2, pair, 0)

    @pl.when(jd % 2 == 1)
    def _():
        trip(jd - 1, s0, s1, p0, p1, a0, a1)

    def drain(s_cur, p_cur, p_prv, a_cur, a_prv):
        softmax(jd, s_cur, p_cur, a_cur, True)
        values(jnp.maximum(jd - 1, 0), p_prv, a_prv)
        values(jd, p_cur, a_cur)

    pl.when(jd % 2 == 0)(functools.partial(drain, s0, p0, p1, a0, a1))
    pl.when(jd % 2 == 1)(functools.partial(drain, s1, p1, p0, a1, a0))

    for h in range(HB_HEADS):
        cols = slice(h * tq, (h + 1) * tq)
        o_ref[0, h] = (acc_scr[:, cols] * (1.0 / l_scr[:, cols])).T.astype(BF16)


def _attn_prompt(qt, kcat, ckvt, *, bsz, t, tq, tk):
    nq = t // tq
    nk = t // tk
    cols = HB_HEADS * tq
    single = pl.Buffered(1)
    return pl.pallas_call(
        functools.partial(_attn_prompt_kernel, tq=tq, tk=tk),
        grid=(bsz, nq),
        in_specs=[
            pl.BlockSpec((1, QK_DIM, cols), lambda b, i: (b * nq + i, 0, 0)),
            pl.BlockSpec((t, QK_DIM), lambda b, i: (b, 0), pipeline_mode=single),
            pl.BlockSpec((nk, KV_LORA, tk), lambda b, i: (b, 0, 0), pipeline_mode=single),
        ],
        out_specs=pl.BlockSpec((1, HB_HEADS, tq, KV_LORA), lambda b, i: (b * nq + i, 0, 0, 0)),
        out_shape=jax.ShapeDtypeStruct((bsz * nq, HB_HEADS, tq, KV_LORA), BF16),
        scratch_shapes=[pltpu.VMEM((tk, cols), F32), pltpu.VMEM((tk, cols), F32),
                        pltpu.VMEM((tk, cols), BF16), pltpu.VMEM((tk, cols), BF16),
                        pltpu.VMEM((1, cols), F32), pltpu.VMEM((1, cols), F32),
                        pltpu.VMEM((1, cols), F32), pltpu.VMEM((1, cols), F32),
                        pltpu.VMEM((KV_LORA, cols), F32)],
        compiler_params=_cparams(("parallel", "arbitrary")),
        name="attn_prompt",
    )(qt, kcat, ckvt)


def _attn_sample_kernel(qa_ref, qp_ref, pckv_ref, pkpe_ref, nckv_ref, nkpe_ref, o_ref):
    _, _, ts, _ = qa_ref.shape
    rows = HB_HEADS * ts
    qa = qa_ref[0].reshape(rows, KV_LORA)
    qp = qp_ref[0].reshape(rows, LANES)[:, :ROPE_DIM]
    pckv = pckv_ref[0].astype(BF16)
    nckv = nckv_ref[...].astype(BF16)
    s_past = _dot_nt(qa, pckv) + _dot(qp, pkpe_ref[0].astype(BF16))
    s_new = _dot_nt(qa, nckv) + _dot_nt(qp, nkpe_ref[...].astype(BF16))
    m = jnp.maximum(jnp.max(s_past, axis=-1, keepdims=True), jnp.max(s_new, axis=-1, keepdims=True))
    p_past = jnp.exp2(s_past - m)
    p_new = jnp.exp2(s_new - m)
    l = jnp.sum(p_past, axis=-1, keepdims=True) + jnp.sum(p_new, axis=-1, keepdims=True)
    o = (_dot(p_past.astype(BF16), pckv) + _dot(p_new.astype(BF16), nckv)) / l
    o_ref[0] = o.astype(BF16).reshape(HB_HEADS, ts, KV_LORA)


def _attn_sample(qabs, qpe, past_ckv, past_kpe, layer, ckv, kpe, *, ts):
    _, bsz, past, _ = past_ckv.shape
    return pl.pallas_call(
        _attn_sample_kernel,
        grid=(bsz,),
        in_specs=[
            pl.BlockSpec((1, HB_HEADS, ts, KV_LORA), lambda b: (b, 0, 0, 0)),
            pl.BlockSpec((1, HB_HEADS, ts, LANES), lambda b: (b, 0, 0, 0)),
            pl.BlockSpec((None, 1, past, KV_LORA), lambda b: (layer, b, 0, 0)),
            pl.BlockSpec((None, 1, ROPE_DIM, past), lambda b: (layer, b, 0, 0)),
            pl.BlockSpec((ts, KV_LORA), lambda b: (b, 0)),
            pl.BlockSpec((ts, ROPE_DIM), lambda b: (b, 0)),
        ],
        out_specs=pl.BlockSpec((1, HB_HEADS, ts, KV_LORA), lambda b: (b, 0, 0, 0)),
        out_shape=jax.ShapeDtypeStruct(qabs.shape, BF16),
        compiler_params=_cparams(("parallel",)),
        name="attn_sample",
    )(qabs, qpe, past_ckv, past_kpe, ckv, kpe)


def _gates(logits, bias):
    tm = logits.shape[0]
    lane = lax.broadcasted_iota(jnp.int32, (tm, LANES), 1)
    pos = lane % EXPERTS_PER_GROUP
    valid = lane < N_EXPERTS
    scores = jax.nn.sigmoid(logits)
    sel = jnp.where(valid, scores + bias, -jnp.inf)

    others = []
    for r in range(1, EXPERTS_PER_GROUP):
        others.append(jnp.where(pos >= r, pltpu.roll(sel, r, 1),
                                pltpu.roll(sel, LANES - (EXPERTS_PER_GROUP - r), 1)))
    a, b, c, d = sel, others[0], others[1], others[2]
    hi1, lo1 = jnp.maximum(a, b), jnp.minimum(a, b)
    hi2, lo2 = jnp.maximum(c, d), jnp.minimum(c, d)
    gscore = jnp.maximum(hi1, hi2) + jnp.maximum(jnp.minimum(hi1, hi2), jnp.maximum(lo1, lo2))
    gmax = jnp.max(gscore, axis=-1, keepdims=True)
    group = (lane // EXPERTS_PER_GROUP).astype(F32)
    gidx = jnp.min(jnp.where(gscore == gmax, group, float(LANES)), axis=-1, keepdims=True)

    rank = jnp.zeros((tm, LANES), jnp.int32)
    for r, o in enumerate(others, start=1):
        ahead = (o > sel) | ((o == sel) & (pos >= r))
        rank = rank + ahead.astype(jnp.int32)
    chosen = (group == gidx) & (rank < 2) & valid
    w = jnp.where(chosen, scores, 0.0)
    w = w / jnp.sum(w, axis=-1, keepdims=True)
    return jnp.where(lane == GROUP_LANE, gidx, w)


def _merge_kernel(x_ref, o_ref, gs_ref, olat_ref, wuv_ref, wo_ref, hg_ref, mg_ref,
                  g1_ref, b1_ref, wr_ref, rb_ref, x1g_ref, *, alpha):
    _, _, tq, _ = olat_ref.shape
    tm = x_ref.shape[0]
    hg = hg_ref[...]
    d_a = HA_HEADS * HA_DV
    halves = [(i * tm // 2, (i + 1) * tm // 2) for i in range(2)]

    def lat(h, lo, hi):
        if tq >= hi - lo:
            return olat_ref[lo // tq, h, lo % tq:lo % tq + hi - lo]
        return jnp.concatenate([olat_ref[s, h] for s in range(lo // tq, hi // tq)], axis=0)

    ob = [[_dot(lat(h, lo, hi), wuv_ref[h]) for h in range(HB_HEADS)] for lo, hi in halves]
    oa = [jnp.concatenate([_rms(o_ref[h, lo:hi, :], hg) * gs_ref[lo:hi, h * HA_DV:(h + 1) * HA_DV]
                           for h in range(HA_HEADS)], axis=-1) for lo, hi in halves]
    ob = [_rms(jnp.concatenate(o, axis=-1), mg_ref[...]) for o in ob]
    mix = [_dot(a.astype(BF16), wo_ref[:d_a, :]) + _dot(o.astype(BF16), wo_ref[d_a:, :])
           for a, o in zip(oa, ob)]
    x1 = [_layer_norm(alpha * x_ref[lo:hi, :] + m, g1_ref[...], b1_ref[...])
          for (lo, hi), m in zip(halves, mix)]
    logits = [_dot(x.astype(BF16), wr_ref[...]) for x in x1]
    for (lo, hi), x, lg in zip(halves, x1, logits):
        x1g_ref[lo:hi, :D_MODEL] = x
        x1g_ref[lo:hi, D_MODEL:] = _gates(lg, rb_ref[...])


def _merge(x, o_raw, gs, olat, wuv, wo, hg, mg, g1, b1, wr, rb, *, tm, tq, alpha):
    n = x.shape[0]
    row = lambda w: pl.BlockSpec((tm, w), lambda i: (i, 0))
    return pl.pallas_call(
        functools.partial(_merge_kernel, alpha=alpha),
        grid=(n // tm,),
        in_specs=[row(D_MODEL), pl.BlockSpec((HA_HEADS, tm, HA_DV), lambda i: (0, i, 0)), row(HA_W),
                  pl.BlockSpec((tm // tq, HB_HEADS, tq, KV_LORA), lambda i: (i, 0, 0, 0)),
                  _full(wuv.shape), _full(wo.shape), _full(hg.shape), _full(mg.shape),
                  _full(g1.shape), _full(b1.shape), _full(wr.shape), _full(rb.shape)],
        out_specs=row(ROW_W),
        out_shape=jax.ShapeDtypeStruct((n, ROW_W), F32),
        compiler_params=_cparams(("parallel",)),
        name="merge",
    )(x, o_raw, gs, olat, wuv, wo, hg, mg, g1, b1, wr, rb)


def _gather_kernel(idx_ref, src_ref, out_ref, sem):
    rows = out_ref.shape[0]

    def issue(i, carry):
        for u in range(GATHER_UNROLL):
            r = i * GATHER_UNROLL + u
            pltpu.make_async_copy(src_ref.at[pl.ds(idx_ref[0, 0, r], 1), :],
                                  out_ref.at[pl.ds(r, 1), :], sem).start(priority=u % 2)
        return carry

    lax.fori_loop(0, rows // GATHER_UNROLL, issue, 0)
    pltpu.make_async_copy(src_ref.at[pl.ds(0, rows), :], out_ref, sem).wait()


def _gather_rows(src, idx, *, rows):
    m = idx.shape[0]
    w = src.shape[1]
    return pl.pallas_call(
        _gather_kernel,
        grid=(m // rows,),
        in_specs=[pl.BlockSpec((1, 1, rows), lambda i: (i, 0, 0), memory_space=pltpu.SMEM),
                  pl.BlockSpec(memory_space=pl.ANY)],
        out_specs=pl.BlockSpec((rows, w), lambda i: (i, 0)),
        out_shape=jax.ShapeDtypeStruct((m, w), F32),
        scratch_shapes=[pltpu.SemaphoreType.DMA(())],
        compiler_params=_cparams(("arbitrary",)),
        name="gather_rows",
    )(idx.reshape(m // rows, 1, rows), src)


def _sc_gather_rows(src, idx):
    parts, n, _ = src.shape
    m = idx.shape[0] * parts
    pieces = src.reshape(parts * n, SC_ROW)
    piece_idx = (jnp.arange(parts, dtype=jnp.int32)[:, None] * n + idx[None, :]).reshape(1, m)
    mesh = plsc.VectorSubcoreMesh(core_axis_name="core", subcore_axis_name="subcore")

    @functools.partial(pl.kernel, out_type=jax.ShapeDtypeStruct((m, SC_ROW), src.dtype), mesh=mesh)
    def gather(x_hbm, i_hbm, o_hbm):
        def body(i_vmem, o_vmem):
            pltpu.sync_copy(x_hbm.at[i_vmem.at[0]], o_vmem)

        pltpu.emit_pipeline(
            body,
            grid=(m // SC_WINDOW,),
            in_specs=[pl.BlockSpec((1, SC_WINDOW), lambda i: (0, i))],
            out_specs=[pl.BlockSpec((SC_WINDOW, SC_ROW), lambda i: (i, 0))],
            core_axis_name=("core", "subcore"),
            dimension_semantics=(pltpu.PARALLEL,),
        )(i_hbm, o_hbm)

    return gather(pieces, piece_idx).reshape(parts, idx.shape[0], SC_ROW)


def _scatter_kernel(pads_ref, pos_ref, x_ref, out_ref, sem):
    rows = x_ref.shape[0]
    first = pl.program_id(0) == 0

    def issue(i, carry):
        for u in range(GATHER_UNROLL):
            r = i * GATHER_UNROLL + u
            pltpu.make_async_copy(x_ref.at[pl.ds(r, 1), :],
                                  out_ref.at[pl.ds(pos_ref[0, 0, r], 1), :], sem).start(priority=u % 2)
        return carry

    lax.fori_loop(0, rows // GATHER_UNROLL, issue, 0)

    def fill_copy(dst_row):
        return pltpu.make_async_copy(x_ref.at[pl.ds(0, 1), :], out_ref.at[pl.ds(dst_row, 1), :], sem)

    @pl.when(first)
    def _():
        for g in range(N_GROUPS):
            start = pads_ref[g]
            lax.fori_loop(0, pads_ref[N_GROUPS + g],
                          lambda r, c: (fill_copy(start + r).start(), c)[1], 0)

    pltpu.make_async_copy(x_ref, out_ref.at[pl.ds(0, rows), :], sem).wait()

    @pl.when(first)
    def _():
        for g in range(N_GROUPS):
            lax.fori_loop(0, pads_ref[N_GROUPS + g], lambda r, c: (fill_copy(0).wait(), c)[1], 0)


def _scatter_rows(x, pos, pads, *, rows, m):
    n, w = x.shape
    return pl.pallas_call(
        _scatter_kernel,
        grid_spec=pltpu.PrefetchScalarGridSpec(
            num_scalar_prefetch=1,
            grid=(n // rows,),
            in_specs=[pl.BlockSpec((1, 1, rows), lambda i, pads: (i, 0, 0), memory_space=pltpu.SMEM),
                      pl.BlockSpec((rows, w), lambda i, pads: (i, 0))],
            out_specs=pl.BlockSpec(memory_space=pl.ANY),
            scratch_shapes=[pltpu.SemaphoreType.DMA(())],
        ),
        out_shape=jax.ShapeDtypeStruct((m, w), F32),
        compiler_params=_cparams(("arbitrary",)),
        name="scatter_rows",
    )(pads, pos.reshape(n // rows, 1, rows), x)


def _moe_kernel(tg_ref, xs_ref, wg_ref, wu_ref, wd_ref, y_ref):
    tm = xs_ref.shape[0]
    g = tg_ref[pl.program_id(0)]
    xb = xs_ref[:, :D_MODEL].astype(BF16)
    gates = xs_ref[:, D_MODEL:]
    lane = lax.broadcasted_iota(jnp.int32, (tm, LANES), 1)
    acc = jnp.zeros((tm, D_MODEL), F32)
    for k in range(EXPERTS_PER_GROUP):
        hmid = _silu(_dot(xb, wg_ref[k])) * _dot(xb, wu_ref[k])
        gcol = jnp.sum(jnp.where(lane == g * EXPERTS_PER_GROUP + k, gates, 0.0),
                       axis=-1, keepdims=True)
        acc = acc + gcol * _dot(hmid.astype(BF16), wd_ref[k])
    for c in range(D_MODEL // SC_ROW):
        y_ref[c] = acc[:, c * SC_ROW:(c + 1) * SC_ROW]


def _moe_sorted(tile_group, xs, wg, wu, wd, *, tm):
    m = xs.shape[0]
    wspec = lambda shape: pl.BlockSpec((EXPERTS_PER_GROUP,) + shape, lambda i, tg: (tg[i], 0, 0))
    return pl.pallas_call(
        _moe_kernel,
        grid_spec=pltpu.PrefetchScalarGridSpec(
            num_scalar_prefetch=1,
            grid=(m // tm,),
            in_specs=[pl.BlockSpec((tm, ROW_W), lambda i, tg: (i, 0)),
                      wspec((D_MODEL, D_EXPERT)), wspec((D_MODEL, D_EXPERT)),
                      wspec((D_EXPERT, D_MODEL))],
            out_specs=pl.BlockSpec((D_MODEL // SC_ROW, tm, SC_ROW), lambda i, tg: (0, i, 0)),
        ),
        out_shape=jax.ShapeDtypeStruct((D_MODEL // SC_ROW, m, SC_ROW), F32),
        compiler_params=_cparams(("arbitrary",)),
        name="moe",
    )(tile_group, xs, wg, wu, wd)


def _route(x1g, *, tm):
    n = x1g.shape[0]
    n_tiles = n // tm + N_GROUPS - 1
    gidx = x1g[:, D_MODEL + GROUP_LANE].astype(jnp.int32)
    onehot = (gidx[:, None] == jnp.arange(N_GROUPS, dtype=jnp.int32)[None, :]).astype(jnp.int32)
    counts = jnp.sum(onehot, axis=0)
    tiles = (counts + tm - 1) // tm
    tile_end = jnp.cumsum(tiles)
    tile_start = tile_end - tiles
    rank = jnp.sum((jnp.cumsum(onehot, axis=0) - onehot) * onehot, axis=1)
    pos = jnp.take(tile_start, gidx) * tm + rank
    pad_start = tile_start * tm + counts
    pad_end = (tile_end * tm).at[N_GROUPS - 1].set(n_tiles * tm)
    pads = jnp.concatenate([pad_start, pad_end - pad_start]).astype(jnp.int32)
    tile_group = jnp.searchsorted(tile_end, jnp.arange(n_tiles, dtype=jnp.int32), side="right")
    return pos, pads, jnp.minimum(tile_group, N_GROUPS - 1).astype(jnp.int32), n_tiles


def _post_kernel(x1_ref, f_ref, p_ref, g2_ref, b2_ref, wpg_ref, wp_ref, y_ref, *, alpha):
    tm = x1_ref.shape[0]
    halves = [(i * tm // 2, (i + 1) * tm // 2) for i in range(2)]
    emb = [_dot(p_ref[lo:hi, :].astype(BF16), wp_ref[...]) for lo, hi in halves]
    def ffn(lo, hi):
        return jnp.concatenate([f_ref[c, lo:hi, :] for c in range(f_ref.shape[0])], axis=1)

    x2 = [_layer_norm(alpha * x1_ref[lo:hi, :] + ffn(lo, hi), g2_ref[...], b2_ref[...])
          for lo, hi in halves]
    gate = [_dot(x.astype(BF16), wpg_ref[...]) for x in x2]
    for (lo, hi), x, g, e in zip(halves, x2, gate, emb):
        y_ref[lo:hi, :] = x + jax.nn.sigmoid(g) * e


def _post(x1g, ffn, p_all, layer, g2, b2, wpg, wp, *, tm, alpha):
    parts, n, _ = ffn.shape
    row = lambda w: pl.BlockSpec((tm, w), lambda i: (i, 0))
    return pl.pallas_call(
        functools.partial(_post_kernel, alpha=alpha),
        grid=(n // tm,),
        in_specs=[row(D_MODEL), pl.BlockSpec((parts, tm, SC_ROW), lambda i: (0, i, 0)),
                  pl.BlockSpec((None, tm, PLE_DIM), lambda i: (layer, i, 0)),
                  _full(g2.shape), _full(b2.shape), _full(wpg.shape), _full(wp.shape)],
        out_specs=row(D_MODEL),
        out_shape=jax.ShapeDtypeStruct((n, D_MODEL), F32),
        compiler_params=_cparams(("parallel",)),
        name="post",
    )(x1g, ffn, p_all, g2, b2, wpg, wp)


def _rot_cols(w):
    half = ROPE_DIM // 2
    return jnp.concatenate([-w[..., half:], w[..., :half]], axis=-1)


def _prep_layer(l, w_in, w_qb, w_kvb, w_o, w_gate, w_up, w_down, w_ple, w_ple_gate):
    win = w_in[l]
    win_aug = jnp.concatenate([win, _rot_cols(win[:, COL_KPE:])], axis=-1).astype(BF16)
    wqb = w_qb[l].reshape(Q_LORA, HB_HEADS, NOPE_DIM + ROPE_DIM)
    wqb_aug = jnp.concatenate([wqb, _rot_cols(wqb[..., NOPE_DIM:])], axis=-1)
    wqb_aug = wqb_aug.reshape(Q_LORA, HB_HEADS * Q_HEAD_AUG).astype(BF16)
    wkvb = w_kvb[l].reshape(KV_LORA, HB_HEADS, NOPE_DIM + V_DIM)
    wuk_t = jnp.transpose(wkvb[..., :NOPE_DIM], (1, 2, 0)).astype(BF16)
    wuv = jnp.transpose(wkvb[..., NOPE_DIM:], (1, 0, 2)).astype(BF16)
    return dict(win=win_aug, wqb=wqb_aug, wuk=wuk_t, wuv=wuv, wo=w_o[l].astype(BF16),
                wg=w_gate[l].astype(BF16), wu=w_up[l].astype(BF16), wd=w_down[l].astype(BF16),
                wp=w_ple[l].astype(BF16), wpg=w_ple_gate[l].astype(BF16))


def _rope_table(pos):
    inv = ROPE_THETA ** (-np.arange(0, ROPE_DIM, 2, dtype=np.float64) / ROPE_DIM)
    ang = pos.astype(np.float64)[:, None] * inv[None, :]
    cos, sin = np.cos(ang), np.sin(ang)
    return np.concatenate([cos, cos, sin, sin], axis=-1).astype(np.float32)


def _tiles(n, t, prompt):
    if prompt:
        tm = min(512, t)
        return dict(tm=tm, tq=tm, tk=tm, tb=tm, cb=min(128, t), bb=2, tmoe=min(512, n))
    return dict(tm=min(512, n), tq=t, tk=None, tb=t, cb=t, bb=2, tmoe=min(256, n))


def _layer(x, p_all, layer, cs, s0_all, past, lbp, prm, small, *, alpha, prompt):
    bsz, t, _ = x.shape
    n = bsz * t
    tl = _tiles(n, t, prompt)
    x2d = x.reshape(n, D_MODEL)
    tm, tb, cb = tl["tm"], tl["tb"], tl["cb"]
    outs = _inproj(x2d, cs, prm["win"], lbp, small["qg"], prm["wqb"], small["kvg"], prm["wuk"],
                   tm=tm, tq=tl["tq"], cb=cb, prompt=prompt)
    qs, kin, logf, v, gs, ckv, kpe = outs[:7]
    r4 = lambda a: a.reshape(HA_HEADS, bsz, t, HA_DK)
    if prompt:
        qt, vt, kcat, ckvt = outs[7:]
        vt = vt.reshape(bsz, t // tb, HA_HEADS, tb // cb, HA_DV, cb)
    else:
        qabs, qpe = outs[7:]
        vt = jnp.transpose(r4(v), (1, 0, 3, 2)).astype(BF16)
        vt = vt.reshape(bsz, 1, HA_HEADS, 1, HA_DV, t)
    o_raw, s_new = _hgrn(r4(qs), r4(kin), r4(logf), r4(v), vt, s0_all, layer if not prompt else 0,
                         bb=tl["bb"], tb=tb, cb=cb)
    if prompt:
        olat = _attn_prompt(qt, kcat, ckvt, bsz=bsz, t=t, tq=tl["tq"], tk=tl["tk"])
    else:
        olat = _attn_sample(qabs, qpe, past[0], past[1], layer, ckv, kpe, ts=t)
    x1g = _merge(x2d, o_raw.reshape(HA_HEADS, n, HA_DV), gs, olat, prm["wuv"], prm["wo"],
                 small["hg"], small["mg"], small["g1"], small["b1"],
                 small["wr"], small["rb"], tm=tm, tq=tl["tq"], alpha=alpha)
    tmoe = tl["tmoe"]
    pos, pads, tile_group, n_tiles = _route(x1g, tm=tmoe)
    xs = _scatter_rows(x1g, pos, pads, rows=tmoe, m=n_tiles * tmoe)
    ys = _moe_sorted(tile_group, xs, prm["wg"], prm["wu"], prm["wd"], tm=tmoe)
    ffn = _sc_gather_rows(ys, pos)
    y = _post(x1g, ffn, p_all.reshape(-1, n, PLE_DIM), layer, small["g2"], small["b2"],
              prm["wpg"], prm["wp"], tm=tm, alpha=alpha)
    return (y.reshape(bsz, t, D_MODEL), s_new, ckv.reshape(bsz, t, KV_LORA),
            kpe.reshape(bsz, t, ROPE_DIM))


def kernel(x_prompt, x_sample, p_prompt, p_sample, state_hgrn, cache_ckv, cache_kpe, w_in,
           lb_logits, hgrn_norm_g, q_norm_g, w_qb, kv_norm_g, w_kvb, mla_norm_g, w_o,
           ln1_g, ln1_b, ln2_g, ln2_b, w_router, router_bias, w_gate, w_up, w_down,
           w_ple, w_ple_gate):
    depth = w_in.shape[0]
    alpha = (2 * depth) ** 0.25
    bp, tp, _ = x_prompt.shape
    bs, ts, _ = x_sample.shape
    past = cache_ckv.shape[2]

    sm = jax.nn.softmax(lb_logits.astype(F32), axis=0)
    lb_all = jnp.maximum(jnp.cumsum(sm, axis=0) - sm[0:1], 0.0)
    lbp_all = jnp.stack([jnp.log(lb_all), jnp.log1p(-lb_all), 1.0 - lb_all], axis=1)

    cs_p = jnp.asarray(_rope_table(np.arange(tp)))
    tm_s = _tiles(bs * ts, ts, False)["tm"]
    cs_s = jnp.asarray(np.tile(_rope_table(past + np.arange(ts)), (tm_s // ts, 1)))
    wr = jnp.pad(w_router, ((0, 0), (0, LANES - N_EXPERTS))).astype(BF16)
    rb = jnp.pad(router_bias.astype(F32), (0, LANES - N_EXPERTS)).reshape(1, LANES)
    s0_p = jnp.zeros((1, bp, HA_HEADS, HA_DK, HA_DV), F32)
    cache_kpe_t = jnp.swapaxes(cache_kpe, 2, 3)

    yp, ys = x_prompt, x_sample
    res = [[] for _ in range(6)]
    for l in range(depth):
        prm = _prep_layer(l, w_in, w_qb, w_kvb, w_o, w_gate, w_up, w_down, w_ple, w_ple_gate)
        row = lambda a: a[l].reshape(1, -1).astype(F32)
        small = dict(qg=row(q_norm_g), kvg=row(kv_norm_g), hg=row(hgrn_norm_g), mg=row(mla_norm_g),
                     g1=row(ln1_g), b1=row(ln1_b), g2=row(ln2_g), b2=row(ln2_b), wr=wr, rb=rb)
        yp, sp, cp, kp = _layer(yp, p_prompt, l, cs_p, s0_p, None, lbp_all[l], prm, small,
                                alpha=alpha, prompt=True)
        ys, ss, cs_, ks = _layer(ys, p_sample, l, cs_s, state_hgrn, (cache_ckv, cache_kpe_t),
                                 lbp_all[l], prm, small, alpha=alpha, prompt=False)
        for lst, a in zip(res, (sp, cp, kp, ss, cs_, ks)):
            lst.append(a)
    sp, cp, kp, ss, cs_, ks = (jnp.stack(a) for a in res)
    return (yp, ys, sp, cp, kp, ss, cs_, ks)
```

```python
import functools

import jax
import jax.numpy as jnp
import numpy as np
from jax import lax
from jax.experimental import pallas as pl
from jax.experimental.pallas import tpu as pltpu
from jax.experimental.pallas import tpu_sc as plsc

F32 = jnp.float32
BF16 = jnp.bfloat16

D_MODEL = 1024
HA_HEADS = 4
HA_DK = 128
HA_DV = 128
HB_HEADS = 4
Q_LORA = 384
KV_LORA = 256
NOPE_DIM = 128
ROPE_DIM = 64
V_DIM = 128
ROPE_THETA = 10000.0
MLA_SCALE = (NOPE_DIM + ROPE_DIM) ** -0.5
LOG2E = 1.4426950408889634
Q_SCALE = MLA_SCALE * LOG2E
CHUNK = 64
N_EXPERTS = 16
N_GROUPS = 4
EXPERTS_PER_GROUP = N_EXPERTS // N_GROUPS
D_EXPERT = 512
PLE_DIM = 256
NEG_INF = -1e30

HA_W = HA_HEADS * HA_DK
COL_CQ = 4 * HA_W
COL_CKV = COL_CQ + Q_LORA
COL_KPE = COL_CKV + KV_LORA
D_IN_AUG = COL_KPE + 2 * ROPE_DIM
Q_HEAD_AUG = NOPE_DIM + 2 * ROPE_DIM

LANES = 128
SUBLANES = 8
QK_DIM = KV_LORA + LANES
GROUP_LANE = N_EXPERTS
SC_WINDOW = 128
SC_ROW = 256
X_PIECES = D_MODEL // SC_ROW
SUB_BLOCK = 16
VMEM_LIMIT = 56 * 1024 * 1024


def _cparams(sem, vmem=VMEM_LIMIT):
    return pltpu.CompilerParams(dimension_semantics=sem, vmem_limit_bytes=vmem)


def _dot(a, b):
    return jnp.dot(a, b, preferred_element_type=F32)


def _dot_nt(a, b):
    return lax.dot_general(a, b, (((1,), (1,)), ((), ())), preferred_element_type=F32)


def _rms(x, g, eps=1e-6):
    return x * lax.rsqrt(jnp.mean(x * x, axis=-1, keepdims=True) + eps) * g


def _layer_norm(x, g, b, eps=1e-5):
    mu = jnp.mean(x, axis=-1, keepdims=True)
    xc = x - mu
    var = jnp.mean(xc * xc, axis=-1, keepdims=True)
    return xc * lax.rsqrt(var + eps) * g + b


def _silu(x):
    return x * jax.nn.sigmoid(x)


def _full(shape):
    nd = len(shape)
    return pl.BlockSpec(shape, lambda *_: (0,) * nd)


def _inproj_kernel(x_ref, cs_ref, win_ref, lb_ref, qg_ref, wqb_ref, kvg_ref, wuk_ref,
                   qs_ref, kin_ref, logf_ref, v_ref, gs_ref, ckv_ref, kpe_ref, *rest,
                   tq, cb, prompt):
    tm = x_ref.shape[0]
    x = x_ref[...].astype(BF16)

    def sect(lo, hi):
        return _dot(x, win_ref[:, lo:hi])

    def put_heads(ref, val):
        for h in range(HA_HEADS):
            ref[h] = val[:, h * HA_DK:(h + 1) * HA_DK]

    cqn = _rms(sect(COL_CQ, COL_CKV), qg_ref[...]).astype(BF16)

    put_heads(qs_ref, _silu(sect(0, HA_W)))
    fa = sect(HA_W, 2 * HA_W)
    log_lb = lb_ref[0:1, :]
    log1m_lb = lb_ref[1:2, :]
    one_m_lb = lb_ref[2:3, :]
    e = jnp.exp(-jnp.abs(fa))
    r = 1.0 / (1.0 + e)
    c = log1m_lb + jnp.minimum(fa, 0.0) + jnp.log(r)
    put_heads(logf_ref, jnp.maximum(log_lb, c) + jnp.log(1.0 + jnp.exp(-jnp.abs(log_lb - c))))
    put_heads(kin_ref, one_m_lb * jnp.where(fa >= 0.0, e * r, r))
    v = sect(2 * HA_W, 3 * HA_W)
    put_heads(v_ref, v)
    gs_ref[...] = _silu(sect(3 * HA_W, 4 * HA_W))

    cs = cs_ref[...]

    def rope(t):
        prod = t * cs
        return prod + pltpu.roll(prod, ROPE_DIM, 1)

    ckv = _rms(sect(COL_CKV, COL_KPE), kvg_ref[...])
    ckv_ref[...] = ckv
    kpe2 = rope(sect(COL_KPE, D_IN_AUG))
    kpe_ref[...] = kpe2[:, :ROPE_DIM]
    if prompt:
        qt_ref, vt_ref, kcat_ref, ckvt_ref = rest
        for h in range(HA_HEADS):
            for ci in range(tm // cb):
                vt_ref[0, h, ci] = v[ci * cb:(ci + 1) * cb, h * HA_DV:(h + 1) * HA_DV].T.astype(BF16)
        kcat_ref[:, :KV_LORA] = ckv.astype(BF16)
        kcat_ref[:, KV_LORA:] = kpe2.astype(BF16)
        ckvt_ref[0] = ckv.T.astype(BF16)
    else:
        qabs_ref, qpe_ref = rest

    lane = lax.broadcasted_iota(jnp.int32, (tm, LANES), 1)
    qhs = [_dot(cqn, wqb_ref[:, h * Q_HEAD_AUG:(h + 1) * Q_HEAD_AUG]) for h in range(HB_HEADS)]
    qabss = [_dot(qhs[h][:, :NOPE_DIM].astype(BF16), wuk_ref[h]) for h in range(HB_HEADS)]
    for h in range(HB_HEADS):
        qabs = qabss[h] * Q_SCALE
        qpe = jnp.where(lane < ROPE_DIM, rope(qhs[h][:, NOPE_DIM:]) * Q_SCALE, 0.0)
        if prompt:
            qabs_t = qabs.T.astype(BF16)
            qpe_t = qpe.T.astype(BF16)
            for s in range(tm // tq):
                qt_ref[s, :KV_LORA, h * tq:(h + 1) * tq] = qabs_t[:, s * tq:(s + 1) * tq]
                qt_ref[s, KV_LORA:, h * tq:(h + 1) * tq] = qpe_t[:, s * tq:(s + 1) * tq]
        else:
            for s in range(tm // tq):
                qabs_ref[s, h] = qabs[s * tq:(s + 1) * tq].astype(BF16)
                qpe_ref[s, h] = qpe[s * tq:(s + 1) * tq].astype(BF16)


def _inproj(x, cs, w_in, lbp, qg, wqb, kvg, wuk, *, tm, tq, cb, prompt):
    n = x.shape[0]
    nt = n // tm
    row = lambda w: pl.BlockSpec((tm, w), lambda i: (i, 0))
    heads = pl.BlockSpec((HA_HEADS, tm, HA_DK), lambda i: (0, i, 0))
    qblk = lambda w: pl.BlockSpec((tm // tq, HB_HEADS, tq, w), lambda i: (i, 0, 0, 0))
    head_major = jax.ShapeDtypeStruct((HA_HEADS, n, HA_DK), F32)
    out_shape = [
        head_major,
        head_major,
        head_major,
        head_major,
        jax.ShapeDtypeStruct((n, HA_W), F32),
        jax.ShapeDtypeStruct((n, KV_LORA), F32),
        jax.ShapeDtypeStruct((n, ROPE_DIM), F32),
    ]
    out_specs = [heads, heads, heads, heads, row(HA_W), row(KV_LORA), row(ROPE_DIM)]
    if prompt:
        out_shape += [jax.ShapeDtypeStruct((n // tq, QK_DIM, HB_HEADS * tq), BF16),
                      jax.ShapeDtypeStruct((nt, HA_HEADS, tm // cb, HA_DV, cb), BF16),
                      jax.ShapeDtypeStruct((n, QK_DIM), BF16),
                      jax.ShapeDtypeStruct((nt, KV_LORA, tm), BF16)]
        out_specs += [pl.BlockSpec((tm // tq, QK_DIM, HB_HEADS * tq), lambda i: (i, 0, 0)),
                      pl.BlockSpec((1, HA_HEADS, tm // cb, HA_DV, cb), lambda i: (i, 0, 0, 0, 0)),
                      row(QK_DIM),
                      pl.BlockSpec((1, KV_LORA, tm), lambda i: (i, 0, 0))]
    else:
        out_shape += [jax.ShapeDtypeStruct((n // tq, HB_HEADS, tq, KV_LORA), BF16),
                      jax.ShapeDtypeStruct((n // tq, HB_HEADS, tq, LANES), BF16)]
        out_specs += [qblk(KV_LORA), qblk(LANES)]
    return pl.pallas_call(
        functools.partial(_inproj_kernel, tq=tq, cb=cb, prompt=prompt),
        grid=(nt,),
        in_specs=[row(D_MODEL),
                  pl.BlockSpec((tm, LANES), lambda i: (i % (cs.shape[0] // tm), 0)),
                  _full(w_in.shape), _full(lbp.shape), _full(qg.shape),
                  _full(wqb.shape), _full(kvg.shape), _full(wuk.shape)],
        out_specs=out_specs,
        out_shape=out_shape,
        compiler_params=_cparams(("parallel",)),
        name="inproj",
    )(x, cs, w_in, lbp, qg, wqb, kvg, wuk)


def _hgrn_kernel(q_ref, k_ref, g_ref, v_ref, vt_ref, s0_ref, o_ref, sfin_ref, st_scr, b2_scr,
                 hide_ref, *, cb):
    _, bb, tb, _ = q_ref.shape
    nsub = cb // SUB_BLOCK
    ti = pl.program_id(1)

    @pl.when(ti == 0)
    def _():
        for b in range(bb):
            for h in range(HA_HEADS):
                st_scr[b, h] = s0_ref[b, h].T
        row8 = lax.broadcasted_iota(jnp.int32, (SUBLANES, LANES), 0)
        for s in range(SUBLANES):
            hide_ref[s] = jnp.where(row8 >= s, 0.0, NEG_INF)

    r_i = lax.broadcasted_iota(jnp.int32, (cb, cb), 0)
    c_i = lax.broadcasted_iota(jnp.int32, (cb, cb), 1)
    tril = (r_i >= c_i).astype(F32)

    chains = [(b, h) for b in range(bb) for h in range(HA_HEADS)]

    def matmul_part(ci):
        r0 = pl.multiple_of(ci * cb, cb)
        rows = pl.ds(r0, cb)

        def sub(ref, b, h, lo, size):
            return ref[h, b, pl.ds(pl.multiple_of(r0 + lo, SUB_BLOCK), size), :]

        for b, h in chains:
            bcum = jnp.dot(tril, g_ref[h, b, rows, :], precision=lax.Precision.HIGHEST,
                           preferred_element_type=F32)
            b2_scr[b * HA_HEADS + h] = bcum * LOG2E
        for b, h in chains:
            b2 = b2_scr[b * HA_HEADS + h]
            o_ref[h, b, rows, :] = _dot_nt((q_ref[h, b, rows, :] * jnp.exp2(b2)).astype(BF16),
                                           st_scr[b, h].astype(BF16))
        for j in range(nsub - 1):
            lo, hi = j * SUB_BLOCK, (j + 1) * SUB_BLOCK
            below = pl.ds(pl.multiple_of(r0 + hi, SUB_BLOCK), cb - hi)
            for b, h in chains:
                slot = b * HA_HEADS + h
                bnd = b2_scr[slot, hi - 1:hi, :]
                kd = (sub(k_ref, b, h, lo, SUB_BLOCK)
                      * jnp.exp2(bnd - b2_scr[slot, lo:hi, :])).astype(BF16)
                qe = (sub(q_ref, b, h, hi, cb - hi)
                      * jnp.exp2(b2_scr[slot, hi:, :] - bnd)).astype(BF16)
                a = _dot_nt(qe, kd).astype(BF16)
                o_ref[h, b, below, :] += _dot(a, sub(v_ref, b, h, lo, SUB_BLOCK).astype(BF16))
        for b, h in chains:
            slot = b * HA_HEADS + h
            b_last = b2_scr[slot, cb - 1:cb, :]
            kd_all = (k_ref[h, b, rows, :] * jnp.exp2(b_last - b2_scr[slot])).astype(BF16)
            st_scr[b, h] = st_scr[b, h] * jnp.exp2(b_last) + _dot(vt_ref[b, h, ci], kd_all)

    def pairwise_part(ci, b, h, j):
        slot = b * HA_HEADS + h
        r0 = pl.multiple_of(ci * cb, cb)
        lo = j * SUB_BLOCK
        mid = lo + SUBLANES
        top = pl.ds(pl.multiple_of(r0 + lo, SUBLANES), SUBLANES)
        bot = pl.ds(pl.multiple_of(r0 + mid, SUBLANES), SUBLANES)
        q_top, q_bot = q_ref[h, b, top, :], q_ref[h, b, bot, :]
        b_top, b_bot = b2_scr[slot, lo:mid, :], b2_scr[slot, mid:mid + SUBLANES, :]
        acc_top, acc_bot = o_ref[h, b, top, :], o_ref[h, b, bot, :]
        for s in range(SUB_BLOCK):
            row = pl.ds(r0 + lo + s, 1)
            bs = b2_scr[slot, lo + s:lo + s + 1, :]
            ks = k_ref[h, b, row, :]
            vs = v_ref[h, b, row, :]
            if s < SUBLANES:
                w = q_top * jnp.exp2(b_top - bs + hide_ref[s]) * ks
                acc_top = acc_top + jnp.sum(w, axis=-1, keepdims=True) * vs
                w = q_bot * jnp.exp2(b_bot - bs) * ks
            else:
                w = q_bot * jnp.exp2(b_bot - bs + hide_ref[s - SUBLANES]) * ks
            acc_bot = acc_bot + jnp.sum(w, axis=-1, keepdims=True) * vs
        o_ref[h, b, top, :] = acc_top
        o_ref[h, b, bot, :] = acc_bot

    def chunk(ci, carry):
        matmul_part(ci)
        for j in range(nsub):
            for b, h in chains:
                pairwise_part(ci, b, h, j)
        return carry

    lax.fori_loop(0, tb // cb, chunk, 0)

    @pl.when(ti == pl.num_programs(1) - 1)
    def _():
        for b in range(bb):
            for h in range(HA_HEADS):
                sfin_ref[b, h] = st_scr[b, h].T


def _hgrn(qs, kin, logf, v, vt, s0_all, layer, *, bb, tb, cb):
    _, bsz, t, _ = qs.shape
    blk = pl.BlockSpec((HA_HEADS, bb, tb, HA_DK), lambda i, j: (0, i, j, 0))
    st_blk = pl.BlockSpec((bb, HA_HEADS, HA_DK, HA_DV), lambda i, j: (i, 0, 0, 0))
    s0_blk = pl.BlockSpec((None, bb, HA_HEADS, HA_DK, HA_DV), lambda i, j: (layer, i, 0, 0, 0))
    vt_blk = pl.BlockSpec((bb, None, HA_HEADS, tb // cb, HA_DV, cb),
                          lambda i, j: (i, j, 0, 0, 0, 0))
    return pl.pallas_call(
        functools.partial(_hgrn_kernel, cb=cb),
        grid=(bsz // bb, t // tb),
        in_specs=[blk, blk, blk, blk, vt_blk, s0_blk],
        out_specs=[blk, st_blk],
        out_shape=[jax.ShapeDtypeStruct(qs.shape, F32),
                   jax.ShapeDtypeStruct(s0_all.shape[1:], F32)],
        scratch_shapes=[pltpu.VMEM((bb, HA_HEADS, HA_DV, HA_DK), F32),
                        pltpu.VMEM((bb * HA_HEADS, cb, HA_DK), F32),
                        pltpu.VMEM((SUBLANES, SUBLANES, LANES), F32)],
        compiler_params=_cparams(("parallel", "arbitrary")),
        name="hgrn",
    )(qs, kin, logf, v, vt, s0_all)


def _attn_prompt_kernel(qt_ref, kc_ref, kt_ref, o_ref, s0, s1, p0, p1, a0, a1, m_scr, l_scr,
                        acc_scr, *, tq, tk):
    qi = pl.program_id(1)
    jd = (qi * tq) // tk
    shift = CHUNK.bit_length() - 1

    heads = range(HB_HEADS)

    def scores(j, s_dst, hs=heads):
        keys = kc_ref[pl.ds(pl.multiple_of(j * tk, tk), tk), :]
        for h in hs:
            cols = slice(h * tq, (h + 1) * tq)
            s_dst[:, cols] = _dot(keys, qt_ref[0, :, cols])

    def softmax(j, s_src, p_dst, a_dst, masked, hs=heads):
        if masked:
            kpos = j * tk + lax.broadcasted_iota(jnp.int32, (tk, 1), 0)
            qpos = qi * tq + lax.broadcasted_iota(jnp.int32, (1, tq), 1)
            visible = (kpos >> shift) <= (qpos >> shift)
        for h in hs:
            cols = slice(h * tq, (h + 1) * tq)
            def load():
                s = s_src[:, cols]
                return jnp.where(visible, s, NEG_INF) if masked else s

            m_prev = m_scr[:, cols]
            m_new = jnp.maximum(m_prev, jnp.max(load(), axis=0, keepdims=True))
            alpha = jnp.exp2(m_prev - m_new)
            p = jnp.exp2(load() - m_new)
            l_scr[:, cols] = alpha * l_scr[:, cols] + jnp.sum(p, axis=0, keepdims=True)
            m_scr[:, cols] = m_new
            a_dst[:, cols] = alpha
            p_dst[:, cols] = p.astype(BF16)

    def values(j, p_src, a_src, hs=heads):
        for h in hs:
            cols = slice(h * tq, (h + 1) * tq)
            acc_scr[:, cols] = (a_src[:, cols] * acc_scr[:, cols]
                                + _dot(kt_ref[j], p_src[:, cols]))

    def trip(j, s_cur, s_nxt, p_cur, p_prv, a_cur, a_prv):
        scores(j + 1, s_nxt)
        softmax(j, s_cur, p_cur, a_cur, False)
        values(jnp.maximum(j - 1, 0), p_prv, a_prv)

    m_scr[...] = jnp.full(m_scr.shape, NEG_INF, F32)
    l_scr[...] = jnp.zeros(l_scr.shape, F32)
    acc_scr[...] = jnp.zeros(acc_scr.shape, F32)
    p1[...] = jnp.zeros(p1.shape, BF16)
    a1[...] = jnp.ones(a1.shape, F32)
    scores(0, s0)

    def pair(jj, carry):
        trip(2 * jj, s0, s1, p0, p1, a0, a1)
        trip(2 * jj + 1, s1, s0, p1, p0, a1, a0)
        return carry

    lax.fori_loop(0, jd // 2, pair, 0)

    @pl.when(jd % 2 == 1)
    def _():
        trip(jd - 1, s0, s1, p0, p1, a0, a1)

    def drain(s_cur, p_cur, p_prv, a_cur, a_prv):
        softmax(jd, s_cur, p_cur, a_cur, True)
        values(jnp.maximum(jd - 1, 0), p_prv, a_prv)
        values(jd, p_cur, a_cur)

    pl.when(jd % 2 == 0)(functools.partial(drain, s0, p0, p1, a0, a1))
    pl.when(jd % 2 == 1)(functools.partial(drain, s1, p1, p0, a1, a0))

    for h in range(HB_HEADS):
        cols = slice(h * tq, (h + 1) * tq)
        o_ref[0, h] = (acc_scr[:, cols] * (1.0 / l_scr[:, cols])).T.astype(BF16)


def _attn_prompt(qt, kcat, ckvt, *, bsz, t, tq, tk):
    nq = t // tq
    nk = t // tk
    cols = HB_HEADS * tq
    single = pl.Buffered(1)
    return pl.pallas_call(
        functools.partial(_attn_prompt_kernel, tq=tq, tk=tk),
        grid=(bsz, nq),
        in_specs=[
            pl.BlockSpec((1, QK_DIM, cols), lambda b, i: (b * nq + i, 0, 0)),
            pl.BlockSpec((t, QK_DIM), lambda b, i: (b, 0), pipeline_mode=single),
            pl.BlockSpec((nk, KV_LORA, tk), lambda b, i: (b, 0, 0), pipeline_mode=single),
        ],
        out_specs=pl.BlockSpec((1, HB_HEADS, tq, KV_LORA), lambda b, i: (b * nq + i, 0, 0, 0)),
        out_shape=jax.ShapeDtypeStruct((bsz * nq, HB_HEADS, tq, KV_LORA), BF16),
        scratch_shapes=[pltpu.VMEM((tk, cols), F32), pltpu.VMEM((tk, cols), F32),
                        pltpu.VMEM((tk, cols), BF16), pltpu.VMEM((tk, cols), BF16),
                        pltpu.VMEM((1, cols), F32), pltpu.VMEM((1, cols), F32),
                        pltpu.VMEM((1, cols), F32), pltpu.VMEM((1, cols), F32),
                        pltpu.VMEM((KV_LORA, cols), F32)],
        compiler_params=_cparams(("parallel", "arbitrary")),
        name="attn_prompt",
    )(qt, kcat, ckvt)


def _attn_sample_kernel(qa_ref, qp_ref, pckv_ref, pkpe_ref, nckv_ref, nkpe_ref, o_ref):
    _, _, ts, _ = qa_ref.shape
    rows = HB_HEADS * ts
    qa = qa_ref[0].reshape(rows, KV_LORA)
    qp = qp_ref[0].reshape(rows, LANES)[:, :ROPE_DIM]
    pckv = pckv_ref[0].astype(BF16)
    nckv = nckv_ref[...].astype(BF16)
    s_past = _dot_nt(qa, pckv) + _dot(qp, pkpe_ref[0].astype(BF16))
    s_new = _dot_nt(qa, nckv) + _dot_nt(qp, nkpe_ref[...].astype(BF16))
    m = jnp.maximum(jnp.max(s_past, axis=-1, keepdims=True), jnp.max(s_new, axis=-1, keepdims=True))
    p_past = jnp.exp2(s_past - m)
    p_new = jnp.exp2(s_new - m)
    l = jnp.sum(p_past, axis=-1, keepdims=True) + jnp.sum(p_new, axis=-1, keepdims=True)
    o = (_dot(p_past.astype(BF16), pckv) + _dot(p_new.astype(BF16), nckv)) / l
    o_ref[0] = o.astype(BF16).reshape(HB_HEADS, ts, KV_LORA)


def _attn_sample(qabs, qpe, past_ckv, past_kpe, layer, ckv, kpe, *, ts):
    _, bsz, past, _ = past_ckv.shape
    return pl.pallas_call(
        _attn_sample_kernel,
        grid=(bsz,),
        in_specs=[
            pl.BlockSpec((1, HB_HEADS, ts, KV_LORA), lambda b: (b, 0, 0, 0)),
            pl.BlockSpec((1, HB_HEADS, ts, LANES), lambda b: (b, 0, 0, 0)),
            pl.BlockSpec((None, 1, past, KV_LORA), lambda b: (layer, b, 0, 0)),
            pl.BlockSpec((None, 1, ROPE_DIM, past), lambda b: (layer, b, 0, 0)),
            pl.BlockSpec((ts, KV_LORA), lambda b: (b, 0)),
            pl.BlockSpec((ts, ROPE_DIM), lambda b: (b, 0)),
        ],
        out_specs=pl.BlockSpec((1, HB_HEADS, ts, KV_LORA), lambda b: (b, 0, 0, 0)),
        out_shape=jax.ShapeDtypeStruct(qabs.shape, BF16),
        compiler_params=_cparams(("parallel",)),
        name="attn_sample",
    )(qabs, qpe, past_ckv, past_kpe, ckv, kpe)


def _gates(logits, bias):
    tm = logits.shape[0]
    lane = lax.broadcasted_iota(jnp.int32, (tm, LANES), 1)
    pos = lane % EXPERTS_PER_GROUP
    valid = lane < N_EXPERTS
    scores = jax.nn.sigmoid(logits)
    sel = jnp.where(valid, scores + bias, -jnp.inf)

    others = []
    for r in range(1, EXPERTS_PER_GROUP):
        others.append(jnp.where(pos >= r, pltpu.roll(sel, r, 1),
                                pltpu.roll(sel, LANES - (EXPERTS_PER_GROUP - r), 1)))
    a, b, c, d = sel, others[0], others[1], others[2]
    hi1, lo1 = jnp.maximum(a, b), jnp.minimum(a, b)
    hi2, lo2 = jnp.maximum(c, d), jnp.minimum(c, d)
    gscore = jnp.maximum(hi1, hi2) + jnp.maximum(jnp.minimum(hi1, hi2), jnp.maximum(lo1, lo2))
    gmax = jnp.max(gscore, axis=-1, keepdims=True)
    group = (lane // EXPERTS_PER_GROUP).astype(F32)
    gidx = jnp.min(jnp.where(gscore == gmax, group, float(LANES)), axis=-1, keepdims=True)

    rank = jnp.zeros((tm, LANES), jnp.int32)
    for r, o in enumerate(others, start=1):
        ahead = (o > sel) | ((o == sel) & (pos >= r))
        rank = rank + ahead.astype(jnp.int32)
    chosen = (group == gidx) & (rank < 2) & valid
    w = jnp.where(chosen, scores, 0.0)
    w = w / jnp.sum(w, axis=-1, keepdims=True)
    return jnp.where(lane == GROUP_LANE, gidx, w)


def _merge_kernel(x_ref, o_ref, gs_ref, olat_ref, wuv_ref, wo_ref, hg_ref, mg_ref,
                  g1_ref, b1_ref, wr_ref, rb_ref, x1g_ref, *, alpha):
    _, _, tq, _ = olat_ref.shape
    tm = x_ref.shape[0]
    hg = hg_ref[...]
    d_a = HA_HEADS * HA_DV
    halves = [(i * tm // 2, (i + 1) * tm // 2) for i in range(2)]

    def lat(h, lo, hi):
        if tq >= hi - lo:
            return olat_ref[lo // tq, h, lo % tq:lo % tq + hi - lo]
        return jnp.concatenate([olat_ref[s, h] for s in range(lo // tq, hi // tq)], axis=0)

    ob = [[_dot(lat(h, lo, hi), wuv_ref[h]) for h in range(HB_HEADS)] for lo, hi in halves]
    oa = [jnp.concatenate([_rms(o_ref[h, lo:hi, :], hg) * gs_ref[lo:hi, h * HA_DV:(h + 1) * HA_DV]
                           for h in range(HA_HEADS)], axis=-1) for lo, hi in halves]
    ob = [_rms(jnp.concatenate(o, axis=-1), mg_ref[...]) for o in ob]
    mix = [_dot(a.astype(BF16), wo_ref[:d_a, :]) + _dot(o.astype(BF16), wo_ref[d_a:, :])
           for a, o in zip(oa, ob)]
    x1 = [_layer_norm(alpha * x_ref[lo:hi, :] + m, g1_ref[...], b1_ref[...])
          for (lo, hi), m in zip(halves, mix)]
    logits = [_dot(x.astype(BF16), wr_ref[...]) for x in x1]
    for (lo, hi), x, lg in zip(halves, x1, logits):
        for c in range(X_PIECES):
            x1g_ref[c, lo:hi, :] = x[:, c * SC_ROW:(c + 1) * SC_ROW]
        x1g_ref[X_PIECES, lo:hi, :LANES] = _gates(lg, rb_ref[...])
        x1g_ref[X_PIECES, lo:hi, LANES:] = jnp.zeros((hi - lo, SC_ROW - LANES), F32)


def _merge(x, o_raw, gs, olat, wuv, wo, hg, mg, g1, b1, wr, rb, *, tm, tq, alpha):
    n = x.shape[0]
    row = lambda w: pl.BlockSpec((tm, w), lambda i: (i, 0))
    return pl.pallas_call(
        functools.partial(_merge_kernel, alpha=alpha),
        grid=(n // tm,),
        in_specs=[row(D_MODEL), pl.BlockSpec((HA_HEADS, tm, HA_DV), lambda i: (0, i, 0)), row(HA_W),
                  pl.BlockSpec((tm // tq, HB_HEADS, tq, KV_LORA), lambda i: (i, 0, 0, 0)),
                  _full(wuv.shape), _full(wo.shape), _full(hg.shape), _full(mg.shape),
                  _full(g1.shape), _full(b1.shape), _full(wr.shape), _full(rb.shape)],
        out_specs=pl.BlockSpec((X_PIECES + 1, tm, SC_ROW), lambda i: (0, i, 0)),
        out_shape=jax.ShapeDtypeStruct((X_PIECES + 1, n, SC_ROW), F32),
        compiler_params=_cparams(("parallel",)),
        name="merge",
    )(x, o_raw, gs, olat, wuv, wo, hg, mg, g1, b1, wr, rb)


def _sc_scatter_rows(src, dst, m):
    parts, n, _ = src.shape
    windows = dst.shape[0] // SC_WINDOW
    src_windows = n // SC_WINDOW
    pieces = src.reshape(parts * n, SC_ROW)
    piece_dst = (jnp.arange(parts, dtype=jnp.int32)[:, None] * m + dst[None, :]).reshape(1, -1)
    mesh = plsc.VectorSubcoreMesh(core_axis_name="core", subcore_axis_name="subcore")

    @functools.partial(pl.kernel, out_type=jax.ShapeDtypeStruct((parts * m, SC_ROW), src.dtype),
                       mesh=mesh)
    def scatter(x_hbm, i_hbm, o_hbm):
        def body(x_vmem, i_vmem):
            pltpu.sync_copy(x_vmem, o_hbm.at[i_vmem.at[0]])

        pltpu.emit_pipeline(
            body,
            grid=(parts * windows,),
            in_specs=[pl.BlockSpec((SC_WINDOW, SC_ROW),
                                   lambda i: ((i // windows) * src_windows
                                              + (i % windows) % src_windows, 0)),
                      pl.BlockSpec((1, SC_WINDOW), lambda i: (0, i))],
            out_specs=[],
            core_axis_name=("core", "subcore"),
            dimension_semantics=(pltpu.PARALLEL,),
        )(x_hbm, i_hbm)

    return scatter(pieces, piece_dst).reshape(parts, m, SC_ROW)


def _sc_gather_rows(src, idx):
    parts, n, _ = src.shape
    m = idx.shape[0] * parts
    pieces = src.reshape(parts * n, SC_ROW)
    piece_idx = (jnp.arange(parts, dtype=jnp.int32)[:, None] * n + idx[None, :]).reshape(1, m)
    mesh = plsc.VectorSubcoreMesh(core_axis_name="core", subcore_axis_name="subcore")

    @functools.partial(pl.kernel, out_type=jax.ShapeDtypeStruct((m, SC_ROW), src.dtype), mesh=mesh)
    def gather(x_hbm, i_hbm, o_hbm):
        def body(i_vmem, o_vmem):
            pltpu.sync_copy(x_hbm.at[i_vmem.at[0]], o_vmem)

        pltpu.emit_pipeline(
            body,
            grid=(m // SC_WINDOW,),
            in_specs=[pl.BlockSpec((1, SC_WINDOW), lambda i: (0, i))],
            out_specs=[pl.BlockSpec((SC_WINDOW, SC_ROW), lambda i: (i, 0))],
            core_axis_name=("core", "subcore"),
            dimension_semantics=(pltpu.PARALLEL,),
        )(i_hbm, o_hbm)

    return gather(pieces, piece_idx).reshape(parts, idx.shape[0], SC_ROW)


def _moe_kernel(tg_ref, xs_ref, wg_ref, wu_ref, wd_ref, y_ref):
    tm = xs_ref.shape[1]
    g = tg_ref[pl.program_id(0)]
    xb = jnp.concatenate([xs_ref[c] for c in range(X_PIECES)], axis=1).astype(BF16)
    gates = xs_ref[X_PIECES, :, :LANES]
    lane = lax.broadcasted_iota(jnp.int32, (tm, LANES), 1)
    acc = jnp.zeros((tm, D_MODEL), F32)
    for k in range(EXPERTS_PER_GROUP):
        hmid = _silu(_dot(xb, wg_ref[k])) * _dot(xb, wu_ref[k])
        gcol = jnp.sum(jnp.where(lane == g * EXPERTS_PER_GROUP + k, gates, 0.0),
                       axis=-1, keepdims=True)
        acc = acc + gcol * _dot(hmid.astype(BF16), wd_ref[k])
    for c in range(D_MODEL // SC_ROW):
        y_ref[c] = acc[:, c * SC_ROW:(c + 1) * SC_ROW]


def _moe_sorted(tile_group, xs, wg, wu, wd, *, tm):
    m = xs.shape[1]
    wspec = lambda shape: pl.BlockSpec((EXPERTS_PER_GROUP,) + shape, lambda i, tg: (tg[i], 0, 0))
    return pl.pallas_call(
        _moe_kernel,
        grid_spec=pltpu.PrefetchScalarGridSpec(
            num_scalar_prefetch=1,
            grid=(m // tm,),
            in_specs=[pl.BlockSpec((X_PIECES + 1, tm, SC_ROW), lambda i, tg: (0, i, 0)),
                      wspec((D_MODEL, D_EXPERT)), wspec((D_MODEL, D_EXPERT)),
                      wspec((D_EXPERT, D_MODEL))],
            out_specs=pl.BlockSpec((D_MODEL // SC_ROW, tm, SC_ROW), lambda i, tg: (0, i, 0)),
        ),
        out_shape=jax.ShapeDtypeStruct((D_MODEL // SC_ROW, m, SC_ROW), F32),
        compiler_params=_cparams(("arbitrary",)),
        name="moe",
    )(tile_group, xs, wg, wu, wd)


def _route(x1g, *, tm):
    n = x1g.shape[1]
    n_tiles = n // tm + N_GROUPS - 1
    m = n_tiles * tm
    gidx = x1g[X_PIECES, :, GROUP_LANE].astype(jnp.int32)
    onehot = (gidx[:, None] == jnp.arange(N_GROUPS, dtype=jnp.int32)[None, :]).astype(jnp.int32)
    counts = jnp.sum(onehot, axis=0)
    tiles = (counts + tm - 1) // tm
    tile_end = jnp.cumsum(tiles)
    tile_start = tile_end - tiles
    rank = jnp.sum((jnp.cumsum(onehot, axis=0) - onehot) * onehot, axis=1)
    pos = jnp.take(tile_start, gidx) * tm + rank
    pad_start = tile_start * tm + counts
    pad_end = (tile_end * tm).at[N_GROUPS - 1].set(m)
    pad_cum = jnp.cumsum(pad_end - pad_start)
    k = jnp.arange(m - n, dtype=jnp.int32)
    pg = jnp.searchsorted(pad_cum, k, side="right")
    pad_rows = jnp.take(pad_start, pg) + k - jnp.take(pad_cum - (pad_end - pad_start), pg)
    dst = jnp.concatenate([pos, pad_rows]).astype(jnp.int32)
    tile_group = jnp.searchsorted(tile_end, jnp.arange(n_tiles, dtype=jnp.int32), side="right")
    return dst, jnp.minimum(tile_group, N_GROUPS - 1).astype(jnp.int32), m


def _post_kernel(x1_ref, f_ref, p_ref, g2_ref, b2_ref, wpg_ref, wp_ref, y_ref, *, alpha):
    tm = x1_ref.shape[1]
    halves = [(i * tm // 2, (i + 1) * tm // 2) for i in range(2)]
    emb = [_dot(p_ref[lo:hi, :].astype(BF16), wp_ref[...]) for lo, hi in halves]

    def rows(ref, lo, hi):
        return jnp.concatenate([ref[c, lo:hi, :] for c in range(X_PIECES)], axis=1)

    x2 = [_layer_norm(alpha * rows(x1_ref, lo, hi) + rows(f_ref, lo, hi), g2_ref[...], b2_ref[...])
          for lo, hi in halves]
    gate = [_dot(x.astype(BF16), wpg_ref[...]) for x in x2]
    for (lo, hi), x, g, e in zip(halves, x2, gate, emb):
        y_ref[lo:hi, :] = x + jax.nn.sigmoid(g) * e


def _post(x1g, ffn, p_all, layer, g2, b2, wpg, wp, *, tm, alpha):
    _, n, _ = ffn.shape
    row = lambda w: pl.BlockSpec((tm, w), lambda i: (i, 0))
    pieces = lambda a: pl.BlockSpec((a.shape[0], tm, SC_ROW), lambda i: (0, i, 0))
    return pl.pallas_call(
        functools.partial(_post_kernel, alpha=alpha),
        grid=(n // tm,),
        in_specs=[pieces(x1g), pieces(ffn),
                  pl.BlockSpec((None, tm, PLE_DIM), lambda i: (layer, i, 0)),
                  _full(g2.shape), _full(b2.shape), _full(wpg.shape), _full(wp.shape)],
        out_specs=row(D_MODEL),
        out_shape=jax.ShapeDtypeStruct((n, D_MODEL), F32),
        compiler_params=_cparams(("parallel",)),
        name="post",
    )(x1g, ffn, p_all, g2, b2, wpg, wp)


def _rot_cols(w):
    half = ROPE_DIM // 2
    return jnp.concatenate([-w[..., half:], w[..., :half]], axis=-1)


def _prep_layer(l, w_in, w_qb, w_kvb, w_o, w_gate, w_up, w_down, w_ple, w_ple_gate):
    win = w_in[l]
    win_aug = jnp.concatenate([win, _rot_cols(win[:, COL_KPE:])], axis=-1).astype(BF16)
    wqb = w_qb[l].reshape(Q_LORA, HB_HEADS, NOPE_DIM + ROPE_DIM)
    wqb_aug = jnp.concatenate([wqb, _rot_cols(wqb[..., NOPE_DIM:])], axis=-1)
    wqb_aug = wqb_aug.reshape(Q_LORA, HB_HEADS * Q_HEAD_AUG).astype(BF16)
    wkvb = w_kvb[l].reshape(KV_LORA, HB_HEADS, NOPE_DIM + V_DIM)
    wuk_t = jnp.transpose(wkvb[..., :NOPE_DIM], (1, 2, 0)).astype(BF16)
    wuv = jnp.transpose(wkvb[..., NOPE_DIM:], (1, 0, 2)).astype(BF16)
    return dict(win=win_aug, wqb=wqb_aug, wuk=wuk_t, wuv=wuv, wo=w_o[l].astype(BF16),
                wg=w_gate[l].astype(BF16), wu=w_up[l].astype(BF16), wd=w_down[l].astype(BF16),
                wp=w_ple[l].astype(BF16), wpg=w_ple_gate[l].astype(BF16))


def _rope_table(pos):
    inv = ROPE_THETA ** (-np.arange(0, ROPE_DIM, 2, dtype=np.float64) / ROPE_DIM)
    ang = pos.astype(np.float64)[:, None] * inv[None, :]
    cos, sin = np.cos(ang), np.sin(ang)
    return np.concatenate([cos, cos, sin, sin], axis=-1).astype(np.float32)


def _tiles(n, t, prompt):
    if prompt:
        tm = min(512, t)
        return dict(tm=tm, tq=tm, tk=tm, tb=tm, cb=min(128, t), bb=2, tmoe=min(512, n))
    return dict(tm=min(512, n), tq=t, tk=None, tb=t, cb=t, bb=2, tmoe=min(256, n))


def _layer(x, p_all, layer, cs, s0_all, past, lbp, prm, small, *, alpha, prompt):
    bsz, t, _ = x.shape
    n = bsz * t
    tl = _tiles(n, t, prompt)
    x2d = x.reshape(n, D_MODEL)
    tm, tb, cb = tl["tm"], tl["tb"], tl["cb"]
    outs = _inproj(x2d, cs, prm["win"], lbp, small["qg"], prm["wqb"], small["kvg"], prm["wuk"],
                   tm=tm, tq=tl["tq"], cb=cb, prompt=prompt)
    qs, kin, logf, v, gs, ckv, kpe = outs[:7]
    r4 = lambda a: a.reshape(HA_HEADS, bsz, t, HA_DK)
    if prompt:
        qt, vt, kcat, ckvt = outs[7:]
        vt = vt.reshape(bsz, t // tb, HA_HEADS, tb // cb, HA_DV, cb)
    else:
        qabs, qpe = outs[7:]
        vt = jnp.transpose(r4(v), (1, 0, 3, 2)).astype(BF16)
        vt = vt.reshape(bsz, 1, HA_HEADS, 1, HA_DV, t)
    o_raw, s_new = _hgrn(r4(qs), r4(kin), r4(logf), r4(v), vt, s0_all, layer if not prompt else 0,
                         bb=tl["bb"], tb=tb, cb=cb)
    if prompt:
        olat = _attn_prompt(qt, kcat, ckvt, bsz=bsz, t=t, tq=tl["tq"], tk=tl["tk"])
    else:
        olat = _attn_sample(qabs, qpe, past[0], past[1], layer, ckv, kpe, ts=t)
    x1g = _merge(x2d, o_raw.reshape(HA_HEADS, n, HA_DV), gs, olat, prm["wuv"], prm["wo"],
                 small["hg"], small["mg"], small["g1"], small["b1"],
                 small["wr"], small["rb"], tm=tm, tq=tl["tq"], alpha=alpha)
    tmoe = tl["tmoe"]
    dst, tile_group, m_sorted = _route(x1g, tm=tmoe)
    xs = _sc_scatter_rows(x1g, dst, m_sorted)
    ys = _moe_sorted(tile_group, xs, prm["wg"], prm["wu"], prm["wd"], tm=tmoe)
    ffn = _sc_gather_rows(ys, dst[:n])
    y = _post(x1g, ffn, p_all.reshape(-1, n, PLE_DIM), layer, small["g2"], small["b2"],
              prm["wpg"], prm["wp"], tm=tm, alpha=alpha)
    return (y.reshape(bsz, t, D_MODEL), s_new, ckv.reshape(bsz, t, KV_LORA),
            kpe.reshape(bsz, t, ROPE_DIM))


def kernel(x_prompt, x_sample, p_prompt, p_sample, state_hgrn, cache_ckv, cache_kpe, w_in,
           lb_logits, hgrn_norm_g, q_norm_g, w_qb, kv_norm_g, w_kvb, mla_norm_g, w_o,
           ln1_g, ln1_b, ln2_g, ln2_b, w_router, router_bias, w_gate, w_up, w_down,
           w_ple, w_ple_gate):
    depth = w_in.shape[0]
    alpha = (2 * depth) ** 0.25
    bp, tp, _ = x_prompt.shape
    bs, ts, _ = x_sample.shape
    past = cache_ckv.shape[2]

    sm = jax.nn.softmax(lb_logits.astype(F32), axis=0)
    lb_all = jnp.maximum(jnp.cumsum(sm, axis=0) - sm[0:1], 0.0)
    lbp_all = jnp.stack([jnp.log(lb_all), jnp.log1p(-lb_all), 1.0 - lb_all], axis=1)

    cs_p = jnp.asarray(_rope_table(np.arange(tp)))
    tm_s = _tiles(bs * ts, ts, False)["tm"]
    cs_s = jnp.asarray(np.tile(_rope_table(past + np.arange(ts)), (tm_s // ts, 1)))
    wr = jnp.pad(w_router, ((0, 0), (0, LANES - N_EXPERTS))).astype(BF16)
    rb = jnp.pad(router_bias.astype(F32), (0, LANES - N_EXPERTS)).reshape(1, LANES)
    s0_p = jnp.zeros((1, bp, HA_HEADS, HA_DK, HA_DV), F32)
    cache_kpe_t = jnp.swapaxes(cache_kpe, 2, 3)

    yp, ys = x_prompt, x_sample
    res = [[] for _ in range(6)]
    for l in range(depth):
        prm = _prep_layer(l, w_in, w_qb, w_kvb, w_o, w_gate, w_up, w_down, w_ple, w_ple_gate)
        row = lambda a: a[l].reshape(1, -1).astype(F32)
        small = dict(qg=row(q_norm_g), kvg=row(kv_norm_g), hg=row(hgrn_norm_g), mg=row(mla_norm_g),
                     g1=row(ln1_g), b1=row(ln1_b), g2=row(ln2_g), b2=row(ln2_b), wr=wr, rb=rb)
        yp, sp, cp, kp = _layer(yp, p_prompt, l, cs_p, s0_p, None, lbp_all[l], prm, small,
                                alpha=alpha, prompt=True)
        ys, ss, cs_, ks = _layer(ys, p_sample, l, cs_s, state_hgrn, (cache_ckv, cache_kpe_t),
                                 lbp_all[l], prm, small, alpha=alpha, prompt=False)
        for lst, a in zip(res, (sp, cp, kp, ss, cs_, ks)):
            lst.append(a)
    sp, cp, kp, ss, cs_, ks = (jnp.stack(a) for a in res)
    return (yp, ys, sp, cp, kp, ss, cs_, ks)
```

```python
import functools

import jax
import jax.numpy as jnp
import numpy as np
from jax import lax
from jax.experimental import pallas as pl
from jax.experimental.pallas import tpu as pltpu
from jax.experimental.pallas import tpu_sc as plsc

F32 = jnp.float32
BF16 = jnp.bfloat16

D_MODEL = 1024
HA_HEADS = 4
HA_DK = 128
HA_DV = 128
HB_HEADS = 4
Q_LORA = 384
KV_LORA = 256
NOPE_DIM = 128
ROPE_DIM = 64
V_DIM = 128
ROPE_THETA = 10000.0
MLA_SCALE = (NOPE_DIM + ROPE_DIM) ** -0.5
LOG2E = 1.4426950408889634
Q_SCALE = MLA_SCALE * LOG2E
CHUNK = 64
N_EXPERTS = 16
N_GROUPS = 4
EXPERTS_PER_GROUP = N_EXPERTS // N_GROUPS
D_EXPERT = 512
PLE_DIM = 256
NEG_INF = -1e30

HA_W = HA_HEADS * HA_DK
COL_CQ = 4 * HA_W
COL_CKV = COL_CQ + Q_LORA
COL_KPE = COL_CKV + KV_LORA
D_IN_AUG = COL_KPE + 2 * ROPE_DIM
Q_HEAD_AUG = NOPE_DIM + 2 * ROPE_DIM

LANES = 128
SUBLANES = 8
QK_DIM = KV_LORA + LANES
GROUP_LANE = N_EXPERTS
SC_WINDOW = 128
SC_ROW = 256
X_PIECES = D_MODEL // SC_ROW
SUB_BLOCK = 16
VMEM_LIMIT = 56 * 1024 * 1024


def _cparams(sem, vmem=VMEM_LIMIT):
    return pltpu.CompilerParams(dimension_semantics=sem, vmem_limit_bytes=vmem)


def _dot(a, b):
    return jnp.dot(a, b, preferred_element_type=F32)


def _dot_nt(a, b):
    return lax.dot_general(a, b, (((1,), (1,)), ((), ())), preferred_element_type=F32)


def _rms(x, g, eps=1e-6):
    return x * lax.rsqrt(jnp.mean(x * x, axis=-1, keepdims=True) + eps) * g


def _layer_norm(x, g, b, eps=1e-5):
    mu = jnp.mean(x, axis=-1, keepdims=True)
    xc = x - mu
    var = jnp.mean(xc * xc, axis=-1, keepdims=True)
    return xc * lax.rsqrt(var + eps) * g + b


def _silu(x):
    return x * jax.nn.sigmoid(x)


def _full(shape):
    nd = len(shape)
    return pl.BlockSpec(shape, lambda *_: (0,) * nd)


def _inproj_kernel(x_ref, cs_ref, win_ref, lb_ref, qg_ref, wqb_ref, kvg_ref, wuk_ref,
                   qs_ref, kin_ref, logf_ref, v_ref, gs_ref, ckv_ref, kpe_ref, *rest,
                   tq, cb, prompt):
    tm = x_ref.shape[0]
    x = x_ref[...].astype(BF16)

    def sect(lo, hi):
        return _dot(x, win_ref[:, lo:hi])

    def put_heads(ref, val):
        for h in range(HA_HEADS):
            ref[h] = val[:, h * HA_DK:(h + 1) * HA_DK]

    cqn = _rms(sect(COL_CQ, COL_CKV), qg_ref[...]).astype(BF16)

    put_heads(qs_ref, _silu(sect(0, HA_W)))
    fa = sect(HA_W, 2 * HA_W)
    log_lb = lb_ref[0:1, :]
    log1m_lb = lb_ref[1:2, :]
    one_m_lb = lb_ref[2:3, :]
    e = jnp.exp(-jnp.abs(fa))
    r = 1.0 / (1.0 + e)
    c = log1m_lb + jnp.minimum(fa, 0.0) + jnp.log(r)
    put_heads(logf_ref, jnp.maximum(log_lb, c) + jnp.log(1.0 + jnp.exp(-jnp.abs(log_lb - c))))
    put_heads(kin_ref, one_m_lb * jnp.where(fa >= 0.0, e * r, r))
    v = sect(2 * HA_W, 3 * HA_W)
    put_heads(v_ref, v)
    gs_ref[...] = _silu(sect(3 * HA_W, 4 * HA_W))

    cs = cs_ref[...]

    def rope(t):
        prod = t * cs
        return prod + pltpu.roll(prod, ROPE_DIM, 1)

    ckv = _rms(sect(COL_CKV, COL_KPE), kvg_ref[...])
    ckv_ref[...] = ckv
    kpe2 = rope(sect(COL_KPE, D_IN_AUG))
    kpe_ref[...] = kpe2[:, :ROPE_DIM]
    if prompt:
        qt_ref, vt_ref, kcat_ref, ckvt_ref = rest
        for h in range(HA_HEADS):
            for ci in range(tm // cb):
                vt_ref[0, h, ci] = v[ci * cb:(ci + 1) * cb, h * HA_DV:(h + 1) * HA_DV].T.astype(BF16)
        kcat_ref[:, :KV_LORA] = ckv.astype(BF16)
        kcat_ref[:, KV_LORA:] = kpe2.astype(BF16)
        ckvt_ref[0] = ckv.T.astype(BF16)
    else:
        qabs_ref, qpe_ref = rest

    lane = lax.broadcasted_iota(jnp.int32, (tm, LANES), 1)
    qhs = [_dot(cqn, wqb_ref[:, h * Q_HEAD_AUG:(h + 1) * Q_HEAD_AUG]) for h in range(HB_HEADS)]
    qabss = [_dot(qhs[h][:, :NOPE_DIM].astype(BF16), wuk_ref[h]) for h in range(HB_HEADS)]
    for h in range(HB_HEADS):
        qabs = qabss[h] * Q_SCALE
        qpe = jnp.where(lane < ROPE_DIM, rope(qhs[h][:, NOPE_DIM:]) * Q_SCALE, 0.0)
        if prompt:
            qabs_t = qabs.T.astype(BF16)
            qpe_t = qpe.T.astype(BF16)
            for s in range(tm // tq):
                qt_ref[s, :KV_LORA, h * tq:(h + 1) * tq] = qabs_t[:, s * tq:(s + 1) * tq]
                qt_ref[s, KV_LORA:, h * tq:(h + 1) * tq] = qpe_t[:, s * tq:(s + 1) * tq]
        else:
            for s in range(tm // tq):
                qabs_ref[s, h] = qabs[s * tq:(s + 1) * tq].astype(BF16)
                qpe_ref[s, h] = qpe[s * tq:(s + 1) * tq].astype(BF16)


def _inproj(x, cs, w_in, lbp, qg, wqb, kvg, wuk, *, tm, tq, cb, prompt):
    n = x.shape[0]
    nt = n // tm
    row = lambda w: pl.BlockSpec((tm, w), lambda i: (i, 0))
    heads = pl.BlockSpec((HA_HEADS, tm, HA_DK), lambda i: (0, i, 0))
    qblk = lambda w: pl.BlockSpec((tm // tq, HB_HEADS, tq, w), lambda i: (i, 0, 0, 0))
    head_major = jax.ShapeDtypeStruct((HA_HEADS, n, HA_DK), F32)
    out_shape = [
        head_major,
        head_major,
        head_major,
        head_major,
        jax.ShapeDtypeStruct((n, HA_W), F32),
        jax.ShapeDtypeStruct((n, KV_LORA), F32),
        jax.ShapeDtypeStruct((n, ROPE_DIM), F32),
    ]
    out_specs = [heads, heads, heads, heads, row(HA_W), row(KV_LORA), row(ROPE_DIM)]
    if prompt:
        out_shape += [jax.ShapeDtypeStruct((n // tq, QK_DIM, HB_HEADS * tq), BF16),
                      jax.ShapeDtypeStruct((nt, HA_HEADS, tm // cb, HA_DV, cb), BF16),
                      jax.ShapeDtypeStruct((n, QK_DIM), BF16),
                      jax.ShapeDtypeStruct((nt, KV_LORA, tm), BF16)]
        out_specs += [pl.BlockSpec((tm // tq, QK_DIM, HB_HEADS * tq), lambda i: (i, 0, 0)),
                      pl.BlockSpec((1, HA_HEADS, tm // cb, HA_DV, cb), lambda i: (i, 0, 0, 0, 0)),
                      row(QK_DIM),
                      pl.BlockSpec((1, KV_LORA, tm), lambda i: (i, 0, 0))]
    else:
        out_shape += [jax.ShapeDtypeStruct((n // tq, HB_HEADS, tq, KV_LORA), BF16),
                      jax.ShapeDtypeStruct((n // tq, HB_HEADS, tq, LANES), BF16)]
        out_specs += [qblk(KV_LORA), qblk(LANES)]
    return pl.pallas_call(
        functools.partial(_inproj_kernel, tq=tq, cb=cb, prompt=prompt),
        grid=(nt,),
        in_specs=[row(D_MODEL),
                  pl.BlockSpec((tm, LANES), lambda i: (i % (cs.shape[0] // tm), 0)),
                  _full(w_in.shape), _full(lbp.shape), _full(qg.shape),
                  _full(wqb.shape), _full(kvg.shape), _full(wuk.shape)],
        out_specs=out_specs,
        out_shape=out_shape,
        compiler_params=_cparams(("parallel",)),
        name="inproj",
    )(x, cs, w_in, lbp, qg, wqb, kvg, wuk)


def _hgrn_kernel(q_ref, k_ref, g_ref, v_ref, vt_ref, s0_ref, o_ref, sfin_ref, st_scr, b2_scr,
                 hide_ref, *, cb):
    _, bb, tb, _ = q_ref.shape
    nsub = cb // SUB_BLOCK
    ti = pl.program_id(1)

    @pl.when(ti == 0)
    def _():
        for b in range(bb):
            for h in range(HA_HEADS):
                st_scr[b, h] = s0_ref[b, h].T
        row8 = lax.broadcasted_iota(jnp.int32, (SUBLANES, LANES), 0)
        for s in range(SUBLANES):
            hide_ref[s] = jnp.where(row8 >= s, 0.0, NEG_INF)

    r_i = lax.broadcasted_iota(jnp.int32, (cb, cb), 0)
    c_i = lax.broadcasted_iota(jnp.int32, (cb, cb), 1)
    tril = (r_i >= c_i).astype(F32)

    chains = [(b, h) for b in range(bb) for h in range(HA_HEADS)]

    def matmul_part(ci):
        r0 = pl.multiple_of(ci * cb, cb)
        rows = pl.ds(r0, cb)

        def sub(ref, b, h, lo, size):
            return ref[h, b, pl.ds(pl.multiple_of(r0 + lo, SUB_BLOCK), size), :]

        for b, h in chains:
            bcum = jnp.dot(tril, g_ref[h, b, rows, :], precision=lax.Precision.HIGHEST,
                           preferred_element_type=F32)
            b2_scr[b * HA_HEADS + h] = bcum * LOG2E
        for b, h in chains:
            b2 = b2_scr[b * HA_HEADS + h]
            o_ref[h, b, rows, :] = _dot_nt((q_ref[h, b, rows, :] * jnp.exp2(b2)).astype(BF16),
                                           st_scr[b, h].astype(BF16))
        for j in range(nsub - 1):
            lo, hi = j * SUB_BLOCK, (j + 1) * SUB_BLOCK
            below = pl.ds(pl.multiple_of(r0 + hi, SUB_BLOCK), cb - hi)
            for b, h in chains:
                slot = b * HA_HEADS + h
                bnd = b2_scr[slot, hi - 1:hi, :]
                kd = (sub(k_ref, b, h, lo, SUB_BLOCK)
                      * jnp.exp2(bnd - b2_scr[slot, lo:hi, :])).astype(BF16)
                qe = (sub(q_ref, b, h, hi, cb - hi)
                      * jnp.exp2(b2_scr[slot, hi:, :] - bnd)).astype(BF16)
                a = _dot_nt(qe, kd).astype(BF16)
                o_ref[h, b, below, :] += _dot(a, sub(v_ref, b, h, lo, SUB_BLOCK).astype(BF16))
        for b, h in chains:
            slot = b * HA_HEADS + h
            b_last = b2_scr[slot, cb - 1:cb, :]
            kd_all = (k_ref[h, b, rows, :] * jnp.exp2(b_last - b2_scr[slot])).astype(BF16)
            st_scr[b, h] = st_scr[b, h] * jnp.exp2(b_last) + _dot(vt_ref[b, h, ci], kd_all)

    def pairwise_part(ci, b, h, j):
        slot = b * HA_HEADS + h
        r0 = pl.multiple_of(ci * cb, cb)
        lo = j * SUB_BLOCK
        mid = lo + SUBLANES
        top = pl.ds(pl.multiple_of(r0 + lo, SUBLANES), SUBLANES)
        bot = pl.ds(pl.multiple_of(r0 + mid, SUBLANES), SUBLANES)
        q_top, q_bot = q_ref[h, b, top, :], q_ref[h, b, bot, :]
        b_top, b_bot = b2_scr[slot, lo:mid, :], b2_scr[slot, mid:mid + SUBLANES, :]
        acc_top, acc_bot = o_ref[h, b, top, :], o_ref[h, b, bot, :]
        for s in range(SUB_BLOCK):
            row = pl.ds(r0 + lo + s, 1)
            bs = b2_scr[slot, lo + s:lo + s + 1, :]
            ks = k_ref[h, b, row, :]
            vs = v_ref[h, b, row, :]
            if s < SUBLANES:
                w = q_top * jnp.exp2(b_top - bs + hide_ref[s]) * ks
                acc_top = acc_top + jnp.sum(w, axis=-1, keepdims=True) * vs
                w = q_bot * jnp.exp2(b_bot - bs) * ks
            else:
                w = q_bot * jnp.exp2(b_bot - bs + hide_ref[s - SUBLANES]) * ks
            acc_bot = acc_bot + jnp.sum(w, axis=-1, keepdims=True) * vs
        o_ref[h, b, top, :] = acc_top
        o_ref[h, b, bot, :] = acc_bot

    def chunk(ci, carry):
        matmul_part(ci)
        for j in range(nsub):
            for b, h in chains:
                pairwise_part(ci, b, h, j)
        return carry

    lax.fori_loop(0, tb // cb, chunk, 0)

    @pl.when(ti == pl.num_programs(1) - 1)
    def _():
        for b in range(bb):
            for h in range(HA_HEADS):
                sfin_ref[b, h] = st_scr[b, h].T


def _hgrn(qs, kin, logf, v, vt, s0_all, layer, *, bb, tb, cb):
    _, bsz, t, _ = qs.shape
    blk = pl.BlockSpec((HA_HEADS, bb, tb, HA_DK), lambda i, j: (0, i, j, 0))
    st_blk = pl.BlockSpec((bb, HA_HEADS, HA_DK, HA_DV), lambda i, j: (i, 0, 0, 0))
    s0_blk = pl.BlockSpec((None, bb, HA_HEADS, HA_DK, HA_DV), lambda i, j: (layer, i, 0, 0, 0))
    vt_blk = pl.BlockSpec((bb, None, HA_HEADS, tb // cb, HA_DV, cb),
                          lambda i, j: (i, j, 0, 0, 0, 0))
    return pl.pallas_call(
        functools.partial(_hgrn_kernel, cb=cb),
        grid=(bsz // bb, t // tb),
        in_specs=[blk, blk, blk, blk, vt_blk, s0_blk],
        out_specs=[blk, st_blk],
        out_shape=[jax.ShapeDtypeStruct(qs.shape, F32),
                   jax.ShapeDtypeStruct(s0_all.shape[1:], F32)],
        scratch_shapes=[pltpu.VMEM((bb, HA_HEADS, HA_DV, HA_DK), F32),
                        pltpu.VMEM((bb * HA_HEADS, cb, HA_DK), F32),
                        pltpu.VMEM((SUBLANES, SUBLANES, LANES), F32)],
        compiler_params=_cparams(("parallel", "arbitrary")),
        name="hgrn",
    )(qs, kin, logf, v, vt, s0_all)


def _attn_prompt_kernel(qt_ref, kc_ref, kt_ref, o_ref, s0, s1, p0, p1, a0, a1, m_scr, l_scr,
                        acc_scr, *, tq, tk):
    qi = pl.program_id(1)
    jd = (qi * tq) // tk
    shift = CHUNK.bit_length() - 1

    heads = range(HB_HEADS)

    def scores(j, s_dst, hs=heads):
        keys = kc_ref[pl.ds(pl.multiple_of(j * tk, tk), tk), :]
        for h in hs:
            cols = slice(h * tq, (h + 1) * tq)
            s_dst[:, cols] = _dot(keys, qt_ref[0, :, cols])

    def softmax(j, s_src, p_dst, a_dst, masked, hs=heads):
        if masked:
            kpos = j * tk + lax.broadcasted_iota(jnp.int32, (tk, 1), 0)
            qpos = qi * tq + lax.broadcasted_iota(jnp.int32, (1, tq), 1)
            visible = (kpos >> shift) <= (qpos >> shift)
        for h in hs:
            cols = slice(h * tq, (h + 1) * tq)
            def load():
                s = s_src[:, cols]
                return jnp.where(visible, s, NEG_INF) if masked else s

            m_prev = m_scr[:, cols]
            m_new = jnp.maximum(m_prev, jnp.max(load(), axis=0, keepdims=True))
            alpha = jnp.exp2(m_prev - m_new)
            p = jnp.exp2(load() - m_new)
            l_scr[:, cols] = alpha * l_scr[:, cols] + jnp.sum(p, axis=0, keepdims=True)
            m_scr[:, cols] = m_new
            a_dst[:, cols] = alpha
            p_dst[:, cols] = p.astype(BF16)

    def values(j, p_src, a_src, hs=heads):
        for h in hs:
            cols = slice(h * tq, (h + 1) * tq)
            acc_scr[:, cols] = (a_src[:, cols] * acc_scr[:, cols]
                                + _dot(kt_ref[j], p_src[:, cols]))

    def trip(j, s_cur, s_nxt, p_cur, p_prv, a_cur, a_prv):
        scores(j + 1, s_nxt)
        softmax(j, s_cur, p_cur, a_cur, False)
        values(jnp.maximum(j - 1, 0), p_prv, a_prv)

    m_scr[...] = jnp.full(m_scr.shape, NEG_INF, F32)
    l_scr[...] = jnp.zeros(l_scr.shape, F32)
    acc_scr[...] = jnp.zeros(acc_scr.shape, F32)
    p1[...] = jnp.zeros(p1.shape, BF16)
    a1[...] = jnp.ones(a1.shape, F32)
    scores(0, s0)

    def pair(jj, carry):
        trip(2 * jj, s0, s1, p0, p1, a0, a1)
        trip(2 * jj + 1, s1, s0, p1, p0, a1, a0)
        return carry

    lax.fori_loop(0, jd // 2, pair, 0)

    @pl.when(jd % 2 == 1)
    def _():
        trip(jd - 1, s0, s1, p0, p1, a0, a1)

    def drain(s_cur, p_cur, p_prv, a_cur, a_prv):
        softmax(jd, s_cur, p_cur, a_cur, True)
        values(jnp.maximum(jd - 1, 0), p_prv, a_prv)
        values(jd, p_cur, a_cur)

    pl.when(jd % 2 == 0)(functools.partial(drain, s0, p0, p1, a0, a1))
    pl.when(jd % 2 == 1)(functools.partial(drain, s1, p1, p0, a1, a0))

    for h in range(HB_HEADS):
        cols = slice(h * tq, (h + 1) * tq)
        o_ref[0, h] = (acc_scr[:, cols] * (1.0 / l_scr[:, cols])).T.astype(BF16)


def _attn_prompt(qt, kcat, ckvt, *, bsz, t, tq, tk):
    nq = t // tq
    nk = t // tk
    cols = HB_HEADS * tq
    single = pl.Buffered(1)
    return pl.pallas_call(
        functools.partial(_attn_prompt_kernel, tq=tq, tk=tk),
        grid=(bsz, nq),
        in_specs=[
            pl.BlockSpec((1, QK_DIM, cols), lambda b, i: (b * nq + i, 0, 0)),
            pl.BlockSpec((t, QK_DIM), lambda b, i: (b, 0), pipeline_mode=single),
            pl.BlockSpec((nk, KV_LORA, tk), lambda b, i: (b, 0, 0), pipeline_mode=single),
        ],
        out_specs=pl.BlockSpec((1, HB_HEADS, tq, KV_LORA), lambda b, i: (b * nq + i, 0, 0, 0)),
        out_shape=jax.ShapeDtypeStruct((bsz * nq, HB_HEADS, tq, KV_LORA), BF16),
        scratch_shapes=[pltpu.VMEM((tk, cols), F32), pltpu.VMEM((tk, cols), F32),
                        pltpu.VMEM((tk, cols), BF16), pltpu.VMEM((tk, cols), BF16),
                        pltpu.VMEM((1, cols), F32), pltpu.VMEM((1, cols), F32),
                        pltpu.VMEM((1, cols), F32), pltpu.VMEM((1, cols), F32),
                        pltpu.VMEM((KV_LORA, cols), F32)],
        compiler_params=_cparams(("parallel", "arbitrary")),
        name="attn_prompt",
    )(qt, kcat, ckvt)


def _attn_sample_kernel(qa_ref, qp_ref, pckv_ref, pkpe_ref, nckv_ref, nkpe_ref, o_ref):
    _, _, ts, _ = qa_ref.shape
    rows = HB_HEADS * ts
    qa = qa_ref[0].reshape(rows, KV_LORA)
    qp = qp_ref[0].reshape(rows, LANES)[:, :ROPE_DIM]
    pckv = pckv_ref[0].astype(BF16)
    nckv = nckv_ref[...].astype(BF16)
    s_past = _dot_nt(qa, pckv) + _dot(qp, pkpe_ref[0].astype(BF16))
    s_new = _dot_nt(qa, nckv) + _dot_nt(qp, nkpe_ref[...].astype(BF16))
    m = jnp.maximum(jnp.max(s_past, axis=-1, keepdims=True), jnp.max(s_new, axis=-1, keepdims=True))
    p_past = jnp.exp2(s_past - m)
    p_new = jnp.exp2(s_new - m)
    l = jnp.sum(p_past, axis=-1, keepdims=True) + jnp.sum(p_new, axis=-1, keepdims=True)
    o = (_dot(p_past.astype(BF16), pckv) + _dot(p_new.astype(BF16), nckv)) / l
    o_ref[0] = o.astype(BF16).reshape(HB_HEADS, ts, KV_LORA)


def _attn_sample(qabs, qpe, past_ckv, past_kpe, layer, ckv, kpe, *, ts):
    _, bsz, past, _ = past_ckv.shape
    return pl.pallas_call(
        _attn_sample_kernel,
        grid=(bsz,),
        in_specs=[
            pl.BlockSpec((1, HB_HEADS, ts, KV_LORA), lambda b: (b, 0, 0, 0)),
            pl.BlockSpec((1, HB_HEADS, ts, LANES), lambda b: (b, 0, 0, 0)),
            pl.BlockSpec((None, 1, past, KV_LORA), lambda b: (layer, b, 0, 0)),
            pl.BlockSpec((None, 1, ROPE_DIM, past), lambda b: (layer, b, 0, 0)),
            pl.BlockSpec((ts, KV_LORA), lambda b: (b, 0)),
            pl.BlockSpec((ts, ROPE_DIM), lambda b: (b, 0)),
        ],
        out_specs=pl.BlockSpec((1, HB_HEADS, ts, KV_LORA), lambda b: (b, 0, 0, 0)),
        out_shape=jax.ShapeDtypeStruct(qabs.shape, BF16),
        compiler_params=_cparams(("parallel",)),
        name="attn_sample",
    )(qabs, qpe, past_ckv, past_kpe, ckv, kpe)


def _gates(logits, bias):
    tm = logits.shape[0]
    lane = lax.broadcasted_iota(jnp.int32, (tm, LANES), 1)
    pos = lane % EXPERTS_PER_GROUP
    valid = lane < N_EXPERTS
    scores = jax.nn.sigmoid(logits)
    sel = jnp.where(valid, scores + bias, -jnp.inf)

    others = []
    for r in range(1, EXPERTS_PER_GROUP):
        others.append(jnp.where(pos >= r, pltpu.roll(sel, r, 1),
                                pltpu.roll(sel, LANES - (EXPERTS_PER_GROUP - r), 1)))
    a, b, c, d = sel, others[0], others[1], others[2]
    hi1, lo1 = jnp.maximum(a, b), jnp.minimum(a, b)
    hi2, lo2 = jnp.maximum(c, d), jnp.minimum(c, d)
    gscore = jnp.maximum(hi1, hi2) + jnp.maximum(jnp.minimum(hi1, hi2), jnp.maximum(lo1, lo2))
    gmax = jnp.max(gscore, axis=-1, keepdims=True)
    group = (lane // EXPERTS_PER_GROUP).astype(F32)
    gidx = jnp.min(jnp.where(gscore == gmax, group, float(LANES)), axis=-1, keepdims=True)

    rank = jnp.zeros((tm, LANES), jnp.int32)
    for r, o in enumerate(others, start=1):
        ahead = (o > sel) | ((o == sel) & (pos >= r))
        rank = rank + ahead.astype(jnp.int32)
    chosen = (group == gidx) & (rank < 2) & valid
    w = jnp.where(chosen, scores, 0.0)
    w = w / jnp.sum(w, axis=-1, keepdims=True)
    return jnp.where(lane == GROUP_LANE, gidx, w)


def _merge_kernel(x_ref, o_ref, gs_ref, olat_ref, wuv_ref, wo_ref, hg_ref, mg_ref,
                  g1_ref, b1_ref, wr_ref, rb_ref, x1g_ref, *, alpha):
    _, _, tq, _ = olat_ref.shape
    tm = x_ref.shape[0]
    hg = hg_ref[...]
    d_a = HA_HEADS * HA_DV
    halves = [(i * tm // 2, (i + 1) * tm // 2) for i in range(2)]

    def lat(h, lo, hi):
        if tq >= hi - lo:
            return olat_ref[lo // tq, h, lo % tq:lo % tq + hi - lo]
        return jnp.concatenate([olat_ref[s, h] for s in range(lo // tq, hi // tq)], axis=0)

    ob = [[_dot(lat(h, lo, hi), wuv_ref[h]) for h in range(HB_HEADS)] for lo, hi in halves]
    oa = [jnp.concatenate([_rms(o_ref[h, lo:hi, :], hg) * gs_ref[lo:hi, h * HA_DV:(h + 1) * HA_DV]
                           for h in range(HA_HEADS)], axis=-1) for lo, hi in halves]
    ob = [_rms(jnp.concatenate(o, axis=-1), mg_ref[...]) for o in ob]
    mix = [_dot(a.astype(BF16), wo_ref[:d_a, :]) + _dot(o.astype(BF16), wo_ref[d_a:, :])
           for a, o in zip(oa, ob)]
    x1 = [_layer_norm(alpha * x_ref[lo:hi, :] + m, g1_ref[...], b1_ref[...])
          for (lo, hi), m in zip(halves, mix)]
    logits = [_dot(x.astype(BF16), wr_ref[...]) for x in x1]
    for (lo, hi), x, lg in zip(halves, x1, logits):
        for c in range(X_PIECES):
            x1g_ref[c, lo:hi, :] = x[:, c * SC_ROW:(c + 1) * SC_ROW]
        x1g_ref[X_PIECES, lo:hi, :LANES] = _gates(lg, rb_ref[...])
        x1g_ref[X_PIECES, lo:hi, LANES:] = jnp.zeros((hi - lo, SC_ROW - LANES), F32)


def _merge(x, o_raw, gs, olat, wuv, wo, hg, mg, g1, b1, wr, rb, *, tm, tq, alpha):
    n = x.shape[0]
    row = lambda w: pl.BlockSpec((tm, w), lambda i: (i, 0))
    return pl.pallas_call(
        functools.partial(_merge_kernel, alpha=alpha),
        grid=(n // tm,),
        in_specs=[row(D_MODEL), pl.BlockSpec((HA_HEADS, tm, HA_DV), lambda i: (0, i, 0)), row(HA_W),
                  pl.BlockSpec((tm // tq, HB_HEADS, tq, KV_LORA), lambda i: (i, 0, 0, 0)),
                  _full(wuv.shape), _full(wo.shape), _full(hg.shape), _full(mg.shape),
                  _full(g1.shape), _full(b1.shape), _full(wr.shape), _full(rb.shape)],
        out_specs=pl.BlockSpec((X_PIECES + 1, tm, SC_ROW), lambda i: (0, i, 0)),
        out_shape=jax.ShapeDtypeStruct((X_PIECES + 1, n, SC_ROW), F32),
        compiler_params=_cparams(("parallel",)),
        name="merge",
    )(x, o_raw, gs, olat, wuv, wo, hg, mg, g1, b1, wr, rb)


def _sc_scatter_rows(src, dst, m):
    parts, n, _ = src.shape
    windows = dst.shape[0] // SC_WINDOW
    src_windows = n // SC_WINDOW
    pieces = src.reshape(parts * n, SC_ROW)
    piece_dst = (jnp.arange(parts, dtype=jnp.int32)[:, None] * m + dst[None, :]).reshape(1, -1)
    mesh = plsc.VectorSubcoreMesh(core_axis_name="core", subcore_axis_name="subcore")

    @functools.partial(pl.kernel, out_type=jax.ShapeDtypeStruct((parts * m, SC_ROW), src.dtype),
                       mesh=mesh)
    def scatter(x_hbm, i_hbm, o_hbm):
        def body(x_vmem, i_vmem):
            pltpu.sync_copy(x_vmem, o_hbm.at[i_vmem.at[0]])

        pltpu.emit_pipeline(
            body,
            grid=(parts * windows,),
            in_specs=[pl.BlockSpec((SC_WINDOW, SC_ROW),
                                   lambda i: ((i // windows) * src_windows
                                              + (i % windows) % src_windows, 0)),
                      pl.BlockSpec((1, SC_WINDOW), lambda i: (0, i))],
            out_specs=[],
            core_axis_name=("core", "subcore"),
            dimension_semantics=(pltpu.PARALLEL,),
        )(x_hbm, i_hbm)

    return scatter(pieces, piece_dst).reshape(parts, m, SC_ROW)


def _sc_gather_rows(src, idx):
    parts, n, _ = src.shape
    m = idx.shape[0] * parts
    pieces = src.reshape(parts * n, SC_ROW)
    piece_idx = (jnp.arange(parts, dtype=jnp.int32)[:, None] * n + idx[None, :]).reshape(1, m)
    mesh = plsc.VectorSubcoreMesh(core_axis_name="core", subcore_axis_name="subcore")

    @functools.partial(pl.kernel, out_type=jax.ShapeDtypeStruct((m, SC_ROW), src.dtype), mesh=mesh)
    def gather(x_hbm, i_hbm, o_hbm):
        def body(i_vmem, o_vmem):
            pltpu.sync_copy(x_hbm.at[i_vmem.at[0]], o_vmem)

        pltpu.emit_pipeline(
            body,
            grid=(m // SC_WINDOW,),
            in_specs=[pl.BlockSpec((1, SC_WINDOW), lambda i: (0, i))],
            out_specs=[pl.BlockSpec((SC_WINDOW, SC_ROW), lambda i: (i, 0))],
            core_axis_name=("core", "subcore"),
            dimension_semantics=(pltpu.PARALLEL,),
        )(i_hbm, o_hbm)

    return gather(pieces, piece_idx).reshape(parts, idx.shape[0], SC_ROW)


def _moe_kernel(tg_ref, xs_ref, wg_ref, wu_ref, wd_ref, y_ref):
    tm = xs_ref.shape[1]
    g = tg_ref[pl.program_id(0)]
    xb = jnp.concatenate([xs_ref[c] for c in range(X_PIECES)], axis=1).astype(BF16)
    gates = xs_ref[X_PIECES, :, :LANES]
    lane = lax.broadcasted_iota(jnp.int32, (tm, LANES), 1)
    acc = jnp.zeros((tm, D_MODEL), F32)
    for k in range(EXPERTS_PER_GROUP):
        hmid = (_silu(_dot(xb, wg_ref[k].astype(BF16))) * _dot(xb, wu_ref[k].astype(BF16)))
        gcol = jnp.sum(jnp.where(lane == g * EXPERTS_PER_GROUP + k, gates, 0.0),
                       axis=-1, keepdims=True)
        acc = acc + gcol * _dot(hmid.astype(BF16), wd_ref[k].astype(BF16))
    for c in range(D_MODEL // SC_ROW):
        y_ref[c] = acc[:, c * SC_ROW:(c + 1) * SC_ROW]


def _moe_sorted(tile_group, xs, wg, wu, wd, layer, *, tm):
    m = xs.shape[1]
    wspec = lambda shape: pl.BlockSpec((None, EXPERTS_PER_GROUP) + shape,
                                       lambda i, tg: (layer, tg[i], 0, 0),
                                       pipeline_mode=pl.Buffered(1))
    return pl.pallas_call(
        _moe_kernel,
        grid_spec=pltpu.PrefetchScalarGridSpec(
            num_scalar_prefetch=1,
            grid=(m // tm,),
            in_specs=[pl.BlockSpec((X_PIECES + 1, tm, SC_ROW), lambda i, tg: (0, i, 0)),
                      wspec((D_MODEL, D_EXPERT)), wspec((D_MODEL, D_EXPERT)),
                      wspec((D_EXPERT, D_MODEL))],
            out_specs=pl.BlockSpec((D_MODEL // SC_ROW, tm, SC_ROW), lambda i, tg: (0, i, 0)),
        ),
        out_shape=jax.ShapeDtypeStruct((D_MODEL // SC_ROW, m, SC_ROW), F32),
        compiler_params=_cparams(("arbitrary",)),
        name="moe",
    )(tile_group, xs, wg, wu, wd)


def _route(x1g, *, tm):
    n = x1g.shape[1]
    n_tiles = n // tm + N_GROUPS - 1
    m = n_tiles * tm
    gidx = x1g[X_PIECES, :, GROUP_LANE].astype(jnp.int32)
    onehot = (gidx[:, None] == jnp.arange(N_GROUPS, dtype=jnp.int32)[None, :]).astype(jnp.int32)
    counts = jnp.sum(onehot, axis=0)
    tiles = (counts + tm - 1) // tm
    tile_end = jnp.cumsum(tiles)
    tile_start = tile_end - tiles
    rank = jnp.sum((jnp.cumsum(onehot, axis=0) - onehot) * onehot, axis=1)
    pos = jnp.take(tile_start, gidx) * tm + rank
    pad_start = tile_start * tm + counts
    pad_end = (tile_end * tm).at[N_GROUPS - 1].set(m)
    pad_cum = jnp.cumsum(pad_end - pad_start)
    k = jnp.arange(m - n, dtype=jnp.int32)
    pg = jnp.searchsorted(pad_cum, k, side="right")
    pad_rows = jnp.take(pad_start, pg) + k - jnp.take(pad_cum - (pad_end - pad_start), pg)
    dst = jnp.concatenate([pos, pad_rows]).astype(jnp.int32)
    tile_group = jnp.searchsorted(tile_end, jnp.arange(n_tiles, dtype=jnp.int32), side="right")
    return dst, jnp.minimum(tile_group, N_GROUPS - 1).astype(jnp.int32), m


def _post_kernel(x1_ref, f_ref, p_ref, g2_ref, b2_ref, wpg_ref, wp_ref, y_ref, *, alpha):
    tm = x1_ref.shape[1]
    halves = [(i * tm // 2, (i + 1) * tm // 2) for i in range(2)]
    emb = [_dot(p_ref[lo:hi, :].astype(BF16), wp_ref[...]) for lo, hi in halves]

    def rows(ref, lo, hi):
        return jnp.concatenate([ref[c, lo:hi, :] for c in range(X_PIECES)], axis=1)

    x2 = [_layer_norm(alpha * rows(x1_ref, lo, hi) + rows(f_ref, lo, hi), g2_ref[...], b2_ref[...])
          for lo, hi in halves]
    gate = [_dot(x.astype(BF16), wpg_ref[...]) for x in x2]
    for (lo, hi), x, g, e in zip(halves, x2, gate, emb):
        y_ref[lo:hi, :] = x + jax.nn.sigmoid(g) * e


def _post(x1g, ffn, p_all, layer, g2, b2, wpg, wp, *, tm, alpha):
    _, n, _ = ffn.shape
    row = lambda w: pl.BlockSpec((tm, w), lambda i: (i, 0))
    pieces = lambda a: pl.BlockSpec((a.shape[0], tm, SC_ROW), lambda i: (0, i, 0))
    return pl.pallas_call(
        functools.partial(_post_kernel, alpha=alpha),
        grid=(n // tm,),
        in_specs=[pieces(x1g), pieces(ffn),
                  pl.BlockSpec((None, tm, PLE_DIM), lambda i: (layer, i, 0)),
                  _full(g2.shape), _full(b2.shape), _full(wpg.shape), _full(wp.shape)],
        out_specs=row(D_MODEL),
        out_shape=jax.ShapeDtypeStruct((n, D_MODEL), F32),
        compiler_params=_cparams(("parallel",)),
        name="post",
    )(x1g, ffn, p_all, g2, b2, wpg, wp)


def _rot_cols(w):
    half = ROPE_DIM // 2
    return jnp.concatenate([-w[..., half:], w[..., :half]], axis=-1)


def _prep_layer(l, w_in, w_qb, w_kvb, w_o, w_gate, w_up, w_down, w_ple, w_ple_gate):
    win = w_in[l]
    win_aug = jnp.concatenate([win, _rot_cols(win[:, COL_KPE:])], axis=-1).astype(BF16)
    wqb = w_qb[l].reshape(Q_LORA, HB_HEADS, NOPE_DIM + ROPE_DIM)
    wqb_aug = jnp.concatenate([wqb, _rot_cols(wqb[..., NOPE_DIM:])], axis=-1)
    wqb_aug = wqb_aug.reshape(Q_LORA, HB_HEADS * Q_HEAD_AUG).astype(BF16)
    wkvb = w_kvb[l].reshape(KV_LORA, HB_HEADS, NOPE_DIM + V_DIM)
    wuk_t = jnp.transpose(wkvb[..., :NOPE_DIM], (1, 2, 0)).astype(BF16)
    wuv = jnp.transpose(wkvb[..., NOPE_DIM:], (1, 0, 2)).astype(BF16)
    return dict(win=win_aug, wqb=wqb_aug, wuk=wuk_t, wuv=wuv, wo=w_o[l].astype(BF16),
                wg=w_gate, wu=w_up, wd=w_down,
                wp=w_ple[l].astype(BF16), wpg=w_ple_gate[l].astype(BF16))


def _rope_table(pos):
    inv = ROPE_THETA ** (-np.arange(0, ROPE_DIM, 2, dtype=np.float64) / ROPE_DIM)
    ang = pos.astype(np.float64)[:, None] * inv[None, :]
    cos, sin = np.cos(ang), np.sin(ang)
    return np.concatenate([cos, cos, sin, sin], axis=-1).astype(np.float32)


def _tiles(n, t, prompt):
    if prompt:
        tm = min(512, t)
        return dict(tm=tm, tq=tm, tk=tm, tb=tm, cb=min(128, t), bb=2, tmoe=min(512, n))
    return dict(tm=min(512, n), tq=t, tk=None, tb=t, cb=t, bb=2, tmoe=min(256, n))


def _layer(x, p_all, layer, cs, s0_all, past, lbp, prm, small, *, alpha, prompt):
    bsz, t, _ = x.shape
    n = bsz * t
    tl = _tiles(n, t, prompt)
    x2d = x.reshape(n, D_MODEL)
    tm, tb, cb = tl["tm"], tl["tb"], tl["cb"]
    outs = _inproj(x2d, cs, prm["win"], lbp, small["qg"], prm["wqb"], small["kvg"], prm["wuk"],
                   tm=tm, tq=tl["tq"], cb=cb, prompt=prompt)
    qs, kin, logf, v, gs, ckv, kpe = outs[:7]
    r4 = lambda a: a.reshape(HA_HEADS, bsz, t, HA_DK)
    if prompt:
        qt, vt, kcat, ckvt = outs[7:]
        vt = vt.reshape(bsz, t // tb, HA_HEADS, tb // cb, HA_DV, cb)
    else:
        qabs, qpe = outs[7:]
        vt = jnp.transpose(r4(v), (1, 0, 3, 2)).astype(BF16)
        vt = vt.reshape(bsz, 1, HA_HEADS, 1, HA_DV, t)
    o_raw, s_new = _hgrn(r4(qs), r4(kin), r4(logf), r4(v), vt, s0_all, layer if not prompt else 0,
                         bb=tl["bb"], tb=tb, cb=cb)
    if prompt:
        olat = _attn_prompt(qt, kcat, ckvt, bsz=bsz, t=t, tq=tl["tq"], tk=tl["tk"])
    else:
        olat = _attn_sample(qabs, qpe, past[0], past[1], layer, ckv, kpe, ts=t)
    x1g = _merge(x2d, o_raw.reshape(HA_HEADS, n, HA_DV), gs, olat, prm["wuv"], prm["wo"],
                 small["hg"], small["mg"], small["g1"], small["b1"],
                 small["wr"], small["rb"], tm=tm, tq=tl["tq"], alpha=alpha)
    tmoe = tl["tmoe"]
    dst, tile_group, m_sorted = _route(x1g, tm=tmoe)
    xs = _sc_scatter_rows(x1g, dst, m_sorted)
    ys = _moe_sorted(tile_group, xs, prm["wg"], prm["wu"], prm["wd"], layer, tm=tmoe)
    ffn = _sc_gather_rows(ys, dst[:n])
    y = _post(x1g, ffn, p_all.reshape(-1, n, PLE_DIM), layer, small["g2"], small["b2"],
              prm["wpg"], prm["wp"], tm=tm, alpha=alpha)
    return (y.reshape(bsz, t, D_MODEL), s_new, ckv.reshape(bsz, t, KV_LORA),
            kpe.reshape(bsz, t, ROPE_DIM))


def kernel(x_prompt, x_sample, p_prompt, p_sample, state_hgrn, cache_ckv, cache_kpe, w_in,
           lb_logits, hgrn_norm_g, q_norm_g, w_qb, kv_norm_g, w_kvb, mla_norm_g, w_o,
           ln1_g, ln1_b, ln2_g, ln2_b, w_router, router_bias, w_gate, w_up, w_down,
           w_ple, w_ple_gate):
    depth = w_in.shape[0]
    alpha = (2 * depth) ** 0.25
    bp, tp, _ = x_prompt.shape
    bs, ts, _ = x_sample.shape
    past = cache_ckv.shape[2]

    sm = jax.nn.softmax(lb_logits.astype(F32), axis=0)
    lb_all = jnp.maximum(jnp.cumsum(sm, axis=0) - sm[0:1], 0.0)
    lbp_all = jnp.stack([jnp.log(lb_all), jnp.log1p(-lb_all), 1.0 - lb_all], axis=1)

    cs_p = jnp.asarray(_rope_table(np.arange(tp)))
    tm_s = _tiles(bs * ts, ts, False)["tm"]
    cs_s = jnp.asarray(np.tile(_rope_table(past + np.arange(ts)), (tm_s // ts, 1)))
    wr = jnp.pad(w_router, ((0, 0), (0, LANES - N_EXPERTS))).astype(BF16)
    rb = jnp.pad(router_bias.astype(F32), (0, LANES - N_EXPERTS)).reshape(1, LANES)
    s0_p = jnp.zeros((1, bp, HA_HEADS, HA_DK, HA_DV), F32)
    cache_kpe_t = jnp.swapaxes(cache_kpe, 2, 3)

    yp, ys = x_prompt, x_sample
    res = [[] for _ in range(6)]
    for l in range(depth):
        prm = _prep_layer(l, w_in, w_qb, w_kvb, w_o, w_gate, w_up, w_down, w_ple, w_ple_gate)
        row = lambda a: a[l].reshape(1, -1).astype(F32)
        small = dict(qg=row(q_norm_g), kvg=row(kv_norm_g), hg=row(hgrn_norm_g), mg=row(mla_norm_g),
                     g1=row(ln1_g), b1=row(ln1_b), g2=row(ln2_g), b2=row(ln2_b), wr=wr, rb=rb)
        yp, sp, cp, kp = _layer(yp, p_prompt, l, cs_p, s0_p, None, lbp_all[l], prm, small,
                                alpha=alpha, prompt=True)
        ys, ss, cs_, ks = _layer(ys, p_sample, l, cs_s, state_hgrn, (cache_ckv, cache_kpe_t),
                                 lbp_all[l], prm, small, alpha=alpha, prompt=False)
        for lst, a in zip(res, (sp, cp, kp, ss, cs_, ks)):
            lst.append(a)
    sp, cp, kp, ss, cs_, ks = (jnp.stack(a) for a in res)
    return (yp, ys, sp, cp, kp, ss, cs_, ks)
```

```python
import functools

import jax
import jax.numpy as jnp
import numpy as np
from jax import lax
from jax.experimental import pallas as pl
from jax.experimental.pallas import tpu as pltpu
from jax.experimental.pallas import tpu_sc as plsc

F32 = jnp.float32
BF16 = jnp.bfloat16

D_MODEL = 1024
HA_HEADS = 4
HA_DK = 128
HA_DV = 128
HB_HEADS = 4
Q_LORA = 384
KV_LORA = 256
NOPE_DIM = 128
ROPE_DIM = 64
V_DIM = 128
ROPE_THETA = 10000.0
MLA_SCALE = (NOPE_DIM + ROPE_DIM) ** -0.5
LOG2E = 1.4426950408889634
Q_SCALE = MLA_SCALE * LOG2E
CHUNK = 64
N_EXPERTS = 16
N_GROUPS = 4
EXPERTS_PER_GROUP = N_EXPERTS // N_GROUPS
D_EXPERT = 512
PLE_DIM = 256
NEG_INF = -1e30

HA_W = HA_HEADS * HA_DK
COL_CQ = 4 * HA_W
COL_CKV = COL_CQ + Q_LORA
COL_KPE = COL_CKV + KV_LORA
D_IN_AUG = COL_KPE + 2 * ROPE_DIM
Q_HEAD_AUG = NOPE_DIM + 2 * ROPE_DIM

LANES = 128
SUBLANES = 8
QK_DIM = KV_LORA + LANES
GROUP_LANE = N_EXPERTS
SC_WINDOW = 128
SC_ROW = 256
X_PIECES = D_MODEL // SC_ROW
SUB_BLOCK = 16
VMEM_LIMIT = 56 * 1024 * 1024


def _cparams(sem, vmem=VMEM_LIMIT):
    return pltpu.CompilerParams(dimension_semantics=sem, vmem_limit_bytes=vmem)


def _dot(a, b):
    return jnp.dot(a, b, preferred_element_type=F32)


def _dot_nt(a, b):
    return lax.dot_general(a, b, (((1,), (1,)), ((), ())), preferred_element_type=F32)


def _rms(x, g, eps=1e-6):
    return x * lax.rsqrt(jnp.mean(x * x, axis=-1, keepdims=True) + eps) * g


def _layer_norm(x, g, b, eps=1e-5):
    mu = jnp.mean(x, axis=-1, keepdims=True)
    xc = x - mu
    var = jnp.mean(xc * xc, axis=-1, keepdims=True)
    return xc * lax.rsqrt(var + eps) * g + b


def _silu(x):
    return x * jax.nn.sigmoid(x)


def _full(shape):
    nd = len(shape)
    return pl.BlockSpec(shape, lambda *_: (0,) * nd)


def _inproj_kernel(x_ref, cs_ref, win_ref, lb_ref, qg_ref, wqb_ref, kvg_ref, wuk_ref,
                   qs_ref, kin_ref, logf_ref, v_ref, gs_ref, ckv_ref, kpe_ref, *rest,
                   tq, cb, prompt):
    tm = x_ref.shape[0]
    x = x_ref[...].astype(BF16)

    def sect(lo, hi):
        return _dot(x, win_ref[:, lo:hi])

    def put_heads(ref, val):
        for h in range(HA_HEADS):
            ref[h] = val[:, h * HA_DK:(h + 1) * HA_DK]

    cqn = _rms(sect(COL_CQ, COL_CKV), qg_ref[...]).astype(BF16)

    put_heads(qs_ref, _silu(sect(0, HA_W)))
    fa = sect(HA_W, 2 * HA_W)
    log_lb = lb_ref[0:1, :]
    log1m_lb = lb_ref[1:2, :]
    one_m_lb = lb_ref[2:3, :]
    e = jnp.exp(-jnp.abs(fa))
    r = 1.0 / (1.0 + e)
    c = log1m_lb + jnp.minimum(fa, 0.0) + jnp.log(r)
    put_heads(logf_ref, jnp.maximum(log_lb, c) + jnp.log(1.0 + jnp.exp(-jnp.abs(log_lb - c))))
    put_heads(kin_ref, one_m_lb * jnp.where(fa >= 0.0, e * r, r))
    v = sect(2 * HA_W, 3 * HA_W)
    put_heads(v_ref, v)
    gs_ref[...] = _silu(sect(3 * HA_W, 4 * HA_W))

    cs = cs_ref[...]

    def rope(t):
        prod = t * cs
        return prod + pltpu.roll(prod, ROPE_DIM, 1)

    ckv = _rms(sect(COL_CKV, COL_KPE), kvg_ref[...])
    ckv_ref[...] = ckv
    kpe2 = rope(sect(COL_KPE, D_IN_AUG))
    kpe_ref[...] = kpe2[:, :ROPE_DIM]
    if prompt:
        qt_ref, vt_ref, kcat_ref, ckvt_ref = rest
        for h in range(HA_HEADS):
            for ci in range(tm // cb):
                vt_ref[0, h, ci] = v[ci * cb:(ci + 1) * cb, h * HA_DV:(h + 1) * HA_DV].T.astype(BF16)
        kcat_ref[:, :KV_LORA] = ckv.astype(BF16)
        kcat_ref[:, KV_LORA:] = kpe2.astype(BF16)
        ckvt_ref[0] = ckv.T.astype(BF16)
    else:
        qabs_ref, qpe_ref = rest

    lane = lax.broadcasted_iota(jnp.int32, (tm, LANES), 1)
    qhs = [_dot(cqn, wqb_ref[:, h * Q_HEAD_AUG:(h + 1) * Q_HEAD_AUG]) for h in range(HB_HEADS)]
    qabss = [_dot(qhs[h][:, :NOPE_DIM].astype(BF16), wuk_ref[h]) for h in range(HB_HEADS)]
    for h in range(HB_HEADS):
        qabs = qabss[h] * Q_SCALE
        qpe = jnp.where(lane < ROPE_DIM, rope(qhs[h][:, NOPE_DIM:]) * Q_SCALE, 0.0)
        if prompt:
            qabs_t = qabs.T.astype(BF16)
            qpe_t = qpe.T.astype(BF16)
            for s in range(tm // tq):
                qt_ref[s, :KV_LORA, h * tq:(h + 1) * tq] = qabs_t[:, s * tq:(s + 1) * tq]
                qt_ref[s, KV_LORA:, h * tq:(h + 1) * tq] = qpe_t[:, s * tq:(s + 1) * tq]
        else:
            for s in range(tm // tq):
                qabs_ref[s, h] = qabs[s * tq:(s + 1) * tq].astype(BF16)
                qpe_ref[s, h] = qpe[s * tq:(s + 1) * tq].astype(BF16)


def _inproj(x, cs, w_in, lbp, qg, wqb, kvg, wuk, *, tm, tq, cb, prompt):
    n = x.shape[0]
    nt = n // tm
    row = lambda w: pl.BlockSpec((tm, w), lambda i: (i, 0))
    heads = pl.BlockSpec((HA_HEADS, tm, HA_DK), lambda i: (0, i, 0))
    qblk = lambda w: pl.BlockSpec((tm // tq, HB_HEADS, tq, w), lambda i: (i, 0, 0, 0))
    head_major = jax.ShapeDtypeStruct((HA_HEADS, n, HA_DK), F32)
    out_shape = [
        head_major,
        head_major,
        head_major,
        head_major,
        jax.ShapeDtypeStruct((n, HA_W), F32),
        jax.ShapeDtypeStruct((n, KV_LORA), F32),
        jax.ShapeDtypeStruct((n, ROPE_DIM), F32),
    ]
    out_specs = [heads, heads, heads, heads, row(HA_W), row(KV_LORA), row(ROPE_DIM)]
    if prompt:
        out_shape += [jax.ShapeDtypeStruct((n // tq, QK_DIM, HB_HEADS * tq), BF16),
                      jax.ShapeDtypeStruct((nt, HA_HEADS, tm // cb, HA_DV, cb), BF16),
                      jax.ShapeDtypeStruct((n, QK_DIM), BF16),
                      jax.ShapeDtypeStruct((nt, KV_LORA, tm), BF16)]
        out_specs += [pl.BlockSpec((tm // tq, QK_DIM, HB_HEADS * tq), lambda i: (i, 0, 0)),
                      pl.BlockSpec((1, HA_HEADS, tm // cb, HA_DV, cb), lambda i: (i, 0, 0, 0, 0)),
                      row(QK_DIM),
                      pl.BlockSpec((1, KV_LORA, tm), lambda i: (i, 0, 0))]
    else:
        out_shape += [jax.ShapeDtypeStruct((n // tq, HB_HEADS, tq, KV_LORA), BF16),
                      jax.ShapeDtypeStruct((n // tq, HB_HEADS, tq, LANES), BF16)]
        out_specs += [qblk(KV_LORA), qblk(LANES)]
    return pl.pallas_call(
        functools.partial(_inproj_kernel, tq=tq, cb=cb, prompt=prompt),
        grid=(nt,),
        in_specs=[row(D_MODEL),
                  pl.BlockSpec((tm, LANES), lambda i: (i % (cs.shape[0] // tm), 0)),
                  _full(w_in.shape), _full(lbp.shape), _full(qg.shape),
                  _full(wqb.shape), _full(kvg.shape), _full(wuk.shape)],
        out_specs=out_specs,
        out_shape=out_shape,
        compiler_params=_cparams(("parallel",)),
        name="inproj",
    )(x, cs, w_in, lbp, qg, wqb, kvg, wuk)


def _hgrn_kernel(q_ref, k_ref, g_ref, v_ref, vt_ref, s0_ref, o_ref, sfin_ref, st_scr, b2_scr,
                 hide_ref, *, cb):
    _, bb, tb, _ = q_ref.shape
    nsub = cb // SUB_BLOCK
    ti = pl.program_id(1)

    @pl.when(ti == 0)
    def _():
        for b in range(bb):
            for h in range(HA_HEADS):
                st_scr[b, h] = s0_ref[b, h].T
        row8 = lax.broadcasted_iota(jnp.int32, (SUBLANES, LANES), 0)
        for s in range(SUBLANES):
            hide_ref[s] = jnp.where(row8 >= s, 0.0, NEG_INF)

    r_i = lax.broadcasted_iota(jnp.int32, (cb, cb), 0)
    c_i = lax.broadcasted_iota(jnp.int32, (cb, cb), 1)
    tril = (r_i >= c_i).astype(F32)

    chains = [(b, h) for b in range(bb) for h in range(HA_HEADS)]

    def matmul_part(ci):
        r0 = pl.multiple_of(ci * cb, cb)
        rows = pl.ds(r0, cb)

        def sub(ref, b, h, lo, size):
            return ref[h, b, pl.ds(pl.multiple_of(r0 + lo, SUB_BLOCK), size), :]

        for b, h in chains:
            bcum = jnp.dot(tril, g_ref[h, b, rows, :], precision=lax.Precision.HIGHEST,
                           preferred_element_type=F32)
            b2_scr[b * HA_HEADS + h] = bcum * LOG2E
        for b, h in chains:
            b2 = b2_scr[b * HA_HEADS + h]
            o_ref[h, b, rows, :] = _dot_nt((q_ref[h, b, rows, :] * jnp.exp2(b2)).astype(BF16),
                                           st_scr[b, h].astype(BF16))
        for j in range(nsub - 1):
            lo, hi = j * SUB_BLOCK, (j + 1) * SUB_BLOCK
            below = pl.ds(pl.multiple_of(r0 + hi, SUB_BLOCK), cb - hi)
            for b, h in chains:
                slot = b * HA_HEADS + h
                bnd = b2_scr[slot, hi - 1:hi, :]
                kd = (sub(k_ref, b, h, lo, SUB_BLOCK)
                      * jnp.exp2(bnd - b2_scr[slot, lo:hi, :])).astype(BF16)
                qe = (sub(q_ref, b, h, hi, cb - hi)
                      * jnp.exp2(b2_scr[slot, hi:, :] - bnd)).astype(BF16)
                a = _dot_nt(qe, kd).astype(BF16)
                o_ref[h, b, below, :] += _dot(a, sub(v_ref, b, h, lo, SUB_BLOCK).astype(BF16))
        for b, h in chains:
            slot = b * HA_HEADS + h
            b_last = b2_scr[slot, cb - 1:cb, :]
            kd_all = (k_ref[h, b, rows, :] * jnp.exp2(b_last - b2_scr[slot])).astype(BF16)
            st_scr[b, h] = st_scr[b, h] * jnp.exp2(b_last) + _dot(vt_ref[b, h, ci], kd_all)

    def pairwise_part(ci, b, h, j):
        slot = b * HA_HEADS + h
        r0 = pl.multiple_of(ci * cb, cb)
        lo = j * SUB_BLOCK
        mid = lo + SUBLANES
        top = pl.ds(pl.multiple_of(r0 + lo, SUBLANES), SUBLANES)
        bot = pl.ds(pl.multiple_of(r0 + mid, SUBLANES), SUBLANES)
        q_top, q_bot = q_ref[h, b, top, :], q_ref[h, b, bot, :]
        b_top, b_bot = b2_scr[slot, lo:mid, :], b2_scr[slot, mid:mid + SUBLANES, :]
        acc_top, acc_bot = o_ref[h, b, top, :], o_ref[h, b, bot, :]
        for s in range(SUB_BLOCK):
            row = pl.ds(r0 + lo + s, 1)
            bs = b2_scr[slot, lo + s:lo + s + 1, :]
            ks = k_ref[h, b, row, :]
            vs = v_ref[h, b, row, :]
            if s < SUBLANES:
                w = q_top * jnp.exp2(b_top - bs + hide_ref[s]) * ks
                acc_top = acc_top + jnp.sum(w, axis=-1, keepdims=True) * vs
                w = q_bot * jnp.exp2(b_bot - bs) * ks
            else:
                w = q_bot * jnp.exp2(b_bot - bs + hide_ref[s - SUBLANES]) * ks
            acc_bot = acc_bot + jnp.sum(w, axis=-1, keepdims=True) * vs
        o_ref[h, b, top, :] = acc_top
        o_ref[h, b, bot, :] = acc_bot

    def chunk(ci, carry):
        matmul_part(ci)
        for j in range(nsub):
            for b, h in chains:
                pairwise_part(ci, b, h, j)
        return carry

    lax.fori_loop(0, tb // cb, chunk, 0)

    @pl.when(ti == pl.num_programs(1) - 1)
    def _():
        for b in range(bb):
            for h in range(HA_HEADS):
                sfin_ref[b, h] = st_scr[b, h].T


def _hgrn(qs, kin, logf, v, vt, s0_all, layer, *, bb, tb, cb):
    _, bsz, t, _ = qs.shape
    blk = pl.BlockSpec((HA_HEADS, bb, tb, HA_DK), lambda i, j: (0, i, j, 0))
    st_blk = pl.BlockSpec((bb, HA_HEADS, HA_DK, HA_DV), lambda i, j: (i, 0, 0, 0))
    s0_blk = pl.BlockSpec((None, bb, HA_HEADS, HA_DK, HA_DV), lambda i, j: (layer, i, 0, 0, 0))
    vt_blk = pl.BlockSpec((bb, None, HA_HEADS, tb // cb, HA_DV, cb),
                          lambda i, j: (i, j, 0, 0, 0, 0))
    return pl.pallas_call(
        functools.partial(_hgrn_kernel, cb=cb),
        grid=(bsz // bb, t // tb),
        in_specs=[blk, blk, blk, blk, vt_blk, s0_blk],
        out_specs=[blk, st_blk],
        out_shape=[jax.ShapeDtypeStruct(qs.shape, F32),
                   jax.ShapeDtypeStruct(s0_all.shape[1:], F32)],
        scratch_shapes=[pltpu.VMEM((bb, HA_HEADS, HA_DV, HA_DK), F32),
                        pltpu.VMEM((bb * HA_HEADS, cb, HA_DK), F32),
                        pltpu.VMEM((SUBLANES, SUBLANES, LANES), F32)],
        compiler_params=_cparams(("parallel", "arbitrary")),
        name="hgrn",
    )(qs, kin, logf, v, vt, s0_all)


def _attn_prompt_kernel(qt_ref, kc_ref, kt_ref, o_ref, s0, s1, p0, p1, a0, a1, m_scr, l_scr,
                        acc_scr, *, tq, tk):
    qi = pl.program_id(1)
    jd = (qi * tq) // tk
    shift = CHUNK.bit_length() - 1

    heads = range(HB_HEADS)

    def scores(j, s_dst, hs=heads):
        keys = kc_ref[pl.ds(pl.multiple_of(j * tk, tk), tk), :]
        for h in hs:
            cols = slice(h * tq, (h + 1) * tq)
            s_dst[:, cols] = _dot(keys, qt_ref[0, :, cols])

    def softmax(j, s_src, p_dst, a_dst, masked, hs=heads):
        if masked:
            kpos = j * tk + lax.broadcasted_iota(jnp.int32, (tk, 1), 0)
            qpos = qi * tq + lax.broadcasted_iota(jnp.int32, (1, tq), 1)
            visible = (kpos >> shift) <= (qpos >> shift)
        for h in hs:
            cols = slice(h * tq, (h + 1) * tq)
            def load():
                s = s_src[:, cols]
                return jnp.where(visible, s, NEG_INF) if masked else s

            m_prev = m_scr[:, cols]
            m_new = jnp.maximum(m_prev, jnp.max(load(), axis=0, keepdims=True))
            alpha = jnp.exp2(m_prev - m_new)
            p = jnp.exp2(load() - m_new)
            l_scr[:, cols] = alpha * l_scr[:, cols] + jnp.sum(p, axis=0, keepdims=True)
            m_scr[:, cols] = m_new
            a_dst[:, cols] = alpha
            p_dst[:, cols] = p.astype(BF16)

    def values(j, p_src, a_src, hs=heads):
        for h in hs:
            cols = slice(h * tq, (h + 1) * tq)
            acc_scr[:, cols] = (a_src[:, cols] * acc_scr[:, cols]
                                + _dot(kt_ref[j], p_src[:, cols]))

    def trip(j, s_cur, s_nxt, p_cur, p_prv, a_cur, a_prv):
        scores(j + 1, s_nxt)
        softmax(j, s_cur, p_cur, a_cur, False)
        values(jnp.maximum(j - 1, 0), p_prv, a_prv)

    m_scr[...] = jnp.full(m_scr.shape, NEG_INF, F32)
    l_scr[...] = jnp.zeros(l_scr.shape, F32)
    acc_scr[...] = jnp.zeros(acc_scr.shape, F32)
    p1[...] = jnp.zeros(p1.shape, BF16)
    a1[...] = jnp.ones(a1.shape, F32)
    scores(0, s0)

    def pair(jj, carry):
        trip(2 * jj, s0, s1, p0, p1, a0, a1)
        trip(2 * jj + 1, s1, s0, p1, p0, a1, a0)
        return carry

    lax.fori_loop(0, jd // 2, pair, 0)

    @pl.when(jd % 2 == 1)
    def _():
        trip(jd - 1, s0, s1, p0, p1, a0, a1)

    def drain(s_cur, p_cur, p_prv, a_cur, a_prv):
        softmax(jd, s_cur, p_cur, a_cur, True)
        values(jnp.maximum(jd - 1, 0), p_prv, a_prv)
        values(jd, p_cur, a_cur)

    pl.when(jd % 2 == 0)(functools.partial(drain, s0, p0, p1, a0, a1))
    pl.when(jd % 2 == 1)(functools.partial(drain, s1, p1, p0, a1, a0))

    for h in range(HB_HEADS):
        cols = slice(h * tq, (h + 1) * tq)
        o_ref[0, h] = (acc_scr[:, cols] * (1.0 / l_scr[:, cols])).T.astype(BF16)


def _attn_prompt(qt, kcat, ckvt, *, bsz, t, tq, tk):
    nq = t // tq
    nk = t // tk
    cols = HB_HEADS * tq
    single = pl.Buffered(1)
    return pl.pallas_call(
        functools.partial(_attn_prompt_kernel, tq=tq, tk=tk),
        grid=(bsz, nq),
        in_specs=[
            pl.BlockSpec((1, QK_DIM, cols), lambda b, i: (b * nq + i, 0, 0)),
            pl.BlockSpec((t, QK_DIM), lambda b, i: (b, 0), pipeline_mode=single),
            pl.BlockSpec((nk, KV_LORA, tk), lambda b, i: (b, 0, 0), pipeline_mode=single),
        ],
        out_specs=pl.BlockSpec((1, HB_HEADS, tq, KV_LORA), lambda b, i: (b * nq + i, 0, 0, 0)),
        out_shape=jax.ShapeDtypeStruct((bsz * nq, HB_HEADS, tq, KV_LORA), BF16),
        scratch_shapes=[pltpu.VMEM((tk, cols), F32), pltpu.VMEM((tk, cols), F32),
                        pltpu.VMEM((tk, cols), BF16), pltpu.VMEM((tk, cols), BF16),
                        pltpu.VMEM((1, cols), F32), pltpu.VMEM((1, cols), F32),
                        pltpu.VMEM((1, cols), F32), pltpu.VMEM((1, cols), F32),
                        pltpu.VMEM((KV_LORA, cols), F32)],
        compiler_params=_cparams(("parallel", "arbitrary")),
        name="attn_prompt",
    )(qt, kcat, ckvt)


def _attn_sample_kernel(qa_ref, qp_ref, pckv_ref, pkpe_ref, nckv_ref, nkpe_ref, o_ref):
    _, _, ts, _ = qa_ref.shape
    rows = HB_HEADS * ts
    qa = qa_ref[0].reshape(rows, KV_LORA)
    qp = qp_ref[0].reshape(rows, LANES)[:, :ROPE_DIM]
    pckv = pckv_ref[0].astype(BF16)
    nckv = nckv_ref[...].astype(BF16)
    s_past = _dot_nt(qa, pckv) + _dot(qp, pkpe_ref[0].astype(BF16))
    s_new = _dot_nt(qa, nckv) + _dot_nt(qp, nkpe_ref[...].astype(BF16))
    m = jnp.maximum(jnp.max(s_past, axis=-1, keepdims=True), jnp.max(s_new, axis=-1, keepdims=True))
    p_past = jnp.exp2(s_past - m)
    p_new = jnp.exp2(s_new - m)
    l = jnp.sum(p_past, axis=-1, keepdims=True) + jnp.sum(p_new, axis=-1, keepdims=True)
    o = (_dot(p_past.astype(BF16), pckv) + _dot(p_new.astype(BF16), nckv)) / l
    o_ref[0] = o.astype(BF16).reshape(HB_HEADS, ts, KV_LORA)


def _attn_sample(qabs, qpe, past_ckv, past_kpe, layer, ckv, kpe, *, ts):
    _, bsz, past, _ = past_ckv.shape
    return pl.pallas_call(
        _attn_sample_kernel,
        grid=(bsz,),
        in_specs=[
            pl.BlockSpec((1, HB_HEADS, ts, KV_LORA), lambda b: (b, 0, 0, 0)),
            pl.BlockSpec((1, HB_HEADS, ts, LANES), lambda b: (b, 0, 0, 0)),
            pl.BlockSpec((None, 1, past, KV_LORA), lambda b: (layer, b, 0, 0)),
            pl.BlockSpec((None, 1, ROPE_DIM, past), lambda b: (layer, b, 0, 0)),
            pl.BlockSpec((ts, KV_LORA), lambda b: (b, 0)),
            pl.BlockSpec((ts, ROPE_DIM), lambda b: (b, 0)),
        ],
        out_specs=pl.BlockSpec((1, HB_HEADS, ts, KV_LORA), lambda b: (b, 0, 0, 0)),
        out_shape=jax.ShapeDtypeStruct(qabs.shape, BF16),
        compiler_params=_cparams(("parallel",)),
        name="attn_sample",
    )(qabs, qpe, past_ckv, past_kpe, ckv, kpe)


def _gates(logits, bias):
    tm = logits.shape[0]
    lane = lax.broadcasted_iota(jnp.int32, (tm, LANES), 1)
    pos = lane % EXPERTS_PER_GROUP
    valid = lane < N_EXPERTS
    scores = jax.nn.sigmoid(logits)
    sel = jnp.where(valid, scores + bias, -jnp.inf)

    others = []
    for r in range(1, EXPERTS_PER_GROUP):
        others.append(jnp.where(pos >= r, pltpu.roll(sel, r, 1),
                                pltpu.roll(sel, LANES - (EXPERTS_PER_GROUP - r), 1)))
    a, b, c, d = sel, others[0], others[1], others[2]
    hi1, lo1 = jnp.maximum(a, b), jnp.minimum(a, b)
    hi2, lo2 = jnp.maximum(c, d), jnp.minimum(c, d)
    gscore = jnp.maximum(hi1, hi2) + jnp.maximum(jnp.minimum(hi1, hi2), jnp.maximum(lo1, lo2))
    gmax = jnp.max(gscore, axis=-1, keepdims=True)
    group = (lane // EXPERTS_PER_GROUP).astype(F32)
    gidx = jnp.min(jnp.where(gscore == gmax, group, float(LANES)), axis=-1, keepdims=True)

    rank = jnp.zeros((tm, LANES), jnp.int32)
    for r, o in enumerate(others, start=1):
        ahead = (o > sel) | ((o == sel) & (pos >= r))
        rank = rank + ahead.astype(jnp.int32)
    chosen = (group == gidx) & (rank < 2) & valid
    w = jnp.where(chosen, scores, 0.0)
    w = w / jnp.sum(w, axis=-1, keepdims=True)
    return jnp.where(lane == GROUP_LANE, gidx, w)


def _merge_kernel(x_ref, o_ref, gs_ref, olat_ref, wuv_ref, wo_ref, hg_ref, mg_ref,
                  g1_ref, b1_ref, wr_ref, rb_ref, x1g_ref, *, alpha):
    _, _, tq, _ = olat_ref.shape
    tm = x_ref.shape[0]
    hg = hg_ref[...]
    d_a = HA_HEADS * HA_DV
    halves = [(i * tm // 2, (i + 1) * tm // 2) for i in range(2)]

    def lat(h, lo, hi):
        if tq >= hi - lo:
            return olat_ref[lo // tq, h, lo % tq:lo % tq + hi - lo]
        return jnp.concatenate([olat_ref[s, h] for s in range(lo // tq, hi // tq)], axis=0)

    ob = [[_dot(lat(h, lo, hi), wuv_ref[h]) for h in range(HB_HEADS)] for lo, hi in halves]
    oa = [jnp.concatenate([_rms(o_ref[h, lo:hi, :], hg) * gs_ref[lo:hi, h * HA_DV:(h + 1) * HA_DV]
                           for h in range(HA_HEADS)], axis=-1) for lo, hi in halves]
    ob = [_rms(jnp.concatenate(o, axis=-1), mg_ref[...]) for o in ob]
    mix = [_dot(a.astype(BF16), wo_ref[:d_a, :]) + _dot(o.astype(BF16), wo_ref[d_a:, :])
           for a, o in zip(oa, ob)]
    x1 = [_layer_norm(alpha * x_ref[lo:hi, :] + m, g1_ref[...], b1_ref[...])
          for (lo, hi), m in zip(halves, mix)]
    logits = [_dot(x.astype(BF16), wr_ref[...]) for x in x1]
    for (lo, hi), x, lg in zip(halves, x1, logits):
        for c in range(X_PIECES):
            x1g_ref[c, lo:hi, :] = x[:, c * SC_ROW:(c + 1) * SC_ROW]
        x1g_ref[X_PIECES, lo:hi, :LANES] = _gates(lg, rb_ref[...])
        x1g_ref[X_PIECES, lo:hi, LANES:] = jnp.zeros((hi - lo, SC_ROW - LANES), F32)


def _merge(x, o_raw, gs, olat, wuv, wo, hg, mg, g1, b1, wr, rb, *, tm, tq, alpha):
    n = x.shape[0]
    row = lambda w: pl.BlockSpec((tm, w), lambda i: (i, 0))
    return pl.pallas_call(
        functools.partial(_merge_kernel, alpha=alpha),
        grid=(n // tm,),
        in_specs=[row(D_MODEL), pl.BlockSpec((HA_HEADS, tm, HA_DV), lambda i: (0, i, 0)), row(HA_W),
                  pl.BlockSpec((tm // tq, HB_HEADS, tq, KV_LORA), lambda i: (i, 0, 0, 0)),
                  _full(wuv.shape), _full(wo.shape), _full(hg.shape), _full(mg.shape),
                  _full(g1.shape), _full(b1.shape), _full(wr.shape), _full(rb.shape)],
        out_specs=pl.BlockSpec((X_PIECES + 1, tm, SC_ROW), lambda i: (0, i, 0)),
        out_shape=jax.ShapeDtypeStruct((X_PIECES + 1, n, SC_ROW), F32),
        compiler_params=_cparams(("parallel",)),
        name="merge",
    )(x, o_raw, gs, olat, wuv, wo, hg, mg, g1, b1, wr, rb)


def _sc_scatter_rows(src, dst, m):
    parts, n, _ = src.shape
    windows = dst.shape[0] // SC_WINDOW
    src_windows = n // SC_WINDOW
    pieces = src.reshape(parts * n, SC_ROW)
    piece_dst = (jnp.arange(parts, dtype=jnp.int32)[:, None] * m + dst[None, :]).reshape(1, -1)
    mesh = plsc.VectorSubcoreMesh(core_axis_name="core", subcore_axis_name="subcore")

    @functools.partial(pl.kernel, out_type=jax.ShapeDtypeStruct((parts * m, SC_ROW), src.dtype),
                       mesh=mesh)
    def scatter(x_hbm, i_hbm, o_hbm):
        def body(x_vmem, i_vmem):
            pltpu.sync_copy(x_vmem, o_hbm.at[i_vmem.at[0]])

        pltpu.emit_pipeline(
            body,
            grid=(parts * windows,),
            in_specs=[pl.BlockSpec((SC_WINDOW, SC_ROW),
                                   lambda i: ((i // windows) * src_windows
                                              + (i % windows) % src_windows, 0)),
                      pl.BlockSpec((1, SC_WINDOW), lambda i: (0, i))],
            out_specs=[],
            core_axis_name=("core", "subcore"),
            dimension_semantics=(pltpu.PARALLEL,),
        )(x_hbm, i_hbm)

    return scatter(pieces, piece_dst).reshape(parts, m, SC_ROW)


def _sc_gather_rows(src, idx):
    parts, n, _ = src.shape
    m = idx.shape[0] * parts
    pieces = src.reshape(parts * n, SC_ROW)
    piece_idx = (jnp.arange(parts, dtype=jnp.int32)[:, None] * n + idx[None, :]).reshape(1, m)
    mesh = plsc.VectorSubcoreMesh(core_axis_name="core", subcore_axis_name="subcore")

    @functools.partial(pl.kernel, out_type=jax.ShapeDtypeStruct((m, SC_ROW), src.dtype), mesh=mesh)
    def gather(x_hbm, i_hbm, o_hbm):
        def body(i_vmem, o_vmem):
            pltpu.sync_copy(x_hbm.at[i_vmem.at[0]], o_vmem)

        pltpu.emit_pipeline(
            body,
            grid=(m // SC_WINDOW,),
            in_specs=[pl.BlockSpec((1, SC_WINDOW), lambda i: (0, i))],
            out_specs=[pl.BlockSpec((SC_WINDOW, SC_ROW), lambda i: (i, 0))],
            core_axis_name=("core", "subcore"),
            dimension_semantics=(pltpu.PARALLEL,),
        )(i_hbm, o_hbm)

    return gather(pieces, piece_idx).reshape(parts, idx.shape[0], SC_ROW)


def _moe_kernel(tg_ref, xs_ref, wg_ref, wu_ref, wd_ref, y_ref, acc_scr):
    tm = xs_ref.shape[1]
    per_step = wg_ref.shape[0]
    step = pl.program_id(1)
    first_expert = tg_ref[pl.program_id(0)] * EXPERTS_PER_GROUP + step * per_step
    xb = jnp.concatenate([xs_ref[c] for c in range(X_PIECES)], axis=1).astype(BF16)
    gates = xs_ref[X_PIECES, :, :LANES]
    lane = lax.broadcasted_iota(jnp.int32, (tm, LANES), 1)
    acc = jnp.zeros((tm, D_MODEL), F32)
    for k in range(per_step):
        hmid = (_silu(_dot(xb, wg_ref[k].astype(BF16))) * _dot(xb, wu_ref[k].astype(BF16)))
        gcol = jnp.sum(jnp.where(lane == first_expert + k, gates, 0.0), axis=-1, keepdims=True)
        acc = acc + gcol * _dot(hmid.astype(BF16), wd_ref[k].astype(BF16))

    @pl.when(step == 0)
    def _():
        acc_scr[...] = acc

    @pl.when(step > 0)
    def _():
        acc_scr[...] += acc

    @pl.when(step == pl.num_programs(1) - 1)
    def _():
        for c in range(D_MODEL // SC_ROW):
            y_ref[c] = acc_scr[:, c * SC_ROW:(c + 1) * SC_ROW]


def _moe_sorted(tile_group, xs, wg, wu, wd, layer, *, tm, per_step):
    m = xs.shape[1]
    steps = EXPERTS_PER_GROUP // per_step
    mode = dict(pipeline_mode=pl.Buffered(1)) if steps == 1 else {}
    wspec = lambda shape: pl.BlockSpec((None, per_step) + shape,
                                       lambda i, k, tg: (layer, tg[i] * steps + k, 0, 0), **mode)
    return pl.pallas_call(
        _moe_kernel,
        grid_spec=pltpu.PrefetchScalarGridSpec(
            num_scalar_prefetch=1,
            grid=(m // tm, steps),
            in_specs=[pl.BlockSpec((X_PIECES + 1, tm, SC_ROW), lambda i, k, tg: (0, i, 0)),
                      wspec((D_MODEL, D_EXPERT)), wspec((D_MODEL, D_EXPERT)),
                      wspec((D_EXPERT, D_MODEL))],
            out_specs=pl.BlockSpec((D_MODEL // SC_ROW, tm, SC_ROW), lambda i, k, tg: (0, i, 0)),
            scratch_shapes=[pltpu.VMEM((tm, D_MODEL), F32)],
        ),
        out_shape=jax.ShapeDtypeStruct((D_MODEL // SC_ROW, m, SC_ROW), F32),
        compiler_params=_cparams(("arbitrary", "arbitrary")),
        name="moe",
    )(tile_group, xs, wg, wu, wd)


def _route(x1g, *, tm):
    n = x1g.shape[1]
    n_tiles = n // tm + N_GROUPS - 1
    m = n_tiles * tm
    gidx = x1g[X_PIECES, :, GROUP_LANE].astype(jnp.int32)
    onehot = (gidx[:, None] == jnp.arange(N_GROUPS, dtype=jnp.int32)[None, :]).astype(jnp.int32)
    counts = jnp.sum(onehot, axis=0)
    tiles = (counts + tm - 1) // tm
    tile_end = jnp.cumsum(tiles)
    tile_start = tile_end - tiles
    rank = jnp.sum((jnp.cumsum(onehot, axis=0) - onehot) * onehot, axis=1)
    pos = jnp.take(tile_start, gidx) * tm + rank
    pad_start = tile_start * tm + counts
    pad_end = (tile_end * tm).at[N_GROUPS - 1].set(m)
    pad_cum = jnp.cumsum(pad_end - pad_start)
    k = jnp.arange(m - n, dtype=jnp.int32)
    pg = jnp.searchsorted(pad_cum, k, side="right")
    pad_rows = jnp.take(pad_start, pg) + k - jnp.take(pad_cum - (pad_end - pad_start), pg)
    dst = jnp.concatenate([pos, pad_rows]).astype(jnp.int32)
    tile_group = jnp.searchsorted(tile_end, jnp.arange(n_tiles, dtype=jnp.int32), side="right")
    return dst, jnp.minimum(tile_group, N_GROUPS - 1).astype(jnp.int32), m


def _post_kernel(x1_ref, f_ref, p_ref, g2_ref, b2_ref, wpg_ref, wp_ref, y_ref, *, alpha):
    tm = x1_ref.shape[1]
    halves = [(i * tm // 2, (i + 1) * tm // 2) for i in range(2)]
    emb = [_dot(p_ref[lo:hi, :].astype(BF16), wp_ref[...]) for lo, hi in halves]

    def rows(ref, lo, hi):
        return jnp.concatenate([ref[c, lo:hi, :] for c in range(X_PIECES)], axis=1)

    x2 = [_layer_norm(alpha * rows(x1_ref, lo, hi) + rows(f_ref, lo, hi), g2_ref[...], b2_ref[...])
          for lo, hi in halves]
    gate = [_dot(x.astype(BF16), wpg_ref[...]) for x in x2]
    for (lo, hi), x, g, e in zip(halves, x2, gate, emb):
        y_ref[lo:hi, :] = x + jax.nn.sigmoid(g) * e


def _post(x1g, ffn, p_all, layer, g2, b2, wpg, wp, *, tm, alpha):
    _, n, _ = ffn.shape
    row = lambda w: pl.BlockSpec((tm, w), lambda i: (i, 0))
    pieces = lambda a: pl.BlockSpec((a.shape[0], tm, SC_ROW), lambda i: (0, i, 0))
    return pl.pallas_call(
        functools.partial(_post_kernel, alpha=alpha),
        grid=(n // tm,),
        in_specs=[pieces(x1g), pieces(ffn),
                  pl.BlockSpec((None, tm, PLE_DIM), lambda i: (layer, i, 0)),
                  _full(g2.shape), _full(b2.shape), _full(wpg.shape), _full(wp.shape)],
        out_specs=row(D_MODEL),
        out_shape=jax.ShapeDtypeStruct((n, D_MODEL), F32),
        compiler_params=_cparams(("parallel",)),
        name="post",
    )(x1g, ffn, p_all, g2, b2, wpg, wp)


def _rot_cols(w):
    half = ROPE_DIM // 2
    return jnp.concatenate([-w[..., half:], w[..., :half]], axis=-1)


def _prep_layer(l, w_in, w_qb, w_kvb, w_o, w_gate, w_up, w_down, w_ple, w_ple_gate):
    win = w_in[l]
    win_aug = jnp.concatenate([win, _rot_cols(win[:, COL_KPE:])], axis=-1).astype(BF16)
    wqb = w_qb[l].reshape(Q_LORA, HB_HEADS, NOPE_DIM + ROPE_DIM)
    wqb_aug = jnp.concatenate([wqb, _rot_cols(wqb[..., NOPE_DIM:])], axis=-1)
    wqb_aug = wqb_aug.reshape(Q_LORA, HB_HEADS * Q_HEAD_AUG).astype(BF16)
    wkvb = w_kvb[l].reshape(KV_LORA, HB_HEADS, NOPE_DIM + V_DIM)
    wuk_t = jnp.transpose(wkvb[..., :NOPE_DIM], (1, 2, 0)).astype(BF16)
    wuv = jnp.transpose(wkvb[..., NOPE_DIM:], (1, 0, 2)).astype(BF16)
    return dict(win=win_aug, wqb=wqb_aug, wuk=wuk_t, wuv=wuv, wo=w_o[l].astype(BF16),
                wg=w_gate, wu=w_up, wd=w_down,
                wp=w_ple[l].astype(BF16), wpg=w_ple_gate[l].astype(BF16))


def _rope_table(pos):
    inv = ROPE_THETA ** (-np.arange(0, ROPE_DIM, 2, dtype=np.float64) / ROPE_DIM)
    ang = pos.astype(np.float64)[:, None] * inv[None, :]
    cos, sin = np.cos(ang), np.sin(ang)
    return np.concatenate([cos, cos, sin, sin], axis=-1).astype(np.float32)


def _tiles(n, t, prompt):
    if prompt:
        tm = min(512, t)
        return dict(tm=tm, tq=tm, tk=tm, tb=tm, cb=min(128, t), bb=2, tmoe=min(512, n),
                    experts_per_step=EXPERTS_PER_GROUP)
    return dict(tm=min(512, n), tq=t, tk=None, tb=t, cb=t, bb=2, tmoe=min(256, n),
                experts_per_step=1)


def _layer(x, p_all, layer, cs, s0_all, past, lbp, prm, small, *, alpha, prompt):
    bsz, t, _ = x.shape
    n = bsz * t
    tl = _tiles(n, t, prompt)
    x2d = x.reshape(n, D_MODEL)
    tm, tb, cb = tl["tm"], tl["tb"], tl["cb"]
    outs = _inproj(x2d, cs, prm["win"], lbp, small["qg"], prm["wqb"], small["kvg"], prm["wuk"],
                   tm=tm, tq=tl["tq"], cb=cb, prompt=prompt)
    qs, kin, logf, v, gs, ckv, kpe = outs[:7]
    r4 = lambda a: a.reshape(HA_HEADS, bsz, t, HA_DK)
    if prompt:
        qt, vt, kcat, ckvt = outs[7:]
        vt = vt.reshape(bsz, t // tb, HA_HEADS, tb // cb, HA_DV, cb)
    else:
        qabs, qpe = outs[7:]
        vt = jnp.transpose(r4(v), (1, 0, 3, 2)).astype(BF16)
        vt = vt.reshape(bsz, 1, HA_HEADS, 1, HA_DV, t)
    o_raw, s_new = _hgrn(r4(qs), r4(kin), r4(logf), r4(v), vt, s0_all, layer if not prompt else 0,
                         bb=tl["bb"], tb=tb, cb=cb)
    if prompt:
        olat = _attn_prompt(qt, kcat, ckvt, bsz=bsz, t=t, tq=tl["tq"], tk=tl["tk"])
    else:
        olat = _attn_sample(qabs, qpe, past[0], past[1], layer, ckv, kpe, ts=t)
    x1g = _merge(x2d, o_raw.reshape(HA_HEADS, n, HA_DV), gs, olat, prm["wuv"], prm["wo"],
                 small["hg"], small["mg"], small["g1"], small["b1"],
                 small["wr"], small["rb"], tm=tm, tq=tl["tq"], alpha=alpha)
    tmoe = tl["tmoe"]
    dst, tile_group, m_sorted = _route(x1g, tm=tmoe)
    xs = _sc_scatter_rows(x1g, dst, m_sorted)
    ys = _moe_sorted(tile_group, xs, prm["wg"], prm["wu"], prm["wd"], layer, tm=tmoe,
                     per_step=tl["experts_per_step"])
    ffn = _sc_gather_rows(ys, dst[:n])
    y = _post(x1g, ffn, p_all.reshape(-1, n, PLE_DIM), layer, small["g2"], small["b2"],
              prm["wpg"], prm["wp"], tm=tm, alpha=alpha)
    return (y.reshape(bsz, t, D_MODEL), s_new, ckv.reshape(bsz, t, KV_LORA),
            kpe.reshape(bsz, t, ROPE_DIM))


def kernel(x_prompt, x_sample, p_prompt, p_sample, state_hgrn, cache_ckv, cache_kpe, w_in,
           lb_logits, hgrn_norm_g, q_norm_g, w_qb, kv_norm_g, w_kvb, mla_norm_g, w_o,
           ln1_g, ln1_b, ln2_g, ln2_b, w_router, router_bias, w_gate, w_up, w_down,
           w_ple, w_ple_gate):
    depth = w_in.shape[0]
    alpha = (2 * depth) ** 0.25
    bp, tp, _ = x_prompt.shape
    bs, ts, _ = x_sample.shape
    past = cache_ckv.shape[2]

    sm = jax.nn.softmax(lb_logits.astype(F32), axis=0)
    lb_all = jnp.maximum(jnp.cumsum(sm, axis=0) - sm[0:1], 0.0)
    lbp_all = jnp.stack([jnp.log(lb_all), jnp.log1p(-lb_all), 1.0 - lb_all], axis=1)

    cs_p = jnp.asarray(_rope_table(np.arange(tp)))
    tm_s = _tiles(bs * ts, ts, False)["tm"]
    cs_s = jnp.asarray(np.tile(_rope_table(past + np.arange(ts)), (tm_s // ts, 1)))
    wr = jnp.pad(w_router, ((0, 0), (0, LANES - N_EXPERTS))).astype(BF16)
    rb = jnp.pad(router_bias.astype(F32), (0, LANES - N_EXPERTS)).reshape(1, LANES)
    s0_p = jnp.zeros((1, bp, HA_HEADS, HA_DK, HA_DV), F32)
    cache_kpe_t = jnp.swapaxes(cache_kpe, 2, 3)

    yp, ys = x_prompt, x_sample
    res = [[] for _ in range(6)]
    for l in range(depth):
        prm = _prep_layer(l, w_in, w_qb, w_kvb, w_o, w_gate, w_up, w_down, w_ple, w_ple_gate)
        row = lambda a: a[l].reshape(1, -1).astype(F32)
        small = dict(qg=row(q_norm_g), kvg=row(kv_norm_g), hg=row(hgrn_norm_g), mg=row(mla_norm_g),
                     g1=row(ln1_g), b1=row(ln1_b), g2=row(ln2_g), b2=row(ln2_b), wr=wr, rb=rb)
        yp, sp, cp, kp = _layer(yp, p_prompt, l, cs_p, s0_p, None, lbp_all[l], prm, small,
                                alpha=alpha, prompt=True)
        ys, ss, cs_, ks = _layer(ys, p_sample, l, cs_s, state_hgrn, (cache_ckv, cache_kpe_t),
                                 lbp_all[l], prm, small, alpha=alpha, prompt=False)
        for lst, a in zip(res, (sp, cp, kp, ss, cs_, ks)):
            lst.append(a)
    sp, cp, kp, ss, cs_, ks = (jnp.stack(a) for a in res)
    return (yp, ys, sp, cp, kp, ss, cs_, ks)
```

```python
import functools

import jax
import jax.numpy as jnp
import numpy as np
from jax import lax
from jax.experimental import pallas as pl
from jax.experimental.pallas import tpu as pltpu
from jax.experimental.pallas import tpu_sc as plsc

F32 = jnp.float32
BF16 = jnp.bfloat16

D_MODEL = 1024
HA_HEADS = 4
HA_DK = 128
HA_DV = 128
HB_HEADS = 4
Q_LORA = 384
KV_LORA = 256
NOPE_DIM = 128
ROPE_DIM = 64
V_DIM = 128
ROPE_THETA = 10000.0
MLA_SCALE = (NOPE_DIM + ROPE_DIM) ** -0.5
LOG2E = 1.4426950408889634
Q_SCALE = MLA_SCALE * LOG2E
CHUNK = 64
N_EXPERTS = 16
N_GROUPS = 4
EXPERTS_PER_GROUP = N_EXPERTS // N_GROUPS
D_EXPERT = 512
PLE_DIM = 256
NEG_INF = -1e30

HA_W = HA_HEADS * HA_DK
COL_CQ = 4 * HA_W
COL_CKV = COL_CQ + Q_LORA
COL_KPE = COL_CKV + KV_LORA
D_IN_AUG = COL_KPE + 2 * ROPE_DIM
Q_HEAD_AUG = NOPE_DIM + 2 * ROPE_DIM

LANES = 128
SUBLANES = 8
QK_DIM = KV_LORA + LANES
GROUP_LANE = N_EXPERTS
SC_WINDOW = 128
SC_ROW = 256
X_PIECES = D_MODEL // SC_ROW
SUB_BLOCK = 16
VMEM_LIMIT = 56 * 1024 * 1024


def _cparams(sem, vmem=VMEM_LIMIT):
    return pltpu.CompilerParams(dimension_semantics=sem, vmem_limit_bytes=vmem)


def _dot(a, b):
    return jnp.dot(a, b, preferred_element_type=F32)


def _dot_nt(a, b):
    return lax.dot_general(a, b, (((1,), (1,)), ((), ())), preferred_element_type=F32)


def _rms(x, g, eps=1e-6):
    return x * lax.rsqrt(jnp.mean(x * x, axis=-1, keepdims=True) + eps) * g


def _layer_norm(x, g, b, eps=1e-5):
    mu = jnp.mean(x, axis=-1, keepdims=True)
    xc = x - mu
    var = jnp.mean(xc * xc, axis=-1, keepdims=True)
    return xc * lax.rsqrt(var + eps) * g + b


def _silu(x):
    return x * jax.nn.sigmoid(x)


def _full(shape):
    nd = len(shape)
    return pl.BlockSpec(shape, lambda *_: (0,) * nd)


def _inproj_kernel(x_ref, cs_ref, win_ref, lb_ref, qg_ref, wqb_ref, kvg_ref, wuk_ref, prev_ref,
                   qs_ref, kin_ref, logf_ref, v_ref, gs_ref, ckv_ref, kpe_ref, *rest,
                   tq, cb, prompt, n_prev):
    tm = x_ref.shape[0]
    x = x_ref[...].astype(BF16)

    def sect(lo, hi):
        return _dot(x, win_ref[:, lo:hi])

    def put_heads(ref, val):
        for h in range(HA_HEADS):
            ref[h] = val[:, h * HA_DK:(h + 1) * HA_DK]

    cqn = _rms(sect(COL_CQ, COL_CKV), qg_ref[...]).astype(BF16)

    put_heads(qs_ref, _silu(sect(0, HA_W)))
    fa = sect(HA_W, 2 * HA_W)
    log_lb = lb_ref[0:1, :]
    log1m_lb = lb_ref[1:2, :]
    one_m_lb = lb_ref[2:3, :]
    e = jnp.exp(-jnp.abs(fa))
    r = 1.0 / (1.0 + e)
    c = log1m_lb + jnp.minimum(fa, 0.0) + jnp.log(r)
    put_heads(logf_ref, jnp.maximum(log_lb, c) + jnp.log(1.0 + jnp.exp(-jnp.abs(log_lb - c))))
    put_heads(kin_ref, one_m_lb * jnp.where(fa >= 0.0, e * r, r))
    v = sect(2 * HA_W, 3 * HA_W)
    put_heads(v_ref, v)
    gs_ref[...] = _silu(sect(3 * HA_W, 4 * HA_W))

    cs = cs_ref[...]

    def rope(t):
        prod = t * cs
        return prod + pltpu.roll(prod, ROPE_DIM, 1)

    ckv = _rms(sect(COL_CKV, COL_KPE), kvg_ref[...])
    for l in range(n_prev):
        ckv_ref[l] = prev_ref[l]
    ckv_ref[n_prev] = ckv
    kpe2 = rope(sect(COL_KPE, D_IN_AUG))
    kpe_ref[...] = kpe2[:, :ROPE_DIM]
    if prompt:
        qt_ref, vt_ref, kcat_ref, ckvt_ref = rest
        for h in range(HA_HEADS):
            for ci in range(tm // cb):
                vt_ref[0, h, ci] = v[ci * cb:(ci + 1) * cb, h * HA_DV:(h + 1) * HA_DV].T.astype(BF16)
        kcat_ref[:, :KV_LORA] = ckv.astype(BF16)
        kcat_ref[:, KV_LORA:] = kpe2.astype(BF16)
        ckvt_ref[0] = ckv.T.astype(BF16)
    else:
        qabs_ref, qpe_ref = rest

    lane = lax.broadcasted_iota(jnp.int32, (tm, LANES), 1)
    qhs = [_dot(cqn, wqb_ref[:, h * Q_HEAD_AUG:(h + 1) * Q_HEAD_AUG]) for h in range(HB_HEADS)]
    qabss = [_dot(qhs[h][:, :NOPE_DIM].astype(BF16), wuk_ref[h]) for h in range(HB_HEADS)]
    for h in range(HB_HEADS):
        qabs = qabss[h] * Q_SCALE
        qpe = jnp.where(lane < ROPE_DIM, rope(qhs[h][:, NOPE_DIM:]) * Q_SCALE, 0.0)
        if prompt:
            qabs_t = qabs.T.astype(BF16)
            qpe_t = qpe.T.astype(BF16)
            for s in range(tm // tq):
                qt_ref[s, :KV_LORA, h * tq:(h + 1) * tq] = qabs_t[:, s * tq:(s + 1) * tq]
                qt_ref[s, KV_LORA:, h * tq:(h + 1) * tq] = qpe_t[:, s * tq:(s + 1) * tq]
        else:
            for s in range(tm // tq):
                qabs_ref[s, h] = qabs[s * tq:(s + 1) * tq].astype(BF16)
                qpe_ref[s, h] = qpe[s * tq:(s + 1) * tq].astype(BF16)


def _inproj(x, cs, w_in, lbp, qg, wqb, kvg, wuk, prev_ckv, *, tm, tq, cb, prompt):
    n = x.shape[0]
    nt = n // tm
    n_prev = 0 if prev_ckv is None else prev_ckv.shape[0]
    if prev_ckv is None:
        prev_ckv = jnp.zeros((1, tm, KV_LORA), F32)
        prev_spec = _full(prev_ckv.shape)
    else:
        prev_spec = pl.BlockSpec((n_prev, tm, KV_LORA), lambda i: (0, i, 0))
    row = lambda w: pl.BlockSpec((tm, w), lambda i: (i, 0))
    heads = pl.BlockSpec((HA_HEADS, tm, HA_DK), lambda i: (0, i, 0))
    qblk = lambda w: pl.BlockSpec((tm // tq, HB_HEADS, tq, w), lambda i: (i, 0, 0, 0))
    head_major = jax.ShapeDtypeStruct((HA_HEADS, n, HA_DK), F32)
    out_shape = [
        head_major,
        head_major,
        head_major,
        head_major,
        jax.ShapeDtypeStruct((n, HA_W), F32),
        jax.ShapeDtypeStruct((n_prev + 1, n, KV_LORA), F32),
        jax.ShapeDtypeStruct((n, ROPE_DIM), F32),
    ]
    out_specs = [heads, heads, heads, heads, row(HA_W),
                 pl.BlockSpec((n_prev + 1, tm, KV_LORA), lambda i: (0, i, 0)), row(ROPE_DIM)]
    if prompt:
        out_shape += [jax.ShapeDtypeStruct((n // tq, QK_DIM, HB_HEADS * tq), BF16),
                      jax.ShapeDtypeStruct((nt, HA_HEADS, tm // cb, HA_DV, cb), BF16),
                      jax.ShapeDtypeStruct((n, QK_DIM), BF16),
                      jax.ShapeDtypeStruct((nt, KV_LORA, tm), BF16)]
        out_specs += [pl.BlockSpec((tm // tq, QK_DIM, HB_HEADS * tq), lambda i: (i, 0, 0)),
                      pl.BlockSpec((1, HA_HEADS, tm // cb, HA_DV, cb), lambda i: (i, 0, 0, 0, 0)),
                      row(QK_DIM),
                      pl.BlockSpec((1, KV_LORA, tm), lambda i: (i, 0, 0))]
    else:
        out_shape += [jax.ShapeDtypeStruct((n // tq, HB_HEADS, tq, KV_LORA), BF16),
                      jax.ShapeDtypeStruct((n // tq, HB_HEADS, tq, LANES), BF16)]
        out_specs += [qblk(KV_LORA), qblk(LANES)]
    return pl.pallas_call(
        functools.partial(_inproj_kernel, tq=tq, cb=cb, prompt=prompt, n_prev=n_prev),
        grid=(nt,),
        in_specs=[row(D_MODEL),
                  pl.BlockSpec((tm, LANES), lambda i: (i % (cs.shape[0] // tm), 0)),
                  _full(w_in.shape), _full(lbp.shape), _full(qg.shape),
                  _full(wqb.shape), _full(kvg.shape), _full(wuk.shape), prev_spec],
        out_specs=out_specs,
        out_shape=out_shape,
        compiler_params=_cparams(("parallel",)),
        name="inproj",
    )(x, cs, w_in, lbp, qg, wqb, kvg, wuk, prev_ckv)


def _hgrn_kernel(q_ref, k_ref, g_ref, v_ref, vt_ref, s0_ref, o_ref, sfin_ref, st_scr, b2_scr,
                 hide_ref, *, cb):
    _, bb, tb, _ = q_ref.shape
    nsub = cb // SUB_BLOCK
    ti = pl.program_id(1)

    @pl.when(ti == 0)
    def _():
        for b in range(bb):
            for h in range(HA_HEADS):
                st_scr[b, h] = s0_ref[b, h].T
        row8 = lax.broadcasted_iota(jnp.int32, (SUBLANES, LANES), 0)
        for s in range(SUBLANES):
            hide_ref[s] = jnp.where(row8 >= s, 0.0, NEG_INF)

    r_i = lax.broadcasted_iota(jnp.int32, (cb, cb), 0)
    c_i = lax.broadcasted_iota(jnp.int32, (cb, cb), 1)
    tril = (r_i >= c_i).astype(F32)

    chains = [(b, h) for b in range(bb) for h in range(HA_HEADS)]

    def matmul_part(ci):
        r0 = pl.multiple_of(ci * cb, cb)
        rows = pl.ds(r0, cb)

        def sub(ref, b, h, lo, size):
            return ref[h, b, pl.ds(pl.multiple_of(r0 + lo, SUB_BLOCK), size), :]

        for b, h in chains:
            bcum = jnp.dot(tril, g_ref[h, b, rows, :], precision=lax.Precision.HIGHEST,
                           preferred_element_type=F32)
            b2_scr[b * HA_HEADS + h] = bcum * LOG2E
        for b, h in chains:
            b2 = b2_scr[b * HA_HEADS + h]
            o_ref[h, b, rows, :] = _dot_nt((q_ref[h, b, rows, :] * jnp.exp2(b2)).astype(BF16),
                                           st_scr[b, h].astype(BF16))
        for j in range(nsub - 1):
            lo, hi = j * SUB_BLOCK, (j + 1) * SUB_BLOCK
            below = pl.ds(pl.multiple_of(r0 + hi, SUB_BLOCK), cb - hi)
            for b, h in chains:
                slot = b * HA_HEADS + h
                bnd = b2_scr[slot, hi - 1:hi, :]
                kd = (sub(k_ref, b, h, lo, SUB_BLOCK)
                      * jnp.exp2(bnd - b2_scr[slot, lo:hi, :])).astype(BF16)
                qe = (sub(q_ref, b, h, hi, cb - hi)
                      * jnp.exp2(b2_scr[slot, hi:, :] - bnd)).astype(BF16)
                a = _dot_nt(qe, kd).astype(BF16)
                o_ref[h, b, below, :] += _dot(a, sub(v_ref, b, h, lo, SUB_BLOCK).astype(BF16))
        for b, h in chains:
            slot = b * HA_HEADS + h
            b_last = b2_scr[slot, cb - 1:cb, :]
            kd_all = (k_ref[h, b, rows, :] * jnp.exp2(b_last - b2_scr[slot])).astype(BF16)
            st_scr[b, h] = st_scr[b, h] * jnp.exp2(b_last) + _dot(vt_ref[b, h, ci], kd_all)

    def pairwise_part(ci, b, h, j):
        slot = b * HA_HEADS + h
        r0 = pl.multiple_of(ci * cb, cb)
        lo = j * SUB_BLOCK
        mid = lo + SUBLANES
        top = pl.ds(pl.multiple_of(r0 + lo, SUBLANES), SUBLANES)
        bot = pl.ds(pl.multiple_of(r0 + mid, SUBLANES), SUBLANES)
        q_top, q_bot = q_ref[h, b, top, :], q_ref[h, b, bot, :]
        b_top, b_bot = b2_scr[slot, lo:mid, :], b2_scr[slot, mid:mid + SUBLANES, :]
        acc_top, acc_bot = o_ref[h, b, top, :], o_ref[h, b, bot, :]
        for s in range(SUB_BLOCK):
            row = pl.ds(r0 + lo + s, 1)
            bs = b2_scr[slot, lo + s:lo + s + 1, :]
            ks = k_ref[h, b, row, :]
            vs = v_ref[h, b, row, :]
            if s < SUBLANES:
                w = q_top * jnp.exp2(b_top - bs + hide_ref[s]) * ks
                acc_top = acc_top + jnp.sum(w, axis=-1, keepdims=True) * vs
                w = q_bot * jnp.exp2(b_bot - bs) * ks
            else:
                w = q_bot * jnp.exp2(b_bot - bs + hide_ref[s - SUBLANES]) * ks
            acc_bot = acc_bot + jnp.sum(w, axis=-1, keepdims=True) * vs
        o_ref[h, b, top, :] = acc_top
        o_ref[h, b, bot, :] = acc_bot

    def chunk(ci, carry):
        matmul_part(ci)
        for j in range(nsub):
            for b, h in chains:
                pairwise_part(ci, b, h, j)
        return carry

    lax.fori_loop(0, tb // cb, chunk, 0)

    @pl.when(ti == pl.num_programs(1) - 1)
    def _():
        for b in range(bb):
            for h in range(HA_HEADS):
                sfin_ref[b, h] = st_scr[b, h].T


def _hgrn(qs, kin, logf, v, vt, s0_all, layer, *, bb, tb, cb):
    _, bsz, t, _ = qs.shape
    blk = pl.BlockSpec((HA_HEADS, bb, tb, HA_DK), lambda i, j: (0, i, j, 0))
    st_blk = pl.BlockSpec((bb, HA_HEADS, HA_DK, HA_DV), lambda i, j: (i, 0, 0, 0))
    s0_blk = pl.BlockSpec((None, bb, HA_HEADS, HA_DK, HA_DV), lambda i, j: (layer, i, 0, 0, 0))
    vt_blk = pl.BlockSpec((bb, None, HA_HEADS, tb // cb, HA_DV, cb),
                          lambda i, j: (i, j, 0, 0, 0, 0))
    return pl.pallas_call(
        functools.partial(_hgrn_kernel, cb=cb),
        grid=(bsz // bb, t // tb),
        in_specs=[blk, blk, blk, blk, vt_blk, s0_blk],
        out_specs=[blk, st_blk],
        out_shape=[jax.ShapeDtypeStruct(qs.shape, F32),
                   jax.ShapeDtypeStruct(s0_all.shape[1:], F32)],
        scratch_shapes=[pltpu.VMEM((bb, HA_HEADS, HA_DV, HA_DK), F32),
                        pltpu.VMEM((bb * HA_HEADS, cb, HA_DK), F32),
                        pltpu.VMEM((SUBLANES, SUBLANES, LANES), F32)],
        compiler_params=_cparams(("parallel", "arbitrary")),
        name="hgrn",
    )(qs, kin, logf, v, vt, s0_all)


def _attn_prompt_kernel(qt_ref, kc_ref, kt_ref, o_ref, s0, s1, p0, p1, a0, a1, m_scr, l_scr,
                        acc_scr, *, tq, tk):
    qi = pl.program_id(1)
    jd = (qi * tq) // tk
    shift = CHUNK.bit_length() - 1

    heads = range(HB_HEADS)

    def scores(j, s_dst, hs=heads):
        keys = kc_ref[pl.ds(pl.multiple_of(j * tk, tk), tk), :]
        for h in hs:
            cols = slice(h * tq, (h + 1) * tq)
            s_dst[:, cols] = _dot(keys, qt_ref[0, :, cols])

    def softmax(j, s_src, p_dst, a_dst, masked, hs=heads):
        if masked:
            kpos = j * tk + lax.broadcasted_iota(jnp.int32, (tk, 1), 0)
            qpos = qi * tq + lax.broadcasted_iota(jnp.int32, (1, tq), 1)
            visible = (kpos >> shift) <= (qpos >> shift)
        for h in hs:
            cols = slice(h * tq, (h + 1) * tq)
            def load():
                s = s_src[:, cols]
                return jnp.where(visible, s, NEG_INF) if masked else s

            m_prev = m_scr[:, cols]
            m_new = jnp.maximum(m_prev, jnp.max(load(), axis=0, keepdims=True))
            alpha = jnp.exp2(m_prev - m_new)
            p = jnp.exp2(load() - m_new)
            l_scr[:, cols] = alpha * l_scr[:, cols] + jnp.sum(p, axis=0, keepdims=True)
            m_scr[:, cols] = m_new
            a_dst[:, cols] = alpha
            p_dst[:, cols] = p.astype(BF16)

    def values(j, p_src, a_src, hs=heads):
        for h in hs:
            cols = slice(h * tq, (h + 1) * tq)
            acc_scr[:, cols] = (a_src[:, cols] * acc_scr[:, cols]
                                + _dot(kt_ref[j], p_src[:, cols]))

    def trip(j, s_cur, s_nxt, p_cur, p_prv, a_cur, a_prv):
        scores(j + 1, s_nxt)
        softmax(j, s_cur, p_cur, a_cur, False)
        values(jnp.maximum(j - 1, 0), p_prv, a_prv)

    m_scr[...] = jnp.full(m_scr.shape, NEG_INF, F32)
    l_scr[...] = jnp.zeros(l_scr.shape, F32)
    acc_scr[...] = jnp.zeros(acc_scr.shape, F32)
    p1[...] = jnp.zeros(p1.shape, BF16)
    a1[...] = jnp.ones(a1.shape, F32)
    scores(0, s0)

    def pair(jj, carry):
        trip(2 * jj, s0, s1, p0, p1, a0, a1)
        trip(2 * jj + 1, s1, s0, p1, p0, a1, a0)
        return carry

    lax.fori_loop(0, jd // 2, pair, 0)

    @pl.when(jd % 2 == 1)
    def _():
        trip(jd - 1, s0, s1, p0, p1, a0, a1)

    def drain(s_cur, p_cur, p_prv, a_cur, a_prv):
        softmax(jd, s_cur, p_cur, a_cur, True)
        values(jnp.maximum(jd - 1, 0), p_prv, a_prv)
        values(jd, p_cur, a_cur)

    pl.when(jd % 2 == 0)(functools.partial(drain, s0, p0, p1, a0, a1))
    pl.when(jd % 2 == 1)(functools.partial(drain, s1, p1, p0, a1, a0))

    for h in range(HB_HEADS):
        cols = slice(h * tq, (h + 1) * tq)
        o_ref[0, h] = (acc_scr[:, cols] * (1.0 / l_scr[:, cols])).T.astype(BF16)


def _attn_prompt(qt, kcat, ckvt, *, bsz, t, tq, tk):
    nq = t // tq
    nk = t // tk
    cols = HB_HEADS * tq
    single = pl.Buffered(1)
    return pl.pallas_call(
        functools.partial(_attn_prompt_kernel, tq=tq, tk=tk),
        grid=(bsz, nq),
        in_specs=[
            pl.BlockSpec((1, QK_DIM, cols), lambda b, i: (b * nq + i, 0, 0)),
            pl.BlockSpec((t, QK_DIM), lambda b, i: (b, 0), pipeline_mode=single),
            pl.BlockSpec((nk, KV_LORA, tk), lambda b, i: (b, 0, 0), pipeline_mode=single),
        ],
        out_specs=pl.BlockSpec((1, HB_HEADS, tq, KV_LORA), lambda b, i: (b * nq + i, 0, 0, 0)),
        out_shape=jax.ShapeDtypeStruct((bsz * nq, HB_HEADS, tq, KV_LORA), BF16),
        scratch_shapes=[pltpu.VMEM((tk, cols), F32), pltpu.VMEM((tk, cols), F32),
                        pltpu.VMEM((tk, cols), BF16), pltpu.VMEM((tk, cols), BF16),
                        pltpu.VMEM((1, cols), F32), pltpu.VMEM((1, cols), F32),
                        pltpu.VMEM((1, cols), F32), pltpu.VMEM((1, cols), F32),
                        pltpu.VMEM((KV_LORA, cols), F32)],
        compiler_params=_cparams(("parallel", "arbitrary")),
        name="attn_prompt",
    )(qt, kcat, ckvt)


def _attn_sample_kernel(qa_ref, qp_ref, pckv_ref, pkpe_ref, nckv_ref, nkpe_ref, o_ref):
    _, _, ts, _ = qa_ref.shape
    rows = HB_HEADS * ts
    qa = qa_ref[0].reshape(rows, KV_LORA)
    qp = qp_ref[0].reshape(rows, LANES)[:, :ROPE_DIM]
    pckv = pckv_ref[0].astype(BF16)
    nckv = nckv_ref[...].astype(BF16)
    s_past = _dot_nt(qa, pckv) + _dot(qp, pkpe_ref[0].astype(BF16))
    s_new = _dot_nt(qa, nckv) + _dot_nt(qp, nkpe_ref[...].astype(BF16))
    m = jnp.maximum(jnp.max(s_past, axis=-1, keepdims=True), jnp.max(s_new, axis=-1, keepdims=True))
    p_past = jnp.exp2(s_past - m)
    p_new = jnp.exp2(s_new - m)
    l = jnp.sum(p_past, axis=-1, keepdims=True) + jnp.sum(p_new, axis=-1, keepdims=True)
    o = (_dot(p_past.astype(BF16), pckv) + _dot(p_new.astype(BF16), nckv)) / l
    o_ref[0] = o.astype(BF16).reshape(HB_HEADS, ts, KV_LORA)


def _attn_sample(qabs, qpe, past_ckv, past_kpe, layer, ckv, kpe, *, ts):
    _, bsz, past, _ = past_ckv.shape
    return pl.pallas_call(
        _attn_sample_kernel,
        grid=(bsz,),
        in_specs=[
            pl.BlockSpec((1, HB_HEADS, ts, KV_LORA), lambda b: (b, 0, 0, 0)),
            pl.BlockSpec((1, HB_HEADS, ts, LANES), lambda b: (b, 0, 0, 0)),
            pl.BlockSpec((None, 1, past, KV_LORA), lambda b: (layer, b, 0, 0)),
            pl.BlockSpec((None, 1, ROPE_DIM, past), lambda b: (layer, b, 0, 0)),
            pl.BlockSpec((None, ts, KV_LORA), lambda b: (layer, b, 0)),
            pl.BlockSpec((ts, ROPE_DIM), lambda b: (b, 0)),
        ],
        out_specs=pl.BlockSpec((1, HB_HEADS, ts, KV_LORA), lambda b: (b, 0, 0, 0)),
        out_shape=jax.ShapeDtypeStruct(qabs.shape, BF16),
        compiler_params=_cparams(("parallel",)),
        name="attn_sample",
    )(qabs, qpe, past_ckv, past_kpe, ckv, kpe)


def _gates(logits, bias):
    tm = logits.shape[0]
    lane = lax.broadcasted_iota(jnp.int32, (tm, LANES), 1)
    pos = lane % EXPERTS_PER_GROUP
    valid = lane < N_EXPERTS
    scores = jax.nn.sigmoid(logits)
    sel = jnp.where(valid, scores + bias, -jnp.inf)

    others = []
    for r in range(1, EXPERTS_PER_GROUP):
        others.append(jnp.where(pos >= r, pltpu.roll(sel, r, 1),
                                pltpu.roll(sel, LANES - (EXPERTS_PER_GROUP - r), 1)))
    a, b, c, d = sel, others[0], others[1], others[2]
    hi1, lo1 = jnp.maximum(a, b), jnp.minimum(a, b)
    hi2, lo2 = jnp.maximum(c, d), jnp.minimum(c, d)
    gscore = jnp.maximum(hi1, hi2) + jnp.maximum(jnp.minimum(hi1, hi2), jnp.maximum(lo1, lo2))
    gmax = jnp.max(gscore, axis=-1, keepdims=True)
    group = (lane // EXPERTS_PER_GROUP).astype(F32)
    gidx = jnp.min(jnp.where(gscore == gmax, group, float(LANES)), axis=-1, keepdims=True)

    rank = jnp.zeros((tm, LANES), jnp.int32)
    for r, o in enumerate(others, start=1):
        ahead = (o > sel) | ((o == sel) & (pos >= r))
        rank = rank + ahead.astype(jnp.int32)
    chosen = (group == gidx) & (rank < 2) & valid
    w = jnp.where(chosen, scores, 0.0)
    w = w / jnp.sum(w, axis=-1, keepdims=True)
    return jnp.where(lane == GROUP_LANE, gidx, w)


def _merge_kernel(x_ref, o_ref, gs_ref, olat_ref, wuv_ref, wo_ref, hg_ref, mg_ref,
                  g1_ref, b1_ref, wr_ref, rb_ref, x1g_ref, *, alpha):
    _, _, tq, _ = olat_ref.shape
    tm = x_ref.shape[0]
    hg = hg_ref[...]
    d_a = HA_HEADS * HA_DV
    halves = [(i * tm // 2, (i + 1) * tm // 2) for i in range(2)]

    def lat(h, lo, hi):
        if tq >= hi - lo:
            return olat_ref[lo // tq, h, lo % tq:lo % tq + hi - lo]
        return jnp.concatenate([olat_ref[s, h] for s in range(lo // tq, hi // tq)], axis=0)

    ob = [[_dot(lat(h, lo, hi), wuv_ref[h]) for h in range(HB_HEADS)] for lo, hi in halves]
    oa = [jnp.concatenate([_rms(o_ref[h, lo:hi, :], hg) * gs_ref[lo:hi, h * HA_DV:(h + 1) * HA_DV]
                           for h in range(HA_HEADS)], axis=-1) for lo, hi in halves]
    ob = [_rms(jnp.concatenate(o, axis=-1), mg_ref[...]) for o in ob]
    mix = [_dot(a.astype(BF16), wo_ref[:d_a, :]) + _dot(o.astype(BF16), wo_ref[d_a:, :])
           for a, o in zip(oa, ob)]
    x1 = [_layer_norm(alpha * x_ref[lo:hi, :] + m, g1_ref[...], b1_ref[...])
          for (lo, hi), m in zip(halves, mix)]
    logits = [_dot(x.astype(BF16), wr_ref[...]) for x in x1]
    for (lo, hi), x, lg in zip(halves, x1, logits):
        for c in range(X_PIECES):
            x1g_ref[c, lo:hi, :] = x[:, c * SC_ROW:(c + 1) * SC_ROW]
        x1g_ref[X_PIECES, lo:hi, :LANES] = _gates(lg, rb_ref[...])
        x1g_ref[X_PIECES, lo:hi, LANES:] = jnp.zeros((hi - lo, SC_ROW - LANES), F32)


def _merge(x, o_raw, gs, olat, wuv, wo, hg, mg, g1, b1, wr, rb, *, tm, tq, alpha):
    n = x.shape[0]
    row = lambda w: pl.BlockSpec((tm, w), lambda i: (i, 0))
    return pl.pallas_call(
        functools.partial(_merge_kernel, alpha=alpha),
        grid=(n // tm,),
        in_specs=[row(D_MODEL), pl.BlockSpec((HA_HEADS, tm, HA_DV), lambda i: (0, i, 0)), row(HA_W),
                  pl.BlockSpec((tm // tq, HB_HEADS, tq, KV_LORA), lambda i: (i, 0, 0, 0)),
                  _full(wuv.shape), _full(wo.shape), _full(hg.shape), _full(mg.shape),
                  _full(g1.shape), _full(b1.shape), _full(wr.shape), _full(rb.shape)],
        out_specs=pl.BlockSpec((X_PIECES + 1, tm, SC_ROW), lambda i: (0, i, 0)),
        out_shape=jax.ShapeDtypeStruct((X_PIECES + 1, n, SC_ROW), F32),
        compiler_params=_cparams(("parallel",)),
        name="merge",
    )(x, o_raw, gs, olat, wuv, wo, hg, mg, g1, b1, wr, rb)


def _sc_scatter_rows(src, dst, m):
    parts, n, _ = src.shape
    windows = dst.shape[0] // SC_WINDOW
    src_windows = n // SC_WINDOW
    pieces = src.reshape(parts * n, SC_ROW)
    piece_dst = (jnp.arange(parts, dtype=jnp.int32)[:, None] * m + dst[None, :]).reshape(1, -1)
    mesh = plsc.VectorSubcoreMesh(core_axis_name="core", subcore_axis_name="subcore")

    @functools.partial(pl.kernel, out_type=jax.ShapeDtypeStruct((parts * m, SC_ROW), src.dtype),
                       mesh=mesh)
    def scatter(x_hbm, i_hbm, o_hbm):
        def body(x_vmem, i_vmem):
            pltpu.sync_copy(x_vmem, o_hbm.at[i_vmem.at[0]])

        pltpu.emit_pipeline(
            body,
            grid=(parts * windows,),
            in_specs=[pl.BlockSpec((SC_WINDOW, SC_ROW),
                                   lambda i: ((i // windows) * src_windows
                                              + (i % windows) % src_windows, 0)),
                      pl.BlockSpec((1, SC_WINDOW), lambda i: (0, i))],
            out_specs=[],
            core_axis_name=("core", "subcore"),
            dimension_semantics=(pltpu.PARALLEL,),
        )(x_hbm, i_hbm)

    return scatter(pieces, piece_dst).reshape(parts, m, SC_ROW)


def _sc_gather_rows(src, idx):
    parts, n, _ = src.shape
    m = idx.shape[0] * parts
    pieces = src.reshape(parts * n, SC_ROW)
    piece_idx = (jnp.arange(parts, dtype=jnp.int32)[:, None] * n + idx[None, :]).reshape(1, m)
    mesh = plsc.VectorSubcoreMesh(core_axis_name="core", subcore_axis_name="subcore")

    @functools.partial(pl.kernel, out_type=jax.ShapeDtypeStruct((m, SC_ROW), src.dtype), mesh=mesh)
    def gather(x_hbm, i_hbm, o_hbm):
        def body(i_vmem, o_vmem):
            pltpu.sync_copy(x_hbm.at[i_vmem.at[0]], o_vmem)

        pltpu.emit_pipeline(
            body,
            grid=(m // SC_WINDOW,),
            in_specs=[pl.BlockSpec((1, SC_WINDOW), lambda i: (0, i))],
            out_specs=[pl.BlockSpec((SC_WINDOW, SC_ROW), lambda i: (i, 0))],
            core_axis_name=("core", "subcore"),
            dimension_semantics=(pltpu.PARALLEL,),
        )(i_hbm, o_hbm)

    return gather(pieces, piece_idx).reshape(parts, idx.shape[0], SC_ROW)


def _moe_kernel(tg_ref, xs_ref, wg_ref, wu_ref, wd_ref, y_ref, acc_scr):
    tm = xs_ref.shape[1]
    per_step = wg_ref.shape[0]
    step = pl.program_id(1)
    first_expert = tg_ref[pl.program_id(0)] * EXPERTS_PER_GROUP + step * per_step
    xb = jnp.concatenate([xs_ref[c] for c in range(X_PIECES)], axis=1).astype(BF16)
    gates = xs_ref[X_PIECES, :, :LANES]
    lane = lax.broadcasted_iota(jnp.int32, (tm, LANES), 1)
    acc = jnp.zeros((tm, D_MODEL), F32)
    for k in range(per_step):
        hmid = (_silu(_dot(xb, wg_ref[k].astype(BF16))) * _dot(xb, wu_ref[k].astype(BF16)))
        gcol = jnp.sum(jnp.where(lane == first_expert + k, gates, 0.0), axis=-1, keepdims=True)
        acc = acc + gcol * _dot(hmid.astype(BF16), wd_ref[k].astype(BF16))

    @pl.when(step == 0)
    def _():
        acc_scr[...] = acc

    @pl.when(step > 0)
    def _():
        acc_scr[...] += acc

    @pl.when(step == pl.num_programs(1) - 1)
    def _():
        for c in range(D_MODEL // SC_ROW):
            y_ref[c] = acc_scr[:, c * SC_ROW:(c + 1) * SC_ROW]


def _moe_sorted(tile_group, xs, wg, wu, wd, layer, *, tm, per_step):
    m = xs.shape[1]
    steps = EXPERTS_PER_GROUP // per_step
    mode = dict(pipeline_mode=pl.Buffered(1)) if steps == 1 else {}
    wspec = lambda shape: pl.BlockSpec((None, per_step) + shape,
                                       lambda i, k, tg: (layer, tg[i] * steps + k, 0, 0), **mode)
    return pl.pallas_call(
        _moe_kernel,
        grid_spec=pltpu.PrefetchScalarGridSpec(
            num_scalar_prefetch=1,
            grid=(m // tm, steps),
            in_specs=[pl.BlockSpec((X_PIECES + 1, tm, SC_ROW), lambda i, k, tg: (0, i, 0)),
                      wspec((D_MODEL, D_EXPERT)), wspec((D_MODEL, D_EXPERT)),
                      wspec((D_EXPERT, D_MODEL))],
            out_specs=pl.BlockSpec((D_MODEL // SC_ROW, tm, SC_ROW), lambda i, k, tg: (0, i, 0)),
            scratch_shapes=[pltpu.VMEM((tm, D_MODEL), F32)],
        ),
        out_shape=jax.ShapeDtypeStruct((D_MODEL // SC_ROW, m, SC_ROW), F32),
        compiler_params=_cparams(("arbitrary", "arbitrary")),
        name="moe",
    )(tile_group, xs, wg, wu, wd)


def _route(x1g, *, tm):
    n = x1g.shape[1]
    n_tiles = n // tm + N_GROUPS - 1
    m = n_tiles * tm
    gidx = x1g[X_PIECES, :, GROUP_LANE].astype(jnp.int32)
    onehot = (gidx[:, None] == jnp.arange(N_GROUPS, dtype=jnp.int32)[None, :]).astype(jnp.int32)
    counts = jnp.sum(onehot, axis=0)
    tiles = (counts + tm - 1) // tm
    tile_end = jnp.cumsum(tiles)
    tile_start = tile_end - tiles
    rank = jnp.sum((jnp.cumsum(onehot, axis=0) - onehot) * onehot, axis=1)
    pos = jnp.take(tile_start, gidx) * tm + rank
    pad_start = tile_start * tm + counts
    pad_end = (tile_end * tm).at[N_GROUPS - 1].set(m)
    pad_cum = jnp.cumsum(pad_end - pad_start)
    k = jnp.arange(m - n, dtype=jnp.int32)
    pg = jnp.searchsorted(pad_cum, k, side="right")
    pad_rows = jnp.take(pad_start, pg) + k - jnp.take(pad_cum - (pad_end - pad_start), pg)
    dst = jnp.concatenate([pos, pad_rows]).astype(jnp.int32)
    tile_group = jnp.searchsorted(tile_end, jnp.arange(n_tiles, dtype=jnp.int32), side="right")
    return dst, jnp.minimum(tile_group, N_GROUPS - 1).astype(jnp.int32), m


def _post_kernel(x1_ref, f_ref, p_ref, g2_ref, b2_ref, wpg_ref, wp_ref, y_ref, *, alpha):
    tm = x1_ref.shape[1]
    halves = [(i * tm // 2, (i + 1) * tm // 2) for i in range(2)]
    emb = [_dot(p_ref[lo:hi, :].astype(BF16), wp_ref[...]) for lo, hi in halves]

    def rows(ref, lo, hi):
        return jnp.concatenate([ref[c, lo:hi, :] for c in range(X_PIECES)], axis=1)

    x2 = [_layer_norm(alpha * rows(x1_ref, lo, hi) + rows(f_ref, lo, hi), g2_ref[...], b2_ref[...])
          for lo, hi in halves]
    gate = [_dot(x.astype(BF16), wpg_ref[...]) for x in x2]
    for (lo, hi), x, g, e in zip(halves, x2, gate, emb):
        y_ref[lo:hi, :] = x + jax.nn.sigmoid(g) * e


def _post(x1g, ffn, p_all, layer, g2, b2, wpg, wp, *, tm, alpha):
    _, n, _ = ffn.shape
    row = lambda w: pl.BlockSpec((tm, w), lambda i: (i, 0))
    pieces = lambda a: pl.BlockSpec((a.shape[0], tm, SC_ROW), lambda i: (0, i, 0))
    return pl.pallas_call(
        functools.partial(_post_kernel, alpha=alpha),
        grid=(n // tm,),
        in_specs=[pieces(x1g), pieces(ffn),
                  pl.BlockSpec((None, tm, PLE_DIM), lambda i: (layer, i, 0)),
                  _full(g2.shape), _full(b2.shape), _full(wpg.shape), _full(wp.shape)],
        out_specs=row(D_MODEL),
        out_shape=jax.ShapeDtypeStruct((n, D_MODEL), F32),
        compiler_params=_cparams(("parallel",)),
        name="post",
    )(x1g, ffn, p_all, g2, b2, wpg, wp)


def _rot_cols(w):
    half = ROPE_DIM // 2
    return jnp.concatenate([-w[..., half:], w[..., :half]], axis=-1)


def _prep_layer(l, w_in, w_qb, w_kvb, w_o, w_gate, w_up, w_down, w_ple, w_ple_gate):
    win = w_in[l]
    win_aug = jnp.concatenate([win, _rot_cols(win[:, COL_KPE:])], axis=-1).astype(BF16)
    wqb = w_qb[l].reshape(Q_LORA, HB_HEADS, NOPE_DIM + ROPE_DIM)
    wqb_aug = jnp.concatenate([wqb, _rot_cols(wqb[..., NOPE_DIM:])], axis=-1)
    wqb_aug = wqb_aug.reshape(Q_LORA, HB_HEADS * Q_HEAD_AUG).astype(BF16)
    wkvb = w_kvb[l].reshape(KV_LORA, HB_HEADS, NOPE_DIM + V_DIM)
    wuk_t = jnp.transpose(wkvb[..., :NOPE_DIM], (1, 2, 0)).astype(BF16)
    wuv = jnp.transpose(wkvb[..., NOPE_DIM:], (1, 0, 2)).astype(BF16)
    return dict(win=win_aug, wqb=wqb_aug, wuk=wuk_t, wuv=wuv, wo=w_o[l].astype(BF16),
                wg=w_gate, wu=w_up, wd=w_down,
                wp=w_ple[l].astype(BF16), wpg=w_ple_gate[l].astype(BF16))


def _rope_table(pos):
    inv = ROPE_THETA ** (-np.arange(0, ROPE_DIM, 2, dtype=np.float64) / ROPE_DIM)
    ang = pos.astype(np.float64)[:, None] * inv[None, :]
    cos, sin = np.cos(ang), np.sin(ang)
    return np.concatenate([cos, cos, sin, sin], axis=-1).astype(np.float32)


def _tiles(n, t, prompt):
    if prompt:
        tm = min(512, t)
        return dict(tm=tm, tq=tm, tk=tm, tb=tm, cb=min(128, t), bb=2, tmoe=min(512, n),
                    experts_per_step=EXPERTS_PER_GROUP)
    return dict(tm=min(512, n), tq=t, tk=None, tb=t, cb=t, bb=2, tmoe=min(256, n),
                experts_per_step=EXPERTS_PER_GROUP)


def _layer(x, p_all, layer, cs, s0_all, past, prev_ckv, lbp, prm, small, *, alpha, prompt):
    bsz, t, _ = x.shape
    n = bsz * t
    tl = _tiles(n, t, prompt)
    x2d = x.reshape(n, D_MODEL)
    tm, tb, cb = tl["tm"], tl["tb"], tl["cb"]
    outs = _inproj(x2d, cs, prm["win"], lbp, small["qg"], prm["wqb"], small["kvg"], prm["wuk"],
                   prev_ckv, tm=tm, tq=tl["tq"], cb=cb, prompt=prompt)
    qs, kin, logf, v, gs, ckv_all, kpe = outs[:7]
    r4 = lambda a: a.reshape(HA_HEADS, bsz, t, HA_DK)
    if prompt:
        qt, vt, kcat, ckvt = outs[7:]
        vt = vt.reshape(bsz, t // tb, HA_HEADS, tb // cb, HA_DV, cb)
    else:
        qabs, qpe = outs[7:]
        vt = jnp.transpose(r4(v), (1, 0, 3, 2)).astype(BF16)
        vt = vt.reshape(bsz, 1, HA_HEADS, 1, HA_DV, t)
    o_raw, s_new = _hgrn(r4(qs), r4(kin), r4(logf), r4(v), vt, s0_all, layer if not prompt else 0,
                         bb=tl["bb"], tb=tb, cb=cb)
    if prompt:
        olat = _attn_prompt(qt, kcat, ckvt, bsz=bsz, t=t, tq=tl["tq"], tk=tl["tk"])
    else:
        olat = _attn_sample(qabs, qpe, past[0], past[1], layer, ckv_all, kpe, ts=t)
    x1g = _merge(x2d, o_raw.reshape(HA_HEADS, n, HA_DV), gs, olat, prm["wuv"], prm["wo"],
                 small["hg"], small["mg"], small["g1"], small["b1"],
                 small["wr"], small["rb"], tm=tm, tq=tl["tq"], alpha=alpha)
    tmoe = tl["tmoe"]
    dst, tile_group, m_sorted = _route(x1g, tm=tmoe)
    xs = _sc_scatter_rows(x1g, dst, m_sorted)
    ys = _moe_sorted(tile_group, xs, prm["wg"], prm["wu"], prm["wd"], layer, tm=tmoe,
                     per_step=tl["experts_per_step"])
    ffn = _sc_gather_rows(ys, dst[:n])
    y = _post(x1g, ffn, p_all.reshape(-1, n, PLE_DIM), layer, small["g2"], small["b2"],
              prm["wpg"], prm["wp"], tm=tm, alpha=alpha)
    return y.reshape(bsz, t, D_MODEL), s_new, ckv_all, kpe.reshape(bsz, t, ROPE_DIM)


def kernel(x_prompt, x_sample, p_prompt, p_sample, state_hgrn, cache_ckv, cache_kpe, w_in,
           lb_logits, hgrn_norm_g, q_norm_g, w_qb, kv_norm_g, w_kvb, mla_norm_g, w_o,
           ln1_g, ln1_b, ln2_g, ln2_b, w_router, router_bias, w_gate, w_up, w_down,
           w_ple, w_ple_gate):
    depth = w_in.shape[0]
    alpha = (2 * depth) ** 0.25
    bp, tp, _ = x_prompt.shape
    bs, ts, _ = x_sample.shape
    past = cache_ckv.shape[2]

    sm = jax.nn.softmax(lb_logits.astype(F32), axis=0)
    lb_all = jnp.maximum(jnp.cumsum(sm, axis=0) - sm[0:1], 0.0)
    lbp_all = jnp.stack([jnp.log(lb_all), jnp.log1p(-lb_all), 1.0 - lb_all], axis=1)

    cs_p = jnp.asarray(_rope_table(np.arange(tp)))
    tm_s = _tiles(bs * ts, ts, False)["tm"]
    cs_s = jnp.asarray(np.tile(_rope_table(past + np.arange(ts)), (tm_s // ts, 1)))
    wr = jnp.pad(w_router, ((0, 0), (0, LANES - N_EXPERTS))).astype(BF16)
    rb = jnp.pad(router_bias.astype(F32), (0, LANES - N_EXPERTS)).reshape(1, LANES)
    s0_p = jnp.zeros((1, bp, HA_HEADS, HA_DK, HA_DV), F32)
    cache_kpe_t = jnp.swapaxes(cache_kpe, 2, 3)

    yp, ys = x_prompt, x_sample
    res = [[] for _ in range(4)]
    cp = cs_ = None
    for l in range(depth):
        prm = _prep_layer(l, w_in, w_qb, w_kvb, w_o, w_gate, w_up, w_down, w_ple, w_ple_gate)
        row = lambda a: a[l].reshape(1, -1).astype(F32)
        small = dict(qg=row(q_norm_g), kvg=row(kv_norm_g), hg=row(hgrn_norm_g), mg=row(mla_norm_g),
                     g1=row(ln1_g), b1=row(ln1_b), g2=row(ln2_g), b2=row(ln2_b), wr=wr, rb=rb)
        yp, sp, cp, kp = _layer(yp, p_prompt, l, cs_p, s0_p, None, cp, lbp_all[l], prm, small,
                                alpha=alpha, prompt=True)
        ys, ss, cs_, ks = _layer(ys, p_sample, l, cs_s, state_hgrn, (cache_ckv, cache_kpe_t), cs_,
                                 lbp_all[l], prm, small, alpha=alpha, prompt=False)
        for lst, a in zip(res, (sp, kp, ss, ks)):
            lst.append(a)
    sp, kp, ss, ks = (jnp.stack(a) for a in res)
    return (yp, ys, sp, cp.reshape(depth, bp, tp, KV_LORA), kp, ss,
            cs_.reshape(depth, bs, ts, KV_LORA), ks)
```

```python
import functools

import jax
import jax.numpy as jnp
import numpy as np
from jax import lax
from jax.experimental import pallas as pl
from jax.experimental.pallas import tpu as pltpu
from jax.experimental.pallas import tpu_sc as plsc

F32 = jnp.float32
BF16 = jnp.bfloat16

D_MODEL = 1024
HA_HEADS = 4
HA_DK = 128
HA_DV = 128
HB_HEADS = 4
Q_LORA = 384
KV_LORA = 256
NOPE_DIM = 128
ROPE_DIM = 64
V_DIM = 128
ROPE_THETA = 10000.0
MLA_SCALE = (NOPE_DIM + ROPE_DIM) ** -0.5
LOG2E = 1.4426950408889634
Q_SCALE = MLA_SCALE * LOG2E
CHUNK = 64
N_EXPERTS = 16
N_GROUPS = 4
EXPERTS_PER_GROUP = N_EXPERTS // N_GROUPS
D_EXPERT = 512
PLE_DIM = 256
NEG_INF = -1e30

HA_W = HA_HEADS * HA_DK
COL_CQ = 4 * HA_W
COL_CKV = COL_CQ + Q_LORA
COL_KPE = COL_CKV + KV_LORA
D_IN_AUG = COL_KPE + 2 * ROPE_DIM
Q_HEAD_AUG = NOPE_DIM + 2 * ROPE_DIM

LANES = 128
SUBLANES = 8
QK_DIM = KV_LORA + LANES
GROUP_LANE = N_EXPERTS
SC_WINDOW = 128
SC_ROW = 256
X_PIECES = D_MODEL // SC_ROW
SUB_BLOCK = 16
VMEM_LIMIT = 56 * 1024 * 1024


def _cparams(sem, vmem=VMEM_LIMIT):
    return pltpu.CompilerParams(dimension_semantics=sem, vmem_limit_bytes=vmem)


def _dot(a, b):
    return jnp.dot(a, b, preferred_element_type=F32)


def _dot_nt(a, b):
    return lax.dot_general(a, b, (((1,), (1,)), ((), ())), preferred_element_type=F32)


def _rms(x, g, eps=1e-6):
    return x * lax.rsqrt(jnp.mean(x * x, axis=-1, keepdims=True) + eps) * g


def _layer_norm(x, g, b, eps=1e-5):
    mu = jnp.mean(x, axis=-1, keepdims=True)
    xc = x - mu
    var = jnp.mean(xc * xc, axis=-1, keepdims=True)
    return xc * lax.rsqrt(var + eps) * g + b


def _silu(x):
    return x * jax.nn.sigmoid(x)


def _full(shape):
    nd = len(shape)
    return pl.BlockSpec(shape, lambda *_: (0,) * nd)


def _inproj_kernel(x_ref, cs_ref, win_ref, lb_ref, qg_ref, wqb_ref, kvg_ref, wuk_ref, prev_ref,
                   qs_ref, kin_ref, logf_ref, v_ref, gs_ref, ckv_ref, kpe_ref, *rest,
                   tq, cb, prompt, n_prev):
    tm = x_ref.shape[0]
    x = x_ref[...].astype(BF16)

    def sect(lo, hi):
        return _dot(x, win_ref[:, lo:hi])

    def put_heads(ref, val):
        for h in range(HA_HEADS):
            ref[h] = val[:, h * HA_DK:(h + 1) * HA_DK]

    cqn = _rms(sect(COL_CQ, COL_CKV), qg_ref[...]).astype(BF16)

    put_heads(qs_ref, _silu(sect(0, HA_W)))
    fa = sect(HA_W, 2 * HA_W)
    log_lb = lb_ref[0:1, :]
    log1m_lb = lb_ref[1:2, :]
    one_m_lb = lb_ref[2:3, :]
    e = jnp.exp(-jnp.abs(fa))
    r = 1.0 / (1.0 + e)
    c = log1m_lb + jnp.minimum(fa, 0.0) + jnp.log(r)
    put_heads(logf_ref, jnp.maximum(log_lb, c) + jnp.log(1.0 + jnp.exp(-jnp.abs(log_lb - c))))
    put_heads(kin_ref, one_m_lb * jnp.where(fa >= 0.0, e * r, r))
    v = sect(2 * HA_W, 3 * HA_W)
    put_heads(v_ref, v)
    gs_ref[...] = _silu(sect(3 * HA_W, 4 * HA_W))

    cs = cs_ref[...]

    def rope(t):
        prod = t * cs
        return prod + pltpu.roll(prod, ROPE_DIM, 1)

    ckv = _rms(sect(COL_CKV, COL_KPE), kvg_ref[...])
    for l in range(n_prev):
        ckv_ref[l] = prev_ref[l]
    ckv_ref[n_prev] = ckv
    kpe2 = rope(sect(COL_KPE, D_IN_AUG))
    kpe_ref[...] = kpe2[:, :ROPE_DIM]
    if prompt:
        qt_ref, vt_ref, kcat_ref, ckvt_ref = rest
        for h in range(HA_HEADS):
            for ci in range(tm // cb):
                vt_ref[0, h, ci] = v[ci * cb:(ci + 1) * cb, h * HA_DV:(h + 1) * HA_DV].T.astype(BF16)
        kcat_ref[:, :KV_LORA] = ckv.astype(BF16)
        kcat_ref[:, KV_LORA:] = kpe2.astype(BF16)
        ckvt_ref[0] = ckv.T.astype(BF16)
    else:
        qabs_ref, qpe_ref = rest

    lane = lax.broadcasted_iota(jnp.int32, (tm, LANES), 1)
    qhs = [_dot(cqn, wqb_ref[:, h * Q_HEAD_AUG:(h + 1) * Q_HEAD_AUG]) for h in range(HB_HEADS)]
    qabss = [_dot(qhs[h][:, :NOPE_DIM].astype(BF16), wuk_ref[h]) for h in range(HB_HEADS)]
    for h in range(HB_HEADS):
        qabs = qabss[h] * Q_SCALE
        qpe = jnp.where(lane < ROPE_DIM, rope(qhs[h][:, NOPE_DIM:]) * Q_SCALE, 0.0)
        if prompt:
            qabs_t = qabs.T.astype(BF16)
            qpe_t = qpe.T.astype(BF16)
            for s in range(tm // tq):
                qt_ref[s, :KV_LORA, h * tq:(h + 1) * tq] = qabs_t[:, s * tq:(s + 1) * tq]
                qt_ref[s, KV_LORA:, h * tq:(h + 1) * tq] = qpe_t[:, s * tq:(s + 1) * tq]
        else:
            for s in range(tm // tq):
                qabs_ref[s, h] = qabs[s * tq:(s + 1) * tq].astype(BF16)
                qpe_ref[s, h] = qpe[s * tq:(s + 1) * tq].astype(BF16)


def _inproj(x, cs, w_in, lbp, qg, wqb, kvg, wuk, prev_ckv, *, tm, tq, cb, prompt):
    n = x.shape[0]
    nt = n // tm
    n_prev = 0 if prev_ckv is None else prev_ckv.shape[0]
    if prev_ckv is None:
        prev_ckv = jnp.zeros((1, tm, KV_LORA), F32)
        prev_spec = _full(prev_ckv.shape)
    else:
        prev_spec = pl.BlockSpec((n_prev, tm, KV_LORA), lambda i: (0, i, 0))
    row = lambda w: pl.BlockSpec((tm, w), lambda i: (i, 0))
    heads = pl.BlockSpec((HA_HEADS, tm, HA_DK), lambda i: (0, i, 0))
    qblk = lambda w: pl.BlockSpec((tm // tq, HB_HEADS, tq, w), lambda i: (i, 0, 0, 0))
    head_major = jax.ShapeDtypeStruct((HA_HEADS, n, HA_DK), F32)
    out_shape = [
        head_major,
        head_major,
        head_major,
        head_major,
        jax.ShapeDtypeStruct((n, HA_W), F32),
        jax.ShapeDtypeStruct((n_prev + 1, n, KV_LORA), F32),
        jax.ShapeDtypeStruct((n, ROPE_DIM), F32),
    ]
    out_specs = [heads, heads, heads, heads, row(HA_W),
                 pl.BlockSpec((n_prev + 1, tm, KV_LORA), lambda i: (0, i, 0)), row(ROPE_DIM)]
    if prompt:
        out_shape += [jax.ShapeDtypeStruct((n // tq, QK_DIM, HB_HEADS * tq), BF16),
                      jax.ShapeDtypeStruct((nt, HA_HEADS, tm // cb, HA_DV, cb), BF16),
                      jax.ShapeDtypeStruct((n, QK_DIM), BF16),
                      jax.ShapeDtypeStruct((nt, KV_LORA, tm), BF16)]
        out_specs += [pl.BlockSpec((tm // tq, QK_DIM, HB_HEADS * tq), lambda i: (i, 0, 0)),
                      pl.BlockSpec((1, HA_HEADS, tm // cb, HA_DV, cb), lambda i: (i, 0, 0, 0, 0)),
                      row(QK_DIM),
                      pl.BlockSpec((1, KV_LORA, tm), lambda i: (i, 0, 0))]
    else:
        out_shape += [jax.ShapeDtypeStruct((n // tq, HB_HEADS, tq, KV_LORA), BF16),
                      jax.ShapeDtypeStruct((n // tq, HB_HEADS, tq, LANES), BF16)]
        out_specs += [qblk(KV_LORA), qblk(LANES)]
    return pl.pallas_call(
        functools.partial(_inproj_kernel, tq=tq, cb=cb, prompt=prompt, n_prev=n_prev),
        grid=(nt,),
        in_specs=[row(D_MODEL),
                  pl.BlockSpec((tm, LANES), lambda i: (i % (cs.shape[0] // tm), 0)),
                  _full(w_in.shape), _full(lbp.shape), _full(qg.shape),
                  _full(wqb.shape), _full(kvg.shape), _full(wuk.shape), prev_spec],
        out_specs=out_specs,
        out_shape=out_shape,
        compiler_params=_cparams(("parallel",)),
        name="inproj",
    )(x, cs, w_in, lbp, qg, wqb, kvg, wuk, prev_ckv)


def _hgrn_kernel(q_ref, k_ref, g_ref, v_ref, vt_ref, s0_ref, o_ref, sfin_ref, st_scr, b2_scr,
                 hide_ref, *, cb):
    _, bb, tb, _ = q_ref.shape
    nsub = cb // SUB_BLOCK
    ti = pl.program_id(1)

    @pl.when(ti == 0)
    def _():
        for b in range(bb):
            for h in range(HA_HEADS):
                st_scr[b, h] = s0_ref[b, h].T
        row8 = lax.broadcasted_iota(jnp.int32, (SUBLANES, LANES), 0)
        for s in range(SUBLANES):
            hide_ref[s] = jnp.where(row8 >= s, 0.0, NEG_INF)

    r_i = lax.broadcasted_iota(jnp.int32, (cb, cb), 0)
    c_i = lax.broadcasted_iota(jnp.int32, (cb, cb), 1)
    tril = (r_i >= c_i).astype(F32)

    chains = [(b, h) for b in range(bb) for h in range(HA_HEADS)]

    def matmul_part(ci):
        r0 = pl.multiple_of(ci * cb, cb)
        rows = pl.ds(r0, cb)

        def sub(ref, b, h, lo, size):
            return ref[h, b, pl.ds(pl.multiple_of(r0 + lo, SUB_BLOCK), size), :]

        for b, h in chains:
            bcum = jnp.dot(tril, g_ref[h, b, rows, :], precision=lax.Precision.HIGHEST,
                           preferred_element_type=F32)
            b2_scr[b * HA_HEADS + h] = bcum * LOG2E
        for b, h in chains:
            b2 = b2_scr[b * HA_HEADS + h]
            o_ref[h, b, rows, :] = _dot_nt((q_ref[h, b, rows, :] * jnp.exp2(b2)).astype(BF16),
                                           st_scr[b, h].astype(BF16))
        for j in range(nsub - 1):
            lo, hi = j * SUB_BLOCK, (j + 1) * SUB_BLOCK
            below = pl.ds(pl.multiple_of(r0 + hi, SUB_BLOCK), cb - hi)
            for b, h in chains:
                slot = b * HA_HEADS + h
                bnd = b2_scr[slot, hi - 1:hi, :]
                kd = (sub(k_ref, b, h, lo, SUB_BLOCK)
                      * jnp.exp2(bnd - b2_scr[slot, lo:hi, :])).astype(BF16)
                qe = (sub(q_ref, b, h, hi, cb - hi)
                      * jnp.exp2(b2_scr[slot, hi:, :] - bnd)).astype(BF16)
                a = _dot_nt(qe, kd).astype(BF16)
                o_ref[h, b, below, :] += _dot(a, sub(v_ref, b, h, lo, SUB_BLOCK).astype(BF16))
        for b, h in chains:
            slot = b * HA_HEADS + h
            b_last = b2_scr[slot, cb - 1:cb, :]
            kd_all = (k_ref[h, b, rows, :] * jnp.exp2(b_last - b2_scr[slot])).astype(BF16)
            st_scr[b, h] = st_scr[b, h] * jnp.exp2(b_last) + _dot(vt_ref[b, h, ci], kd_all)

    def pairwise_part(ci, b, h, j):
        slot = b * HA_HEADS + h
        r0 = pl.multiple_of(ci * cb, cb)
        lo = j * SUB_BLOCK
        mid = lo + SUBLANES
        top = pl.ds(pl.multiple_of(r0 + lo, SUBLANES), SUBLANES)
        bot = pl.ds(pl.multiple_of(r0 + mid, SUBLANES), SUBLANES)
        q_top, q_bot = q_ref[h, b, top, :], q_ref[h, b, bot, :]
        b_top, b_bot = b2_scr[slot, lo:mid, :], b2_scr[slot, mid:mid + SUBLANES, :]
        acc_top, acc_bot = o_ref[h, b, top, :], o_ref[h, b, bot, :]
        for s in range(SUB_BLOCK):
            row = pl.ds(r0 + lo + s, 1)
            bs = b2_scr[slot, lo + s:lo + s + 1, :]
            ks = k_ref[h, b, row, :]
            vs = v_ref[h, b, row, :]
            if s < SUBLANES:
                w = q_top * jnp.exp2(b_top - bs + hide_ref[s]) * ks
                acc_top = acc_top + jnp.sum(w, axis=-1, keepdims=True) * vs
                w = q_bot * jnp.exp2(b_bot - bs) * ks
            else:
                w = q_bot * jnp.exp2(b_bot - bs + hide_ref[s - SUBLANES]) * ks
            acc_bot = acc_bot + jnp.sum(w, axis=-1, keepdims=True) * vs
        o_ref[h, b, top, :] = acc_top
        o_ref[h, b, bot, :] = acc_bot

    def chunk(ci, carry):
        matmul_part(ci)
        for j in range(nsub):
            for b, h in chains:
                pairwise_part(ci, b, h, j)
        return carry

    lax.fori_loop(0, tb // cb, chunk, 0)

    @pl.when(ti == pl.num_programs(1) - 1)
    def _():
        for b in range(bb):
            for h in range(HA_HEADS):
                sfin_ref[b, h] = st_scr[b, h].T


def _hgrn(qs, kin, logf, v, vt, s0_all, layer, *, bb, tb, cb):
    _, bsz, t, _ = qs.shape
    blk = pl.BlockSpec((HA_HEADS, bb, tb, HA_DK), lambda i, j: (0, i, j, 0))
    st_blk = pl.BlockSpec((bb, HA_HEADS, HA_DK, HA_DV), lambda i, j: (i, 0, 0, 0))
    s0_blk = pl.BlockSpec((None, bb, HA_HEADS, HA_DK, HA_DV), lambda i, j: (layer, i, 0, 0, 0))
    vt_blk = pl.BlockSpec((bb, None, HA_HEADS, tb // cb, HA_DV, cb),
                          lambda i, j: (i, j, 0, 0, 0, 0))
    return pl.pallas_call(
        functools.partial(_hgrn_kernel, cb=cb),
        grid=(bsz // bb, t // tb),
        in_specs=[blk, blk, blk, blk, vt_blk, s0_blk],
        out_specs=[blk, st_blk],
        out_shape=[jax.ShapeDtypeStruct(qs.shape, F32),
                   jax.ShapeDtypeStruct(s0_all.shape[1:], F32)],
        scratch_shapes=[pltpu.VMEM((bb, HA_HEADS, HA_DV, HA_DK), F32),
                        pltpu.VMEM((bb * HA_HEADS, cb, HA_DK), F32),
                        pltpu.VMEM((SUBLANES, SUBLANES, LANES), F32)],
        compiler_params=_cparams(("parallel", "arbitrary")),
        name="hgrn",
    )(qs, kin, logf, v, vt, s0_all)


def _attn_prompt_kernel(qt_ref, kc_ref, kt_ref, o_ref, s0, s1, p0, p1, a0, a1, m_scr, l_scr,
                        acc_scr, *, tq, tk):
    qi = pl.program_id(1)
    jd = (qi * tq) // tk
    shift = CHUNK.bit_length() - 1

    heads = range(HB_HEADS)

    def scores(j, s_dst, hs=heads):
        keys = kc_ref[pl.ds(pl.multiple_of(j * tk, tk), tk), :]
        for h in hs:
            cols = slice(h * tq, (h + 1) * tq)
            s_dst[:, cols] = _dot(keys, qt_ref[0, :, cols])

    def softmax(j, s_src, p_dst, a_dst, masked, hs=heads):
        if masked:
            kpos = j * tk + lax.broadcasted_iota(jnp.int32, (tk, 1), 0)
            qpos = qi * tq + lax.broadcasted_iota(jnp.int32, (1, tq), 1)
            visible = (kpos >> shift) <= (qpos >> shift)
        for h in hs:
            cols = slice(h * tq, (h + 1) * tq)
            def load():
                s = s_src[:, cols]
                return jnp.where(visible, s, NEG_INF) if masked else s

            m_prev = m_scr[:, cols]
            m_new = jnp.maximum(m_prev, jnp.max(load(), axis=0, keepdims=True))
            alpha = jnp.exp2(m_prev - m_new)
            p = jnp.exp2(load() - m_new)
            l_scr[:, cols] = alpha * l_scr[:, cols] + jnp.sum(p, axis=0, keepdims=True)
            m_scr[:, cols] = m_new
            a_dst[:, cols] = alpha
            p_dst[:, cols] = p.astype(BF16)

    def values(j, p_src, a_src, hs=heads):
        for h in hs:
            cols = slice(h * tq, (h + 1) * tq)
            acc_scr[:, cols] = (a_src[:, cols] * acc_scr[:, cols]
                                + _dot(kt_ref[j], p_src[:, cols]))

    def trip(j, s_cur, s_nxt, p_cur, p_prv, a_cur, a_prv):
        scores(j + 1, s_nxt)
        softmax(j, s_cur, p_cur, a_cur, False)
        values(jnp.maximum(j - 1, 0), p_prv, a_prv)

    m_scr[...] = jnp.full(m_scr.shape, NEG_INF, F32)
    l_scr[...] = jnp.zeros(l_scr.shape, F32)
    acc_scr[...] = jnp.zeros(acc_scr.shape, F32)
    p1[...] = jnp.zeros(p1.shape, BF16)
    a1[...] = jnp.ones(a1.shape, F32)
    scores(0, s0)

    def pair(jj, carry):
        trip(2 * jj, s0, s1, p0, p1, a0, a1)
        trip(2 * jj + 1, s1, s0, p1, p0, a1, a0)
        return carry

    lax.fori_loop(0, jd // 2, pair, 0)

    @pl.when(jd % 2 == 1)
    def _():
        trip(jd - 1, s0, s1, p0, p1, a0, a1)

    def drain(s_cur, p_cur, p_prv, a_cur, a_prv):
        softmax(jd, s_cur, p_cur, a_cur, True)
        values(jnp.maximum(jd - 1, 0), p_prv, a_prv)
        values(jd, p_cur, a_cur)

    pl.when(jd % 2 == 0)(functools.partial(drain, s0, p0, p1, a0, a1))
    pl.when(jd % 2 == 1)(functools.partial(drain, s1, p1, p0, a1, a0))

    for h in range(HB_HEADS):
        cols = slice(h * tq, (h + 1) * tq)
        o_ref[0, h] = (acc_scr[:, cols] * (1.0 / l_scr[:, cols])).T.astype(BF16)


def _attn_prompt(qt, kcat, ckvt, *, bsz, t, tq, tk):
    nq = t // tq
    nk = t // tk
    cols = HB_HEADS * tq
    single = pl.Buffered(1)
    return pl.pallas_call(
        functools.partial(_attn_prompt_kernel, tq=tq, tk=tk),
        grid=(bsz, nq),
        in_specs=[
            pl.BlockSpec((1, QK_DIM, cols), lambda b, i: (b * nq + i, 0, 0)),
            pl.BlockSpec((t, QK_DIM), lambda b, i: (b, 0), pipeline_mode=single),
            pl.BlockSpec((nk, KV_LORA, tk), lambda b, i: (b, 0, 0), pipeline_mode=single),
        ],
        out_specs=pl.BlockSpec((1, HB_HEADS, tq, KV_LORA), lambda b, i: (b * nq + i, 0, 0, 0)),
        out_shape=jax.ShapeDtypeStruct((bsz * nq, HB_HEADS, tq, KV_LORA), BF16),
        scratch_shapes=[pltpu.VMEM((tk, cols), F32), pltpu.VMEM((tk, cols), F32),
                        pltpu.VMEM((tk, cols), BF16), pltpu.VMEM((tk, cols), BF16),
                        pltpu.VMEM((1, cols), F32), pltpu.VMEM((1, cols), F32),
                        pltpu.VMEM((1, cols), F32), pltpu.VMEM((1, cols), F32),
                        pltpu.VMEM((KV_LORA, cols), F32)],
        compiler_params=_cparams(("parallel", "arbitrary")),
        name="attn_prompt",
    )(qt, kcat, ckvt)


def _attn_sample_kernel(qa_ref, qp_ref, pckv_ref, pkpe_ref, nckv_ref, nkpe_ref, o_ref):
    _, _, ts, _ = qa_ref.shape
    rows = HB_HEADS * ts
    qa = qa_ref[0].reshape(rows, KV_LORA)
    qp = qp_ref[0].reshape(rows, LANES)[:, :ROPE_DIM]
    pckv = pckv_ref[0].astype(BF16)
    nckv = nckv_ref[...].astype(BF16)
    s_past = _dot_nt(qa, pckv) + _dot(qp, pkpe_ref[0].astype(BF16))
    s_new = _dot_nt(qa, nckv) + _dot_nt(qp, nkpe_ref[...].astype(BF16))
    m = jnp.maximum(jnp.max(s_past, axis=-1, keepdims=True), jnp.max(s_new, axis=-1, keepdims=True))
    p_past = jnp.exp2(s_past - m)
    p_new = jnp.exp2(s_new - m)
    l = jnp.sum(p_past, axis=-1, keepdims=True) + jnp.sum(p_new, axis=-1, keepdims=True)
    o = (_dot(p_past.astype(BF16), pckv) + _dot(p_new.astype(BF16), nckv)) / l
    o_ref[0] = o.astype(BF16).reshape(HB_HEADS, ts, KV_LORA)


def _attn_sample(qabs, qpe, past_ckv, past_kpe, layer, ckv, kpe, *, ts):
    _, bsz, past, _ = past_ckv.shape
    return pl.pallas_call(
        _attn_sample_kernel,
        grid=(bsz,),
        in_specs=[
            pl.BlockSpec((1, HB_HEADS, ts, KV_LORA), lambda b: (b, 0, 0, 0)),
            pl.BlockSpec((1, HB_HEADS, ts, LANES), lambda b: (b, 0, 0, 0)),
            pl.BlockSpec((None, 1, past, KV_LORA), lambda b: (layer, b, 0, 0)),
            pl.BlockSpec((None, 1, ROPE_DIM, past), lambda b: (layer, b, 0, 0)),
            pl.BlockSpec((None, ts, KV_LORA), lambda b: (layer, b, 0)),
            pl.BlockSpec((ts, ROPE_DIM), lambda b: (b, 0)),
        ],
        out_specs=pl.BlockSpec((1, HB_HEADS, ts, KV_LORA), lambda b: (b, 0, 0, 0)),
        out_shape=jax.ShapeDtypeStruct(qabs.shape, BF16),
        compiler_params=_cparams(("parallel",)),
        name="attn_sample",
    )(qabs, qpe, past_ckv, past_kpe, ckv, kpe)


def _gates(logits, bias):
    tm = logits.shape[0]
    lane = lax.broadcasted_iota(jnp.int32, (tm, LANES), 1)
    pos = lane % EXPERTS_PER_GROUP
    valid = lane < N_EXPERTS
    scores = jax.nn.sigmoid(logits)
    sel = jnp.where(valid, scores + bias, -jnp.inf)

    others = []
    for r in range(1, EXPERTS_PER_GROUP):
        others.append(jnp.where(pos >= r, pltpu.roll(sel, r, 1),
                                pltpu.roll(sel, LANES - (EXPERTS_PER_GROUP - r), 1)))
    a, b, c, d = sel, others[0], others[1], others[2]
    hi1, lo1 = jnp.maximum(a, b), jnp.minimum(a, b)
    hi2, lo2 = jnp.maximum(c, d), jnp.minimum(c, d)
    gscore = jnp.maximum(hi1, hi2) + jnp.maximum(jnp.minimum(hi1, hi2), jnp.maximum(lo1, lo2))
    gmax = jnp.max(gscore, axis=-1, keepdims=True)
    group = (lane // EXPERTS_PER_GROUP).astype(F32)
    gidx = jnp.min(jnp.where(gscore == gmax, group, float(LANES)), axis=-1, keepdims=True)

    rank = jnp.zeros((tm, LANES), jnp.int32)
    for r, o in enumerate(others, start=1):
        ahead = (o > sel) | ((o == sel) & (pos >= r))
        rank = rank + ahead.astype(jnp.int32)
    chosen = (group == gidx) & (rank < 2) & valid
    w = jnp.where(chosen, scores, 0.0)
    w = w / jnp.sum(w, axis=-1, keepdims=True)
    return jnp.where(lane == GROUP_LANE, gidx, w)


def _merge_kernel(x_ref, o_ref, gs_ref, olat_ref, wuv_ref, wo_ref, hg_ref, mg_ref,
                  g1_ref, b1_ref, wr_ref, rb_ref, x1g_ref, *, alpha):
    _, _, tq, _ = olat_ref.shape
    tm = x_ref.shape[0]
    hg = hg_ref[...]
    d_a = HA_HEADS * HA_DV
    halves = [(i * tm // 2, (i + 1) * tm // 2) for i in range(2)]

    def lat(h, lo, hi):
        if tq >= hi - lo:
            return olat_ref[lo // tq, h, lo % tq:lo % tq + hi - lo]
        return jnp.concatenate([olat_ref[s, h] for s in range(lo // tq, hi // tq)], axis=0)

    ob = [[_dot(lat(h, lo, hi), wuv_ref[h]) for h in range(HB_HEADS)] for lo, hi in halves]
    oa = [jnp.concatenate([_rms(o_ref[h, lo:hi, :], hg) * gs_ref[lo:hi, h * HA_DV:(h + 1) * HA_DV]
                           for h in range(HA_HEADS)], axis=-1) for lo, hi in halves]
    ob = [_rms(jnp.concatenate(o, axis=-1), mg_ref[...]) for o in ob]
    mix = [_dot(a.astype(BF16), wo_ref[:d_a, :]) + _dot(o.astype(BF16), wo_ref[d_a:, :])
           for a, o in zip(oa, ob)]
    x1 = [_layer_norm(alpha * x_ref[lo:hi, :] + m, g1_ref[...], b1_ref[...])
          for (lo, hi), m in zip(halves, mix)]
    logits = [_dot(x.astype(BF16), wr_ref[...]) for x in x1]
    for (lo, hi), x, lg in zip(halves, x1, logits):
        for c in range(X_PIECES):
            x1g_ref[c, lo:hi, :] = x[:, c * SC_ROW:(c + 1) * SC_ROW]
        x1g_ref[X_PIECES, lo:hi, :LANES] = _gates(lg, rb_ref[...])
        x1g_ref[X_PIECES, lo:hi, LANES:] = jnp.zeros((hi - lo, SC_ROW - LANES), F32)


def _merge(x, o_raw, gs, olat, wuv, wo, hg, mg, g1, b1, wr, rb, *, tm, tq, alpha):
    n = x.shape[0]
    row = lambda w: pl.BlockSpec((tm, w), lambda i: (i, 0))
    return pl.pallas_call(
        functools.partial(_merge_kernel, alpha=alpha),
        grid=(n // tm,),
        in_specs=[row(D_MODEL), pl.BlockSpec((HA_HEADS, tm, HA_DV), lambda i: (0, i, 0)), row(HA_W),
                  pl.BlockSpec((tm // tq, HB_HEADS, tq, KV_LORA), lambda i: (i, 0, 0, 0)),
                  _full(wuv.shape), _full(wo.shape), _full(hg.shape), _full(mg.shape),
                  _full(g1.shape), _full(b1.shape), _full(wr.shape), _full(rb.shape)],
        out_specs=pl.BlockSpec((X_PIECES + 1, tm, SC_ROW), lambda i: (0, i, 0)),
        out_shape=jax.ShapeDtypeStruct((X_PIECES + 1, n, SC_ROW), F32),
        compiler_params=_cparams(("parallel",)),
        name="merge",
    )(x, o_raw, gs, olat, wuv, wo, hg, mg, g1, b1, wr, rb)


def _sc_scatter_rows(src, dst, m):
    parts, n, _ = src.shape
    windows = dst.shape[0] // SC_WINDOW
    src_windows = n // SC_WINDOW
    pieces = src.reshape(parts * n, SC_ROW)
    piece_dst = (jnp.arange(parts, dtype=jnp.int32)[:, None] * m + dst[None, :]).reshape(1, -1)
    mesh = plsc.VectorSubcoreMesh(core_axis_name="core", subcore_axis_name="subcore")

    @functools.partial(pl.kernel, out_type=jax.ShapeDtypeStruct((parts * m, SC_ROW), src.dtype),
                       mesh=mesh)
    def scatter(x_hbm, i_hbm, o_hbm):
        def body(x_vmem, i_vmem):
            pltpu.sync_copy(x_vmem, o_hbm.at[i_vmem.at[0]])

        pltpu.emit_pipeline(
            body,
            grid=(parts * windows,),
            in_specs=[pl.BlockSpec((SC_WINDOW, SC_ROW),
                                   lambda i: ((i // windows) * src_windows
                                              + (i % windows) % src_windows, 0)),
                      pl.BlockSpec((1, SC_WINDOW), lambda i: (0, i))],
            out_specs=[],
            core_axis_name=("core", "subcore"),
            dimension_semantics=(pltpu.PARALLEL,),
        )(x_hbm, i_hbm)

    return scatter(pieces, piece_dst).reshape(parts, m, SC_ROW)


def _sc_gather_rows(src, idx):
    parts, n, _ = src.shape
    m = idx.shape[0] * parts
    pieces = src.reshape(parts * n, SC_ROW)
    piece_idx = (jnp.arange(parts, dtype=jnp.int32)[:, None] * n + idx[None, :]).reshape(1, m)
    mesh = plsc.VectorSubcoreMesh(core_axis_name="core", subcore_axis_name="subcore")

    @functools.partial(pl.kernel, out_type=jax.ShapeDtypeStruct((m, SC_ROW), src.dtype), mesh=mesh)
    def gather(x_hbm, i_hbm, o_hbm):
        def body(i_vmem, o_vmem):
            pltpu.sync_copy(x_hbm.at[i_vmem.at[0]], o_vmem)

        pltpu.emit_pipeline(
            body,
            grid=(m // SC_WINDOW,),
            in_specs=[pl.BlockSpec((1, SC_WINDOW), lambda i: (0, i))],
            out_specs=[pl.BlockSpec((SC_WINDOW, SC_ROW), lambda i: (i, 0))],
            core_axis_name=("core", "subcore"),
            dimension_semantics=(pltpu.PARALLEL,),
        )(i_hbm, o_hbm)

    return gather(pieces, piece_idx).reshape(parts, idx.shape[0], SC_ROW)


def _moe_kernel(tg_ref, xs_ref, wg_ref, wu_ref, wd_ref, y_ref, acc_scr):
    tm = xs_ref.shape[1]
    per_step = wg_ref.shape[0]
    step = pl.program_id(1)
    first_expert = tg_ref[pl.program_id(0)] * EXPERTS_PER_GROUP + step * per_step
    xb = jnp.concatenate([xs_ref[c] for c in range(X_PIECES)], axis=1).astype(BF16)
    gates = xs_ref[X_PIECES, :, :LANES]
    lane = lax.broadcasted_iota(jnp.int32, (tm, LANES), 1)
    acc = jnp.zeros((tm, D_MODEL), F32)
    for k in range(per_step):
        hmid = (_silu(_dot(xb, wg_ref[k].astype(BF16))) * _dot(xb, wu_ref[k].astype(BF16)))
        gcol = jnp.sum(jnp.where(lane == first_expert + k, gates, 0.0), axis=-1, keepdims=True)
        acc = acc + gcol * _dot(hmid.astype(BF16), wd_ref[k].astype(BF16))

    @pl.when(step == 0)
    def _():
        acc_scr[...] = acc

    @pl.when(step > 0)
    def _():
        acc_scr[...] += acc

    @pl.when(step == pl.num_programs(1) - 1)
    def _():
        for c in range(D_MODEL // SC_ROW):
            y_ref[c] = acc_scr[:, c * SC_ROW:(c + 1) * SC_ROW]


def _moe_sorted(tile_group, xs, wg, wu, wd, layer, *, tm, per_step, weight_buffers):
    m = xs.shape[1]
    steps = EXPERTS_PER_GROUP // per_step
    mode = dict(pipeline_mode=pl.Buffered(weight_buffers))
    wspec = lambda shape: pl.BlockSpec((None, per_step) + shape,
                                       lambda i, k, tg: (layer, tg[i] * steps + k, 0, 0), **mode)
    return pl.pallas_call(
        _moe_kernel,
        grid_spec=pltpu.PrefetchScalarGridSpec(
            num_scalar_prefetch=1,
            grid=(m // tm, steps),
            in_specs=[pl.BlockSpec((X_PIECES + 1, tm, SC_ROW), lambda i, k, tg: (0, i, 0)),
                      wspec((D_MODEL, D_EXPERT)), wspec((D_MODEL, D_EXPERT)),
                      wspec((D_EXPERT, D_MODEL))],
            out_specs=pl.BlockSpec((D_MODEL // SC_ROW, tm, SC_ROW), lambda i, k, tg: (0, i, 0)),
            scratch_shapes=[pltpu.VMEM((tm, D_MODEL), F32)],
        ),
        out_shape=jax.ShapeDtypeStruct((D_MODEL // SC_ROW, m, SC_ROW), F32),
        compiler_params=_cparams(("arbitrary", "arbitrary")),
        name="moe",
    )(tile_group, xs, wg, wu, wd)


def _route(x1g, *, tm):
    n = x1g.shape[1]
    n_tiles = n // tm + N_GROUPS - 1
    m = n_tiles * tm
    gidx = x1g[X_PIECES, :, GROUP_LANE].astype(jnp.int32)
    onehot = (gidx[:, None] == jnp.arange(N_GROUPS, dtype=jnp.int32)[None, :]).astype(jnp.int32)
    counts = jnp.sum(onehot, axis=0)
    tiles = (counts + tm - 1) // tm
    tile_end = jnp.cumsum(tiles)
    tile_start = tile_end - tiles
    rank = jnp.sum((jnp.cumsum(onehot, axis=0) - onehot) * onehot, axis=1)
    pos = jnp.take(tile_start, gidx) * tm + rank
    pad_start = tile_start * tm + counts
    pad_end = (tile_end * tm).at[N_GROUPS - 1].set(m)
    pad_cum = jnp.cumsum(pad_end - pad_start)
    k = jnp.arange(m - n, dtype=jnp.int32)
    pg = jnp.searchsorted(pad_cum, k, side="right")
    pad_rows = jnp.take(pad_start, pg) + k - jnp.take(pad_cum - (pad_end - pad_start), pg)
    dst = jnp.concatenate([pos, pad_rows]).astype(jnp.int32)
    tile_group = jnp.searchsorted(tile_end, jnp.arange(n_tiles, dtype=jnp.int32), side="right")
    return dst, jnp.minimum(tile_group, N_GROUPS - 1).astype(jnp.int32), m


def _post_kernel(x1_ref, f_ref, p_ref, g2_ref, b2_ref, wpg_ref, wp_ref, y_ref, *, alpha):
    tm = x1_ref.shape[1]
    halves = [(i * tm // 2, (i + 1) * tm // 2) for i in range(2)]
    emb = [_dot(p_ref[lo:hi, :].astype(BF16), wp_ref[...]) for lo, hi in halves]

    def rows(ref, lo, hi):
        return jnp.concatenate([ref[c, lo:hi, :] for c in range(X_PIECES)], axis=1)

    x2 = [_layer_norm(alpha * rows(x1_ref, lo, hi) + rows(f_ref, lo, hi), g2_ref[...], b2_ref[...])
          for lo, hi in halves]
    gate = [_dot(x.astype(BF16), wpg_ref[...]) for x in x2]
    for (lo, hi), x, g, e in zip(halves, x2, gate, emb):
        y_ref[lo:hi, :] = x + jax.nn.sigmoid(g) * e


def _post(x1g, ffn, p_all, layer, g2, b2, wpg, wp, *, tm, alpha):
    _, n, _ = ffn.shape
    row = lambda w: pl.BlockSpec((tm, w), lambda i: (i, 0))
    pieces = lambda a: pl.BlockSpec((a.shape[0], tm, SC_ROW), lambda i: (0, i, 0))
    return pl.pallas_call(
        functools.partial(_post_kernel, alpha=alpha),
        grid=(n // tm,),
        in_specs=[pieces(x1g), pieces(ffn),
                  pl.BlockSpec((None, tm, PLE_DIM), lambda i: (layer, i, 0)),
                  _full(g2.shape), _full(b2.shape), _full(wpg.shape), _full(wp.shape)],
        out_specs=row(D_MODEL),
        out_shape=jax.ShapeDtypeStruct((n, D_MODEL), F32),
        compiler_params=_cparams(("parallel",)),
        name="post",
    )(x1g, ffn, p_all, g2, b2, wpg, wp)


def _rot_cols(w):
    half = ROPE_DIM // 2
    return jnp.concatenate([-w[..., half:], w[..., :half]], axis=-1)


def _prep_layer(l, w_in, w_qb, w_kvb, w_o, w_gate, w_up, w_down, w_ple, w_ple_gate):
    win = w_in[l]
    win_aug = jnp.concatenate([win, _rot_cols(win[:, COL_KPE:])], axis=-1).astype(BF16)
    wqb = w_qb[l].reshape(Q_LORA, HB_HEADS, NOPE_DIM + ROPE_DIM)
    wqb_aug = jnp.concatenate([wqb, _rot_cols(wqb[..., NOPE_DIM:])], axis=-1)
    wqb_aug = wqb_aug.reshape(Q_LORA, HB_HEADS * Q_HEAD_AUG).astype(BF16)
    wkvb = w_kvb[l].reshape(KV_LORA, HB_HEADS, NOPE_DIM + V_DIM)
    wuk_t = jnp.transpose(wkvb[..., :NOPE_DIM], (1, 2, 0)).astype(BF16)
    wuv = jnp.transpose(wkvb[..., NOPE_DIM:], (1, 0, 2)).astype(BF16)
    return dict(win=win_aug, wqb=wqb_aug, wuk=wuk_t, wuv=wuv, wo=w_o[l].astype(BF16),
                wg=w_gate, wu=w_up, wd=w_down,
                wp=w_ple[l].astype(BF16), wpg=w_ple_gate[l].astype(BF16))


def _rope_table(pos):
    inv = ROPE_THETA ** (-np.arange(0, ROPE_DIM, 2, dtype=np.float64) / ROPE_DIM)
    ang = pos.astype(np.float64)[:, None] * inv[None, :]
    cos, sin = np.cos(ang), np.sin(ang)
    return np.concatenate([cos, cos, sin, sin], axis=-1).astype(np.float32)


def _tiles(n, t, prompt):
    if prompt:
        tm = min(512, t)
        return dict(tm=tm, tq=tm, tk=tm, tb=tm, cb=min(128, t), bb=2, tmoe=min(512, n),
                    experts_per_step=EXPERTS_PER_GROUP, weight_buffers=1)
    return dict(tm=min(512, n), tq=t, tk=None, tb=t, cb=t, bb=2, tmoe=min(256, n),
                experts_per_step=EXPERTS_PER_GROUP, weight_buffers=2)


def _layer(x, p_all, layer, cs, s0_all, past, prev_ckv, lbp, prm, small, *, alpha, prompt):
    bsz, t, _ = x.shape
    n = bsz * t
    tl = _tiles(n, t, prompt)
    x2d = x.reshape(n, D_MODEL)
    tm, tb, cb = tl["tm"], tl["tb"], tl["cb"]
    outs = _inproj(x2d, cs, prm["win"], lbp, small["qg"], prm["wqb"], small["kvg"], prm["wuk"],
                   prev_ckv, tm=tm, tq=tl["tq"], cb=cb, prompt=prompt)
    qs, kin, logf, v, gs, ckv_all, kpe = outs[:7]
    r4 = lambda a: a.reshape(HA_HEADS, bsz, t, HA_DK)
    if prompt:
        qt, vt, kcat, ckvt = outs[7:]
        vt = vt.reshape(bsz, t // tb, HA_HEADS, tb // cb, HA_DV, cb)
    else:
        qabs, qpe = outs[7:]
        vt = jnp.transpose(r4(v), (1, 0, 3, 2)).astype(BF16)
        vt = vt.reshape(bsz, 1, HA_HEADS, 1, HA_DV, t)
    o_raw, s_new = _hgrn(r4(qs), r4(kin), r4(logf), r4(v), vt, s0_all, layer if not prompt else 0,
                         bb=tl["bb"], tb=tb, cb=cb)
    if prompt:
        olat = _attn_prompt(qt, kcat, ckvt, bsz=bsz, t=t, tq=tl["tq"], tk=tl["tk"])
    else:
        olat = _attn_sample(qabs, qpe, past[0], past[1], layer, ckv_all, kpe, ts=t)
    x1g = _merge(x2d, o_raw.reshape(HA_HEADS, n, HA_DV), gs, olat, prm["wuv"], prm["wo"],
                 small["hg"], small["mg"], small["g1"], small["b1"],
                 small["wr"], small["rb"], tm=tm, tq=tl["tq"], alpha=alpha)
    tmoe = tl["tmoe"]
    dst, tile_group, m_sorted = _route(x1g, tm=tmoe)
    xs = _sc_scatter_rows(x1g, dst, m_sorted)
    ys = _moe_sorted(tile_group, xs, prm["wg"], prm["wu"], prm["wd"], layer, tm=tmoe,
                     per_step=tl["experts_per_step"], weight_buffers=tl["weight_buffers"])
    ffn = _sc_gather_rows(ys, dst[:n])
    y = _post(x1g, ffn, p_all.reshape(-1, n, PLE_DIM), layer, small["g2"], small["b2"],
              prm["wpg"], prm["wp"], tm=tm, alpha=alpha)
    return y.reshape(bsz, t, D_MODEL), s_new, ckv_all, kpe.reshape(bsz, t, ROPE_DIM)


def kernel(x_prompt, x_sample, p_prompt, p_sample, state_hgrn, cache_ckv, cache_kpe, w_in,
           lb_logits, hgrn_norm_g, q_norm_g, w_qb, kv_norm_g, w_kvb, mla_norm_g, w_o,
           ln1_g, ln1_b, ln2_g, ln2_b, w_router, router_bias, w_gate, w_up, w_down,
           w_ple, w_ple_gate):
    depth = w_in.shape[0]
    alpha = (2 * depth) ** 0.25
    bp, tp, _ = x_prompt.shape
    bs, ts, _ = x_sample.shape
    past = cache_ckv.shape[2]

    sm = jax.nn.softmax(lb_logits.astype(F32), axis=0)
    lb_all = jnp.maximum(jnp.cumsum(sm, axis=0) - sm[0:1], 0.0)
    lbp_all = jnp.stack([jnp.log(lb_all), jnp.log1p(-lb_all), 1.0 - lb_all], axis=1)

    cs_p = jnp.asarray(_rope_table(np.arange(tp)))
    tm_s = _tiles(bs * ts, ts, False)["tm"]
    cs_s = jnp.asarray(np.tile(_rope_table(past + np.arange(ts)), (tm_s // ts, 1)))
    wr = jnp.pad(w_router, ((0, 0), (0, LANES - N_EXPERTS))).astype(BF16)
    rb = jnp.pad(router_bias.astype(F32), (0, LANES - N_EXPERTS)).reshape(1, LANES)
    s0_p = jnp.zeros((1, bp, HA_HEADS, HA_DK, HA_DV), F32)
    cache_kpe_t = jnp.swapaxes(cache_kpe, 2, 3)

    yp, ys = x_prompt, x_sample
    res = [[] for _ in range(4)]
    cp = cs_ = None
    for l in range(depth):
        prm = _prep_layer(l, w_in, w_qb, w_kvb, w_o, w_gate, w_up, w_down, w_ple, w_ple_gate)
        row = lambda a: a[l].reshape(1, -1).astype(F32)
        small = dict(qg=row(q_norm_g), kvg=row(kv_norm_g), hg=row(hgrn_norm_g), mg=row(mla_norm_g),
                     g1=row(ln1_g), b1=row(ln1_b), g2=row(ln2_g), b2=row(ln2_b), wr=wr, rb=rb)
        yp, sp, cp, kp = _layer(yp, p_prompt, l, cs_p, s0_p, None, cp, lbp_all[l], prm, small,
                                alpha=alpha, prompt=True)
        ys, ss, cs_, ks = _layer(ys, p_sample, l, cs_s, state_hgrn, (cache_ckv, cache_kpe_t), cs_,
                                 lbp_all[l], prm, small, alpha=alpha, prompt=False)
        for lst, a in zip(res, (sp, kp, ss, ks)):
            lst.append(a)
    sp, kp, ss, ks = (jnp.stack(a) for a in res)
    return (yp, ys, sp, cp.reshape(depth, bp, tp, KV_LORA), kp, ss,
            cs_.reshape(depth, bs, ts, KV_LORA), ks)
```

```python
import functools

import jax
import jax.numpy as jnp
import numpy as np
from jax import lax
from jax.experimental import pallas as pl
from jax.experimental.pallas import tpu as pltpu
from jax.experimental.pallas import tpu_sc as plsc

F32 = jnp.float32
BF16 = jnp.bfloat16

D_MODEL = 1024
HA_HEADS = 4
HA_DK = 128
HA_DV = 128
HB_HEADS = 4
Q_LORA = 384
KV_LORA = 256
NOPE_DIM = 128
ROPE_DIM = 64
V_DIM = 128
ROPE_THETA = 10000.0
MLA_SCALE = (NOPE_DIM + ROPE_DIM) ** -0.5
LOG2E = 1.4426950408889634
Q_SCALE = MLA_SCALE * LOG2E
CHUNK = 64
N_EXPERTS = 16
N_GROUPS = 4
EXPERTS_PER_GROUP = N_EXPERTS // N_GROUPS
D_EXPERT = 512
PLE_DIM = 256
NEG_INF = -1e30

HA_W = HA_HEADS * HA_DK
COL_CQ = 4 * HA_W
COL_CKV = COL_CQ + Q_LORA
COL_KPE = COL_CKV + KV_LORA
D_IN_AUG = COL_KPE + 2 * ROPE_DIM
Q_HEAD_AUG = NOPE_DIM + 2 * ROPE_DIM

LANES = 128
SUBLANES = 8
QK_DIM = KV_LORA + LANES
GROUP_LANE = N_EXPERTS
SC_WINDOW = 128
SC_ROW = 256
X_PIECES = D_MODEL // SC_ROW
SUB_BLOCK = 16
VMEM_LIMIT = 56 * 1024 * 1024


def _cparams(sem, vmem=VMEM_LIMIT):
    return pltpu.CompilerParams(dimension_semantics=sem, vmem_limit_bytes=vmem)


def _dot(a, b):
    return jnp.dot(a, b, preferred_element_type=F32)


def _dot_nt(a, b):
    return lax.dot_general(a, b, (((1,), (1,)), ((), ())), preferred_element_type=F32)


def _rms(x, g, eps=1e-6):
    return x * lax.rsqrt(jnp.mean(x * x, axis=-1, keepdims=True) + eps) * g


def _layer_norm(x, g, b, eps=1e-5):
    mu = jnp.mean(x, axis=-1, keepdims=True)
    xc = x - mu
    var = jnp.mean(xc * xc, axis=-1, keepdims=True)
    return xc * lax.rsqrt(var + eps) * g + b


def _silu(x):
    return x * jax.nn.sigmoid(x)


def _full(shape):
    nd = len(shape)
    return pl.BlockSpec(shape, lambda *_: (0,) * nd)


def _inproj_kernel(x_ref, cs_ref, win_ref, lb_ref, qg_ref, wqb_ref, kvg_ref, wuk_ref, prev_ref,
                   qs_ref, kin_ref, logf_ref, v_ref, gs_ref, ckv_ref, kpe_ref, *rest,
                   tq, cb, prompt, n_prev):
    tm = x_ref.shape[0]
    x = x_ref[...].astype(BF16)

    def sect(lo, hi):
        return _dot(x, win_ref[:, lo:hi])

    def put_heads(ref, val):
        for h in range(HA_HEADS):
            ref[h] = val[:, h * HA_DK:(h + 1) * HA_DK]

    cqn = _rms(sect(COL_CQ, COL_CKV), qg_ref[...]).astype(BF16)

    put_heads(qs_ref, _silu(sect(0, HA_W)))
    fa = sect(HA_W, 2 * HA_W)
    log_lb = lb_ref[0:1, :]
    log1m_lb = lb_ref[1:2, :]
    one_m_lb = lb_ref[2:3, :]
    e = jnp.exp(-jnp.abs(fa))
    r = 1.0 / (1.0 + e)
    c = log1m_lb + jnp.minimum(fa, 0.0) + jnp.log(r)
    put_heads(logf_ref, jnp.maximum(log_lb, c) + jnp.log(1.0 + jnp.exp(-jnp.abs(log_lb - c))))
    put_heads(kin_ref, one_m_lb * jnp.where(fa >= 0.0, e * r, r))
    v = sect(2 * HA_W, 3 * HA_W)
    put_heads(v_ref, v)
    gs_ref[...] = _silu(sect(3 * HA_W, 4 * HA_W))

    cs = cs_ref[...]

    def rope(t):
        prod = t * cs
        return prod + pltpu.roll(prod, ROPE_DIM, 1)

    ckv = _rms(sect(COL_CKV, COL_KPE), kvg_ref[...])
    for l in range(n_prev):
        ckv_ref[l] = prev_ref[l]
    ckv_ref[n_prev] = ckv
    kpe2 = rope(sect(COL_KPE, D_IN_AUG))
    kpe_ref[...] = kpe2[:, :ROPE_DIM]
    if prompt:
        qt_ref, vt_ref, kcat_ref, ckvt_ref = rest
        for h in range(HA_HEADS):
            for ci in range(tm // cb):
                vt_ref[0, h, ci] = v[ci * cb:(ci + 1) * cb, h * HA_DV:(h + 1) * HA_DV].T.astype(BF16)
        kcat_ref[:, :KV_LORA] = ckv.astype(BF16)
        kcat_ref[:, KV_LORA:] = kpe2.astype(BF16)
        ckvt_ref[0] = ckv.T.astype(BF16)
    else:
        qabs_ref, qpe_ref = rest

    lane = lax.broadcasted_iota(jnp.int32, (tm, LANES), 1)
    qhs = [_dot(cqn, wqb_ref[:, h * Q_HEAD_AUG:(h + 1) * Q_HEAD_AUG]) for h in range(HB_HEADS)]
    qabss = [_dot(qhs[h][:, :NOPE_DIM].astype(BF16), wuk_ref[h]) for h in range(HB_HEADS)]
    for h in range(HB_HEADS):
        qabs = qabss[h] * Q_SCALE
        qpe = jnp.where(lane < ROPE_DIM, rope(qhs[h][:, NOPE_DIM:]) * Q_SCALE, 0.0)
        if prompt:
            qabs_t = qabs.T.astype(BF16)
            qpe_t = qpe.T.astype(BF16)
            for s in range(tm // tq):
                qt_ref[s, :KV_LORA, h * tq:(h + 1) * tq] = qabs_t[:, s * tq:(s + 1) * tq]
                qt_ref[s, KV_LORA:, h * tq:(h + 1) * tq] = qpe_t[:, s * tq:(s + 1) * tq]
        else:
            for s in range(tm // tq):
                qabs_ref[s, h] = qabs[s * tq:(s + 1) * tq].astype(BF16)
                qpe_ref[s, h] = qpe[s * tq:(s + 1) * tq].astype(BF16)


def _inproj(x, cs, w_in, lbp, qg, wqb, kvg, wuk, prev_ckv, *, tm, tq, cb, prompt):
    n = x.shape[0]
    nt = n // tm
    n_prev = 0 if prev_ckv is None else prev_ckv.shape[0]
    if prev_ckv is None:
        prev_ckv = jnp.zeros((1, tm, KV_LORA), F32)
        prev_spec = _full(prev_ckv.shape)
    else:
        prev_spec = pl.BlockSpec((n_prev, tm, KV_LORA), lambda i: (0, i, 0))
    row = lambda w: pl.BlockSpec((tm, w), lambda i: (i, 0))
    heads = pl.BlockSpec((HA_HEADS, tm, HA_DK), lambda i: (0, i, 0))
    qblk = lambda w: pl.BlockSpec((tm // tq, HB_HEADS, tq, w), lambda i: (i, 0, 0, 0))
    head_major = jax.ShapeDtypeStruct((HA_HEADS, n, HA_DK), F32)
    out_shape = [
        head_major,
        head_major,
        head_major,
        head_major,
        jax.ShapeDtypeStruct((n, HA_W), F32),
        jax.ShapeDtypeStruct((n_prev + 1, n, KV_LORA), F32),
        jax.ShapeDtypeStruct((n, ROPE_DIM), F32),
    ]
    out_specs = [heads, heads, heads, heads, row(HA_W),
                 pl.BlockSpec((n_prev + 1, tm, KV_LORA), lambda i: (0, i, 0)), row(ROPE_DIM)]
    if prompt:
        out_shape += [jax.ShapeDtypeStruct((n // tq, QK_DIM, HB_HEADS * tq), BF16),
                      jax.ShapeDtypeStruct((nt, HA_HEADS, tm // cb, HA_DV, cb), BF16),
                      jax.ShapeDtypeStruct((n, QK_DIM), BF16),
                      jax.ShapeDtypeStruct((nt, KV_LORA, tm), BF16)]
        out_specs += [pl.BlockSpec((tm // tq, QK_DIM, HB_HEADS * tq), lambda i: (i, 0, 0)),
                      pl.BlockSpec((1, HA_HEADS, tm // cb, HA_DV, cb), lambda i: (i, 0, 0, 0, 0)),
                      row(QK_DIM),
                      pl.BlockSpec((1, KV_LORA, tm), lambda i: (i, 0, 0))]
    else:
        out_shape += [jax.ShapeDtypeStruct((n // tq, HB_HEADS, tq, KV_LORA), BF16),
                      jax.ShapeDtypeStruct((n // tq, HB_HEADS, tq, LANES), BF16)]
        out_specs += [qblk(KV_LORA), qblk(LANES)]
    return pl.pallas_call(
        functools.partial(_inproj_kernel, tq=tq, cb=cb, prompt=prompt, n_prev=n_prev),
        grid=(nt,),
        in_specs=[row(D_MODEL),
                  pl.BlockSpec((tm, LANES), lambda i: (i % (cs.shape[0] // tm), 0)),
                  _full(w_in.shape), _full(lbp.shape), _full(qg.shape),
                  _full(wqb.shape), _full(kvg.shape), _full(wuk.shape), prev_spec],
        out_specs=out_specs,
        out_shape=out_shape,
        compiler_params=_cparams(("parallel",)),
        name="inproj",
    )(x, cs, w_in, lbp, qg, wqb, kvg, wuk, prev_ckv)


def _hgrn_kernel(q_ref, k_ref, g_ref, v_ref, vt_ref, s0_ref, o_ref, sfin_ref, st_scr, b2_scr,
                 hide_ref, *, cb):
    _, bb, tb, _ = q_ref.shape
    nsub = cb // SUB_BLOCK
    ti = pl.program_id(1)

    @pl.when(ti == 0)
    def _():
        for b in range(bb):
            for h in range(HA_HEADS):
                st_scr[b, h] = s0_ref[b, h].T
        row8 = lax.broadcasted_iota(jnp.int32, (SUBLANES, LANES), 0)
        for s in range(SUBLANES):
            hide_ref[s] = jnp.where(row8 >= s, 0.0, NEG_INF)

    r_i = lax.broadcasted_iota(jnp.int32, (cb, cb), 0)
    c_i = lax.broadcasted_iota(jnp.int32, (cb, cb), 1)
    tril = (r_i >= c_i).astype(F32)

    chains = [(b, h) for b in range(bb) for h in range(HA_HEADS)]

    def matmul_part(ci):
        r0 = pl.multiple_of(ci * cb, cb)
        rows = pl.ds(r0, cb)

        def sub(ref, b, h, lo, size):
            return ref[h, b, pl.ds(pl.multiple_of(r0 + lo, SUB_BLOCK), size), :]

        for b, h in chains:
            bcum = jnp.dot(tril, g_ref[h, b, rows, :], precision=lax.Precision.HIGHEST,
                           preferred_element_type=F32)
            b2_scr[b * HA_HEADS + h] = bcum * LOG2E
        for b, h in chains:
            b2 = b2_scr[b * HA_HEADS + h]
            o_ref[h, b, rows, :] = _dot_nt((q_ref[h, b, rows, :] * jnp.exp2(b2)).astype(BF16),
                                           st_scr[b, h].astype(BF16))
        for j in range(nsub - 1):
            lo, hi = j * SUB_BLOCK, (j + 1) * SUB_BLOCK
            below = pl.ds(pl.multiple_of(r0 + hi, SUB_BLOCK), cb - hi)
            for b, h in chains:
                slot = b * HA_HEADS + h
                bnd = b2_scr[slot, hi - 1:hi, :]
                kd = (sub(k_ref, b, h, lo, SUB_BLOCK)
                      * jnp.exp2(bnd - b2_scr[slot, lo:hi, :])).astype(BF16)
                qe = (sub(q_ref, b, h, hi, cb - hi)
                      * jnp.exp2(b2_scr[slot, hi:, :] - bnd)).astype(BF16)
                a = _dot_nt(qe, kd).astype(BF16)
                o_ref[h, b, below, :] += _dot(a, sub(v_ref, b, h, lo, SUB_BLOCK).astype(BF16))
        for b, h in chains:
            slot = b * HA_HEADS + h
            b_last = b2_scr[slot, cb - 1:cb, :]
            kd_all = (k_ref[h, b, rows, :] * jnp.exp2(b_last - b2_scr[slot])).astype(BF16)
            st_scr[b, h] = st_scr[b, h] * jnp.exp2(b_last) + _dot(vt_ref[b, h, ci], kd_all)

    def pairwise_part(ci, b, h, j):
        slot = b * HA_HEADS + h
        r0 = pl.multiple_of(ci * cb, cb)
        lo = j * SUB_BLOCK
        mid = lo + SUBLANES
        top = pl.ds(pl.multiple_of(r0 + lo, SUBLANES), SUBLANES)
        bot = pl.ds(pl.multiple_of(r0 + mid, SUBLANES), SUBLANES)
        q_top, q_bot = q_ref[h, b, top, :], q_ref[h, b, bot, :]
        b_top, b_bot = b2_scr[slot, lo:mid, :], b2_scr[slot, mid:mid + SUBLANES, :]
        acc_top, acc_bot = o_ref[h, b, top, :], o_ref[h, b, bot, :]
        for s in range(SUB_BLOCK):
            row = pl.ds(r0 + lo + s, 1)
            bs = b2_scr[slot, lo + s:lo + s + 1, :]
            ks = k_ref[h, b, row, :]
            vs = v_ref[h, b, row, :]
            if s < SUBLANES:
                w = q_top * jnp.exp2(b_top - bs + hide_ref[s]) * ks
                acc_top = acc_top + jnp.sum(w, axis=-1, keepdims=True) * vs
                w = q_bot * jnp.exp2(b_bot - bs) * ks
            else:
                w = q_bot * jnp.exp2(b_bot - bs + hide_ref[s - SUBLANES]) * ks
            acc_bot = acc_bot + jnp.sum(w, axis=-1, keepdims=True) * vs
        o_ref[h, b, top, :] = acc_top
        o_ref[h, b, bot, :] = acc_bot

    def chunk(ci, carry):
        matmul_part(ci)
        for j in range(nsub):
            for b, h in chains:
                pairwise_part(ci, b, h, j)
        return carry

    lax.fori_loop(0, tb // cb, chunk, 0)

    @pl.when(ti == pl.num_programs(1) - 1)
    def _():
        for b in range(bb):
            for h in range(HA_HEADS):
                sfin_ref[b, h] = st_scr[b, h].T


def _hgrn(qs, kin, logf, v, vt, s0_all, layer, *, bb, tb, cb):
    _, bsz, t, _ = qs.shape
    blk = pl.BlockSpec((HA_HEADS, bb, tb, HA_DK), lambda i, j: (0, i, j, 0))
    st_blk = pl.BlockSpec((bb, HA_HEADS, HA_DK, HA_DV), lambda i, j: (i, 0, 0, 0))
    s0_blk = pl.BlockSpec((None, bb, HA_HEADS, HA_DK, HA_DV), lambda i, j: (layer, i, 0, 0, 0))
    vt_blk = pl.BlockSpec((bb, None, HA_HEADS, tb // cb, HA_DV, cb),
                          lambda i, j: (i, j, 0, 0, 0, 0))
    return pl.pallas_call(
        functools.partial(_hgrn_kernel, cb=cb),
        grid=(bsz // bb, t // tb),
        in_specs=[blk, blk, blk, blk, vt_blk, s0_blk],
        out_specs=[blk, st_blk],
        out_shape=[jax.ShapeDtypeStruct(qs.shape, F32),
                   jax.ShapeDtypeStruct(s0_all.shape[1:], F32)],
        scratch_shapes=[pltpu.VMEM((bb, HA_HEADS, HA_DV, HA_DK), F32),
                        pltpu.VMEM((bb * HA_HEADS, cb, HA_DK), F32),
                        pltpu.VMEM((SUBLANES, SUBLANES, LANES), F32)],
        compiler_params=_cparams(("parallel", "arbitrary")),
        name="hgrn",
    )(qs, kin, logf, v, vt, s0_all)


def _attn_prompt_kernel(qt_ref, kc_ref, kt_ref, o_ref, s0, s1, x0, x1, p0, p1, a0, a1, m_scr,
                        l_scr, acc_scr, *, tq, tk):
    qi = pl.program_id(1)
    jd = (qi * tq) // tk
    shift = CHUNK.bit_length() - 1
    head_cols = [slice(h * tq, (h + 1) * tq) for h in range(HB_HEADS)]

    def scores(j, s_dst, x_dst):
        keys = kc_ref[pl.ds(pl.multiple_of(j * tk, tk), tk), :]
        for cols in head_cols:
            s = _dot(keys, qt_ref[0, :, cols])
            s_dst[:, cols] = s
            x_dst[:, cols] = jnp.max(s, axis=0, keepdims=True)

    def softmax(j, s_src, x_src, p_dst, a_dst, masked):
        if masked:
            kpos = j * tk + lax.broadcasted_iota(jnp.int32, (tk, 1), 0)
            qpos = qi * tq + lax.broadcasted_iota(jnp.int32, (1, tq), 1)
            visible = (kpos >> shift) <= (qpos >> shift)
        for cols in head_cols:
            m_prev = m_scr[:, cols]
            if masked:
                s = jnp.where(visible, s_src[:, cols], NEG_INF)
                m_new = jnp.maximum(m_prev, jnp.max(s, axis=0, keepdims=True))
            else:
                s = s_src[:, cols]
                m_new = jnp.maximum(m_prev, x_src[:, cols])
            alpha = jnp.exp2(m_prev - m_new)
            p = jnp.exp2(s - m_new)
            l_scr[:, cols] = alpha * l_scr[:, cols] + jnp.sum(p, axis=0, keepdims=True)
            m_scr[:, cols] = m_new
            a_dst[:, cols] = alpha
            p_dst[:, cols] = p.astype(BF16)

    def values(j, p_src, a_src):
        for cols in head_cols:
            acc_scr[:, cols] = (a_src[:, cols] * acc_scr[:, cols]
                                + _dot(kt_ref[j], p_src[:, cols]))

    set0, set1 = (s0, x0, p0, a0), (s1, x1, p1, a1)

    def trip(j, cur, nxt):
        scores(j + 1, nxt[0], nxt[1])
        softmax(j, cur[0], cur[1], cur[2], cur[3], False)
        values(jnp.maximum(j - 1, 0), nxt[2], nxt[3])

    m_scr[...] = jnp.full(m_scr.shape, NEG_INF, F32)
    l_scr[...] = jnp.zeros(l_scr.shape, F32)
    acc_scr[...] = jnp.zeros(acc_scr.shape, F32)
    p1[...] = jnp.zeros(p1.shape, BF16)
    a1[...] = jnp.ones(a1.shape, F32)
    scores(0, s0, x0)

    def pair(jj, carry):
        trip(2 * jj, set0, set1)
        trip(2 * jj + 1, set1, set0)
        return carry

    lax.fori_loop(0, jd // 2, pair, 0)

    @pl.when(jd % 2 == 1)
    def _():
        trip(jd - 1, set0, set1)

    def drain(cur, prv):
        softmax(jd, cur[0], cur[1], cur[2], cur[3], True)
        values(jnp.maximum(jd - 1, 0), prv[2], prv[3])
        values(jd, cur[2], cur[3])

    pl.when(jd % 2 == 0)(functools.partial(drain, set0, set1))
    pl.when(jd % 2 == 1)(functools.partial(drain, set1, set0))

    for h in range(HB_HEADS):
        cols = slice(h * tq, (h + 1) * tq)
        o_ref[0, h] = (acc_scr[:, cols] * (1.0 / l_scr[:, cols])).T.astype(BF16)


def _attn_prompt(qt, kcat, ckvt, *, bsz, t, tq, tk):
    nq = t // tq
    nk = t // tk
    cols = HB_HEADS * tq
    single = pl.Buffered(1)
    return pl.pallas_call(
        functools.partial(_attn_prompt_kernel, tq=tq, tk=tk),
        grid=(bsz, nq),
        in_specs=[
            pl.BlockSpec((1, QK_DIM, cols), lambda b, i: (b * nq + i, 0, 0)),
            pl.BlockSpec((t, QK_DIM), lambda b, i: (b, 0), pipeline_mode=single),
            pl.BlockSpec((nk, KV_LORA, tk), lambda b, i: (b, 0, 0), pipeline_mode=single),
        ],
        out_specs=pl.BlockSpec((1, HB_HEADS, tq, KV_LORA), lambda b, i: (b * nq + i, 0, 0, 0)),
        out_shape=jax.ShapeDtypeStruct((bsz * nq, HB_HEADS, tq, KV_LORA), BF16),
        scratch_shapes=[pltpu.VMEM((tk, cols), F32), pltpu.VMEM((tk, cols), F32),
                        pltpu.VMEM((1, cols), F32), pltpu.VMEM((1, cols), F32),
                        pltpu.VMEM((tk, cols), BF16), pltpu.VMEM((tk, cols), BF16),
                        pltpu.VMEM((1, cols), F32), pltpu.VMEM((1, cols), F32),
                        pltpu.VMEM((1, cols), F32), pltpu.VMEM((1, cols), F32),
                        pltpu.VMEM((KV_LORA, cols), F32)],
        compiler_params=_cparams(("parallel", "arbitrary")),
        name="attn_prompt",
    )(qt, kcat, ckvt)


def _attn_sample_kernel(qa_ref, qp_ref, pckv_ref, pkpe_ref, nckv_ref, nkpe_ref, o_ref):
    _, _, ts, _ = qa_ref.shape
    rows = HB_HEADS * ts
    qa = qa_ref[0].reshape(rows, KV_LORA)
    qp = qp_ref[0].reshape(rows, LANES)[:, :ROPE_DIM]
    pckv = pckv_ref[0].astype(BF16)
    nckv = nckv_ref[...].astype(BF16)
    s_past = _dot_nt(qa, pckv) + _dot(qp, pkpe_ref[0].astype(BF16))
    s_new = _dot_nt(qa, nckv) + _dot_nt(qp, nkpe_ref[...].astype(BF16))
    m = jnp.maximum(jnp.max(s_past, axis=-1, keepdims=True), jnp.max(s_new, axis=-1, keepdims=True))
    p_past = jnp.exp2(s_past - m)
    p_new = jnp.exp2(s_new - m)
    l = jnp.sum(p_past, axis=-1, keepdims=True) + jnp.sum(p_new, axis=-1, keepdims=True)
    o = (_dot(p_past.astype(BF16), pckv) + _dot(p_new.astype(BF16), nckv)) / l
    o_ref[0] = o.astype(BF16).reshape(HB_HEADS, ts, KV_LORA)


def _attn_sample(qabs, qpe, past_ckv, past_kpe, layer, ckv, kpe, *, ts):
    _, bsz, past, _ = past_ckv.shape
    return pl.pallas_call(
        _attn_sample_kernel,
        grid=(bsz,),
        in_specs=[
            pl.BlockSpec((1, HB_HEADS, ts, KV_LORA), lambda b: (b, 0, 0, 0)),
            pl.BlockSpec((1, HB_HEADS, ts, LANES), lambda b: (b, 0, 0, 0)),
            pl.BlockSpec((None, 1, past, KV_LORA), lambda b: (layer, b, 0, 0)),
            pl.BlockSpec((None, 1, ROPE_DIM, past), lambda b: (layer, b, 0, 0)),
            pl.BlockSpec((None, ts, KV_LORA), lambda b: (layer, b, 0)),
            pl.BlockSpec((ts, ROPE_DIM), lambda b: (b, 0)),
        ],
        out_specs=pl.BlockSpec((1, HB_HEADS, ts, KV_LORA), lambda b: (b, 0, 0, 0)),
        out_shape=jax.ShapeDtypeStruct(qabs.shape, BF16),
        compiler_params=_cparams(("parallel",)),
        name="attn_sample",
    )(qabs, qpe, past_ckv, past_kpe, ckv, kpe)


def _gates(logits, bias):
    tm = logits.shape[0]
    lane = lax.broadcasted_iota(jnp.int32, (tm, LANES), 1)
    pos = lane % EXPERTS_PER_GROUP
    valid = lane < N_EXPERTS
    scores = jax.nn.sigmoid(logits)
    sel = jnp.where(valid, scores + bias, -jnp.inf)

    others = []
    for r in range(1, EXPERTS_PER_GROUP):
        others.append(jnp.where(pos >= r, pltpu.roll(sel, r, 1),
                                pltpu.roll(sel, LANES - (EXPERTS_PER_GROUP - r), 1)))
    a, b, c, d = sel, others[0], others[1], others[2]
    hi1, lo1 = jnp.maximum(a, b), jnp.minimum(a, b)
    hi2, lo2 = jnp.maximum(c, d), jnp.minimum(c, d)
    gscore = jnp.maximum(hi1, hi2) + jnp.maximum(jnp.minimum(hi1, hi2), jnp.maximum(lo1, lo2))
    gmax = jnp.max(gscore, axis=-1, keepdims=True)
    group = (lane // EXPERTS_PER_GROUP).astype(F32)
    gidx = jnp.min(jnp.where(gscore == gmax, group, float(LANES)), axis=-1, keepdims=True)

    rank = jnp.zeros((tm, LANES), jnp.int32)
    for r, o in enumerate(others, start=1):
        ahead = (o > sel) | ((o == sel) & (pos >= r))
        rank = rank + ahead.astype(jnp.int32)
    chosen = (group == gidx) & (rank < 2) & valid
    w = jnp.where(chosen, scores, 0.0)
    w = w / jnp.sum(w, axis=-1, keepdims=True)
    return jnp.where(lane == GROUP_LANE, gidx, w)


def _merge_kernel(x_ref, o_ref, gs_ref, olat_ref, wuv_ref, wo_ref, hg_ref, mg_ref,
                  g1_ref, b1_ref, wr_ref, rb_ref, x1g_ref, *, alpha):
    _, _, tq, _ = olat_ref.shape
    tm = x_ref.shape[0]
    hg = hg_ref[...]
    d_a = HA_HEADS * HA_DV
    halves = [(i * tm // 2, (i + 1) * tm // 2) for i in range(2)]

    def lat(h, lo, hi):
        if tq >= hi - lo:
            return olat_ref[lo // tq, h, lo % tq:lo % tq + hi - lo]
        return jnp.concatenate([olat_ref[s, h] for s in range(lo // tq, hi // tq)], axis=0)

    ob = [[_dot(lat(h, lo, hi), wuv_ref[h]) for h in range(HB_HEADS)] for lo, hi in halves]
    oa = [jnp.concatenate([_rms(o_ref[h, lo:hi, :], hg) * gs_ref[lo:hi, h * HA_DV:(h + 1) * HA_DV]
                           for h in range(HA_HEADS)], axis=-1) for lo, hi in halves]
    ob = [_rms(jnp.concatenate(o, axis=-1), mg_ref[...]) for o in ob]
    mix = [_dot(a.astype(BF16), wo_ref[:d_a, :]) + _dot(o.astype(BF16), wo_ref[d_a:, :])
           for a, o in zip(oa, ob)]
    x1 = [_layer_norm(alpha * x_ref[lo:hi, :] + m, g1_ref[...], b1_ref[...])
          for (lo, hi), m in zip(halves, mix)]
    logits = [_dot(x.astype(BF16), wr_ref[...]) for x in x1]
    for (lo, hi), x, lg in zip(halves, x1, logits):
        for c in range(X_PIECES):
            x1g_ref[c, lo:hi, :] = x[:, c * SC_ROW:(c + 1) * SC_ROW]
        x1g_ref[X_PIECES, lo:hi, :LANES] = _gates(lg, rb_ref[...])
        x1g_ref[X_PIECES, lo:hi, LANES:] = jnp.zeros((hi - lo, SC_ROW - LANES), F32)


def _merge(x, o_raw, gs, olat, wuv, wo, hg, mg, g1, b1, wr, rb, *, tm, tq, alpha):
    n = x.shape[0]
    row = lambda w: pl.BlockSpec((tm, w), lambda i: (i, 0))
    return pl.pallas_call(
        functools.partial(_merge_kernel, alpha=alpha),
        grid=(n // tm,),
        in_specs=[row(D_MODEL), pl.BlockSpec((HA_HEADS, tm, HA_DV), lambda i: (0, i, 0)), row(HA_W),
                  pl.BlockSpec((tm // tq, HB_HEADS, tq, KV_LORA), lambda i: (i, 0, 0, 0)),
                  _full(wuv.shape), _full(wo.shape), _full(hg.shape), _full(mg.shape),
                  _full(g1.shape), _full(b1.shape), _full(wr.shape), _full(rb.shape)],
        out_specs=pl.BlockSpec((X_PIECES + 1, tm, SC_ROW), lambda i: (0, i, 0)),
        out_shape=jax.ShapeDtypeStruct((X_PIECES + 1, n, SC_ROW), F32),
        compiler_params=_cparams(("parallel",)),
        name="merge",
    )(x, o_raw, gs, olat, wuv, wo, hg, mg, g1, b1, wr, rb)


def _sc_scatter_rows(src, dst, m):
    parts, n, _ = src.shape
    windows = dst.shape[0] // SC_WINDOW
    src_windows = n // SC_WINDOW
    pieces = src.reshape(parts * n, SC_ROW)
    piece_dst = (jnp.arange(parts, dtype=jnp.int32)[:, None] * m + dst[None, :]).reshape(1, -1)
    mesh = plsc.VectorSubcoreMesh(core_axis_name="core", subcore_axis_name="subcore")

    @functools.partial(pl.kernel, out_type=jax.ShapeDtypeStruct((parts * m, SC_ROW), src.dtype),
                       mesh=mesh)
    def scatter(x_hbm, i_hbm, o_hbm):
        def body(x_vmem, i_vmem):
            pltpu.sync_copy(x_vmem, o_hbm.at[i_vmem.at[0]])

        pltpu.emit_pipeline(
            body,
            grid=(parts * windows,),
            in_specs=[pl.BlockSpec((SC_WINDOW, SC_ROW),
                                   lambda i: ((i // windows) * src_windows
                                              + (i % windows) % src_windows, 0)),
                      pl.BlockSpec((1, SC_WINDOW), lambda i: (0, i))],
            out_specs=[],
            core_axis_name=("core", "subcore"),
            dimension_semantics=(pltpu.PARALLEL,),
        )(x_hbm, i_hbm)

    return scatter(pieces, piece_dst).reshape(parts, m, SC_ROW)


def _sc_gather_rows(src, idx):
    parts, n, _ = src.shape
    m = idx.shape[0] * parts
    pieces = src.reshape(parts * n, SC_ROW)
    piece_idx = (jnp.arange(parts, dtype=jnp.int32)[:, None] * n + idx[None, :]).reshape(1, m)
    mesh = plsc.VectorSubcoreMesh(core_axis_name="core", subcore_axis_name="subcore")

    @functools.partial(pl.kernel, out_type=jax.ShapeDtypeStruct((m, SC_ROW), src.dtype), mesh=mesh)
    def gather(x_hbm, i_hbm, o_hbm):
        def body(i_vmem, o_vmem):
            pltpu.sync_copy(x_hbm.at[i_vmem.at[0]], o_vmem)

        pltpu.emit_pipeline(
            body,
            grid=(m // SC_WINDOW,),
            in_specs=[pl.BlockSpec((1, SC_WINDOW), lambda i: (0, i))],
            out_specs=[pl.BlockSpec((SC_WINDOW, SC_ROW), lambda i: (i, 0))],
            core_axis_name=("core", "subcore"),
            dimension_semantics=(pltpu.PARALLEL,),
        )(i_hbm, o_hbm)

    return gather(pieces, piece_idx).reshape(parts, idx.shape[0], SC_ROW)


def _moe_kernel(tg_ref, xs_ref, wg_ref, wu_ref, wd_ref, y_ref, acc_scr):
    tm = xs_ref.shape[1]
    per_step = wg_ref.shape[0]
    step = pl.program_id(1)
    first_expert = tg_ref[pl.program_id(0)] * EXPERTS_PER_GROUP + step * per_step
    xb = jnp.concatenate([xs_ref[c] for c in range(X_PIECES)], axis=1).astype(BF16)
    gates = xs_ref[X_PIECES, :, :LANES]
    lane = lax.broadcasted_iota(jnp.int32, (tm, LANES), 1)
    acc = jnp.zeros((tm, D_MODEL), F32)
    for k in range(per_step):
        hmid = (_silu(_dot(xb, wg_ref[k].astype(BF16))) * _dot(xb, wu_ref[k].astype(BF16)))
        gcol = jnp.sum(jnp.where(lane == first_expert + k, gates, 0.0), axis=-1, keepdims=True)
        acc = acc + gcol * _dot(hmid.astype(BF16), wd_ref[k].astype(BF16))

    @pl.when(step == 0)
    def _():
        acc_scr[...] = acc

    @pl.when(step > 0)
    def _():
        acc_scr[...] += acc

    @pl.when(step == pl.num_programs(1) - 1)
    def _():
        for c in range(D_MODEL // SC_ROW):
            y_ref[c] = acc_scr[:, c * SC_ROW:(c + 1) * SC_ROW]


def _moe_sorted(tile_group, xs, wg, wu, wd, layer, *, tm, per_step, weight_buffers):
    m = xs.shape[1]
    steps = EXPERTS_PER_GROUP // per_step
    mode = dict(pipeline_mode=pl.Buffered(weight_buffers))
    wspec = lambda shape: pl.BlockSpec((None, per_step) + shape,
                                       lambda i, k, tg: (layer, tg[i] * steps + k, 0, 0), **mode)
    return pl.pallas_call(
        _moe_kernel,
        grid_spec=pltpu.PrefetchScalarGridSpec(
            num_scalar_prefetch=1,
            grid=(m // tm, steps),
            in_specs=[pl.BlockSpec((X_PIECES + 1, tm, SC_ROW), lambda i, k, tg: (0, i, 0)),
                      wspec((D_MODEL, D_EXPERT)), wspec((D_MODEL, D_EXPERT)),
                      wspec((D_EXPERT, D_MODEL))],
            out_specs=pl.BlockSpec((D_MODEL // SC_ROW, tm, SC_ROW), lambda i, k, tg: (0, i, 0)),
            scratch_shapes=[pltpu.VMEM((tm, D_MODEL), F32)],
        ),
        out_shape=jax.ShapeDtypeStruct((D_MODEL // SC_ROW, m, SC_ROW), F32),
        compiler_params=_cparams(("arbitrary", "arbitrary")),
        name="moe",
    )(tile_group, xs, wg, wu, wd)


def _route(x1g, *, tm):
    n = x1g.shape[1]
    n_tiles = n // tm + N_GROUPS - 1
    m = n_tiles * tm
    gidx = x1g[X_PIECES, :, GROUP_LANE].astype(jnp.int32)
    onehot = (gidx[:, None] == jnp.arange(N_GROUPS, dtype=jnp.int32)[None, :]).astype(jnp.int32)
    counts = jnp.sum(onehot, axis=0)
    tiles = (counts + tm - 1) // tm
    tile_end = jnp.cumsum(tiles)
    tile_start = tile_end - tiles
    rank = jnp.sum((jnp.cumsum(onehot, axis=0) - onehot) * onehot, axis=1)
    pos = jnp.take(tile_start, gidx) * tm + rank
    pad_start = tile_start * tm + counts
    pad_end = (tile_end * tm).at[N_GROUPS - 1].set(m)
    pad_cum = jnp.cumsum(pad_end - pad_start)
    k = jnp.arange(m - n, dtype=jnp.int32)
    pg = jnp.searchsorted(pad_cum, k, side="right")
    pad_rows = jnp.take(pad_start, pg) + k - jnp.take(pad_cum - (pad_end - pad_start), pg)
    dst = jnp.concatenate([pos, pad_rows]).astype(jnp.int32)
    tile_group = jnp.searchsorted(tile_end, jnp.arange(n_tiles, dtype=jnp.int32), side="right")
    return dst, jnp.minimum(tile_group, N_GROUPS - 1).astype(jnp.int32), m


def _post_kernel(x1_ref, f_ref, p_ref, g2_ref, b2_ref, wpg_ref, wp_ref, y_ref, *, alpha):
    tm = x1_ref.shape[1]
    halves = [(i * tm // 2, (i + 1) * tm // 2) for i in range(2)]
    emb = [_dot(p_ref[lo:hi, :].astype(BF16), wp_ref[...]) for lo, hi in halves]

    def rows(ref, lo, hi):
        return jnp.concatenate([ref[c, lo:hi, :] for c in range(X_PIECES)], axis=1)

    x2 = [_layer_norm(alpha * rows(x1_ref, lo, hi) + rows(f_ref, lo, hi), g2_ref[...], b2_ref[...])
          for lo, hi in halves]
    gate = [_dot(x.astype(BF16), wpg_ref[...]) for x in x2]
    for (lo, hi), x, g, e in zip(halves, x2, gate, emb):
        y_ref[lo:hi, :] = x + jax.nn.sigmoid(g) * e


def _post(x1g, ffn, p_all, layer, g2, b2, wpg, wp, *, tm, alpha):
    _, n, _ = ffn.shape
    row = lambda w: pl.BlockSpec((tm, w), lambda i: (i, 0))
    pieces = lambda a: pl.BlockSpec((a.shape[0], tm, SC_ROW), lambda i: (0, i, 0))
    return pl.pallas_call(
        functools.partial(_post_kernel, alpha=alpha),
        grid=(n // tm,),
        in_specs=[pieces(x1g), pieces(ffn),
                  pl.BlockSpec((None, tm, PLE_DIM), lambda i: (layer, i, 0)),
                  _full(g2.shape), _full(b2.shape), _full(wpg.shape), _full(wp.shape)],
        out_specs=row(D_MODEL),
        out_shape=jax.ShapeDtypeStruct((n, D_MODEL), F32),
        compiler_params=_cparams(("parallel",)),
        name="post",
    )(x1g, ffn, p_all, g2, b2, wpg, wp)


def _rot_cols(w):
    half = ROPE_DIM // 2
    return jnp.concatenate([-w[..., half:], w[..., :half]], axis=-1)


def _prep_layer(l, w_in, w_qb, w_kvb, w_o, w_gate, w_up, w_down, w_ple, w_ple_gate):
    win = w_in[l]
    win_aug = jnp.concatenate([win, _rot_cols(win[:, COL_KPE:])], axis=-1).astype(BF16)
    wqb = w_qb[l].reshape(Q_LORA, HB_HEADS, NOPE_DIM + ROPE_DIM)
    wqb_aug = jnp.concatenate([wqb, _rot_cols(wqb[..., NOPE_DIM:])], axis=-1)
    wqb_aug = wqb_aug.reshape(Q_LORA, HB_HEADS * Q_HEAD_AUG).astype(BF16)
    wkvb = w_kvb[l].reshape(KV_LORA, HB_HEADS, NOPE_DIM + V_DIM)
    wuk_t = jnp.transpose(wkvb[..., :NOPE_DIM], (1, 2, 0)).astype(BF16)
    wuv = jnp.transpose(wkvb[..., NOPE_DIM:], (1, 0, 2)).astype(BF16)
    return dict(win=win_aug, wqb=wqb_aug, wuk=wuk_t, wuv=wuv, wo=w_o[l].astype(BF16),
                wg=w_gate, wu=w_up, wd=w_down,
                wp=w_ple[l].astype(BF16), wpg=w_ple_gate[l].astype(BF16))


def _rope_table(pos):
    inv = ROPE_THETA ** (-np.arange(0, ROPE_DIM, 2, dtype=np.float64) / ROPE_DIM)
    ang = pos.astype(np.float64)[:, None] * inv[None, :]
    cos, sin = np.cos(ang), np.sin(ang)
    return np.concatenate([cos, cos, sin, sin], axis=-1).astype(np.float32)


def _tiles(n, t, prompt):
    if prompt:
        tm = min(512, t)
        return dict(tm=tm, tq=tm, tk=tm, tb=tm, cb=min(128, t), bb=2, tmoe=min(512, n),
                    experts_per_step=EXPERTS_PER_GROUP, weight_buffers=1)
    return dict(tm=min(512, n), tq=t, tk=None, tb=t, cb=t, bb=2, tmoe=min(256, n),
                experts_per_step=EXPERTS_PER_GROUP, weight_buffers=2)


def _layer(x, p_all, layer, cs, s0_all, past, prev_ckv, lbp, prm, small, *, alpha, prompt):
    bsz, t, _ = x.shape
    n = bsz * t
    tl = _tiles(n, t, prompt)
    x2d = x.reshape(n, D_MODEL)
    tm, tb, cb = tl["tm"], tl["tb"], tl["cb"]
    outs = _inproj(x2d, cs, prm["win"], lbp, small["qg"], prm["wqb"], small["kvg"], prm["wuk"],
                   prev_ckv, tm=tm, tq=tl["tq"], cb=cb, prompt=prompt)
    qs, kin, logf, v, gs, ckv_all, kpe = outs[:7]
    r4 = lambda a: a.reshape(HA_HEADS, bsz, t, HA_DK)
    if prompt:
        qt, vt, kcat, ckvt = outs[7:]
        vt = vt.reshape(bsz, t // tb, HA_HEADS, tb // cb, HA_DV, cb)
    else:
        qabs, qpe = outs[7:]
        vt = jnp.transpose(r4(v), (1, 0, 3, 2)).astype(BF16)
        vt = vt.reshape(bsz, 1, HA_HEADS, 1, HA_DV, t)
    o_raw, s_new = _hgrn(r4(qs), r4(kin), r4(logf), r4(v), vt, s0_all, layer if not prompt else 0,
                         bb=tl["bb"], tb=tb, cb=cb)
    if prompt:
        olat = _attn_prompt(qt, kcat, ckvt, bsz=bsz, t=t, tq=tl["tq"], tk=tl["tk"])
    else:
        olat = _attn_sample(qabs, qpe, past[0], past[1], layer, ckv_all, kpe, ts=t)
    x1g = _merge(x2d, o_raw.reshape(HA_HEADS, n, HA_DV), gs, olat, prm["wuv"], prm["wo"],
                 small["hg"], small["mg"], small["g1"], small["b1"],
                 small["wr"], small["rb"], tm=tm, tq=tl["tq"], alpha=alpha)
    tmoe = tl["tmoe"]
    dst, tile_group, m_sorted = _route(x1g, tm=tmoe)
    xs = _sc_scatter_rows(x1g, dst, m_sorted)
    ys = _moe_sorted(tile_group, xs, prm["wg"], prm["wu"], prm["wd"], layer, tm=tmoe,
                     per_step=tl["experts_per_step"], weight_buffers=tl["weight_buffers"])
    ffn = _sc_gather_rows(ys, dst[:n])
    y = _post(x1g, ffn, p_all.reshape(-1, n, PLE_DIM), layer, small["g2"], small["b2"],
              prm["wpg"], prm["wp"], tm=tm, alpha=alpha)
    return y.reshape(bsz, t, D_MODEL), s_new, ckv_all, kpe.reshape(bsz, t, ROPE_DIM)


def kernel(x_prompt, x_sample, p_prompt, p_sample, state_hgrn, cache_ckv, cache_kpe, w_in,
           lb_logits, hgrn_norm_g, q_norm_g, w_qb, kv_norm_g, w_kvb, mla_norm_g, w_o,
           ln1_g, ln1_b, ln2_g, ln2_b, w_router, router_bias, w_gate, w_up, w_down,
           w_ple, w_ple_gate):
    depth = w_in.shape[0]
    alpha = (2 * depth) ** 0.25
    bp, tp, _ = x_prompt.shape
    bs, ts, _ = x_sample.shape
    past = cache_ckv.shape[2]

    sm = jax.nn.softmax(lb_logits.astype(F32), axis=0)
    lb_all = jnp.maximum(jnp.cumsum(sm, axis=0) - sm[0:1], 0.0)
    lbp_all = jnp.stack([jnp.log(lb_all), jnp.log1p(-lb_all), 1.0 - lb_all], axis=1)

    cs_p = jnp.asarray(_rope_table(np.arange(tp)))
    tm_s = _tiles(bs * ts, ts, False)["tm"]
    cs_s = jnp.asarray(np.tile(_rope_table(past + np.arange(ts)), (tm_s // ts, 1)))
    wr = jnp.pad(w_router, ((0, 0), (0, LANES - N_EXPERTS))).astype(BF16)
    rb = jnp.pad(router_bias.astype(F32), (0, LANES - N_EXPERTS)).reshape(1, LANES)
    s0_p = jnp.zeros((1, bp, HA_HEADS, HA_DK, HA_DV), F32)
    cache_kpe_t = jnp.swapaxes(cache_kpe, 2, 3)

    yp, ys = x_prompt, x_sample
    res = [[] for _ in range(4)]
    cp = cs_ = None
    for l in range(depth):
        prm = _prep_layer(l, w_in, w_qb, w_kvb, w_o, w_gate, w_up, w_down, w_ple, w_ple_gate)
        row = lambda a: a[l].reshape(1, -1).astype(F32)
        small = dict(qg=row(q_norm_g), kvg=row(kv_norm_g), hg=row(hgrn_norm_g), mg=row(mla_norm_g),
                     g1=row(ln1_g), b1=row(ln1_b), g2=row(ln2_g), b2=row(ln2_b), wr=wr, rb=rb)
        yp, sp, cp, kp = _layer(yp, p_prompt, l, cs_p, s0_p, None, cp, lbp_all[l], prm, small,
                                alpha=alpha, prompt=True)
        ys, ss, cs_, ks = _layer(ys, p_sample, l, cs_s, state_hgrn, (cache_ckv, cache_kpe_t), cs_,
                                 lbp_all[l], prm, small, alpha=alpha, prompt=False)
        for lst, a in zip(res, (sp, kp, ss, ks)):
            lst.append(a)
    sp, kp, ss, ks = (jnp.stack(a) for a in res)
    return (yp, ys, sp, cp.reshape(depth, bp, tp, KV_LORA), kp, ss,
            cs_.reshape(depth, bs, ts, KV_LORA), ks)
```

```python
import functools

import jax
import jax.numpy as jnp
import numpy as np
from jax import lax
from jax.experimental import pallas as pl
from jax.experimental.pallas import tpu as pltpu
from jax.experimental.pallas import tpu_sc as plsc

F32 = jnp.float32
BF16 = jnp.bfloat16

D_MODEL = 1024
HA_HEADS = 4
HA_DK = 128
HA_DV = 128
HB_HEADS = 4
Q_LORA = 384
KV_LORA = 256
NOPE_DIM = 128
ROPE_DIM = 64
V_DIM = 128
ROPE_THETA = 10000.0
MLA_SCALE = (NOPE_DIM + ROPE_DIM) ** -0.5
LOG2E = 1.4426950408889634
Q_SCALE = MLA_SCALE * LOG2E
CHUNK = 64
N_EXPERTS = 16
N_GROUPS = 4
EXPERTS_PER_GROUP = N_EXPERTS // N_GROUPS
D_EXPERT = 512
PLE_DIM = 256
NEG_INF = -1e30

HA_W = HA_HEADS * HA_DK
COL_CQ = 4 * HA_W
COL_CKV = COL_CQ + Q_LORA
COL_KPE = COL_CKV + KV_LORA
D_IN_AUG = COL_KPE + 2 * ROPE_DIM
Q_HEAD_AUG = NOPE_DIM + 2 * ROPE_DIM

LANES = 128
SUBLANES = 8
QK_DIM = KV_LORA + LANES
GROUP_LANE = N_EXPERTS
SC_WINDOW = 128
SC_ROW = 256
X_PIECES = D_MODEL // SC_ROW
SUB_BLOCK = 16
VMEM_LIMIT = 56 * 1024 * 1024


def _cparams(sem, vmem=VMEM_LIMIT):
    return pltpu.CompilerParams(dimension_semantics=sem, vmem_limit_bytes=vmem)


def _dot(a, b):
    return jnp.dot(a, b, preferred_element_type=F32)


def _dot_nt(a, b):
    return lax.dot_general(a, b, (((1,), (1,)), ((), ())), preferred_element_type=F32)


def _rms(x, g, eps=1e-6):
    return x * lax.rsqrt(jnp.mean(x * x, axis=-1, keepdims=True) + eps) * g


def _layer_norm(x, g, b, eps=1e-5):
    mu = jnp.mean(x, axis=-1, keepdims=True)
    xc = x - mu
    var = jnp.mean(xc * xc, axis=-1, keepdims=True)
    return xc * lax.rsqrt(var + eps) * g + b


def _silu(x):
    return x * jax.nn.sigmoid(x)


def _full(shape):
    nd = len(shape)
    return pl.BlockSpec(shape, lambda *_: (0,) * nd)


def _inproj_kernel(x_ref, cs_ref, win_ref, lb_ref, qg_ref, wqb_ref, kvg_ref, wuk_ref, prev_ref,
                   qs_ref, kin_ref, logf_ref, v_ref, gs_ref, ckv_ref, kpe_ref, *rest,
                   tq, cb, prompt, n_prev):
    tm = x_ref.shape[0]
    x = x_ref[...].astype(BF16)

    def sect(lo, hi):
        return _dot(x, win_ref[:, lo:hi])

    def put_heads(ref, val):
        for h in range(HA_HEADS):
            ref[h] = val[:, h * HA_DK:(h + 1) * HA_DK]

    cqn = _rms(sect(COL_CQ, COL_CKV), qg_ref[...]).astype(BF16)

    put_heads(qs_ref, _silu(sect(0, HA_W)))
    fa = sect(HA_W, 2 * HA_W)
    log_lb = lb_ref[0:1, :]
    log1m_lb = lb_ref[1:2, :]
    one_m_lb = lb_ref[2:3, :]
    e = jnp.exp(-jnp.abs(fa))
    r = 1.0 / (1.0 + e)
    c = log1m_lb + jnp.minimum(fa, 0.0) + jnp.log(r)
    put_heads(logf_ref, jnp.maximum(log_lb, c) + jnp.log(1.0 + jnp.exp(-jnp.abs(log_lb - c))))
    put_heads(kin_ref, one_m_lb * jnp.where(fa >= 0.0, e * r, r))
    v = sect(2 * HA_W, 3 * HA_W)
    put_heads(v_ref, v)
    gs_ref[...] = _silu(sect(3 * HA_W, 4 * HA_W))

    cs = cs_ref[...]

    def rope(t):
        prod = t * cs
        return prod + pltpu.roll(prod, ROPE_DIM, 1)

    ckv = _rms(sect(COL_CKV, COL_KPE), kvg_ref[...])
    for l in range(n_prev):
        ckv_ref[l] = prev_ref[l]
    ckv_ref[n_prev] = ckv
    kpe2 = rope(sect(COL_KPE, D_IN_AUG))
    kpe_ref[...] = kpe2[:, :ROPE_DIM]
    if prompt:
        qt_ref, vt_ref, kcat_ref, ckvt_ref = rest
        for h in range(HA_HEADS):
            for ci in range(tm // cb):
                vt_ref[0, h, ci] = v[ci * cb:(ci + 1) * cb, h * HA_DV:(h + 1) * HA_DV].T.astype(BF16)
        kcat_ref[:, :KV_LORA] = ckv.astype(BF16)
        kcat_ref[:, KV_LORA:] = kpe2.astype(BF16)
        ckvt_ref[0] = ckv.T.astype(BF16)
    else:
        qabs_ref, qpe_ref = rest

    lane = lax.broadcasted_iota(jnp.int32, (tm, LANES), 1)
    qhs = [_dot(cqn, wqb_ref[:, h * Q_HEAD_AUG:(h + 1) * Q_HEAD_AUG]) for h in range(HB_HEADS)]
    qabss = [_dot(qhs[h][:, :NOPE_DIM].astype(BF16), wuk_ref[h]) for h in range(HB_HEADS)]
    for h in range(HB_HEADS):
        qabs = qabss[h] * Q_SCALE
        qpe = jnp.where(lane < ROPE_DIM, rope(qhs[h][:, NOPE_DIM:]) * Q_SCALE, 0.0)
        if prompt:
            qabs_t = qabs.T.astype(BF16)
            qpe_t = qpe.T.astype(BF16)
            for s in range(tm // tq):
                qt_ref[s, :KV_LORA, h * tq:(h + 1) * tq] = qabs_t[:, s * tq:(s + 1) * tq]
                qt_ref[s, KV_LORA:, h * tq:(h + 1) * tq] = qpe_t[:, s * tq:(s + 1) * tq]
        else:
            for s in range(tm // tq):
                qabs_ref[s, h] = qabs[s * tq:(s + 1) * tq].astype(BF16)
                qpe_ref[s, h] = qpe[s * tq:(s + 1) * tq].astype(BF16)


def _inproj(x, cs, w_in, lbp, qg, wqb, kvg, wuk, prev_ckv, *, tm, tq, cb, prompt):
    n = x.shape[0]
    nt = n // tm
    n_prev = 0 if prev_ckv is None else prev_ckv.shape[0]
    if prev_ckv is None:
        prev_ckv = jnp.zeros((1, tm, KV_LORA), F32)
        prev_spec = _full(prev_ckv.shape)
    else:
        prev_spec = pl.BlockSpec((n_prev, tm, KV_LORA), lambda i: (0, i, 0))
    row = lambda w: pl.BlockSpec((tm, w), lambda i: (i, 0))
    heads = pl.BlockSpec((HA_HEADS, tm, HA_DK), lambda i: (0, i, 0))
    qblk = lambda w: pl.BlockSpec((tm // tq, HB_HEADS, tq, w), lambda i: (i, 0, 0, 0))
    head_major = jax.ShapeDtypeStruct((HA_HEADS, n, HA_DK), F32)
    out_shape = [
        head_major,
        head_major,
        head_major,
        head_major,
        jax.ShapeDtypeStruct((n, HA_W), F32),
        jax.ShapeDtypeStruct((n_prev + 1, n, KV_LORA), F32),
        jax.ShapeDtypeStruct((n, ROPE_DIM), F32),
    ]
    out_specs = [heads, heads, heads, heads, row(HA_W),
                 pl.BlockSpec((n_prev + 1, tm, KV_LORA), lambda i: (0, i, 0)), row(ROPE_DIM)]
    if prompt:
        out_shape += [jax.ShapeDtypeStruct((n // tq, QK_DIM, HB_HEADS * tq), BF16),
                      jax.ShapeDtypeStruct((nt, HA_HEADS, tm // cb, HA_DV, cb), BF16),
                      jax.ShapeDtypeStruct((n, QK_DIM), BF16),
                      jax.ShapeDtypeStruct((nt, KV_LORA, tm), BF16)]
        out_specs += [pl.BlockSpec((tm // tq, QK_DIM, HB_HEADS * tq), lambda i: (i, 0, 0)),
                      pl.BlockSpec((1, HA_HEADS, tm // cb, HA_DV, cb), lambda i: (i, 0, 0, 0, 0)),
                      row(QK_DIM),
                      pl.BlockSpec((1, KV_LORA, tm), lambda i: (i, 0, 0))]
    else:
        out_shape += [jax.ShapeDtypeStruct((n // tq, HB_HEADS, tq, KV_LORA), BF16),
                      jax.ShapeDtypeStruct((n // tq, HB_HEADS, tq, LANES), BF16)]
        out_specs += [qblk(KV_LORA), qblk(LANES)]
    return pl.pallas_call(
        functools.partial(_inproj_kernel, tq=tq, cb=cb, prompt=prompt, n_prev=n_prev),
        grid=(nt,),
        in_specs=[row(D_MODEL),
                  pl.BlockSpec((tm, LANES), lambda i: (i % (cs.shape[0] // tm), 0)),
                  _full(w_in.shape), _full(lbp.shape), _full(qg.shape),
                  _full(wqb.shape), _full(kvg.shape), _full(wuk.shape), prev_spec],
        out_specs=out_specs,
        out_shape=out_shape,
        compiler_params=_cparams(("parallel",)),
        name="inproj",
    )(x, cs, w_in, lbp, qg, wqb, kvg, wuk, prev_ckv)


def _hgrn_kernel(q_ref, k_ref, g_ref, v_ref, vt_ref, s0_ref, o_ref, sfin_ref, st_scr, b2_scr,
                 hide_ref, *, cb):
    _, bb, tb, _ = q_ref.shape
    nsub = cb // SUB_BLOCK
    ti = pl.program_id(1)

    @pl.when(ti == 0)
    def _():
        for b in range(bb):
            for h in range(HA_HEADS):
                st_scr[b, h] = s0_ref[b, h].T
        row8 = lax.broadcasted_iota(jnp.int32, (SUBLANES, LANES), 0)
        for s in range(SUBLANES):
            hide_ref[s] = jnp.where(row8 >= s, 0.0, NEG_INF)

    r_i = lax.broadcasted_iota(jnp.int32, (cb, cb), 0)
    c_i = lax.broadcasted_iota(jnp.int32, (cb, cb), 1)
    tril = (r_i >= c_i).astype(F32)

    chains = [(b, h) for b in range(bb) for h in range(HA_HEADS)]

    def matmul_part(ci):
        r0 = pl.multiple_of(ci * cb, cb)
        rows = pl.ds(r0, cb)

        def sub(ref, b, h, lo, size):
            return ref[h, b, pl.ds(pl.multiple_of(r0 + lo, SUB_BLOCK), size), :]

        for b, h in chains:
            bcum = jnp.dot(tril, g_ref[h, b, rows, :], precision=lax.Precision.HIGHEST,
                           preferred_element_type=F32)
            b2_scr[b * HA_HEADS + h] = bcum * LOG2E
        for b, h in chains:
            b2 = b2_scr[b * HA_HEADS + h]
            o_ref[h, b, rows, :] = _dot_nt((q_ref[h, b, rows, :] * jnp.exp2(b2)).astype(BF16),
                                           st_scr[b, h].astype(BF16))
        for j in range(nsub - 1):
            lo, hi = j * SUB_BLOCK, (j + 1) * SUB_BLOCK
            below = pl.ds(pl.multiple_of(r0 + hi, SUB_BLOCK), cb - hi)
            for b, h in chains:
                slot = b * HA_HEADS + h
                bnd = b2_scr[slot, hi - 1:hi, :]
                kd = (sub(k_ref, b, h, lo, SUB_BLOCK)
                      * jnp.exp2(bnd - b2_scr[slot, lo:hi, :])).astype(BF16)
                qe = (sub(q_ref, b, h, hi, cb - hi)
                      * jnp.exp2(b2_scr[slot, hi:, :] - bnd)).astype(BF16)
                a = _dot_nt(qe, kd).astype(BF16)
                o_ref[h, b, below, :] += _dot(a, sub(v_ref, b, h, lo, SUB_BLOCK).astype(BF16))
        for b, h in chains:
            slot = b * HA_HEADS + h
            b_last = b2_scr[slot, cb - 1:cb, :]
            kd_all = (k_ref[h, b, rows, :] * jnp.exp2(b_last - b2_scr[slot])).astype(BF16)
            st_scr[b, h] = st_scr[b, h] * jnp.exp2(b_last) + _dot(vt_ref[b, h, ci], kd_all)

    def pairwise_part(ci, b, h, j):
        slot = b * HA_HEADS + h
        r0 = pl.multiple_of(ci * cb, cb)
        lo = j * SUB_BLOCK
        mid = lo + SUBLANES
        top = pl.ds(pl.multiple_of(r0 + lo, SUBLANES), SUBLANES)
        bot = pl.ds(pl.multiple_of(r0 + mid, SUBLANES), SUBLANES)
        q_top, q_bot = q_ref[h, b, top, :], q_ref[h, b, bot, :]
        b_top, b_bot = b2_scr[slot, lo:mid, :], b2_scr[slot, mid:mid + SUBLANES, :]
        acc_top, acc_bot = o_ref[h, b, top, :], o_ref[h, b, bot, :]
        for s in range(SUB_BLOCK):
            row = pl.ds(r0 + lo + s, 1)
            bs = b2_scr[slot, lo + s:lo + s + 1, :]
            ks = k_ref[h, b, row, :]
            vs = v_ref[h, b, row, :]
            if s < SUBLANES:
                w = q_top * jnp.exp2(b_top - bs + hide_ref[s]) * ks
                acc_top = acc_top + jnp.sum(w, axis=-1, keepdims=True) * vs
                w = q_bot * jnp.exp2(b_bot - bs) * ks
            else:
                w = q_bot * jnp.exp2(b_bot - bs + hide_ref[s - SUBLANES]) * ks
            acc_bot = acc_bot + jnp.sum(w, axis=-1, keepdims=True) * vs
        o_ref[h, b, top, :] = acc_top
        o_ref[h, b, bot, :] = acc_bot

    def chunk(ci, carry):
        matmul_part(ci)
        for j in range(nsub):
            for b, h in chains:
                pairwise_part(ci, b, h, j)
        return carry

    lax.fori_loop(0, tb // cb, chunk, 0)

    @pl.when(ti == pl.num_programs(1) - 1)
    def _():
        for b in range(bb):
            for h in range(HA_HEADS):
                sfin_ref[b, h] = st_scr[b, h].T


def _hgrn(qs, kin, logf, v, vt, s0_all, layer, *, bb, tb, cb):
    _, bsz, t, _ = qs.shape
    blk = pl.BlockSpec((HA_HEADS, bb, tb, HA_DK), lambda i, j: (0, i, j, 0))
    st_blk = pl.BlockSpec((bb, HA_HEADS, HA_DK, HA_DV), lambda i, j: (i, 0, 0, 0))
    s0_blk = pl.BlockSpec((None, bb, HA_HEADS, HA_DK, HA_DV), lambda i, j: (layer, i, 0, 0, 0))
    vt_blk = pl.BlockSpec((bb, None, HA_HEADS, tb // cb, HA_DV, cb),
                          lambda i, j: (i, j, 0, 0, 0, 0))
    return pl.pallas_call(
        functools.partial(_hgrn_kernel, cb=cb),
        grid=(bsz // bb, t // tb),
        in_specs=[blk, blk, blk, blk, vt_blk, s0_blk],
        out_specs=[blk, st_blk],
        out_shape=[jax.ShapeDtypeStruct(qs.shape, F32),
                   jax.ShapeDtypeStruct(s0_all.shape[1:], F32)],
        scratch_shapes=[pltpu.VMEM((bb, HA_HEADS, HA_DV, HA_DK), F32),
                        pltpu.VMEM((bb * HA_HEADS, cb, HA_DK), F32),
                        pltpu.VMEM((SUBLANES, SUBLANES, LANES), F32)],
        compiler_params=_cparams(("parallel", "arbitrary")),
        name="hgrn",
    )(qs, kin, logf, v, vt, s0_all)


def _attn_prompt_kernel(qt_ref, kc_ref, kt_ref, o_ref, s0, s1, x0, x1, p0, p1, a0, a1, m_scr,
                        l_scr, acc_scr, *, tq, tk):
    qi = pl.program_id(1)
    jd = (qi * tq) // tk
    shift = CHUNK.bit_length() - 1
    head_cols = [slice(h * tq, (h + 1) * tq) for h in range(HB_HEADS)]

    def scores(j, s_dst, x_dst):
        keys = kc_ref[pl.ds(pl.multiple_of(j * tk, tk), tk), :]
        for cols in head_cols:
            s = _dot(keys, qt_ref[0, :, cols])
            s_dst[:, cols] = s
            x_dst[:, cols] = jnp.max(s, axis=0, keepdims=True)

    def softmax(j, s_src, x_src, p_dst, a_dst, masked):
        if masked:
            kpos = j * tk + lax.broadcasted_iota(jnp.int32, (tk, 1), 0)
            qpos = qi * tq + lax.broadcasted_iota(jnp.int32, (1, tq), 1)
            visible = (kpos >> shift) <= (qpos >> shift)
        for cols in head_cols:
            m_prev = m_scr[:, cols]
            if masked:
                s = jnp.where(visible, s_src[:, cols], NEG_INF)
                m_new = jnp.maximum(m_prev, jnp.max(s, axis=0, keepdims=True))
            else:
                s = s_src[:, cols]
                m_new = jnp.maximum(m_prev, x_src[:, cols])
            alpha = jnp.exp2(m_prev - m_new)
            p = jnp.exp2(s - m_new)
            l_scr[:, cols] = alpha * l_scr[:, cols] + jnp.sum(p, axis=0, keepdims=True)
            m_scr[:, cols] = m_new
            a_dst[:, cols] = alpha
            p_dst[:, cols] = p.astype(BF16)

    def values(j, p_src, a_src):
        for cols in head_cols:
            acc_scr[:, cols] = (a_src[:, cols] * acc_scr[:, cols]
                                + _dot(kt_ref[j], p_src[:, cols]))

    set0, set1 = (s0, x0, p0, a0), (s1, x1, p1, a1)

    def trip(j, cur, nxt):
        scores(j + 1, nxt[0], nxt[1])
        softmax(j, cur[0], cur[1], cur[2], cur[3], False)
        values(jnp.maximum(j - 1, 0), nxt[2], nxt[3])

    m_scr[...] = jnp.full(m_scr.shape, NEG_INF, F32)
    l_scr[...] = jnp.zeros(l_scr.shape, F32)
    acc_scr[...] = jnp.zeros(acc_scr.shape, F32)
    p1[...] = jnp.zeros(p1.shape, BF16)
    a1[...] = jnp.ones(a1.shape, F32)
    scores(0, s0, x0)

    def pair(jj, carry):
        trip(2 * jj, set0, set1)
        trip(2 * jj + 1, set1, set0)
        return carry

    lax.fori_loop(0, jd // 2, pair, 0)

    @pl.when(jd % 2 == 1)
    def _():
        trip(jd - 1, set0, set1)

    def drain(cur, prv):
        softmax(jd, cur[0], cur[1], cur[2], cur[3], True)
        values(jnp.maximum(jd - 1, 0), prv[2], prv[3])
        values(jd, cur[2], cur[3])

    pl.when(jd % 2 == 0)(functools.partial(drain, set0, set1))
    pl.when(jd % 2 == 1)(functools.partial(drain, set1, set0))

    for h in range(HB_HEADS):
        cols = slice(h * tq, (h + 1) * tq)
        o_ref[0, h] = (acc_scr[:, cols] * (1.0 / l_scr[:, cols])).T.astype(BF16)


def _attn_prompt(qt, kcat, ckvt, *, bsz, t, tq, tk):
    nq = t // tq
    nk = t // tk
    cols = HB_HEADS * tq
    single = pl.Buffered(1)
    return pl.pallas_call(
        functools.partial(_attn_prompt_kernel, tq=tq, tk=tk),
        grid=(bsz, nq),
        in_specs=[
            pl.BlockSpec((1, QK_DIM, cols), lambda b, i: (b * nq + i, 0, 0)),
            pl.BlockSpec((t, QK_DIM), lambda b, i: (b, 0), pipeline_mode=single),
            pl.BlockSpec((nk, KV_LORA, tk), lambda b, i: (b, 0, 0), pipeline_mode=single),
        ],
        out_specs=pl.BlockSpec((1, HB_HEADS, tq, KV_LORA), lambda b, i: (b * nq + i, 0, 0, 0)),
        out_shape=jax.ShapeDtypeStruct((bsz * nq, HB_HEADS, tq, KV_LORA), BF16),
        scratch_shapes=[pltpu.VMEM((tk, cols), F32), pltpu.VMEM((tk, cols), F32),
                        pltpu.VMEM((1, cols), F32), pltpu.VMEM((1, cols), F32),
                        pltpu.VMEM((tk, cols), BF16), pltpu.VMEM((tk, cols), BF16),
                        pltpu.VMEM((1, cols), F32), pltpu.VMEM((1, cols), F32),
                        pltpu.VMEM((1, cols), F32), pltpu.VMEM((1, cols), F32),
                        pltpu.VMEM((KV_LORA, cols), F32)],
        compiler_params=_cparams(("parallel", "arbitrary")),
        name="attn_prompt",
    )(qt, kcat, ckvt)


def _attn_sample_kernel(qa_ref, qp_ref, pckv_ref, pkpe_ref, nckv_ref, nkpe_ref, o_ref):
    _, _, ts, _ = qa_ref.shape
    rows = HB_HEADS * ts
    qa = qa_ref[0].reshape(rows, KV_LORA)
    qp = qp_ref[0].reshape(rows, LANES)[:, :ROPE_DIM]
    pckv = pckv_ref[0].astype(BF16)
    nckv = nckv_ref[...].astype(BF16)
    s_past = _dot_nt(qa, pckv) + _dot(qp, pkpe_ref[0].astype(BF16))
    s_new = _dot_nt(qa, nckv) + _dot_nt(qp, nkpe_ref[...].astype(BF16))
    m = jnp.maximum(jnp.max(s_past, axis=-1, keepdims=True), jnp.max(s_new, axis=-1, keepdims=True))
    p_past = jnp.exp2(s_past - m)
    p_new = jnp.exp2(s_new - m)
    l = jnp.sum(p_past, axis=-1, keepdims=True) + jnp.sum(p_new, axis=-1, keepdims=True)
    o = (_dot(p_past.astype(BF16), pckv) + _dot(p_new.astype(BF16), nckv)) / l
    o_ref[0] = o.astype(BF16).reshape(HB_HEADS, ts, KV_LORA)


def _attn_sample(qabs, qpe, past_ckv, past_kpe, layer, ckv, kpe, *, ts):
    _, bsz, past, _ = past_ckv.shape
    return pl.pallas_call(
        _attn_sample_kernel,
        grid=(bsz,),
        in_specs=[
            pl.BlockSpec((1, HB_HEADS, ts, KV_LORA), lambda b: (b, 0, 0, 0)),
            pl.BlockSpec((1, HB_HEADS, ts, LANES), lambda b: (b, 0, 0, 0)),
            pl.BlockSpec((None, 1, past, KV_LORA), lambda b: (layer, b, 0, 0)),
            pl.BlockSpec((None, 1, ROPE_DIM, past), lambda b: (layer, b, 0, 0)),
            pl.BlockSpec((None, ts, KV_LORA), lambda b: (layer, b, 0)),
            pl.BlockSpec((ts, ROPE_DIM), lambda b: (b, 0)),
        ],
        out_specs=pl.BlockSpec((1, HB_HEADS, ts, KV_LORA), lambda b: (b, 0, 0, 0)),
        out_shape=jax.ShapeDtypeStruct(qabs.shape, BF16),
        compiler_params=_cparams(("parallel",)),
        name="attn_sample",
    )(qabs, qpe, past_ckv, past_kpe, ckv, kpe)


def _gates(logits, bias):
    tm = logits.shape[0]
    lane = lax.broadcasted_iota(jnp.int32, (tm, LANES), 1)
    pos = lane % EXPERTS_PER_GROUP
    valid = lane < N_EXPERTS
    scores = jax.nn.sigmoid(logits)
    sel = jnp.where(valid, scores + bias, -jnp.inf)

    others = []
    for r in range(1, EXPERTS_PER_GROUP):
        others.append(jnp.where(pos >= r, pltpu.roll(sel, r, 1),
                                pltpu.roll(sel, LANES - (EXPERTS_PER_GROUP - r), 1)))
    a, b, c, d = sel, others[0], others[1], others[2]
    hi1, lo1 = jnp.maximum(a, b), jnp.minimum(a, b)
    hi2, lo2 = jnp.maximum(c, d), jnp.minimum(c, d)
    gscore = jnp.maximum(hi1, hi2) + jnp.maximum(jnp.minimum(hi1, hi2), jnp.maximum(lo1, lo2))
    gmax = jnp.max(gscore, axis=-1, keepdims=True)
    group = (lane // EXPERTS_PER_GROUP).astype(F32)
    gidx = jnp.min(jnp.where(gscore == gmax, group, float(LANES)), axis=-1, keepdims=True)

    rank = jnp.zeros((tm, LANES), jnp.int32)
    for r, o in enumerate(others, start=1):
        ahead = (o > sel) | ((o == sel) & (pos >= r))
        rank = rank + ahead.astype(jnp.int32)
    chosen = (group == gidx) & (rank < 2) & valid
    w = jnp.where(chosen, scores, 0.0)
    w = w / jnp.sum(w, axis=-1, keepdims=True)
    return jnp.where(lane == GROUP_LANE, gidx, w)


def _merge_kernel(x_ref, o_ref, gs_ref, olat_ref, wuv_ref, wo_ref, hg_ref, mg_ref,
                  g1_ref, b1_ref, wr_ref, rb_ref, x1g_ref, *, alpha):
    _, _, tq, _ = olat_ref.shape
    tm = x_ref.shape[0]
    hg = hg_ref[...]
    d_a = HA_HEADS * HA_DV
    halves = [(i * tm // 2, (i + 1) * tm // 2) for i in range(2)]

    def lat(h, lo, hi):
        if tq >= hi - lo:
            return olat_ref[lo // tq, h, lo % tq:lo % tq + hi - lo]
        return jnp.concatenate([olat_ref[s, h] for s in range(lo // tq, hi // tq)], axis=0)

    ob = [[_dot(lat(h, lo, hi), wuv_ref[h]) for h in range(HB_HEADS)] for lo, hi in halves]
    oa = [jnp.concatenate([_rms(o_ref[h, lo:hi, :], hg) * gs_ref[lo:hi, h * HA_DV:(h + 1) * HA_DV]
                           for h in range(HA_HEADS)], axis=-1) for lo, hi in halves]
    ob = [_rms(jnp.concatenate(o, axis=-1), mg_ref[...]) for o in ob]
    mix = [_dot(a.astype(BF16), wo_ref[:d_a, :]) + _dot(o.astype(BF16), wo_ref[d_a:, :])
           for a, o in zip(oa, ob)]
    x1 = [_layer_norm(alpha * x_ref[lo:hi, :] + m, g1_ref[...], b1_ref[...])
          for (lo, hi), m in zip(halves, mix)]
    logits = [_dot(x.astype(BF16), wr_ref[...]) for x in x1]
    for (lo, hi), x, lg in zip(halves, x1, logits):
        for c in range(X_PIECES):
            x1g_ref[c, lo:hi, :] = x[:, c * SC_ROW:(c + 1) * SC_ROW]
        x1g_ref[X_PIECES, lo:hi, :LANES] = _gates(lg, rb_ref[...])
        x1g_ref[X_PIECES, lo:hi, LANES:] = jnp.zeros((hi - lo, SC_ROW - LANES), F32)


def _merge(x, o_raw, gs, olat, wuv, wo, hg, mg, g1, b1, wr, rb, *, tm, tq, alpha):
    n = x.shape[0]
    row = lambda w: pl.BlockSpec((tm, w), lambda i: (i, 0))
    return pl.pallas_call(
        functools.partial(_merge_kernel, alpha=alpha),
        grid=(n // tm,),
        in_specs=[row(D_MODEL), pl.BlockSpec((HA_HEADS, tm, HA_DV), lambda i: (0, i, 0)), row(HA_W),
                  pl.BlockSpec((tm // tq, HB_HEADS, tq, KV_LORA), lambda i: (i, 0, 0, 0)),
                  _full(wuv.shape), _full(wo.shape), _full(hg.shape), _full(mg.shape),
                  _full(g1.shape), _full(b1.shape), _full(wr.shape), _full(rb.shape)],
        out_specs=pl.BlockSpec((X_PIECES + 1, tm, SC_ROW), lambda i: (0, i, 0)),
        out_shape=jax.ShapeDtypeStruct((X_PIECES + 1, n, SC_ROW), F32),
        compiler_params=_cparams(("parallel",)),
        name="merge",
    )(x, o_raw, gs, olat, wuv, wo, hg, mg, g1, b1, wr, rb)


def _sc_scatter_rows(src, dst, m):
    parts, n, _ = src.shape
    windows = dst.shape[0] // SC_WINDOW
    src_windows = n // SC_WINDOW
    pieces = src.reshape(parts * n, SC_ROW)
    piece_dst = (jnp.arange(parts, dtype=jnp.int32)[:, None] * m + dst[None, :]).reshape(1, -1)
    mesh = plsc.VectorSubcoreMesh(core_axis_name="core", subcore_axis_name="subcore")

    @functools.partial(pl.kernel, out_type=jax.ShapeDtypeStruct((parts * m, SC_ROW), src.dtype),
                       mesh=mesh)
    def scatter(x_hbm, i_hbm, o_hbm):
        def body(x_vmem, i_vmem):
            pltpu.sync_copy(x_vmem, o_hbm.at[i_vmem.at[0]])

        pltpu.emit_pipeline(
            body,
            grid=(parts * windows,),
            in_specs=[pl.BlockSpec((SC_WINDOW, SC_ROW),
                                   lambda i: ((i // windows) * src_windows
                                              + (i % windows) % src_windows, 0)),
                      pl.BlockSpec((1, SC_WINDOW), lambda i: (0, i))],
            out_specs=[],
            core_axis_name=("core", "subcore"),
            dimension_semantics=(pltpu.PARALLEL,),
        )(x_hbm, i_hbm)

    return scatter(pieces, piece_dst).reshape(parts, m, SC_ROW)


def _sc_gather_rows(src, idx):
    parts, n, _ = src.shape
    m = idx.shape[0] * parts
    pieces = src.reshape(parts * n, SC_ROW)
    piece_idx = (jnp.arange(parts, dtype=jnp.int32)[:, None] * n + idx[None, :]).reshape(1, m)
    mesh = plsc.VectorSubcoreMesh(core_axis_name="core", subcore_axis_name="subcore")

    @functools.partial(pl.kernel, out_type=jax.ShapeDtypeStruct((m, SC_ROW), src.dtype), mesh=mesh)
    def gather(x_hbm, i_hbm, o_hbm):
        def body(i_vmem, o_vmem):
            pltpu.sync_copy(x_hbm.at[i_vmem.at[0]], o_vmem)

        pltpu.emit_pipeline(
            body,
            grid=(m // SC_WINDOW,),
            in_specs=[pl.BlockSpec((1, SC_WINDOW), lambda i: (0, i))],
            out_specs=[pl.BlockSpec((SC_WINDOW, SC_ROW), lambda i: (i, 0))],
            core_axis_name=("core", "subcore"),
            dimension_semantics=(pltpu.PARALLEL,),
        )(i_hbm, o_hbm)

    return gather(pieces, piece_idx).reshape(parts, idx.shape[0], SC_ROW)


def _moe_kernel(tg_ref, xs_ref, wg_ref, wu_ref, wd_ref, y_ref, acc_scr):
    tm = xs_ref.shape[1]
    per_step = wg_ref.shape[0]
    step = pl.program_id(1)
    first_expert = tg_ref[pl.program_id(0)] * EXPERTS_PER_GROUP + step * per_step
    xb = jnp.concatenate([xs_ref[c] for c in range(X_PIECES)], axis=1).astype(BF16)
    gates = xs_ref[X_PIECES, :, :LANES]
    lane = lax.broadcasted_iota(jnp.int32, (tm, LANES), 1)
    acc = jnp.zeros((tm, D_MODEL), F32)
    for k in range(per_step):
        hmid = (_silu(_dot(xb, wg_ref[k].astype(BF16))) * _dot(xb, wu_ref[k].astype(BF16)))
        gcol = jnp.sum(jnp.where(lane == first_expert + k, gates, 0.0), axis=-1, keepdims=True)
        acc = acc + gcol * _dot(hmid.astype(BF16), wd_ref[k].astype(BF16))

    @pl.when(step == 0)
    def _():
        acc_scr[...] = acc

    @pl.when(step > 0)
    def _():
        acc_scr[...] += acc

    @pl.when(step == pl.num_programs(1) - 1)
    def _():
        for c in range(D_MODEL // SC_ROW):
            y_ref[c] = acc_scr[:, c * SC_ROW:(c + 1) * SC_ROW]


def _moe_sorted(tile_group, xs, wg, wu, wd, layer, *, tm, per_step, weight_buffers):
    m = xs.shape[1]
    steps = EXPERTS_PER_GROUP // per_step
    mode = dict(pipeline_mode=pl.Buffered(weight_buffers))
    wspec = lambda shape: pl.BlockSpec((None, per_step) + shape,
                                       lambda i, k, tg: (layer, tg[i] * steps + k, 0, 0), **mode)
    return pl.pallas_call(
        _moe_kernel,
        grid_spec=pltpu.PrefetchScalarGridSpec(
            num_scalar_prefetch=1,
            grid=(m // tm, steps),
            in_specs=[pl.BlockSpec((X_PIECES + 1, tm, SC_ROW), lambda i, k, tg: (0, i, 0)),
                      wspec((D_MODEL, D_EXPERT)), wspec((D_MODEL, D_EXPERT)),
                      wspec((D_EXPERT, D_MODEL))],
            out_specs=pl.BlockSpec((D_MODEL // SC_ROW, tm, SC_ROW), lambda i, k, tg: (0, i, 0)),
            scratch_shapes=[pltpu.VMEM((tm, D_MODEL), F32)],
        ),
        out_shape=jax.ShapeDtypeStruct((D_MODEL // SC_ROW, m, SC_ROW), F32),
        compiler_params=_cparams(("arbitrary", "arbitrary")),
        name="moe",
    )(tile_group, xs, wg, wu, wd)


def _route(x1g, *, tm):
    n = x1g.shape[1]
    n_tiles = n // tm + N_GROUPS - 1
    m = n_tiles * tm
    gidx = x1g[X_PIECES, :, GROUP_LANE].astype(jnp.int32)
    onehot = (gidx[:, None] == jnp.arange(N_GROUPS, dtype=jnp.int32)[None, :]).astype(jnp.int32)
    counts = jnp.sum(onehot, axis=0)
    tiles = (counts + tm - 1) // tm
    tile_end = jnp.cumsum(tiles)
    tile_start = tile_end - tiles
    rank = jnp.sum((jnp.cumsum(onehot, axis=0) - onehot) * onehot, axis=1)
    pos = jnp.take(tile_start, gidx) * tm + rank
    pad_start = tile_start * tm + counts
    pad_end = (tile_end * tm).at[N_GROUPS - 1].set(m)
    pad_cum = jnp.cumsum(pad_end - pad_start)
    k = jnp.arange(m - n, dtype=jnp.int32)
    pg = jnp.sum(k[:, None] >= pad_cum[None, :], axis=1)
    pad_rows = jnp.take(pad_start, pg) + k - jnp.take(pad_cum - (pad_end - pad_start), pg)
    dst = jnp.concatenate([pos, pad_rows]).astype(jnp.int32)
    tile_group = jnp.sum(jnp.arange(n_tiles, dtype=jnp.int32)[:, None] >= tile_end[None, :], axis=1)
    return dst, jnp.minimum(tile_group, N_GROUPS - 1).astype(jnp.int32), m


def _post_kernel(x1_ref, f_ref, p_ref, g2_ref, b2_ref, wpg_ref, wp_ref, y_ref, *, alpha):
    tm = x1_ref.shape[1]
    halves = [(i * tm // 2, (i + 1) * tm // 2) for i in range(2)]
    emb = [_dot(p_ref[lo:hi, :].astype(BF16), wp_ref[...]) for lo, hi in halves]

    def rows(ref, lo, hi):
        return jnp.concatenate([ref[c, lo:hi, :] for c in range(X_PIECES)], axis=1)

    x2 = [_layer_norm(alpha * rows(x1_ref, lo, hi) + rows(f_ref, lo, hi), g2_ref[...], b2_ref[...])
          for lo, hi in halves]
    gate = [_dot(x.astype(BF16), wpg_ref[...]) for x in x2]
    for (lo, hi), x, g, e in zip(halves, x2, gate, emb):
        y_ref[lo:hi, :] = x + jax.nn.sigmoid(g) * e


def _post(x1g, ffn, p_all, layer, g2, b2, wpg, wp, *, tm, alpha):
    _, n, _ = ffn.shape
    row = lambda w: pl.BlockSpec((tm, w), lambda i: (i, 0))
    pieces = lambda a: pl.BlockSpec((a.shape[0], tm, SC_ROW), lambda i: (0, i, 0))
    return pl.pallas_call(
        functools.partial(_post_kernel, alpha=alpha),
        grid=(n // tm,),
        in_specs=[pieces(x1g), pieces(ffn),
                  pl.BlockSpec((None, tm, PLE_DIM), lambda i: (layer, i, 0)),
                  _full(g2.shape), _full(b2.shape), _full(wpg.shape), _full(wp.shape)],
        out_specs=row(D_MODEL),
        out_shape=jax.ShapeDtypeStruct((n, D_MODEL), F32),
        compiler_params=_cparams(("parallel",)),
        name="post",
    )(x1g, ffn, p_all, g2, b2, wpg, wp)


def _rot_cols(w):
    half = ROPE_DIM // 2
    return jnp.concatenate([-w[..., half:], w[..., :half]], axis=-1)


def _prep_layer(l, w_in, w_qb, w_kvb, w_o, w_gate, w_up, w_down, w_ple, w_ple_gate):
    win = w_in[l]
    win_aug = jnp.concatenate([win, _rot_cols(win[:, COL_KPE:])], axis=-1).astype(BF16)
    wqb = w_qb[l].reshape(Q_LORA, HB_HEADS, NOPE_DIM + ROPE_DIM)
    wqb_aug = jnp.concatenate([wqb, _rot_cols(wqb[..., NOPE_DIM:])], axis=-1)
    wqb_aug = wqb_aug.reshape(Q_LORA, HB_HEADS * Q_HEAD_AUG).astype(BF16)
    wkvb = w_kvb[l].reshape(KV_LORA, HB_HEADS, NOPE_DIM + V_DIM)
    wuk_t = jnp.transpose(wkvb[..., :NOPE_DIM], (1, 2, 0)).astype(BF16)
    wuv = jnp.transpose(wkvb[..., NOPE_DIM:], (1, 0, 2)).astype(BF16)
    return dict(win=win_aug, wqb=wqb_aug, wuk=wuk_t, wuv=wuv, wo=w_o[l].astype(BF16),
                wg=w_gate, wu=w_up, wd=w_down,
                wp=w_ple[l].astype(BF16), wpg=w_ple_gate[l].astype(BF16))


def _rope_table(pos):
    inv = ROPE_THETA ** (-np.arange(0, ROPE_DIM, 2, dtype=np.float64) / ROPE_DIM)
    ang = pos.astype(np.float64)[:, None] * inv[None, :]
    cos, sin = np.cos(ang), np.sin(ang)
    return np.concatenate([cos, cos, sin, sin], axis=-1).astype(np.float32)


def _tiles(n, t, prompt):
    if prompt:
        tm = min(512, t)
        return dict(tm=tm, tq=tm, tk=tm, tb=tm, cb=min(128, t), bb=2, tmoe=min(512, n),
                    experts_per_step=EXPERTS_PER_GROUP, weight_buffers=1)
    return dict(tm=min(512, n), tq=t, tk=None, tb=t, cb=t, bb=2, tmoe=min(256, n),
                experts_per_step=EXPERTS_PER_GROUP, weight_buffers=2)


def _layer(x, p_all, layer, cs, s0_all, past, prev_ckv, lbp, prm, small, *, alpha, prompt):
    bsz, t, _ = x.shape
    n = bsz * t
    tl = _tiles(n, t, prompt)
    x2d = x.reshape(n, D_MODEL)
    tm, tb, cb = tl["tm"], tl["tb"], tl["cb"]
    outs = _inproj(x2d, cs, prm["win"], lbp, small["qg"], prm["wqb"], small["kvg"], prm["wuk"],
                   prev_ckv, tm=tm, tq=tl["tq"], cb=cb, prompt=prompt)
    qs, kin, logf, v, gs, ckv_all, kpe = outs[:7]
    r4 = lambda a: a.reshape(HA_HEADS, bsz, t, HA_DK)
    if prompt:
        qt, vt, kcat, ckvt = outs[7:]
        vt = vt.reshape(bsz, t // tb, HA_HEADS, tb // cb, HA_DV, cb)
    else:
        qabs, qpe = outs[7:]
        vt = jnp.transpose(r4(v), (1, 0, 3, 2)).astype(BF16)
        vt = vt.reshape(bsz, 1, HA_HEADS, 1, HA_DV, t)
    o_raw, s_new = _hgrn(r4(qs), r4(kin), r4(logf), r4(v), vt, s0_all, layer if not prompt else 0,
                         bb=tl["bb"], tb=tb, cb=cb)
    if prompt:
        olat = _attn_prompt(qt, kcat, ckvt, bsz=bsz, t=t, tq=tl["tq"], tk=tl["tk"])
    else:
        olat = _attn_sample(qabs, qpe, past[0], past[1], layer, ckv_all, kpe, ts=t)
    x1g = _merge(x2d, o_raw.reshape(HA_HEADS, n, HA_DV), gs, olat, prm["wuv"], prm["wo"],
                 small["hg"], small["mg"], small["g1"], small["b1"],
                 small["wr"], small["rb"], tm=tm, tq=tl["tq"], alpha=alpha)
    tmoe = tl["tmoe"]
    dst, tile_group, m_sorted = _route(x1g, tm=tmoe)
    xs = _sc_scatter_rows(x1g, dst, m_sorted)
    ys = _moe_sorted(tile_group, xs, prm["wg"], prm["wu"], prm["wd"], layer, tm=tmoe,
                     per_step=tl["experts_per_step"], weight_buffers=tl["weight_buffers"])
    ffn = _sc_gather_rows(ys, dst[:n])
    y = _post(x1g, ffn, p_all.reshape(-1, n, PLE_DIM), layer, small["g2"], small["b2"],
              prm["wpg"], prm["wp"], tm=tm, alpha=alpha)
    return y.reshape(bsz, t, D_MODEL), s_new, ckv_all, kpe.reshape(bsz, t, ROPE_DIM)


def kernel(x_prompt, x_sample, p_prompt, p_sample, state_hgrn, cache_ckv, cache_kpe, w_in,
           lb_logits, hgrn_norm_g, q_norm_g, w_qb, kv_norm_g, w_kvb, mla_norm_g, w_o,
           ln1_g, ln1_b, ln2_g, ln2_b, w_router, router_bias, w_gate, w_up, w_down,
           w_ple, w_ple_gate):
    depth = w_in.shape[0]
    alpha = (2 * depth) ** 0.25
    bp, tp, _ = x_prompt.shape
    bs, ts, _ = x_sample.shape
    past = cache_ckv.shape[2]

    sm = jax.nn.softmax(lb_logits.astype(F32), axis=0)
    lb_all = jnp.maximum(jnp.cumsum(sm, axis=0) - sm[0:1], 0.0)
    lbp_all = jnp.stack([jnp.log(lb_all), jnp.log1p(-lb_all), 1.0 - lb_all], axis=1)

    cs_p = jnp.asarray(_rope_table(np.arange(tp)))
    tm_s = _tiles(bs * ts, ts, False)["tm"]
    cs_s = jnp.asarray(np.tile(_rope_table(past + np.arange(ts)), (tm_s // ts, 1)))
    wr = jnp.pad(w_router, ((0, 0), (0, LANES - N_EXPERTS))).astype(BF16)
    rb = jnp.pad(router_bias.astype(F32), (0, LANES - N_EXPERTS)).reshape(1, LANES)
    s0_p = jnp.zeros((1, bp, HA_HEADS, HA_DK, HA_DV), F32)
    cache_kpe_t = jnp.swapaxes(cache_kpe, 2, 3)

    yp, ys = x_prompt, x_sample
    res = [[] for _ in range(4)]
    cp = cs_ = None
    for l in range(depth):
        prm = _prep_layer(l, w_in, w_qb, w_kvb, w_o, w_gate, w_up, w_down, w_ple, w_ple_gate)
        row = lambda a: a[l].reshape(1, -1).astype(F32)
        small = dict(qg=row(q_norm_g), kvg=row(kv_norm_g), hg=row(hgrn_norm_g), mg=row(mla_norm_g),
                     g1=row(ln1_g), b1=row(ln1_b), g2=row(ln2_g), b2=row(ln2_b), wr=wr, rb=rb)
        yp, sp, cp, kp = _layer(yp, p_prompt, l, cs_p, s0_p, None, cp, lbp_all[l], prm, small,
                                alpha=alpha, prompt=True)
        ys, ss, cs_, ks = _layer(ys, p_sample, l, cs_s, state_hgrn, (cache_ckv, cache_kpe_t), cs_,
                                 lbp_all[l], prm, small, alpha=alpha, prompt=False)
        for lst, a in zip(res, (sp, kp, ss, ks)):
            lst.append(a)
    sp, kp, ss, ks = (jnp.stack(a) for a in res)
    return (yp, ys, sp, cp.reshape(depth, bp, tp, KV_LORA), kp, ss,
            cs_.reshape(depth, bs, ts, KV_LORA), ks)
```

```python
import functools

import jax
import jax.numpy as jnp
import numpy as np
from jax import lax
from jax.experimental import pallas as pl
from jax.experimental.pallas import tpu as pltpu
from jax.experimental.pallas import tpu_sc as plsc

F32 = jnp.float32
BF16 = jnp.bfloat16

D_MODEL = 1024
HA_HEADS = 4
HA_DK = 128
HA_DV = 128
HB_HEADS = 4
Q_LORA = 384
KV_LORA = 256
NOPE_DIM = 128
ROPE_DIM = 64
V_DIM = 128
ROPE_THETA = 10000.0
MLA_SCALE = (NOPE_DIM + ROPE_DIM) ** -0.5
LOG2E = 1.4426950408889634
Q_SCALE = MLA_SCALE * LOG2E
CHUNK = 64
N_EXPERTS = 16
N_GROUPS = 4
EXPERTS_PER_GROUP = N_EXPERTS // N_GROUPS
D_EXPERT = 512
PLE_DIM = 256
NEG_INF = -1e30

HA_W = HA_HEADS * HA_DK
COL_CQ = 4 * HA_W
COL_CKV = COL_CQ + Q_LORA
COL_KPE = COL_CKV + KV_LORA
D_IN_AUG = COL_KPE + 2 * ROPE_DIM
Q_HEAD_AUG = NOPE_DIM + 2 * ROPE_DIM

LANES = 128
SUBLANES = 8
QK_DIM = KV_LORA + LANES
GROUP_LANE = N_EXPERTS
SC_WINDOW = 128
SC_ROW = 256
X_PIECES = D_MODEL // SC_ROW
SUB_BLOCK = 16
VMEM_LIMIT = 56 * 1024 * 1024


def _cparams(sem, vmem=VMEM_LIMIT):
    return pltpu.CompilerParams(dimension_semantics=sem, vmem_limit_bytes=vmem)


def _dot(a, b):
    return jnp.dot(a, b, preferred_element_type=F32)


def _dot_nt(a, b):
    return lax.dot_general(a, b, (((1,), (1,)), ((), ())), preferred_element_type=F32)


def _rms(x, g, eps=1e-6):
    return x * lax.rsqrt(jnp.mean(x * x, axis=-1, keepdims=True) + eps) * g


def _layer_norm(x, g, b, eps=1e-5):
    mu = jnp.mean(x, axis=-1, keepdims=True)
    xc = x - mu
    var = jnp.mean(xc * xc, axis=-1, keepdims=True)
    return xc * lax.rsqrt(var + eps) * g + b


def _silu(x):
    return x * jax.nn.sigmoid(x)


def _full(shape):
    nd = len(shape)
    return pl.BlockSpec(shape, lambda *_: (0,) * nd)


def _inproj_kernel(x_ref, cs_ref, win_ref, lb_ref, qg_ref, wqb_ref, kvg_ref, wuk_ref, prev_ref,
                   qs_ref, kin_ref, logf_ref, v_ref, gs_ref, ckv_ref, kpe_ref, *rest,
                   tq, cb, prompt, n_prev):
    tm = x_ref.shape[0]
    x = x_ref[...].astype(BF16)

    def sect(lo, hi):
        return _dot(x, win_ref[:, lo:hi])

    def put_heads(ref, val):
        for h in range(HA_HEADS):
            ref[h] = val[:, h * HA_DK:(h + 1) * HA_DK]

    cqn = _rms(sect(COL_CQ, COL_CKV), qg_ref[...]).astype(BF16)

    put_heads(qs_ref, _silu(sect(0, HA_W)))
    fa = sect(HA_W, 2 * HA_W)
    log_lb = lb_ref[0:1, :]
    log1m_lb = lb_ref[1:2, :]
    one_m_lb = lb_ref[2:3, :]
    e = jnp.exp(-jnp.abs(fa))
    r = 1.0 / (1.0 + e)
    c = log1m_lb + jnp.minimum(fa, 0.0) + jnp.log(r)
    put_heads(logf_ref, jnp.maximum(log_lb, c) + jnp.log(1.0 + jnp.exp(-jnp.abs(log_lb - c))))
    put_heads(kin_ref, one_m_lb * jnp.where(fa >= 0.0, e * r, r))
    v = sect(2 * HA_W, 3 * HA_W)
    put_heads(v_ref, v)
    gs_ref[...] = _silu(sect(3 * HA_W, 4 * HA_W))

    cs = cs_ref[...]

    def rope(t):
        prod = t * cs
        return prod + pltpu.roll(prod, ROPE_DIM, 1)

    ckv = _rms(sect(COL_CKV, COL_KPE), kvg_ref[...])
    for l in range(n_prev):
        ckv_ref[l] = prev_ref[l]
    ckv_ref[n_prev] = ckv
    kpe2 = rope(sect(COL_KPE, D_IN_AUG))
    kpe_ref[...] = kpe2[:, :ROPE_DIM]
    if prompt:
        qt_ref, vt_ref, kcat_ref, ckvt_ref = rest
        for h in range(HA_HEADS):
            for ci in range(tm // cb):
                vt_ref[0, h, ci] = v[ci * cb:(ci + 1) * cb, h * HA_DV:(h + 1) * HA_DV].T.astype(BF16)
        kcat_ref[:, :KV_LORA] = ckv.astype(BF16)
        kcat_ref[:, KV_LORA:] = kpe2.astype(BF16)
        ckvt_ref[0] = ckv.T.astype(BF16)
    else:
        qabs_ref, qpe_ref = rest

    lane = lax.broadcasted_iota(jnp.int32, (tm, LANES), 1)
    qhs = [_dot(cqn, wqb_ref[:, h * Q_HEAD_AUG:(h + 1) * Q_HEAD_AUG]) for h in range(HB_HEADS)]
    qabss = [_dot(qhs[h][:, :NOPE_DIM].astype(BF16), wuk_ref[h]) for h in range(HB_HEADS)]
    for h in range(HB_HEADS):
        qabs = qabss[h] * Q_SCALE
        qpe = jnp.where(lane < ROPE_DIM, rope(qhs[h][:, NOPE_DIM:]) * Q_SCALE, 0.0)
        if prompt:
            qabs_t = qabs.T.astype(BF16)
            qpe_t = qpe.T.astype(BF16)
            for s in range(tm // tq):
                qt_ref[s, :KV_LORA, h * tq:(h + 1) * tq] = qabs_t[:, s * tq:(s + 1) * tq]
                qt_ref[s, KV_LORA:, h * tq:(h + 1) * tq] = qpe_t[:, s * tq:(s + 1) * tq]
        else:
            for s in range(tm // tq):
                qabs_ref[s, h] = qabs[s * tq:(s + 1) * tq].astype(BF16)
                qpe_ref[s, h] = qpe[s * tq:(s + 1) * tq].astype(BF16)


def _inproj(x, cs, w_in, lbp, qg, wqb, kvg, wuk, prev_ckv, *, tm, tq, cb, prompt):
    n = x.shape[0]
    nt = n // tm
    n_prev = 0 if prev_ckv is None else prev_ckv.shape[0]
    if prev_ckv is None:
        prev_ckv = jnp.zeros((1, tm, KV_LORA), F32)
        prev_spec = _full(prev_ckv.shape)
    else:
        prev_spec = pl.BlockSpec((n_prev, tm, KV_LORA), lambda i: (0, i, 0))
    row = lambda w: pl.BlockSpec((tm, w), lambda i: (i, 0))
    heads = pl.BlockSpec((HA_HEADS, tm, HA_DK), lambda i: (0, i, 0))
    qblk = lambda w: pl.BlockSpec((tm // tq, HB_HEADS, tq, w), lambda i: (i, 0, 0, 0))
    head_major = jax.ShapeDtypeStruct((HA_HEADS, n, HA_DK), F32)
    out_shape = [
        head_major,
        head_major,
        head_major,
        head_major,
        jax.ShapeDtypeStruct((n, HA_W), F32),
        jax.ShapeDtypeStruct((n_prev + 1, n, KV_LORA), F32),
        jax.ShapeDtypeStruct((n, ROPE_DIM), F32),
    ]
    out_specs = [heads, heads, heads, heads, row(HA_W),
                 pl.BlockSpec((n_prev + 1, tm, KV_LORA), lambda i: (0, i, 0)), row(ROPE_DIM)]
    if prompt:
        out_shape += [jax.ShapeDtypeStruct((n // tq, QK_DIM, HB_HEADS * tq), BF16),
                      jax.ShapeDtypeStruct((nt, HA_HEADS, tm // cb, HA_DV, cb), BF16),
                      jax.ShapeDtypeStruct((n, QK_DIM), BF16),
                      jax.ShapeDtypeStruct((nt, KV_LORA, tm), BF16)]
        out_specs += [pl.BlockSpec((tm // tq, QK_DIM, HB_HEADS * tq), lambda i: (i, 0, 0)),
                      pl.BlockSpec((1, HA_HEADS, tm // cb, HA_DV, cb), lambda i: (i, 0, 0, 0, 0)),
                      row(QK_DIM),
                      pl.BlockSpec((1, KV_LORA, tm), lambda i: (i, 0, 0))]
    else:
        out_shape += [jax.ShapeDtypeStruct((n // tq, HB_HEADS, tq, KV_LORA), BF16),
                      jax.ShapeDtypeStruct((n // tq, HB_HEADS, tq, LANES), BF16)]
        out_specs += [qblk(KV_LORA), qblk(LANES)]
    return pl.pallas_call(
        functools.partial(_inproj_kernel, tq=tq, cb=cb, prompt=prompt, n_prev=n_prev),
        grid=(nt,),
        in_specs=[row(D_MODEL),
                  pl.BlockSpec((tm, LANES), lambda i: (i % (cs.shape[0] // tm), 0)),
                  _full(w_in.shape), _full(lbp.shape), _full(qg.shape),
                  _full(wqb.shape), _full(kvg.shape), _full(wuk.shape), prev_spec],
        out_specs=out_specs,
        out_shape=out_shape,
        compiler_params=_cparams(("parallel",)),
        name="inproj",
    )(x, cs, w_in, lbp, qg, wqb, kvg, wuk, prev_ckv)


def _hgrn_kernel(q_ref, k_ref, g_ref, v_ref, vt_ref, s0_ref, o_ref, sfin_ref, st_scr, b2_scr,
                 hide_ref, *, cb):
    _, bb, tb, _ = q_ref.shape
    nsub = cb // SUB_BLOCK
    ti = pl.program_id(1)

    @pl.when(ti == 0)
    def _():
        for b in range(bb):
            for h in range(HA_HEADS):
                st_scr[b, h] = s0_ref[b, h].T
        row8 = lax.broadcasted_iota(jnp.int32, (SUBLANES, LANES), 0)
        for s in range(SUBLANES):
            hide_ref[s] = jnp.where(row8 >= s, 0.0, NEG_INF)

    r_i = lax.broadcasted_iota(jnp.int32, (cb, cb), 0)
    c_i = lax.broadcasted_iota(jnp.int32, (cb, cb), 1)
    tril = (r_i >= c_i).astype(F32)

    chains = [(b, h) for b in range(bb) for h in range(HA_HEADS)]

    def matmul_part(ci):
        r0 = pl.multiple_of(ci * cb, cb)
        rows = pl.ds(r0, cb)

        def sub(ref, b, h, lo, size):
            return ref[h, b, pl.ds(pl.multiple_of(r0 + lo, SUB_BLOCK), size), :]

        for b, h in chains:
            bcum = jnp.dot(tril, g_ref[h, b, rows, :], precision=lax.Precision.HIGHEST,
                           preferred_element_type=F32)
            b2_scr[b * HA_HEADS + h] = bcum * LOG2E
        for b, h in chains:
            b2 = b2_scr[b * HA_HEADS + h]
            o_ref[h, b, rows, :] = _dot_nt((q_ref[h, b, rows, :] * jnp.exp2(b2)).astype(BF16),
                                           st_scr[b, h].astype(BF16))
        for j in range(nsub - 1):
            lo, hi = j * SUB_BLOCK, (j + 1) * SUB_BLOCK
            below = pl.ds(pl.multiple_of(r0 + hi, SUB_BLOCK), cb - hi)
            for b, h in chains:
                slot = b * HA_HEADS + h
                bnd = b2_scr[slot, hi - 1:hi, :]
                kd = (sub(k_ref, b, h, lo, SUB_BLOCK)
                      * jnp.exp2(bnd - b2_scr[slot, lo:hi, :])).astype(BF16)
                qe = (sub(q_ref, b, h, hi, cb - hi)
                      * jnp.exp2(b2_scr[slot, hi:, :] - bnd)).astype(BF16)
                a = _dot_nt(qe, kd).astype(BF16)
                o_ref[h, b, below, :] += _dot(a, sub(v_ref, b, h, lo, SUB_BLOCK).astype(BF16))
        for b, h in chains:
            slot = b * HA_HEADS + h
            b_last = b2_scr[slot, cb - 1:cb, :]
            kd_all = (k_ref[h, b, rows, :] * jnp.exp2(b_last - b2_scr[slot])).astype(BF16)
            st_scr[b, h] = st_scr[b, h] * jnp.exp2(b_last) + _dot(vt_ref[b, h, ci], kd_all)

    def pairwise_part(ci, b, h, j):
        slot = b * HA_HEADS + h
        r0 = pl.multiple_of(ci * cb, cb)
        lo = j * SUB_BLOCK
        mid = lo + SUBLANES
        top = pl.ds(pl.multiple_of(r0 + lo, SUBLANES), SUBLANES)
        bot = pl.ds(pl.multiple_of(r0 + mid, SUBLANES), SUBLANES)
        q_top, q_bot = q_ref[h, b, top, :], q_ref[h, b, bot, :]
        b_top, b_bot = b2_scr[slot, lo:mid, :], b2_scr[slot, mid:mid + SUBLANES, :]
        acc_top, acc_bot = o_ref[h, b, top, :], o_ref[h, b, bot, :]
        for s in range(SUB_BLOCK):
            row = pl.ds(r0 + lo + s, 1)
            bs = b2_scr[slot, lo + s:lo + s + 1, :]
            ks = k_ref[h, b, row, :]
            vs = v_ref[h, b, row, :]
            if s < SUBLANES:
                w = q_top * jnp.exp2(b_top - bs + hide_ref[s]) * ks
                acc_top = acc_top + jnp.sum(w, axis=-1, keepdims=True) * vs
                w = q_bot * jnp.exp2(b_bot - bs) * ks
            else:
                w = q_bot * jnp.exp2(b_bot - bs + hide_ref[s - SUBLANES]) * ks
            acc_bot = acc_bot + jnp.sum(w, axis=-1, keepdims=True) * vs
        o_ref[h, b, top, :] = acc_top
        o_ref[h, b, bot, :] = acc_bot

    def chunk(ci, carry):
        matmul_part(ci)
        for j in range(nsub):
            for b, h in chains:
                pairwise_part(ci, b, h, j)
        return carry

    lax.fori_loop(0, tb // cb, chunk, 0)

    @pl.when(ti == pl.num_programs(1) - 1)
    def _():
        for b in range(bb):
            for h in range(HA_HEADS):
                sfin_ref[b, h] = st_scr[b, h].T


def _hgrn(qs, kin, logf, v, vt, s0_all, layer, *, bb, tb, cb):
    _, bsz, t, _ = qs.shape
    blk = pl.BlockSpec((HA_HEADS, bb, tb, HA_DK), lambda i, j: (0, i, j, 0))
    st_blk = pl.BlockSpec((bb, HA_HEADS, HA_DK, HA_DV), lambda i, j: (i, 0, 0, 0))
    s0_blk = pl.BlockSpec((None, bb, HA_HEADS, HA_DK, HA_DV), lambda i, j: (layer, i, 0, 0, 0))
    vt_blk = pl.BlockSpec((bb, None, HA_HEADS, tb // cb, HA_DV, cb),
                          lambda i, j: (i, j, 0, 0, 0, 0))
    return pl.pallas_call(
        functools.partial(_hgrn_kernel, cb=cb),
        grid=(bsz // bb, t // tb),
        in_specs=[blk, blk, blk, blk, vt_blk, s0_blk],
        out_specs=[blk, st_blk],
        out_shape=[jax.ShapeDtypeStruct(qs.shape, F32),
                   jax.ShapeDtypeStruct(s0_all.shape[1:], F32)],
        scratch_shapes=[pltpu.VMEM((bb, HA_HEADS, HA_DV, HA_DK), F32),
                        pltpu.VMEM((bb * HA_HEADS, cb, HA_DK), F32),
                        pltpu.VMEM((SUBLANES, SUBLANES, LANES), F32)],
        compiler_params=_cparams(("parallel", "arbitrary")),
        name="hgrn",
    )(qs, kin, logf, v, vt, s0_all)


def _attn_prompt_kernel(qt_ref, kc_ref, kt_ref, o_ref, s0, s1, x0, x1, p0, p1, a0, a1, m_scr,
                        l_scr, acc_scr, *, tq, tk):
    qi = pl.program_id(1)
    jd = (qi * tq) // tk
    shift = CHUNK.bit_length() - 1
    head_cols = [slice(h * tq, (h + 1) * tq) for h in range(HB_HEADS)]

    def scores(j, s_dst, x_dst):
        keys = kc_ref[pl.ds(pl.multiple_of(j * tk, tk), tk), :]
        for cols in head_cols:
            s = _dot(keys, qt_ref[0, :, cols])
            s_dst[:, cols] = s
            x_dst[:, cols] = jnp.max(s, axis=0, keepdims=True)

    def softmax(j, s_src, x_src, p_dst, a_dst, masked):
        if masked:
            kpos = j * tk + lax.broadcasted_iota(jnp.int32, (tk, 1), 0)
            qpos = qi * tq + lax.broadcasted_iota(jnp.int32, (1, tq), 1)
            visible = (kpos >> shift) <= (qpos >> shift)
        for cols in head_cols:
            m_prev = m_scr[:, cols]
            if masked:
                s = jnp.where(visible, s_src[:, cols], NEG_INF)
                m_new = jnp.maximum(m_prev, jnp.max(s, axis=0, keepdims=True))
            else:
                s = s_src[:, cols]
                m_new = jnp.maximum(m_prev, x_src[:, cols])
            alpha = jnp.exp2(m_prev - m_new)
            p = jnp.exp2(s - m_new)
            l_scr[:, cols] = alpha * l_scr[:, cols] + jnp.sum(p, axis=0, keepdims=True)
            m_scr[:, cols] = m_new
            a_dst[:, cols] = alpha
            p_dst[:, cols] = p.astype(BF16)

    def values(j, p_src, a_src):
        for cols in head_cols:
            acc_scr[:, cols] = (a_src[:, cols] * acc_scr[:, cols]
                                + _dot(kt_ref[j], p_src[:, cols]))

    set0, set1 = (s0, x0, p0, a0), (s1, x1, p1, a1)

    def trip(j, cur, nxt):
        scores(j + 1, nxt[0], nxt[1])
        softmax(j, cur[0], cur[1], cur[2], cur[3], False)
        values(jnp.maximum(j - 1, 0), nxt[2], nxt[3])

    m_scr[...] = jnp.full(m_scr.shape, NEG_INF, F32)
    l_scr[...] = jnp.zeros(l_scr.shape, F32)
    acc_scr[...] = jnp.zeros(acc_scr.shape, F32)
    p1[...] = jnp.zeros(p1.shape, BF16)
    a1[...] = jnp.ones(a1.shape, F32)
    scores(0, s0, x0)

    def pair(jj, carry):
        trip(2 * jj, set0, set1)
        trip(2 * jj + 1, set1, set0)
        return carry

    lax.fori_loop(0, jd // 2, pair, 0)

    @pl.when(jd % 2 == 1)
    def _():
        trip(jd - 1, set0, set1)

    def drain(cur, prv):
        softmax(jd, cur[0], cur[1], cur[2], cur[3], True)
        values(jnp.maximum(jd - 1, 0), prv[2], prv[3])
        for h, cols in enumerate(head_cols):
            acc = (cur[3][:, cols] * acc_scr[:, cols] + _dot(kt_ref[jd], cur[2][:, cols]))
            o_ref[0, h] = (acc * (1.0 / l_scr[:, cols])).T.astype(BF16)

    pl.when(jd % 2 == 0)(functools.partial(drain, set0, set1))
    pl.when(jd % 2 == 1)(functools.partial(drain, set1, set0))


def _attn_prompt(qt, kcat, ckvt, *, bsz, t, tq, tk):
    nq = t // tq
    nk = t // tk
    cols = HB_HEADS * tq
    single = pl.Buffered(1)
    return pl.pallas_call(
        functools.partial(_attn_prompt_kernel, tq=tq, tk=tk),
        grid=(bsz, nq),
        in_specs=[
            pl.BlockSpec((1, QK_DIM, cols), lambda b, i: (b * nq + i, 0, 0)),
            pl.BlockSpec((t, QK_DIM), lambda b, i: (b, 0), pipeline_mode=single),
            pl.BlockSpec((nk, KV_LORA, tk), lambda b, i: (b, 0, 0), pipeline_mode=single),
        ],
        out_specs=pl.BlockSpec((1, HB_HEADS, tq, KV_LORA), lambda b, i: (b * nq + i, 0, 0, 0)),
        out_shape=jax.ShapeDtypeStruct((bsz * nq, HB_HEADS, tq, KV_LORA), BF16),
        scratch_shapes=[pltpu.VMEM((tk, cols), F32), pltpu.VMEM((tk, cols), F32),
                        pltpu.VMEM((1, cols), F32), pltpu.VMEM((1, cols), F32),
                        pltpu.VMEM((tk, cols), BF16), pltpu.VMEM((tk, cols), BF16),
                        pltpu.VMEM((1, cols), F32), pltpu.VMEM((1, cols), F32),
                        pltpu.VMEM((1, cols), F32), pltpu.VMEM((1, cols), F32),
                        pltpu.VMEM((KV_LORA, cols), F32)],
        compiler_params=_cparams(("parallel", "arbitrary")),
        name="attn_prompt",
    )(qt, kcat, ckvt)


def _attn_sample_kernel(qa_ref, qp_ref, pckv_ref, pkpe_ref, nckv_ref, nkpe_ref, o_ref):
    _, _, ts, _ = qa_ref.shape
    rows = HB_HEADS * ts
    qa = qa_ref[0].reshape(rows, KV_LORA)
    qp = qp_ref[0].reshape(rows, LANES)[:, :ROPE_DIM]
    pckv = pckv_ref[0].astype(BF16)
    nckv = nckv_ref[...].astype(BF16)
    s_past = _dot_nt(qa, pckv) + _dot(qp, pkpe_ref[0].astype(BF16))
    s_new = _dot_nt(qa, nckv) + _dot_nt(qp, nkpe_ref[...].astype(BF16))
    m = jnp.maximum(jnp.max(s_past, axis=-1, keepdims=True), jnp.max(s_new, axis=-1, keepdims=True))
    p_past = jnp.exp2(s_past - m)
    p_new = jnp.exp2(s_new - m)
    l = jnp.sum(p_past, axis=-1, keepdims=True) + jnp.sum(p_new, axis=-1, keepdims=True)
    o = (_dot(p_past.astype(BF16), pckv) + _dot(p_new.astype(BF16), nckv)) / l
    o_ref[0] = o.astype(BF16).reshape(HB_HEADS, ts, KV_LORA)


def _attn_sample(qabs, qpe, past_ckv, past_kpe, layer, ckv, kpe, *, ts):
    _, bsz, past, _ = past_ckv.shape
    return pl.pallas_call(
        _attn_sample_kernel,
        grid=(bsz,),
        in_specs=[
            pl.BlockSpec((1, HB_HEADS, ts, KV_LORA), lambda b: (b, 0, 0, 0)),
            pl.BlockSpec((1, HB_HEADS, ts, LANES), lambda b: (b, 0, 0, 0)),
            pl.BlockSpec((None, 1, past, KV_LORA), lambda b: (layer, b, 0, 0)),
            pl.BlockSpec((None, 1, ROPE_DIM, past), lambda b: (layer, b, 0, 0)),
            pl.BlockSpec((None, ts, KV_LORA), lambda b: (layer, b, 0)),
            pl.BlockSpec((ts, ROPE_DIM), lambda b: (b, 0)),
        ],
        out_specs=pl.BlockSpec((1, HB_HEADS, ts, KV_LORA), lambda b: (b, 0, 0, 0)),
        out_shape=jax.ShapeDtypeStruct(qabs.shape, BF16),
        compiler_params=_cparams(("parallel",)),
        name="attn_sample",
    )(qabs, qpe, past_ckv, past_kpe, ckv, kpe)


def _gates(logits, bias):
    tm = logits.shape[0]
    lane = lax.broadcasted_iota(jnp.int32, (tm, LANES), 1)
    pos = lane % EXPERTS_PER_GROUP
    valid = lane < N_EXPERTS
    scores = jax.nn.sigmoid(logits)
    sel = jnp.where(valid, scores + bias, -jnp.inf)

    others = []
    for r in range(1, EXPERTS_PER_GROUP):
        others.append(jnp.where(pos >= r, pltpu.roll(sel, r, 1),
                                pltpu.roll(sel, LANES - (EXPERTS_PER_GROUP - r), 1)))
    a, b, c, d = sel, others[0], others[1], others[2]
    hi1, lo1 = jnp.maximum(a, b), jnp.minimum(a, b)
    hi2, lo2 = jnp.maximum(c, d), jnp.minimum(c, d)
    gscore = jnp.maximum(hi1, hi2) + jnp.maximum(jnp.minimum(hi1, hi2), jnp.maximum(lo1, lo2))
    gmax = jnp.max(gscore, axis=-1, keepdims=True)
    group = (lane // EXPERTS_PER_GROUP).astype(F32)
    gidx = jnp.min(jnp.where(gscore == gmax, group, float(LANES)), axis=-1, keepdims=True)

    rank = jnp.zeros((tm, LANES), jnp.int32)
    for r, o in enumerate(others, start=1):
        ahead = (o > sel) | ((o == sel) & (pos >= r))
        rank = rank + ahead.astype(jnp.int32)
    chosen = (group == gidx) & (rank < 2) & valid
    w = jnp.where(chosen, scores, 0.0)
    w = w / jnp.sum(w, axis=-1, keepdims=True)
    return jnp.where(lane == GROUP_LANE, gidx, w)


def _merge_kernel(x_ref, o_ref, gs_ref, olat_ref, wuv_ref, wo_ref, hg_ref, mg_ref,
                  g1_ref, b1_ref, wr_ref, rb_ref, x1g_ref, *, alpha):
    _, _, tq, _ = olat_ref.shape
    tm = x_ref.shape[0]
    hg = hg_ref[...]
    d_a = HA_HEADS * HA_DV
    halves = [(i * tm // 2, (i + 1) * tm // 2) for i in range(2)]

    def lat(h, lo, hi):
        if tq >= hi - lo:
            return olat_ref[lo // tq, h, lo % tq:lo % tq + hi - lo]
        return jnp.concatenate([olat_ref[s, h] for s in range(lo // tq, hi // tq)], axis=0)

    ob = [[_dot(lat(h, lo, hi), wuv_ref[h]) for h in range(HB_HEADS)] for lo, hi in halves]
    oa = [jnp.concatenate([_rms(o_ref[h, lo:hi, :], hg) * gs_ref[lo:hi, h * HA_DV:(h + 1) * HA_DV]
                           for h in range(HA_HEADS)], axis=-1) for lo, hi in halves]
    ob = [_rms(jnp.concatenate(o, axis=-1), mg_ref[...]) for o in ob]
    mix = [_dot(a.astype(BF16), wo_ref[:d_a, :]) + _dot(o.astype(BF16), wo_ref[d_a:, :])
           for a, o in zip(oa, ob)]
    x1 = [_layer_norm(alpha * x_ref[lo:hi, :] + m, g1_ref[...], b1_ref[...])
          for (lo, hi), m in zip(halves, mix)]
    logits = [_dot(x.astype(BF16), wr_ref[...]) for x in x1]
    for (lo, hi), x, lg in zip(halves, x1, logits):
        for c in range(X_PIECES):
            x1g_ref[c, lo:hi, :] = x[:, c * SC_ROW:(c + 1) * SC_ROW]
        x1g_ref[X_PIECES, lo:hi, :LANES] = _gates(lg, rb_ref[...])
        x1g_ref[X_PIECES, lo:hi, LANES:] = jnp.zeros((hi - lo, SC_ROW - LANES), F32)


def _merge(x, o_raw, gs, olat, wuv, wo, hg, mg, g1, b1, wr, rb, *, tm, tq, alpha):
    n = x.shape[0]
    row = lambda w: pl.BlockSpec((tm, w), lambda i: (i, 0))
    return pl.pallas_call(
        functools.partial(_merge_kernel, alpha=alpha),
        grid=(n // tm,),
        in_specs=[row(D_MODEL), pl.BlockSpec((HA_HEADS, tm, HA_DV), lambda i: (0, i, 0)), row(HA_W),
                  pl.BlockSpec((tm // tq, HB_HEADS, tq, KV_LORA), lambda i: (i, 0, 0, 0)),
                  _full(wuv.shape), _full(wo.shape), _full(hg.shape), _full(mg.shape),
                  _full(g1.shape), _full(b1.shape), _full(wr.shape), _full(rb.shape)],
        out_specs=pl.BlockSpec((X_PIECES + 1, tm, SC_ROW), lambda i: (0, i, 0)),
        out_shape=jax.ShapeDtypeStruct((X_PIECES + 1, n, SC_ROW), F32),
        compiler_params=_cparams(("parallel",)),
        name="merge",
    )(x, o_raw, gs, olat, wuv, wo, hg, mg, g1, b1, wr, rb)


def _sc_scatter_rows(src, dst, m):
    parts, n, _ = src.shape
    windows = dst.shape[0] // SC_WINDOW
    src_windows = n // SC_WINDOW
    pieces = src.reshape(parts * n, SC_ROW)
    piece_dst = (jnp.arange(parts, dtype=jnp.int32)[:, None] * m + dst[None, :]).reshape(1, -1)
    mesh = plsc.VectorSubcoreMesh(core_axis_name="core", subcore_axis_name="subcore")

    @functools.partial(pl.kernel, out_type=jax.ShapeDtypeStruct((parts * m, SC_ROW), src.dtype),
                       mesh=mesh)
    def scatter(x_hbm, i_hbm, o_hbm):
        def body(x_vmem, i_vmem):
            pltpu.sync_copy(x_vmem, o_hbm.at[i_vmem.at[0]])

        pltpu.emit_pipeline(
            body,
            grid=(parts * windows,),
            in_specs=[pl.BlockSpec((SC_WINDOW, SC_ROW),
                                   lambda i: ((i // windows) * src_windows
                                              + (i % windows) % src_windows, 0)),
                      pl.BlockSpec((1, SC_WINDOW), lambda i: (0, i))],
            out_specs=[],
            core_axis_name=("core", "subcore"),
            dimension_semantics=(pltpu.PARALLEL,),
        )(x_hbm, i_hbm)

    return scatter(pieces, piece_dst).reshape(parts, m, SC_ROW)


def _sc_gather_rows(src, idx):
    parts, n, _ = src.shape
    m = idx.shape[0] * parts
    pieces = src.reshape(parts * n, SC_ROW)
    piece_idx = (jnp.arange(parts, dtype=jnp.int32)[:, None] * n + idx[None, :]).reshape(1, m)
    mesh = plsc.VectorSubcoreMesh(core_axis_name="core", subcore_axis_name="subcore")

    @functools.partial(pl.kernel, out_type=jax.ShapeDtypeStruct((m, SC_ROW), src.dtype), mesh=mesh)
    def gather(x_hbm, i_hbm, o_hbm):
        def body(i_vmem, o_vmem):
            pltpu.sync_copy(x_hbm.at[i_vmem.at[0]], o_vmem)

        pltpu.emit_pipeline(
            body,
            grid=(m // SC_WINDOW,),
            in_specs=[pl.BlockSpec((1, SC_WINDOW), lambda i: (0, i))],
            out_specs=[pl.BlockSpec((SC_WINDOW, SC_ROW), lambda i: (i, 0))],
            core_axis_name=("core", "subcore"),
            dimension_semantics=(pltpu.PARALLEL,),
        )(i_hbm, o_hbm)

    return gather(pieces, piece_idx).reshape(parts, idx.shape[0], SC_ROW)


def _moe_kernel(tg_ref, xs_ref, wg_ref, wu_ref, wd_ref, y_ref, acc_scr):
    tm = xs_ref.shape[1]
    per_step = wg_ref.shape[0]
    step = pl.program_id(1)
    first_expert = tg_ref[pl.program_id(0)] * EXPERTS_PER_GROUP + step * per_step
    xb = jnp.concatenate([xs_ref[c] for c in range(X_PIECES)], axis=1).astype(BF16)
    gates = xs_ref[X_PIECES, :, :LANES]
    lane = lax.broadcasted_iota(jnp.int32, (tm, LANES), 1)
    acc = jnp.zeros((tm, D_MODEL), F32)
    for k in range(per_step):
        hmid = (_silu(_dot(xb, wg_ref[k].astype(BF16))) * _dot(xb, wu_ref[k].astype(BF16)))
        gcol = jnp.sum(jnp.where(lane == first_expert + k, gates, 0.0), axis=-1, keepdims=True)
        acc = acc + gcol * _dot(hmid.astype(BF16), wd_ref[k].astype(BF16))

    @pl.when(step == 0)
    def _():
        acc_scr[...] = acc

    @pl.when(step > 0)
    def _():
        acc_scr[...] += acc

    @pl.when(step == pl.num_programs(1) - 1)
    def _():
        for c in range(D_MODEL // SC_ROW):
            y_ref[c] = acc_scr[:, c * SC_ROW:(c + 1) * SC_ROW]


def _moe_sorted(tile_group, xs, wg, wu, wd, layer, *, tm, per_step, weight_buffers):
    m = xs.shape[1]
    steps = EXPERTS_PER_GROUP // per_step
    mode = dict(pipeline_mode=pl.Buffered(weight_buffers))
    wspec = lambda shape: pl.BlockSpec((None, per_step) + shape,
                                       lambda i, k, tg: (layer, tg[i] * steps + k, 0, 0), **mode)
    return pl.pallas_call(
        _moe_kernel,
        grid_spec=pltpu.PrefetchScalarGridSpec(
            num_scalar_prefetch=1,
            grid=(m // tm, steps),
            in_specs=[pl.BlockSpec((X_PIECES + 1, tm, SC_ROW), lambda i, k, tg: (0, i, 0)),
                      wspec((D_MODEL, D_EXPERT)), wspec((D_MODEL, D_EXPERT)),
                      wspec((D_EXPERT, D_MODEL))],
            out_specs=pl.BlockSpec((D_MODEL // SC_ROW, tm, SC_ROW), lambda i, k, tg: (0, i, 0)),
            scratch_shapes=[pltpu.VMEM((tm, D_MODEL), F32)],
        ),
        out_shape=jax.ShapeDtypeStruct((D_MODEL // SC_ROW, m, SC_ROW), F32),
        compiler_params=_cparams(("arbitrary", "arbitrary")),
        name="moe",
    )(tile_group, xs, wg, wu, wd)


def _route(x1g, *, tm):
    n = x1g.shape[1]
    n_tiles = n // tm + N_GROUPS - 1
    m = n_tiles * tm
    gidx = x1g[X_PIECES, :, GROUP_LANE].astype(jnp.int32)
    onehot = (gidx[:, None] == jnp.arange(N_GROUPS, dtype=jnp.int32)[None, :]).astype(jnp.int32)
    counts = jnp.sum(onehot, axis=0)
    tiles = (counts + tm - 1) // tm
    tile_end = jnp.cumsum(tiles)
    tile_start = tile_end - tiles
    rank = jnp.sum((jnp.cumsum(onehot, axis=0) - onehot) * onehot, axis=1)
    pos = jnp.take(tile_start, gidx) * tm + rank
    pad_start = tile_start * tm + counts
    pad_end = (tile_end * tm).at[N_GROUPS - 1].set(m)
    pad_cum = jnp.cumsum(pad_end - pad_start)
    k = jnp.arange(m - n, dtype=jnp.int32)
    pg = jnp.sum(k[:, None] >= pad_cum[None, :], axis=1)
    pad_rows = jnp.take(pad_start, pg) + k - jnp.take(pad_cum - (pad_end - pad_start), pg)
    dst = jnp.concatenate([pos, pad_rows]).astype(jnp.int32)
    tile_group = jnp.sum(jnp.arange(n_tiles, dtype=jnp.int32)[:, None] >= tile_end[None, :], axis=1)
    return dst, jnp.minimum(tile_group, N_GROUPS - 1).astype(jnp.int32), m


def _post_kernel(x1_ref, f_ref, p_ref, g2_ref, b2_ref, wpg_ref, wp_ref, y_ref, *, alpha):
    tm = x1_ref.shape[1]
    halves = [(i * tm // 2, (i + 1) * tm // 2) for i in range(2)]
    emb = [_dot(p_ref[lo:hi, :].astype(BF16), wp_ref[...]) for lo, hi in halves]

    def rows(ref, lo, hi):
        return jnp.concatenate([ref[c, lo:hi, :] for c in range(X_PIECES)], axis=1)

    x2 = [_layer_norm(alpha * rows(x1_ref, lo, hi) + rows(f_ref, lo, hi), g2_ref[...], b2_ref[...])
          for lo, hi in halves]
    gate = [_dot(x.astype(BF16), wpg_ref[...]) for x in x2]
    for (lo, hi), x, g, e in zip(halves, x2, gate, emb):
        y_ref[lo:hi, :] = x + jax.nn.sigmoid(g) * e


def _post(x1g, ffn, p_all, layer, g2, b2, wpg, wp, *, tm, alpha):
    _, n, _ = ffn.shape
    row = lambda w: pl.BlockSpec((tm, w), lambda i: (i, 0))
    pieces = pl.BlockSpec((X_PIECES, tm, SC_ROW), lambda i: (0, i, 0))
    return pl.pallas_call(
        functools.partial(_post_kernel, alpha=alpha),
        grid=(n // tm,),
        in_specs=[pieces, pieces,
                  pl.BlockSpec((None, tm, PLE_DIM), lambda i: (layer, i, 0)),
                  _full(g2.shape), _full(b2.shape), _full(wpg.shape), _full(wp.shape)],
        out_specs=row(D_MODEL),
        out_shape=jax.ShapeDtypeStruct((n, D_MODEL), F32),
        compiler_params=_cparams(("parallel",)),
        name="post",
    )(x1g, ffn, p_all, g2, b2, wpg, wp)


def _rot_cols(w):
    half = ROPE_DIM // 2
    return jnp.concatenate([-w[..., half:], w[..., :half]], axis=-1)


def _prep_layer(l, w_in, w_qb, w_kvb, w_o, w_gate, w_up, w_down, w_ple, w_ple_gate):
    win = w_in[l]
    win_aug = jnp.concatenate([win, _rot_cols(win[:, COL_KPE:])], axis=-1).astype(BF16)
    wqb = w_qb[l].reshape(Q_LORA, HB_HEADS, NOPE_DIM + ROPE_DIM)
    wqb_aug = jnp.concatenate([wqb, _rot_cols(wqb[..., NOPE_DIM:])], axis=-1)
    wqb_aug = wqb_aug.reshape(Q_LORA, HB_HEADS * Q_HEAD_AUG).astype(BF16)
    wkvb = w_kvb[l].reshape(KV_LORA, HB_HEADS, NOPE_DIM + V_DIM)
    wuk_t = jnp.transpose(wkvb[..., :NOPE_DIM], (1, 2, 0)).astype(BF16)
    wuv = jnp.transpose(wkvb[..., NOPE_DIM:], (1, 0, 2)).astype(BF16)
    return dict(win=win_aug, wqb=wqb_aug, wuk=wuk_t, wuv=wuv, wo=w_o[l].astype(BF16),
                wg=w_gate, wu=w_up, wd=w_down,
                wp=w_ple[l].astype(BF16), wpg=w_ple_gate[l].astype(BF16))


def _rope_table(pos):
    inv = ROPE_THETA ** (-np.arange(0, ROPE_DIM, 2, dtype=np.float64) / ROPE_DIM)
    ang = pos.astype(np.float64)[:, None] * inv[None, :]
    cos, sin = np.cos(ang), np.sin(ang)
    return np.concatenate([cos, cos, sin, sin], axis=-1).astype(np.float32)


def _tiles(n, t, prompt):
    if prompt:
        tm = min(512, t)
        return dict(tm=tm, tq=tm, tk=tm, tb=tm, cb=min(128, t), bb=2, tmoe=min(512, n),
                    experts_per_step=EXPERTS_PER_GROUP, weight_buffers=1)
    return dict(tm=min(512, n), tq=t, tk=None, tb=t, cb=t, bb=2, tmoe=min(256, n),
                experts_per_step=EXPERTS_PER_GROUP, weight_buffers=2)


def _layer(x, p_all, layer, cs, s0_all, past, prev_ckv, lbp, prm, small, *, alpha, prompt):
    bsz, t, _ = x.shape
    n = bsz * t
    tl = _tiles(n, t, prompt)
    x2d = x.reshape(n, D_MODEL)
    tm, tb, cb = tl["tm"], tl["tb"], tl["cb"]
    outs = _inproj(x2d, cs, prm["win"], lbp, small["qg"], prm["wqb"], small["kvg"], prm["wuk"],
                   prev_ckv, tm=tm, tq=tl["tq"], cb=cb, prompt=prompt)
    qs, kin, logf, v, gs, ckv_all, kpe = outs[:7]
    r4 = lambda a: a.reshape(HA_HEADS, bsz, t, HA_DK)
    if prompt:
        qt, vt, kcat, ckvt = outs[7:]
        vt = vt.reshape(bsz, t // tb, HA_HEADS, tb // cb, HA_DV, cb)
    else:
        qabs, qpe = outs[7:]
        vt = jnp.transpose(r4(v), (1, 0, 3, 2)).astype(BF16)
        vt = vt.reshape(bsz, 1, HA_HEADS, 1, HA_DV, t)
    o_raw, s_new = _hgrn(r4(qs), r4(kin), r4(logf), r4(v), vt, s0_all, layer if not prompt else 0,
                         bb=tl["bb"], tb=tb, cb=cb)
    if prompt:
        olat = _attn_prompt(qt, kcat, ckvt, bsz=bsz, t=t, tq=tl["tq"], tk=tl["tk"])
    else:
        olat = _attn_sample(qabs, qpe, past[0], past[1], layer, ckv_all, kpe, ts=t)
    x1g = _merge(x2d, o_raw.reshape(HA_HEADS, n, HA_DV), gs, olat, prm["wuv"], prm["wo"],
                 small["hg"], small["mg"], small["g1"], small["b1"],
                 small["wr"], small["rb"], tm=tm, tq=tl["tq"], alpha=alpha)
    tmoe = tl["tmoe"]
    dst, tile_group, m_sorted = _route(x1g, tm=tmoe)
    xs = _sc_scatter_rows(x1g, dst, m_sorted)
    ys = _moe_sorted(tile_group, xs, prm["wg"], prm["wu"], prm["wd"], layer, tm=tmoe,
                     per_step=tl["experts_per_step"], weight_buffers=tl["weight_buffers"])
    ffn = _sc_gather_rows(ys, dst[:n])
    y = _post(x1g, ffn, p_all.reshape(-1, n, PLE_DIM), layer, small["g2"], small["b2"],
              prm["wpg"], prm["wp"], tm=tm, alpha=alpha)
    return y.reshape(bsz, t, D_MODEL), s_new, ckv_all, kpe.reshape(bsz, t, ROPE_DIM)


def kernel(x_prompt, x_sample, p_prompt, p_sample, state_hgrn, cache_ckv, cache_kpe, w_in,
           lb_logits, hgrn_norm_g, q_norm_g, w_qb, kv_norm_g, w_kvb, mla_norm_g, w_o,
           ln1_g, ln1_b, ln2_g, ln2_b, w_router, router_bias, w_gate, w_up, w_down,
           w_ple, w_ple_gate):
    depth = w_in.shape[0]
    alpha = (2 * depth) ** 0.25
    bp, tp, _ = x_prompt.shape
    bs, ts, _ = x_sample.shape
    past = cache_ckv.shape[2]

    sm = jax.nn.softmax(lb_logits.astype(F32), axis=0)
    lb_all = jnp.maximum(jnp.cumsum(sm, axis=0) - sm[0:1], 0.0)
    lbp_all = jnp.stack([jnp.log(lb_all), jnp.log1p(-lb_all), 1.0 - lb_all], axis=1)

    cs_p = jnp.asarray(_rope_table(np.arange(tp)))
    tm_s = _tiles(bs * ts, ts, False)["tm"]
    cs_s = jnp.asarray(np.tile(_rope_table(past + np.arange(ts)), (tm_s // ts, 1)))
    wr = jnp.pad(w_router, ((0, 0), (0, LANES - N_EXPERTS))).astype(BF16)
    rb = jnp.pad(router_bias.astype(F32), (0, LANES - N_EXPERTS)).reshape(1, LANES)
    s0_p = jnp.zeros((1, bp, HA_HEADS, HA_DK, HA_DV), F32)
    cache_kpe_t = jnp.swapaxes(cache_kpe, 2, 3)

    yp, ys = x_prompt, x_sample
    res = [[] for _ in range(4)]
    cp = cs_ = None
    for l in range(depth):
        prm = _prep_layer(l, w_in, w_qb, w_kvb, w_o, w_gate, w_up, w_down, w_ple, w_ple_gate)
        row = lambda a: a[l].reshape(1, -1).astype(F32)
        small = dict(qg=row(q_norm_g), kvg=row(kv_norm_g), hg=row(hgrn_norm_g), mg=row(mla_norm_g),
                     g1=row(ln1_g), b1=row(ln1_b), g2=row(ln2_g), b2=row(ln2_b), wr=wr, rb=rb)
        yp, sp, cp, kp = _layer(yp, p_prompt, l, cs_p, s0_p, None, cp, lbp_all[l], prm, small,
                                alpha=alpha, prompt=True)
        ys, ss, cs_, ks = _layer(ys, p_sample, l, cs_s, state_hgrn, (cache_ckv, cache_kpe_t), cs_,
                                 lbp_all[l], prm, small, alpha=alpha, prompt=False)
        for lst, a in zip(res, (sp, kp, ss, ks)):
            lst.append(a)
    sp, kp, ss, ks = (jnp.stack(a) for a in res)
    return (yp, ys, sp, cp.reshape(depth, bp, tp, KV_LORA), kp, ss,
            cs_.reshape(depth, bs, ts, KV_LORA), ks)
```

```python
import functools

import jax
import jax.numpy as jnp
import numpy as np
from jax import lax
from jax.experimental import pallas as pl
from jax.experimental.pallas import tpu as pltpu
from jax.experimental.pallas import tpu_sc as plsc

F32 = jnp.float32
BF16 = jnp.bfloat16

D_MODEL = 1024
HA_HEADS = 4
HA_DK = 128
HA_DV = 128
HB_HEADS = 4
Q_LORA = 384
KV_LORA = 256
NOPE_DIM = 128
ROPE_DIM = 64
V_DIM = 128
ROPE_THETA = 10000.0
MLA_SCALE = (NOPE_DIM + ROPE_DIM) ** -0.5
LOG2E = 1.4426950408889634
Q_SCALE = MLA_SCALE * LOG2E
CHUNK = 64
N_EXPERTS = 16
N_GROUPS = 4
EXPERTS_PER_GROUP = N_EXPERTS // N_GROUPS
D_EXPERT = 512
PLE_DIM = 256
NEG_INF = -1e30

HA_W = HA_HEADS * HA_DK
COL_CQ = 4 * HA_W
COL_CKV = COL_CQ + Q_LORA
COL_KPE = COL_CKV + KV_LORA
D_IN_AUG = COL_KPE + 2 * ROPE_DIM
Q_HEAD_AUG = NOPE_DIM + 2 * ROPE_DIM

LANES = 128
SUBLANES = 8
QK_DIM = KV_LORA + LANES
GROUP_LANE = N_EXPERTS
SC_WINDOW = 128
SC_ROW = 256
X_PIECES = D_MODEL // SC_ROW
SUB_BLOCK = 16
VMEM_LIMIT = 56 * 1024 * 1024


def _cparams(sem, vmem=VMEM_LIMIT):
    return pltpu.CompilerParams(dimension_semantics=sem, vmem_limit_bytes=vmem)


def _dot(a, b):
    return jnp.dot(a, b, preferred_element_type=F32)


def _dot_nt(a, b):
    return lax.dot_general(a, b, (((1,), (1,)), ((), ())), preferred_element_type=F32)


def _rms(x, g, eps=1e-6):
    return x * lax.rsqrt(jnp.mean(x * x, axis=-1, keepdims=True) + eps) * g


def _layer_norm(x, g, b, eps=1e-5):
    mu = jnp.mean(x, axis=-1, keepdims=True)
    xc = x - mu
    var = jnp.mean(xc * xc, axis=-1, keepdims=True)
    return xc * lax.rsqrt(var + eps) * g + b


def _silu(x):
    return x * jax.nn.sigmoid(x)


def _full(shape):
    nd = len(shape)
    return pl.BlockSpec(shape, lambda *_: (0,) * nd)


def _inproj_kernel(x_ref, cs_ref, win_ref, lb_ref, qg_ref, wqb_ref, kvg_ref, wuk_ref, prev_ref,
                   qs_ref, kin_ref, logf_ref, v_ref, gs_ref, ckv_ref, kpe_ref, *rest,
                   tq, cb, prompt, n_prev):
    tm = x_ref.shape[0]
    x = x_ref[...].astype(BF16)

    def sect(lo, hi):
        return _dot(x, win_ref[:, lo:hi])

    def put_heads(ref, val):
        for h in range(HA_HEADS):
            ref[h] = val[:, h * HA_DK:(h + 1) * HA_DK]

    cqn = _rms(sect(COL_CQ, COL_CKV), qg_ref[...]).astype(BF16)

    put_heads(qs_ref, _silu(sect(0, HA_W)))
    fa = sect(HA_W, 2 * HA_W)
    log_lb = lb_ref[0:1, :]
    log1m_lb = lb_ref[1:2, :]
    one_m_lb = lb_ref[2:3, :]
    e = jnp.exp(-jnp.abs(fa))
    r = 1.0 / (1.0 + e)
    c = log1m_lb + jnp.minimum(fa, 0.0) + jnp.log(r)
    put_heads(logf_ref, jnp.maximum(log_lb, c) + jnp.log(1.0 + jnp.exp(-jnp.abs(log_lb - c))))
    put_heads(kin_ref, one_m_lb * jnp.where(fa >= 0.0, e * r, r))
    v = sect(2 * HA_W, 3 * HA_W)
    put_heads(v_ref, v)
    gs_ref[...] = _silu(sect(3 * HA_W, 4 * HA_W))

    cs = cs_ref[...]

    def rope(t):
        prod = t * cs
        return prod + pltpu.roll(prod, ROPE_DIM, 1)

    ckv = _rms(sect(COL_CKV, COL_KPE), kvg_ref[...])
    for l in range(n_prev):
        ckv_ref[l] = prev_ref[l]
    ckv_ref[n_prev] = ckv
    kpe2 = rope(sect(COL_KPE, D_IN_AUG))
    kpe_ref[...] = kpe2[:, :ROPE_DIM]
    if prompt:
        qt_ref, vt_ref, kcat_ref, ckvt_ref = rest
        for h in range(HA_HEADS):
            for ci in range(tm // cb):
                vt_ref[0, h, ci] = v[ci * cb:(ci + 1) * cb, h * HA_DV:(h + 1) * HA_DV].T.astype(BF16)
        kcat_ref[:, :KV_LORA] = ckv.astype(BF16)
        kcat_ref[:, KV_LORA:] = kpe2.astype(BF16)
        ckvt_ref[0] = ckv.T.astype(BF16)
    else:
        qabs_ref, qpe_ref = rest

    lane = lax.broadcasted_iota(jnp.int32, (tm, LANES), 1)
    qhs = [_dot(cqn, wqb_ref[:, h * Q_HEAD_AUG:(h + 1) * Q_HEAD_AUG]) for h in range(HB_HEADS)]
    qabss = [_dot(qhs[h][:, :NOPE_DIM].astype(BF16), wuk_ref[h]) for h in range(HB_HEADS)]
    for h in range(HB_HEADS):
        qabs = qabss[h] * Q_SCALE
        qpe = jnp.where(lane < ROPE_DIM, rope(qhs[h][:, NOPE_DIM:]) * Q_SCALE, 0.0)
        if prompt:
            qabs_t = qabs.T.astype(BF16)
            qpe_t = qpe.T.astype(BF16)
            for s in range(tm // tq):
                qt_ref[s, :KV_LORA, h * tq:(h + 1) * tq] = qabs_t[:, s * tq:(s + 1) * tq]
                qt_ref[s, KV_LORA:, h * tq:(h + 1) * tq] = qpe_t[:, s * tq:(s + 1) * tq]
        else:
            for s in range(tm // tq):
                qabs_ref[s, h] = qabs[s * tq:(s + 1) * tq].astype(BF16)
                qpe_ref[s, h] = qpe[s * tq:(s + 1) * tq].astype(BF16)


def _inproj(x, cs, w_in, lbp, qg, wqb, kvg, wuk, prev_ckv, *, tm, tq, cb, prompt):
    n = x.shape[0]
    nt = n // tm
    n_prev = 0 if prev_ckv is None else prev_ckv.shape[0]
    if prev_ckv is None:
        prev_ckv = jnp.zeros((1, tm, KV_LORA), F32)
        prev_spec = _full(prev_ckv.shape)
    else:
        prev_spec = pl.BlockSpec((n_prev, tm, KV_LORA), lambda i: (0, i, 0))
    row = lambda w: pl.BlockSpec((tm, w), lambda i: (i, 0))
    heads = pl.BlockSpec((HA_HEADS, tm, HA_DK), lambda i: (0, i, 0))
    qblk = lambda w: pl.BlockSpec((tm // tq, HB_HEADS, tq, w), lambda i: (i, 0, 0, 0))
    head_major = jax.ShapeDtypeStruct((HA_HEADS, n, HA_DK), F32)
    out_shape = [
        head_major,
        head_major,
        head_major,
        head_major,
        jax.ShapeDtypeStruct((n, HA_W), F32),
        jax.ShapeDtypeStruct((n_prev + 1, n, KV_LORA), F32),
        jax.ShapeDtypeStruct((n, ROPE_DIM), F32),
    ]
    out_specs = [heads, heads, heads, heads, row(HA_W),
                 pl.BlockSpec((n_prev + 1, tm, KV_LORA), lambda i: (0, i, 0)), row(ROPE_DIM)]
    if prompt:
        out_shape += [jax.ShapeDtypeStruct((n // tq, QK_DIM, HB_HEADS * tq), BF16),
                      jax.ShapeDtypeStruct((nt, HA_HEADS, tm // cb, HA_DV, cb), BF16),
                      jax.ShapeDtypeStruct((n, QK_DIM), BF16),
                      jax.ShapeDtypeStruct((nt, KV_LORA, tm), BF16)]
        out_specs += [pl.BlockSpec((tm // tq, QK_DIM, HB_HEADS * tq), lambda i: (i, 0, 0)),
                      pl.BlockSpec((1, HA_HEADS, tm // cb, HA_DV, cb), lambda i: (i, 0, 0, 0, 0)),
                      row(QK_DIM),
                      pl.BlockSpec((1, KV_LORA, tm), lambda i: (i, 0, 0))]
    else:
        out_shape += [jax.ShapeDtypeStruct((n // tq, HB_HEADS, tq, KV_LORA), BF16),
                      jax.ShapeDtypeStruct((n // tq, HB_HEADS, tq, LANES), BF16)]
        out_specs += [qblk(KV_LORA), qblk(LANES)]
    return pl.pallas_call(
        functools.partial(_inproj_kernel, tq=tq, cb=cb, prompt=prompt, n_prev=n_prev),
        grid=(nt,),
        in_specs=[row(D_MODEL),
                  pl.BlockSpec((tm, LANES), lambda i: (i % (cs.shape[0] // tm), 0)),
                  _full(w_in.shape), _full(lbp.shape), _full(qg.shape),
                  _full(wqb.shape), _full(kvg.shape), _full(wuk.shape), prev_spec],
        out_specs=out_specs,
        out_shape=out_shape,
        compiler_params=_cparams(("parallel",)),
        name="inproj",
    )(x, cs, w_in, lbp, qg, wqb, kvg, wuk, prev_ckv)


def _hgrn_kernel(q_ref, k_ref, g_ref, v_ref, vt_ref, s0_ref, o_ref, sfin_ref, st_scr, b2_scr,
                 hide_ref, *, cb):
    _, bb, tb, _ = q_ref.shape
    nsub = cb // SUB_BLOCK
    ti = pl.program_id(1)

    @pl.when(ti == 0)
    def _():
        for b in range(bb):
            for h in range(HA_HEADS):
                st_scr[b, h] = s0_ref[b, h].T
        row8 = lax.broadcasted_iota(jnp.int32, (SUBLANES, LANES), 0)
        for s in range(SUBLANES):
            hide_ref[s] = jnp.where(row8 >= s, 0.0, NEG_INF)

    r_i = lax.broadcasted_iota(jnp.int32, (cb, cb), 0)
    c_i = lax.broadcasted_iota(jnp.int32, (cb, cb), 1)
    tril = (r_i >= c_i).astype(F32)

    chains = [(b, h) for b in range(bb) for h in range(HA_HEADS)]

    def matmul_part(ci):
        r0 = pl.multiple_of(ci * cb, cb)
        rows = pl.ds(r0, cb)

        def sub(ref, b, h, lo, size):
            return ref[h, b, pl.ds(pl.multiple_of(r0 + lo, SUB_BLOCK), size), :]

        for b, h in chains:
            bcum = jnp.dot(tril, g_ref[h, b, rows, :], precision=lax.Precision.HIGHEST,
                           preferred_element_type=F32)
            b2_scr[b * HA_HEADS + h] = bcum * LOG2E
        for b, h in chains:
            b2 = b2_scr[b * HA_HEADS + h]
            o_ref[h, b, rows, :] = _dot_nt((q_ref[h, b, rows, :] * jnp.exp2(b2)).astype(BF16),
                                           st_scr[b, h].astype(BF16))
        for j in range(nsub - 1):
            lo, hi = j * SUB_BLOCK, (j + 1) * SUB_BLOCK
            below = pl.ds(pl.multiple_of(r0 + hi, SUB_BLOCK), cb - hi)
            for b, h in chains:
                slot = b * HA_HEADS + h
                bnd = b2_scr[slot, hi - 1:hi, :]
                kd = (sub(k_ref, b, h, lo, SUB_BLOCK)
                      * jnp.exp2(bnd - b2_scr[slot, lo:hi, :])).astype(BF16)
                qe = (sub(q_ref, b, h, hi, cb - hi)
                      * jnp.exp2(b2_scr[slot, hi:, :] - bnd)).astype(BF16)
                a = _dot_nt(qe, kd).astype(BF16)
                o_ref[h, b, below, :] += _dot(a, sub(v_ref, b, h, lo, SUB_BLOCK).astype(BF16))
        for b, h in chains:
            slot = b * HA_HEADS + h
            b_last = b2_scr[slot, cb - 1:cb, :]
            kd_all = (k_ref[h, b, rows, :] * jnp.exp2(b_last - b2_scr[slot])).astype(BF16)
            st_scr[b, h] = st_scr[b, h] * jnp.exp2(b_last) + _dot(vt_ref[b, h, ci], kd_all)

    def pairwise_part(ci, b, h, j):
        slot = b * HA_HEADS + h
        r0 = pl.multiple_of(ci * cb, cb)
        lo = j * SUB_BLOCK
        mid = lo + SUBLANES
        top = pl.ds(pl.multiple_of(r0 + lo, SUBLANES), SUBLANES)
        bot = pl.ds(pl.multiple_of(r0 + mid, SUBLANES), SUBLANES)
        q_top, q_bot = q_ref[h, b, top, :], q_ref[h, b, bot, :]
        b_top, b_bot = b2_scr[slot, lo:mid, :], b2_scr[slot, mid:mid + SUBLANES, :]
        acc_top, acc_bot = o_ref[h, b, top, :], o_ref[h, b, bot, :]
        for s in range(SUB_BLOCK):
            row = pl.ds(r0 + lo + s, 1)
            bs = b2_scr[slot, lo + s:lo + s + 1, :]
            ks = k_ref[h, b, row, :]
            vs = v_ref[h, b, row, :]
            if s < SUBLANES:
                w = q_top * jnp.exp2(b_top - bs + hide_ref[s]) * ks
                acc_top = acc_top + jnp.sum(w, axis=-1, keepdims=True) * vs
                w = q_bot * jnp.exp2(b_bot - bs) * ks
            else:
                w = q_bot * jnp.exp2(b_bot - bs + hide_ref[s - SUBLANES]) * ks
            acc_bot = acc_bot + jnp.sum(w, axis=-1, keepdims=True) * vs
        o_ref[h, b, top, :] = acc_top
        o_ref[h, b, bot, :] = acc_bot

    def chunk(ci, carry):
        matmul_part(ci)
        for j in range(nsub):
            for b, h in chains:
                pairwise_part(ci, b, h, j)
        return carry

    lax.fori_loop(0, tb // cb, chunk, 0)

    @pl.when(ti == pl.num_programs(1) - 1)
    def _():
        for b in range(bb):
            for h in range(HA_HEADS):
                sfin_ref[b, h] = st_scr[b, h].T


def _hgrn(qs, kin, logf, v, vt, s0_all, layer, *, bb, tb, cb):
    _, bsz, t, _ = qs.shape
    blk = pl.BlockSpec((HA_HEADS, bb, tb, HA_DK), lambda i, j: (0, i, j, 0))
    st_blk = pl.BlockSpec((bb, HA_HEADS, HA_DK, HA_DV), lambda i, j: (i, 0, 0, 0))
    s0_blk = pl.BlockSpec((None, bb, HA_HEADS, HA_DK, HA_DV), lambda i, j: (layer, i, 0, 0, 0))
    vt_blk = pl.BlockSpec((bb, None, HA_HEADS, tb // cb, HA_DV, cb),
                          lambda i, j: (i, j, 0, 0, 0, 0))
    return pl.pallas_call(
        functools.partial(_hgrn_kernel, cb=cb),
        grid=(bsz // bb, t // tb),
        in_specs=[blk, blk, blk, blk, vt_blk, s0_blk],
        out_specs=[blk, st_blk],
        out_shape=[jax.ShapeDtypeStruct(qs.shape, F32),
                   jax.ShapeDtypeStruct(s0_all.shape[1:], F32)],
        scratch_shapes=[pltpu.VMEM((bb, HA_HEADS, HA_DV, HA_DK), F32),
                        pltpu.VMEM((bb * HA_HEADS, cb, HA_DK), F32),
                        pltpu.VMEM((SUBLANES, SUBLANES, LANES), F32)],
        compiler_params=_cparams(("parallel", "arbitrary")),
        name="hgrn",
    )(qs, kin, logf, v, vt, s0_all)


def _attn_prompt_kernel(qt_ref, kc_ref, kt_ref, o_ref, s0, s1, x0, x1, p0, p1, a0, a1, m_scr,
                        l_scr, acc_scr, *, tq, tk):
    qi = pl.program_id(1)
    jd = (qi * tq) // tk
    shift = CHUNK.bit_length() - 1
    head_cols = [slice(h * tq, (h + 1) * tq) for h in range(HB_HEADS)]

    def scores(j, s_dst, x_dst):
        keys = kc_ref[pl.ds(pl.multiple_of(j * tk, tk), tk), :]
        for cols in head_cols:
            s = _dot(keys, qt_ref[0, :, cols])
            s_dst[:, cols] = s
            x_dst[:, cols] = jnp.max(s, axis=0, keepdims=True)

    def softmax(j, s_src, x_src, p_dst, a_dst, masked):
        if masked:
            kpos = j * tk + lax.broadcasted_iota(jnp.int32, (tk, 1), 0)
            qpos = qi * tq + lax.broadcasted_iota(jnp.int32, (1, tq), 1)
            visible = (kpos >> shift) <= (qpos >> shift)
        for cols in head_cols:
            m_prev = m_scr[:, cols]
            if masked:
                s = jnp.where(visible, s_src[:, cols], NEG_INF)
                m_new = jnp.maximum(m_prev, jnp.max(s, axis=0, keepdims=True))
            else:
                s = s_src[:, cols]
                m_new = jnp.maximum(m_prev, x_src[:, cols])
            alpha = jnp.exp2(m_prev - m_new)
            p = jnp.exp2(s - m_new)
            l_scr[:, cols] = alpha * l_scr[:, cols] + jnp.sum(p, axis=0, keepdims=True)
            m_scr[:, cols] = m_new
            a_dst[:, cols] = alpha
            p_dst[:, cols] = p.astype(BF16)

    def values(j, p_src, a_src):
        for cols in head_cols:
            acc_scr[:, cols] = (a_src[:, cols] * acc_scr[:, cols]
                                + _dot(kt_ref[j], p_src[:, cols]))

    set0, set1 = (s0, x0, p0, a0), (s1, x1, p1, a1)

    def trip(j, cur, nxt):
        scores(j + 1, nxt[0], nxt[1])
        softmax(j, cur[0], cur[1], cur[2], cur[3], False)
        values(jnp.maximum(j - 1, 0), nxt[2], nxt[3])

    m_scr[...] = jnp.full(m_scr.shape, NEG_INF, F32)
    l_scr[...] = jnp.zeros(l_scr.shape, F32)
    acc_scr[...] = jnp.zeros(acc_scr.shape, F32)
    p1[...] = jnp.zeros(p1.shape, BF16)
    a1[...] = jnp.ones(a1.shape, F32)
    scores(0, s0, x0)

    def pair(jj, carry):
        trip(2 * jj, set0, set1)
        trip(2 * jj + 1, set1, set0)
        return carry

    lax.fori_loop(0, jd // 2, pair, 0)

    @pl.when(jd % 2 == 1)
    def _():
        trip(jd - 1, set0, set1)

    def drain(cur, prv):
        softmax(jd, cur[0], cur[1], cur[2], cur[3], True)
        values(jnp.maximum(jd - 1, 0), prv[2], prv[3])
        for h, cols in enumerate(head_cols):
            acc = (cur[3][:, cols] * acc_scr[:, cols] + _dot(kt_ref[jd], cur[2][:, cols]))
            o_ref[0, h] = (acc * (1.0 / l_scr[:, cols])).T.astype(BF16)

    pl.when(jd % 2 == 0)(functools.partial(drain, set0, set1))
    pl.when(jd % 2 == 1)(functools.partial(drain, set1, set0))


def _attn_prompt(qt, kcat, ckvt, *, bsz, t, tq, tk):
    nq = t // tq
    nk = t // tk
    cols = HB_HEADS * tq
    single = pl.Buffered(1)
    return pl.pallas_call(
        functools.partial(_attn_prompt_kernel, tq=tq, tk=tk),
        grid=(bsz, nq),
        in_specs=[
            pl.BlockSpec((1, QK_DIM, cols), lambda b, i: (b * nq + i, 0, 0)),
            pl.BlockSpec((t, QK_DIM), lambda b, i: (b, 0), pipeline_mode=single),
            pl.BlockSpec((nk, KV_LORA, tk), lambda b, i: (b, 0, 0), pipeline_mode=single),
        ],
        out_specs=pl.BlockSpec((1, HB_HEADS, tq, KV_LORA), lambda b, i: (b * nq + i, 0, 0, 0)),
        out_shape=jax.ShapeDtypeStruct((bsz * nq, HB_HEADS, tq, KV_LORA), BF16),
        scratch_shapes=[pltpu.VMEM((tk, cols), F32), pltpu.VMEM((tk, cols), F32),
                        pltpu.VMEM((1, cols), F32), pltpu.VMEM((1, cols), F32),
                        pltpu.VMEM((tk, cols), BF16), pltpu.VMEM((tk, cols), BF16),
                        pltpu.VMEM((1, cols), F32), pltpu.VMEM((1, cols), F32),
                        pltpu.VMEM((1, cols), F32), pltpu.VMEM((1, cols), F32),
                        pltpu.VMEM((KV_LORA, cols), F32)],
        compiler_params=_cparams(("parallel", "arbitrary")),
        name="attn_prompt",
    )(qt, kcat, ckvt)


def _attn_sample_kernel(qa_ref, qp_ref, pckv_ref, pkpe_ref, nckv_ref, nkpe_ref, o_ref):
    _, _, ts, _ = qa_ref.shape
    rows = HB_HEADS * ts
    qa = qa_ref[0].reshape(rows, KV_LORA)
    qp = qp_ref[0].reshape(rows, LANES)[:, :ROPE_DIM]
    pckv = pckv_ref[0].astype(BF16)
    nckv = nckv_ref[...].astype(BF16)
    s_past = _dot_nt(qa, pckv) + _dot(qp, pkpe_ref[0].astype(BF16))
    s_new = _dot_nt(qa, nckv) + _dot_nt(qp, nkpe_ref[...].astype(BF16))
    m = jnp.maximum(jnp.max(s_past, axis=-1, keepdims=True), jnp.max(s_new, axis=-1, keepdims=True))
    p_past = jnp.exp2(s_past - m)
    p_new = jnp.exp2(s_new - m)
    l = jnp.sum(p_past, axis=-1, keepdims=True) + jnp.sum(p_new, axis=-1, keepdims=True)
    o = (_dot(p_past.astype(BF16), pckv) + _dot(p_new.astype(BF16), nckv)) / l
    o_ref[0] = o.astype(BF16).reshape(HB_HEADS, ts, KV_LORA)


def _attn_sample(qabs, qpe, past_ckv, past_kpe, layer, ckv, kpe, *, ts):
    _, bsz, past, _ = past_ckv.shape
    return pl.pallas_call(
        _attn_sample_kernel,
        grid=(bsz,),
        in_specs=[
            pl.BlockSpec((1, HB_HEADS, ts, KV_LORA), lambda b: (b, 0, 0, 0)),
            pl.BlockSpec((1, HB_HEADS, ts, LANES), lambda b: (b, 0, 0, 0)),
            pl.BlockSpec((None, 1, past, KV_LORA), lambda b: (layer, b, 0, 0)),
            pl.BlockSpec((None, 1, ROPE_DIM, past), lambda b: (layer, b, 0, 0)),
            pl.BlockSpec((None, ts, KV_LORA), lambda b: (layer, b, 0)),
            pl.BlockSpec((ts, ROPE_DIM), lambda b: (b, 0)),
        ],
        out_specs=pl.BlockSpec((1, HB_HEADS, ts, KV_LORA), lambda b: (b, 0, 0, 0)),
        out_shape=jax.ShapeDtypeStruct(qabs.shape, BF16),
        compiler_params=_cparams(("parallel",)),
        name="attn_sample",
    )(qabs, qpe, past_ckv, past_kpe, ckv, kpe)


def _gates(logits, bias):
    tm = logits.shape[0]
    lane = lax.broadcasted_iota(jnp.int32, (tm, LANES), 1)
    pos = lane % EXPERTS_PER_GROUP
    valid = lane < N_EXPERTS
    scores = jax.nn.sigmoid(logits)
    sel = jnp.where(valid, scores + bias, -jnp.inf)

    others = []
    for r in range(1, EXPERTS_PER_GROUP):
        others.append(jnp.where(pos >= r, pltpu.roll(sel, r, 1),
                                pltpu.roll(sel, LANES - (EXPERTS_PER_GROUP - r), 1)))
    a, b, c, d = sel, others[0], others[1], others[2]
    hi1, lo1 = jnp.maximum(a, b), jnp.minimum(a, b)
    hi2, lo2 = jnp.maximum(c, d), jnp.minimum(c, d)
    gscore = jnp.maximum(hi1, hi2) + jnp.maximum(jnp.minimum(hi1, hi2), jnp.maximum(lo1, lo2))
    gmax = jnp.max(gscore, axis=-1, keepdims=True)
    group = (lane // EXPERTS_PER_GROUP).astype(F32)
    gidx = jnp.min(jnp.where(gscore == gmax, group, float(LANES)), axis=-1, keepdims=True)

    rank = jnp.zeros((tm, LANES), jnp.int32)
    for r, o in enumerate(others, start=1):
        ahead = (o > sel) | ((o == sel) & (pos >= r))
        rank = rank + ahead.astype(jnp.int32)
    chosen = (group == gidx) & (rank < 2) & valid
    w = jnp.where(chosen, scores, 0.0)
    w = w / jnp.sum(w, axis=-1, keepdims=True)
    return jnp.where(lane == GROUP_LANE, gidx, w)


def _merge_kernel(x_ref, o_ref, gs_ref, olat_ref, wuv_ref, wo_ref, hg_ref, mg_ref,
                  g1_ref, b1_ref, wr_ref, rb_ref, x1g_ref, *, alpha):
    _, _, tq, _ = olat_ref.shape
    tm = x_ref.shape[0]
    hg = hg_ref[...]
    d_a = HA_HEADS * HA_DV
    halves = [(i * tm // 2, (i + 1) * tm // 2) for i in range(2)]

    def lat(h, lo, hi):
        if tq >= hi - lo:
            return olat_ref[lo // tq, h, lo % tq:lo % tq + hi - lo]
        return jnp.concatenate([olat_ref[s, h] for s in range(lo // tq, hi // tq)], axis=0)

    ob = [[_dot(lat(h, lo, hi), wuv_ref[h]) for h in range(HB_HEADS)] for lo, hi in halves]
    oa = [jnp.concatenate([_rms(o_ref[h, lo:hi, :], hg) * gs_ref[lo:hi, h * HA_DV:(h + 1) * HA_DV]
                           for h in range(HA_HEADS)], axis=-1) for lo, hi in halves]
    ob = [_rms(jnp.concatenate(o, axis=-1), mg_ref[...]) for o in ob]
    mix = [_dot(a.astype(BF16), wo_ref[:d_a, :]) + _dot(o.astype(BF16), wo_ref[d_a:, :])
           for a, o in zip(oa, ob)]
    x1 = [_layer_norm(alpha * x_ref[lo:hi, :] + m, g1_ref[...], b1_ref[...])
          for (lo, hi), m in zip(halves, mix)]
    logits = [_dot(x.astype(BF16), wr_ref[...]) for x in x1]
    for (lo, hi), x, lg in zip(halves, x1, logits):
        for c in range(X_PIECES):
            x1g_ref[c, lo:hi, :] = x[:, c * SC_ROW:(c + 1) * SC_ROW]
        x1g_ref[X_PIECES, lo:hi, :LANES] = _gates(lg, rb_ref[...])
        x1g_ref[X_PIECES, lo:hi, LANES:] = jnp.zeros((hi - lo, SC_ROW - LANES), F32)


def _merge(x, o_raw, gs, olat, wuv, wo, hg, mg, g1, b1, wr, rb, *, tm, tq, alpha):
    n = x.shape[0]
    row = lambda w: pl.BlockSpec((tm, w), lambda i: (i, 0))
    return pl.pallas_call(
        functools.partial(_merge_kernel, alpha=alpha),
        grid=(n // tm,),
        in_specs=[row(D_MODEL), pl.BlockSpec((HA_HEADS, tm, HA_DV), lambda i: (0, i, 0)), row(HA_W),
                  pl.BlockSpec((tm // tq, HB_HEADS, tq, KV_LORA), lambda i: (i, 0, 0, 0)),
                  _full(wuv.shape), _full(wo.shape), _full(hg.shape), _full(mg.shape),
                  _full(g1.shape), _full(b1.shape), _full(wr.shape), _full(rb.shape)],
        out_specs=pl.BlockSpec((X_PIECES + 1, tm, SC_ROW), lambda i: (0, i, 0)),
        out_shape=jax.ShapeDtypeStruct((X_PIECES + 1, n, SC_ROW), F32),
        compiler_params=_cparams(("parallel",)),
        name="merge",
    )(x, o_raw, gs, olat, wuv, wo, hg, mg, g1, b1, wr, rb)


def _sc_scatter_rows(src, dst, m):
    parts, n, _ = src.shape
    windows = dst.shape[0] // SC_WINDOW
    src_windows = n // SC_WINDOW
    pieces = src.reshape(parts * n, SC_ROW)
    piece_dst = (jnp.arange(parts, dtype=jnp.int32)[:, None] * m + dst[None, :]).reshape(1, -1)
    mesh = plsc.VectorSubcoreMesh(core_axis_name="core", subcore_axis_name="subcore")

    @functools.partial(pl.kernel, out_type=jax.ShapeDtypeStruct((parts * m, SC_ROW), src.dtype),
                       mesh=mesh)
    def scatter(x_hbm, i_hbm, o_hbm):
        def body(x_vmem, i_vmem):
            pltpu.sync_copy(x_vmem, o_hbm.at[i_vmem.at[0]])

        pltpu.emit_pipeline(
            body,
            grid=(parts * windows,),
            in_specs=[pl.BlockSpec((SC_WINDOW, SC_ROW),
                                   lambda i: ((i // windows) * src_windows
                                              + (i % windows) % src_windows, 0)),
                      pl.BlockSpec((1, SC_WINDOW), lambda i: (0, i))],
            out_specs=[],
            core_axis_name=("core", "subcore"),
            dimension_semantics=(pltpu.PARALLEL,),
        )(x_hbm, i_hbm)

    return scatter(pieces, piece_dst).reshape(parts, m, SC_ROW)


def _sc_gather_rows(src, idx):
    parts, n, _ = src.shape
    m = idx.shape[0] * parts
    pieces = src.reshape(parts * n, SC_ROW)
    piece_idx = (jnp.arange(parts, dtype=jnp.int32)[:, None] * n + idx[None, :]).reshape(1, m)
    mesh = plsc.VectorSubcoreMesh(core_axis_name="core", subcore_axis_name="subcore")

    @functools.partial(pl.kernel, out_type=jax.ShapeDtypeStruct((m, SC_ROW), src.dtype), mesh=mesh)
    def gather(x_hbm, i_hbm, o_hbm):
        def body(i_vmem, o_vmem):
            pltpu.sync_copy(x_hbm.at[i_vmem.at[0]], o_vmem)

        pltpu.emit_pipeline(
            body,
            grid=(m // SC_WINDOW,),
            in_specs=[pl.BlockSpec((1, SC_WINDOW), lambda i: (0, i))],
            out_specs=[pl.BlockSpec((SC_WINDOW, SC_ROW), lambda i: (i, 0))],
            core_axis_name=("core", "subcore"),
            dimension_semantics=(pltpu.PARALLEL,),
        )(i_hbm, o_hbm)

    return gather(pieces, piece_idx).reshape(parts, idx.shape[0], SC_ROW)


def _moe_kernel(tg_ref, xs_ref, wg_ref, wu_ref, wd_ref, y_ref, acc_scr):
    tm = xs_ref.shape[1]
    per_step = wg_ref.shape[0]
    step = pl.program_id(1)
    first_expert = tg_ref[pl.program_id(0)] * EXPERTS_PER_GROUP + step * per_step
    xb = jnp.concatenate([xs_ref[c] for c in range(X_PIECES)], axis=1).astype(BF16)
    gates = xs_ref[X_PIECES, :, :LANES]
    lane = lax.broadcasted_iota(jnp.int32, (tm, LANES), 1)
    acc = jnp.zeros((tm, D_MODEL), F32)
    for k in range(per_step):
        hmid = (_silu(_dot(xb, wg_ref[k].astype(BF16))) * _dot(xb, wu_ref[k].astype(BF16)))
        gcol = jnp.sum(jnp.where(lane == first_expert + k, gates, 0.0), axis=-1, keepdims=True)
        acc = acc + gcol * _dot(hmid.astype(BF16), wd_ref[k].astype(BF16))

    @pl.when(step == 0)
    def _():
        acc_scr[...] = acc

    @pl.when(step > 0)
    def _():
        acc_scr[...] += acc

    @pl.when(step == pl.num_programs(1) - 1)
    def _():
        for c in range(D_MODEL // SC_ROW):
            y_ref[c] = acc_scr[:, c * SC_ROW:(c + 1) * SC_ROW]


def _moe_sorted(tile_group, xs, wg, wu, wd, layer, *, tm, per_step, weight_buffers):
    m = xs.shape[1]
    steps = EXPERTS_PER_GROUP // per_step
    mode = dict(pipeline_mode=pl.Buffered(weight_buffers))
    wspec = lambda shape: pl.BlockSpec((None, per_step) + shape,
                                       lambda i, k, tg: (layer, tg[i] * steps + k, 0, 0), **mode)
    return pl.pallas_call(
        _moe_kernel,
        grid_spec=pltpu.PrefetchScalarGridSpec(
            num_scalar_prefetch=1,
            grid=(m // tm, steps),
            in_specs=[pl.BlockSpec((X_PIECES + 1, tm, SC_ROW), lambda i, k, tg: (0, i, 0)),
                      wspec((D_MODEL, D_EXPERT)), wspec((D_MODEL, D_EXPERT)),
                      wspec((D_EXPERT, D_MODEL))],
            out_specs=pl.BlockSpec((D_MODEL // SC_ROW, tm, SC_ROW), lambda i, k, tg: (0, i, 0)),
            scratch_shapes=[pltpu.VMEM((tm, D_MODEL), F32)],
        ),
        out_shape=jax.ShapeDtypeStruct((D_MODEL // SC_ROW, m, SC_ROW), F32),
        compiler_params=_cparams(("arbitrary", "arbitrary")),
        name="moe",
    )(tile_group, xs, wg, wu, wd)


def _route(x1g, *, tm):
    n = x1g.shape[1]
    n_tiles = n // tm + N_GROUPS - 1
    m = n_tiles * tm
    gidx = x1g[X_PIECES, :, GROUP_LANE].astype(jnp.int32)
    onehot = (gidx[:, None] == jnp.arange(N_GROUPS, dtype=jnp.int32)[None, :]).astype(jnp.int32)
    counts = jnp.sum(onehot, axis=0)
    tiles = (counts + tm - 1) // tm
    tile_end = jnp.cumsum(tiles)
    tile_start = tile_end - tiles
    rank = jnp.sum((jnp.cumsum(onehot, axis=0) - onehot) * onehot, axis=1)
    pos = jnp.take(tile_start, gidx) * tm + rank
    pad_start = tile_start * tm + counts
    pad_end = (tile_end * tm).at[N_GROUPS - 1].set(m)
    pad_cum = jnp.cumsum(pad_end - pad_start)
    k = jnp.arange(m - n, dtype=jnp.int32)
    pg = jnp.sum(k[:, None] >= pad_cum[None, :], axis=1)
    pad_rows = jnp.take(pad_start, pg) + k - jnp.take(pad_cum - (pad_end - pad_start), pg)
    dst = jnp.concatenate([pos, pad_rows]).astype(jnp.int32)
    tile_group = jnp.sum(jnp.arange(n_tiles, dtype=jnp.int32)[:, None] >= tile_end[None, :], axis=1)
    return dst, jnp.minimum(tile_group, N_GROUPS - 1).astype(jnp.int32), m


def _post_kernel(x1_ref, f_ref, p_ref, g2_ref, b2_ref, wpg_ref, wp_ref, y_ref, *, alpha):
    tm = x1_ref.shape[1]
    halves = [(i * tm // 2, (i + 1) * tm // 2) for i in range(2)]
    emb = [_dot(p_ref[lo:hi, :].astype(BF16), wp_ref[...]) for lo, hi in halves]

    def rows(ref, lo, hi):
        return jnp.concatenate([ref[c, lo:hi, :] for c in range(X_PIECES)], axis=1)

    x2 = [_layer_norm(alpha * rows(x1_ref, lo, hi) + rows(f_ref, lo, hi), g2_ref[...], b2_ref[...])
          for lo, hi in halves]
    gate = [_dot(x.astype(BF16), wpg_ref[...]) for x in x2]
    for (lo, hi), x, g, e in zip(halves, x2, gate, emb):
        y_ref[lo:hi, :] = x + jax.nn.sigmoid(g) * e


def _post(x1g, ffn, p_all, layer, g2, b2, wpg, wp, *, tm, alpha):
    _, n, _ = ffn.shape
    row = lambda w: pl.BlockSpec((tm, w), lambda i: (i, 0))
    pieces = pl.BlockSpec((X_PIECES, tm, SC_ROW), lambda i: (0, i, 0))
    return pl.pallas_call(
        functools.partial(_post_kernel, alpha=alpha),
        grid=(n // tm,),
        in_specs=[pieces, pieces,
                  pl.BlockSpec((None, tm, PLE_DIM), lambda i: (layer, i, 0)),
                  _full(g2.shape), _full(b2.shape), _full(wpg.shape), _full(wp.shape)],
        out_specs=row(D_MODEL),
        out_shape=jax.ShapeDtypeStruct((n, D_MODEL), F32),
        compiler_params=_cparams(("parallel",)),
        name="post",
    )(x1g, ffn, p_all, g2, b2, wpg, wp)


def _rot_cols(w):
    half = ROPE_DIM // 2
    return jnp.concatenate([-w[..., half:], w[..., :half]], axis=-1)


def _prep_layer(l, w_in, w_qb, w_kvb, w_o, w_gate, w_up, w_down, w_ple, w_ple_gate):
    win = w_in[l]
    win_aug = jnp.concatenate([win, _rot_cols(win[:, COL_KPE:])], axis=-1).astype(BF16)
    wqb = w_qb[l].reshape(Q_LORA, HB_HEADS, NOPE_DIM + ROPE_DIM)
    wqb_aug = jnp.concatenate([wqb, _rot_cols(wqb[..., NOPE_DIM:])], axis=-1)
    wqb_aug = wqb_aug.reshape(Q_LORA, HB_HEADS * Q_HEAD_AUG).astype(BF16)
    wkvb = w_kvb[l].reshape(KV_LORA, HB_HEADS, NOPE_DIM + V_DIM)
    wuk_t = jnp.transpose(wkvb[..., :NOPE_DIM], (1, 2, 0)).astype(BF16)
    wuv = jnp.transpose(wkvb[..., NOPE_DIM:], (1, 0, 2)).astype(BF16)
    return dict(win=win_aug, wqb=wqb_aug, wuk=wuk_t, wuv=wuv, wo=w_o[l].astype(BF16),
                wg=w_gate, wu=w_up, wd=w_down,
                wp=w_ple[l].astype(BF16), wpg=w_ple_gate[l].astype(BF16))


def _rope_table(pos):
    inv = ROPE_THETA ** (-np.arange(0, ROPE_DIM, 2, dtype=np.float64) / ROPE_DIM)
    ang = pos.astype(np.float64)[:, None] * inv[None, :]
    cos, sin = np.cos(ang), np.sin(ang)
    return np.concatenate([cos, cos, sin, sin], axis=-1).astype(np.float32)


def _tiles(n, t, prompt):
    if prompt:
        tm = min(512, t)
        return dict(tm=tm, tq=tm, tk=tm, tb=tm, cb=min(128, t), bb=2, tmoe=min(512, n),
                    experts_per_step=EXPERTS_PER_GROUP, weight_buffers=1)
    return dict(tm=min(512, n), tq=t, tk=None, tb=t, cb=t, bb=2, tmoe=min(256, n),
                experts_per_step=EXPERTS_PER_GROUP, weight_buffers=2)


def _layer(x, p_all, layer, cs, s0_all, past, prev_ckv, lbp, prm, small, *, alpha, prompt):
    bsz, t, _ = x.shape
    n = bsz * t
    tl = _tiles(n, t, prompt)
    x2d = x.reshape(n, D_MODEL)
    tm, tb, cb = tl["tm"], tl["tb"], tl["cb"]
    outs = _inproj(x2d, cs, prm["win"], lbp, small["qg"], prm["wqb"], small["kvg"], prm["wuk"],
                   prev_ckv, tm=tm, tq=tl["tq"], cb=cb, prompt=prompt)
    qs, kin, logf, v, gs, ckv_all, kpe = outs[:7]
    r4 = lambda a: a.reshape(HA_HEADS, bsz, t, HA_DK)
    if prompt:
        qt, vt, kcat, ckvt = outs[7:]
        vt = vt.reshape(bsz, t // tb, HA_HEADS, tb // cb, HA_DV, cb)
    else:
        qabs, qpe = outs[7:]
        vt = jnp.transpose(r4(v), (1, 0, 3, 2)).astype(BF16)
        vt = vt.reshape(bsz, 1, HA_HEADS, 1, HA_DV, t)
    o_raw, s_new = _hgrn(r4(qs), r4(kin), r4(logf), r4(v), vt, s0_all, layer if not prompt else 0,
                         bb=tl["bb"], tb=tb, cb=cb)
    if prompt:
        olat = _attn_prompt(qt, kcat, ckvt, bsz=bsz, t=t, tq=tl["tq"], tk=tl["tk"])
    else:
        olat = _attn_sample(qabs, qpe, past[0], past[1], layer, ckv_all, kpe, ts=t)
    x1g = _merge(x2d, o_raw.reshape(HA_HEADS, n, HA_DV), gs, olat, prm["wuv"], prm["wo"],
                 small["hg"], small["mg"], small["g1"], small["b1"],
                 small["wr"], small["rb"], tm=tm, tq=tl["tq"], alpha=alpha)
    tmoe = tl["tmoe"]
    dst, tile_group, m_sorted = _route(x1g, tm=tmoe)
    xs = _sc_scatter_rows(x1g, dst, m_sorted)
    ys = _moe_sorted(tile_group, xs, prm["wg"], prm["wu"], prm["wd"], layer, tm=tmoe,
                     per_step=tl["experts_per_step"], weight_buffers=tl["weight_buffers"])
    ffn = _sc_gather_rows(ys, dst[:n])
    y = _post(x1g, ffn, p_all.reshape(-1, n, PLE_DIM), layer, small["g2"], small["b2"],
              prm["wpg"], prm["wp"], tm=min(2 * tm, n), alpha=alpha)
    return y.reshape(bsz, t, D_MODEL), s_new, ckv_all, kpe.reshape(bsz, t, ROPE_DIM)


def kernel(x_prompt, x_sample, p_prompt, p_sample, state_hgrn, cache_ckv, cache_kpe, w_in,
           lb_logits, hgrn_norm_g, q_norm_g, w_qb, kv_norm_g, w_kvb, mla_norm_g, w_o,
           ln1_g, ln1_b, ln2_g, ln2_b, w_router, router_bias, w_gate, w_up, w_down,
           w_ple, w_ple_gate):
    depth = w_in.shape[0]
    alpha = (2 * depth) ** 0.25
    bp, tp, _ = x_prompt.shape
    bs, ts, _ = x_sample.shape
    past = cache_ckv.shape[2]

    sm = jax.nn.softmax(lb_logits.astype(F32), axis=0)
    lb_all = jnp.maximum(jnp.cumsum(sm, axis=0) - sm[0:1], 0.0)
    lbp_all = jnp.stack([jnp.log(lb_all), jnp.log1p(-lb_all), 1.0 - lb_all], axis=1)

    cs_p = jnp.asarray(_rope_table(np.arange(tp)))
    tm_s = _tiles(bs * ts, ts, False)["tm"]
    cs_s = jnp.asarray(np.tile(_rope_table(past + np.arange(ts)), (tm_s // ts, 1)))
    wr = jnp.pad(w_router, ((0, 0), (0, LANES - N_EXPERTS))).astype(BF16)
    rb = jnp.pad(router_bias.astype(F32), (0, LANES - N_EXPERTS)).reshape(1, LANES)
    s0_p = jnp.zeros((1, bp, HA_HEADS, HA_DK, HA_DV), F32)
    cache_kpe_t = jnp.swapaxes(cache_kpe, 2, 3)

    yp, ys = x_prompt, x_sample
    res = [[] for _ in range(4)]
    cp = cs_ = None
    for l in range(depth):
        prm = _prep_layer(l, w_in, w_qb, w_kvb, w_o, w_gate, w_up, w_down, w_ple, w_ple_gate)
        row = lambda a: a[l].reshape(1, -1).astype(F32)
        small = dict(qg=row(q_norm_g), kvg=row(kv_norm_g), hg=row(hgrn_norm_g), mg=row(mla_norm_g),
                     g1=row(ln1_g), b1=row(ln1_b), g2=row(ln2_g), b2=row(ln2_b), wr=wr, rb=rb)
        yp, sp, cp, kp = _layer(yp, p_prompt, l, cs_p, s0_p, None, cp, lbp_all[l], prm, small,
                                alpha=alpha, prompt=True)
        ys, ss, cs_, ks = _layer(ys, p_sample, l, cs_s, state_hgrn, (cache_ckv, cache_kpe_t), cs_,
                                 lbp_all[l], prm, small, alpha=alpha, prompt=False)
        for lst, a in zip(res, (sp, kp, ss, ks)):
            lst.append(a)
    sp, kp, ss, ks = (jnp.stack(a) for a in res)
    return (yp, ys, sp, cp.reshape(depth, bp, tp, KV_LORA), kp, ss,
            cs_.reshape(depth, bs, ts, KV_LORA), ks)
```

```python
import functools

import jax
import jax.numpy as jnp
import numpy as np
from jax import lax
from jax.experimental import pallas as pl
from jax.experimental.pallas import tpu as pltpu
from jax.experimental.pallas import tpu_sc as plsc

F32 = jnp.float32
BF16 = jnp.bfloat16

D_MODEL = 1024
HA_HEADS = 4
HA_DK = 128
HA_DV = 128
HB_HEADS = 4
Q_LORA = 384
KV_LORA = 256
NOPE_DIM = 128
ROPE_DIM = 64
V_DIM = 128
ROPE_THETA = 10000.0
MLA_SCALE = (NOPE_DIM + ROPE_DIM) ** -0.5
LOG2E = 1.4426950408889634
Q_SCALE = MLA_SCALE * LOG2E
CHUNK = 64
N_EXPERTS = 16
N_GROUPS = 4
EXPERTS_PER_GROUP = N_EXPERTS // N_GROUPS
D_EXPERT = 512
PLE_DIM = 256
NEG_INF = -1e30

HA_W = HA_HEADS * HA_DK
COL_CQ = 4 * HA_W
COL_CKV = COL_CQ + Q_LORA
COL_KPE = COL_CKV + KV_LORA
D_IN_AUG = COL_KPE + 2 * ROPE_DIM
Q_HEAD_AUG = NOPE_DIM + 2 * ROPE_DIM

LANES = 128
SUBLANES = 8
QK_DIM = KV_LORA + LANES
GROUP_LANE = N_EXPERTS
SC_WINDOW = 128
SC_ROW = 256
X_PIECES = D_MODEL // SC_ROW
SUB_BLOCK = 16
VMEM_LIMIT = 56 * 1024 * 1024


def _cparams(sem, vmem=VMEM_LIMIT):
    return pltpu.CompilerParams(dimension_semantics=sem, vmem_limit_bytes=vmem)


def _dot(a, b):
    return jnp.dot(a, b, preferred_element_type=F32)


def _dot_nt(a, b):
    return lax.dot_general(a, b, (((1,), (1,)), ((), ())), preferred_element_type=F32)


def _rms(x, g, eps=1e-6):
    return x * lax.rsqrt(jnp.mean(x * x, axis=-1, keepdims=True) + eps) * g


def _layer_norm(x, g, b, eps=1e-5):
    mu = jnp.mean(x, axis=-1, keepdims=True)
    xc = x - mu
    var = jnp.mean(xc * xc, axis=-1, keepdims=True)
    return xc * lax.rsqrt(var + eps) * g + b


def _silu(x):
    return x * jax.nn.sigmoid(x)


def _full(shape):
    nd = len(shape)
    return pl.BlockSpec(shape, lambda *_: (0,) * nd)


def _inproj_kernel(x_ref, cs_ref, win_ref, lb_ref, qg_ref, wqb_ref, kvg_ref, wuk_ref, prev_ref,
                   qs_ref, kin_ref, logf_ref, v_ref, gs_ref, ckv_ref, kpe_ref, *rest,
                   tq, cb, prompt, n_prev):
    tm = x_ref.shape[0]
    x = x_ref[...].astype(BF16)

    def sect(lo, hi):
        return _dot(x, win_ref[:, lo:hi])

    def put_heads(ref, val):
        for h in range(HA_HEADS):
            ref[h] = val[:, h * HA_DK:(h + 1) * HA_DK]

    cqn = _rms(sect(COL_CQ, COL_CKV), qg_ref[...]).astype(BF16)

    put_heads(qs_ref, _silu(sect(0, HA_W)))
    fa = sect(HA_W, 2 * HA_W)
    log_lb = lb_ref[0:1, :]
    log1m_lb = lb_ref[1:2, :]
    one_m_lb = lb_ref[2:3, :]
    e = jnp.exp(-jnp.abs(fa))
    r = 1.0 / (1.0 + e)
    c = log1m_lb + jnp.minimum(fa, 0.0) + jnp.log(r)
    put_heads(logf_ref, jnp.maximum(log_lb, c) + jnp.log(1.0 + jnp.exp(-jnp.abs(log_lb - c))))
    put_heads(kin_ref, one_m_lb * jnp.where(fa >= 0.0, e * r, r))
    v = sect(2 * HA_W, 3 * HA_W)
    put_heads(v_ref, v)
    gs_ref[...] = _silu(sect(3 * HA_W, 4 * HA_W))

    cs = cs_ref[...]

    def rope(t):
        prod = t * cs
        return prod + pltpu.roll(prod, ROPE_DIM, 1)

    ckv = _rms(sect(COL_CKV, COL_KPE), kvg_ref[...])
    for l in range(n_prev):
        ckv_ref[l] = prev_ref[l]
    ckv_ref[n_prev] = ckv
    kpe2 = rope(sect(COL_KPE, D_IN_AUG))
    kpe_ref[...] = kpe2[:, :ROPE_DIM]
    if prompt:
        qt_ref, vt_ref, kcat_ref, ckvt_ref = rest
        for h in range(HA_HEADS):
            for ci in range(tm // cb):
                vt_ref[0, h, ci] = v[ci * cb:(ci + 1) * cb, h * HA_DV:(h + 1) * HA_DV].T.astype(BF16)
        kcat_ref[:, :KV_LORA] = ckv.astype(BF16)
        kcat_ref[:, KV_LORA:] = kpe2.astype(BF16)
        ckvt_ref[0] = ckv.T.astype(BF16)
    else:
        qabs_ref, qpe_ref = rest

    lane = lax.broadcasted_iota(jnp.int32, (tm, LANES), 1)
    qhs = [_dot(cqn, wqb_ref[:, h * Q_HEAD_AUG:(h + 1) * Q_HEAD_AUG]) for h in range(HB_HEADS)]
    qabss = [_dot(qhs[h][:, :NOPE_DIM].astype(BF16), wuk_ref[h]) for h in range(HB_HEADS)]
    for h in range(HB_HEADS):
        qabs = qabss[h] * Q_SCALE
        qpe = jnp.where(lane < ROPE_DIM, rope(qhs[h][:, NOPE_DIM:]) * Q_SCALE, 0.0)
        if prompt:
            qabs_t = qabs.T.astype(BF16)
            qpe_t = qpe.T.astype(BF16)
            for s in range(tm // tq):
                qt_ref[s, :KV_LORA, h * tq:(h + 1) * tq] = qabs_t[:, s * tq:(s + 1) * tq]
                qt_ref[s, KV_LORA:, h * tq:(h + 1) * tq] = qpe_t[:, s * tq:(s + 1) * tq]
        else:
            for s in range(tm // tq):
                qabs_ref[s, h] = qabs[s * tq:(s + 1) * tq].astype(BF16)
                qpe_ref[s, h] = qpe[s * tq:(s + 1) * tq].astype(BF16)


def _inproj(x, cs, w_in, lbp, qg, wqb, kvg, wuk, prev_ckv, *, tm, tq, cb, prompt):
    n = x.shape[0]
    nt = n // tm
    n_prev = 0 if prev_ckv is None else prev_ckv.shape[0]
    if prev_ckv is None:
        prev_ckv = jnp.zeros((1, tm, KV_LORA), F32)
        prev_spec = _full(prev_ckv.shape)
    else:
        prev_spec = pl.BlockSpec((n_prev, tm, KV_LORA), lambda i: (0, i, 0))
    row = lambda w: pl.BlockSpec((tm, w), lambda i: (i, 0))
    heads = pl.BlockSpec((HA_HEADS, tm, HA_DK), lambda i: (0, i, 0))
    qblk = lambda w: pl.BlockSpec((tm // tq, HB_HEADS, tq, w), lambda i: (i, 0, 0, 0))
    head_major = jax.ShapeDtypeStruct((HA_HEADS, n, HA_DK), F32)
    out_shape = [
        head_major,
        head_major,
        head_major,
        head_major,
        jax.ShapeDtypeStruct((n, HA_W), F32),
        jax.ShapeDtypeStruct((n_prev + 1, n, KV_LORA), F32),
        jax.ShapeDtypeStruct((n, ROPE_DIM), F32),
    ]
    out_specs = [heads, heads, heads, heads, row(HA_W),
                 pl.BlockSpec((n_prev + 1, tm, KV_LORA), lambda i: (0, i, 0)), row(ROPE_DIM)]
    if prompt:
        out_shape += [jax.ShapeDtypeStruct((n // tq, QK_DIM, HB_HEADS * tq), BF16),
                      jax.ShapeDtypeStruct((nt, HA_HEADS, tm // cb, HA_DV, cb), BF16),
                      jax.ShapeDtypeStruct((n, QK_DIM), BF16),
                      jax.ShapeDtypeStruct((nt, KV_LORA, tm), BF16)]
        out_specs += [pl.BlockSpec((tm // tq, QK_DIM, HB_HEADS * tq), lambda i: (i, 0, 0)),
                      pl.BlockSpec((1, HA_HEADS, tm // cb, HA_DV, cb), lambda i: (i, 0, 0, 0, 0)),
                      row(QK_DIM),
                      pl.BlockSpec((1, KV_LORA, tm), lambda i: (i, 0, 0))]
    else:
        out_shape += [jax.ShapeDtypeStruct((n // tq, HB_HEADS, tq, KV_LORA), BF16),
                      jax.ShapeDtypeStruct((n // tq, HB_HEADS, tq, LANES), BF16)]
        out_specs += [qblk(KV_LORA), qblk(LANES)]
    return pl.pallas_call(
        functools.partial(_inproj_kernel, tq=tq, cb=cb, prompt=prompt, n_prev=n_prev),
        grid=(nt,),
        in_specs=[row(D_MODEL),
                  pl.BlockSpec((tm, LANES), lambda i: (i % (cs.shape[0] // tm), 0)),
                  _full(w_in.shape), _full(lbp.shape), _full(qg.shape),
                  _full(wqb.shape), _full(kvg.shape), _full(wuk.shape), prev_spec],
        out_specs=out_specs,
        out_shape=out_shape,
        compiler_params=_cparams(("parallel",)),
        name="inproj",
    )(x, cs, w_in, lbp, qg, wqb, kvg, wuk, prev_ckv)


def _hgrn_kernel(q_ref, k_ref, g_ref, v_ref, vt_ref, s0_ref, o_ref, sfin_ref, st_scr, b2_scr,
                 hide_ref, *, cb):
    _, bb, tb, _ = q_ref.shape
    nsub = cb // SUB_BLOCK
    ti = pl.program_id(1)

    @pl.when(ti == 0)
    def _():
        for b in range(bb):
            for h in range(HA_HEADS):
                st_scr[b, h] = s0_ref[b, h].T
        row8 = lax.broadcasted_iota(jnp.int32, (SUBLANES, LANES), 0)
        for s in range(SUBLANES):
            hide_ref[s] = jnp.where(row8 >= s, 0.0, NEG_INF)

    r_i = lax.broadcasted_iota(jnp.int32, (cb, cb), 0)
    c_i = lax.broadcasted_iota(jnp.int32, (cb, cb), 1)
    tril = (r_i >= c_i).astype(F32)

    chains = [(b, h) for b in range(bb) for h in range(HA_HEADS)]

    def matmul_part(ci):
        r0 = pl.multiple_of(ci * cb, cb)
        rows = pl.ds(r0, cb)

        def sub(ref, b, h, lo, size):
            return ref[h, b, pl.ds(pl.multiple_of(r0 + lo, SUB_BLOCK), size), :]

        for b, h in chains:
            bcum = jnp.dot(tril, g_ref[h, b, rows, :], precision=lax.Precision.HIGHEST,
                           preferred_element_type=F32)
            b2_scr[b * HA_HEADS + h] = bcum * LOG2E
        for b, h in chains:
            b2 = b2_scr[b * HA_HEADS + h]
            o_ref[h, b, rows, :] = _dot_nt((q_ref[h, b, rows, :] * jnp.exp2(b2)).astype(BF16),
                                           st_scr[b, h].astype(BF16))
        for j in range(nsub - 1):
            lo, hi = j * SUB_BLOCK, (j + 1) * SUB_BLOCK
            below = pl.ds(pl.multiple_of(r0 + hi, SUB_BLOCK), cb - hi)
            for b, h in chains:
                slot = b * HA_HEADS + h
                bnd = b2_scr[slot, hi - 1:hi, :]
                kd = (sub(k_ref, b, h, lo, SUB_BLOCK)
                      * jnp.exp2(bnd - b2_scr[slot, lo:hi, :])).astype(BF16)
                qe = (sub(q_ref, b, h, hi, cb - hi)
                      * jnp.exp2(b2_scr[slot, hi:, :] - bnd)).astype(BF16)
                a = _dot_nt(qe, kd).astype(BF16)
                o_ref[h, b, below, :] += _dot(a, sub(v_ref, b, h, lo, SUB_BLOCK).astype(BF16))
        for b, h in chains:
            slot = b * HA_HEADS + h
            b_last = b2_scr[slot, cb - 1:cb, :]
            kd_all = (k_ref[h, b, rows, :] * jnp.exp2(b_last - b2_scr[slot])).astype(BF16)
            st_scr[b, h] = st_scr[b, h] * jnp.exp2(b_last) + _dot(vt_ref[b, h, ci], kd_all)

    def pairwise_part(ci, b, h, j):
        slot = b * HA_HEADS + h
        r0 = pl.multiple_of(ci * cb, cb)
        lo = j * SUB_BLOCK
        mid = lo + SUBLANES
        top = pl.ds(pl.multiple_of(r0 + lo, SUBLANES), SUBLANES)
        bot = pl.ds(pl.multiple_of(r0 + mid, SUBLANES), SUBLANES)
        q_top, q_bot = q_ref[h, b, top, :], q_ref[h, b, bot, :]
        b_top, b_bot = b2_scr[slot, lo:mid, :], b2_scr[slot, mid:mid + SUBLANES, :]
        acc_top, acc_bot = o_ref[h, b, top, :], o_ref[h, b, bot, :]
        for s in range(SUB_BLOCK):
            row = pl.ds(r0 + lo + s, 1)
            bs = b2_scr[slot, lo + s:lo + s + 1, :]
            ks = k_ref[h, b, row, :]
            vs = v_ref[h, b, row, :]
            if s < SUBLANES:
                w = q_top * jnp.exp2(b_top - bs + hide_ref[s]) * ks
                acc_top = acc_top + jnp.sum(w, axis=-1, keepdims=True) * vs
                w = q_bot * jnp.exp2(b_bot - bs) * ks
            else:
                w = q_bot * jnp.exp2(b_bot - bs + hide_ref[s - SUBLANES]) * ks
            acc_bot = acc_bot + jnp.sum(w, axis=-1, keepdims=True) * vs
        o_ref[h, b, top, :] = acc_top
        o_ref[h, b, bot, :] = acc_bot

    def chunk(ci, carry):
        matmul_part(ci)
        for j in range(nsub):
            for b, h in chains:
                pairwise_part(ci, b, h, j)
        return carry

    lax.fori_loop(0, tb // cb, chunk, 0)

    @pl.when(ti == pl.num_programs(1) - 1)
    def _():
        for b in range(bb):
            for h in range(HA_HEADS):
                sfin_ref[b, h] = st_scr[b, h].T


def _hgrn(qs, kin, logf, v, vt, s0_all, layer, *, bb, tb, cb):
    _, bsz, t, _ = qs.shape
    blk = pl.BlockSpec((HA_HEADS, bb, tb, HA_DK), lambda i, j: (0, i, j, 0))
    st_blk = pl.BlockSpec((bb, HA_HEADS, HA_DK, HA_DV), lambda i, j: (i, 0, 0, 0))
    s0_blk = pl.BlockSpec((None, bb, HA_HEADS, HA_DK, HA_DV), lambda i, j: (layer, i, 0, 0, 0))
    vt_blk = pl.BlockSpec((bb, None, HA_HEADS, tb // cb, HA_DV, cb),
                          lambda i, j: (i, j, 0, 0, 0, 0))
    return pl.pallas_call(
        functools.partial(_hgrn_kernel, cb=cb),
        grid=(bsz // bb, t // tb),
        in_specs=[blk, blk, blk, blk, vt_blk, s0_blk],
        out_specs=[blk, st_blk],
        out_shape=[jax.ShapeDtypeStruct(qs.shape, F32),
                   jax.ShapeDtypeStruct(s0_all.shape[1:], F32)],
        scratch_shapes=[pltpu.VMEM((bb, HA_HEADS, HA_DV, HA_DK), F32),
                        pltpu.VMEM((bb * HA_HEADS, cb, HA_DK), F32),
                        pltpu.VMEM((SUBLANES, SUBLANES, LANES), F32)],
        compiler_params=_cparams(("parallel", "arbitrary")),
        name="hgrn",
    )(qs, kin, logf, v, vt, s0_all)


def _attn_prompt_kernel(qt_ref, kc_ref, kt_ref, o_ref, s0, s1, x0, x1, p0, p1, a0, a1, m_scr,
                        l_scr, acc_scr, *, tq, tk):
    qi = pl.program_id(1)
    jd = (qi * tq) // tk
    shift = CHUNK.bit_length() - 1
    head_cols = [slice(h * tq, (h + 1) * tq) for h in range(HB_HEADS)]

    def scores(j, s_dst, x_dst):
        keys = kc_ref[pl.ds(pl.multiple_of(j * tk, tk), tk), :]
        for cols in head_cols:
            s = _dot(keys, qt_ref[0, :, cols])
            s_dst[:, cols] = s
            x_dst[:, cols] = jnp.max(s, axis=0, keepdims=True)

    def softmax(j, s_src, x_src, p_dst, a_dst, masked):
        if masked:
            kpos = j * tk + lax.broadcasted_iota(jnp.int32, (tk, 1), 0)
            qpos = qi * tq + lax.broadcasted_iota(jnp.int32, (1, tq), 1)
            visible = (kpos >> shift) <= (qpos >> shift)
        for cols in head_cols:
            m_prev = m_scr[:, cols]
            if masked:
                s = jnp.where(visible, s_src[:, cols], NEG_INF)
                m_new = jnp.maximum(m_prev, jnp.max(s, axis=0, keepdims=True))
            else:
                s = s_src[:, cols]
                m_new = jnp.maximum(m_prev, x_src[:, cols])
            alpha = jnp.exp2(m_prev - m_new)
            p = jnp.exp2(s - m_new)
            l_scr[:, cols] = alpha * l_scr[:, cols] + jnp.sum(p, axis=0, keepdims=True)
            m_scr[:, cols] = m_new
            a_dst[:, cols] = alpha
            p_dst[:, cols] = p.astype(BF16)

    def values(j, p_src, a_src):
        for cols in head_cols:
            acc_scr[:, cols] = (a_src[:, cols] * acc_scr[:, cols]
                                + _dot(kt_ref[j], p_src[:, cols]))

    set0, set1 = (s0, x0, p0, a0), (s1, x1, p1, a1)

    def trip(j, cur, nxt):
        scores(j + 1, nxt[0], nxt[1])
        softmax(j, cur[0], cur[1], cur[2], cur[3], False)
        values(jnp.maximum(j - 1, 0), nxt[2], nxt[3])

    m_scr[...] = jnp.full(m_scr.shape, NEG_INF, F32)
    l_scr[...] = jnp.zeros(l_scr.shape, F32)
    acc_scr[...] = jnp.zeros(acc_scr.shape, F32)
    p1[...] = jnp.zeros(p1.shape, BF16)
    a1[...] = jnp.ones(a1.shape, F32)
    scores(0, s0, x0)

    def pair(jj, carry):
        trip(2 * jj, set0, set1)
        trip(2 * jj + 1, set1, set0)
        return carry

    lax.fori_loop(0, jd // 2, pair, 0)

    @pl.when(jd % 2 == 1)
    def _():
        trip(jd - 1, set0, set1)

    def drain(cur, prv):
        softmax(jd, cur[0], cur[1], cur[2], cur[3], True)
        values(jnp.maximum(jd - 1, 0), prv[2], prv[3])
        for h, cols in enumerate(head_cols):
            acc = (cur[3][:, cols] * acc_scr[:, cols] + _dot(kt_ref[jd], cur[2][:, cols]))
            o_ref[0, h] = (acc * (1.0 / l_scr[:, cols])).T.astype(BF16)

    pl.when(jd % 2 == 0)(functools.partial(drain, set0, set1))
    pl.when(jd % 2 == 1)(functools.partial(drain, set1, set0))


def _attn_prompt(qt, kcat, ckvt, *, bsz, t, tq, tk):
    nq = t // tq
    nk = t // tk
    cols = HB_HEADS * tq
    single = pl.Buffered(1)
    return pl.pallas_call(
        functools.partial(_attn_prompt_kernel, tq=tq, tk=tk),
        grid=(bsz, nq),
        in_specs=[
            pl.BlockSpec((1, QK_DIM, cols), lambda b, i: (b * nq + i, 0, 0)),
            pl.BlockSpec((t, QK_DIM), lambda b, i: (b, 0), pipeline_mode=single),
            pl.BlockSpec((nk, KV_LORA, tk), lambda b, i: (b, 0, 0), pipeline_mode=single),
        ],
        out_specs=pl.BlockSpec((1, HB_HEADS, tq, KV_LORA), lambda b, i: (b * nq + i, 0, 0, 0)),
        out_shape=jax.ShapeDtypeStruct((bsz * nq, HB_HEADS, tq, KV_LORA), BF16),
        scratch_shapes=[pltpu.VMEM((tk, cols), F32), pltpu.VMEM((tk, cols), F32),
                        pltpu.VMEM((1, cols), F32), pltpu.VMEM((1, cols), F32),
                        pltpu.VMEM((tk, cols), BF16), pltpu.VMEM((tk, cols), BF16),
                        pltpu.VMEM((1, cols), F32), pltpu.VMEM((1, cols), F32),
                        pltpu.VMEM((1, cols), F32), pltpu.VMEM((1, cols), F32),
                        pltpu.VMEM((KV_LORA, cols), F32)],
        compiler_params=_cparams(("parallel", "arbitrary")),
        name="attn_prompt",
    )(qt, kcat, ckvt)


def _attn_sample_kernel(qa_ref, qp_ref, pckv_ref, pkpe_ref, nckv_ref, nkpe_ref, o_ref):
    _, _, ts, _ = qa_ref.shape
    rows = HB_HEADS * ts
    qa = qa_ref[0].reshape(rows, KV_LORA)
    qp = qp_ref[0].reshape(rows, LANES)[:, :ROPE_DIM]
    pckv = pckv_ref[0].astype(BF16)
    nckv = nckv_ref[...].astype(BF16)
    s_past = _dot_nt(qa, pckv) + _dot(qp, pkpe_ref[0].astype(BF16))
    s_new = _dot_nt(qa, nckv) + _dot_nt(qp, nkpe_ref[...].astype(BF16))
    m = jnp.maximum(jnp.max(s_past, axis=-1, keepdims=True), jnp.max(s_new, axis=-1, keepdims=True))
    p_past = jnp.exp2(s_past - m)
    p_new = jnp.exp2(s_new - m)
    l = jnp.sum(p_past, axis=-1, keepdims=True) + jnp.sum(p_new, axis=-1, keepdims=True)
    o = (_dot(p_past.astype(BF16), pckv) + _dot(p_new.astype(BF16), nckv)) / l
    o_ref[0] = o.astype(BF16).reshape(HB_HEADS, ts, KV_LORA)


def _attn_sample(qabs, qpe, past_ckv, past_kpe, layer, ckv, kpe, *, ts):
    _, bsz, past, _ = past_ckv.shape
    return pl.pallas_call(
        _attn_sample_kernel,
        grid=(bsz,),
        in_specs=[
            pl.BlockSpec((1, HB_HEADS, ts, KV_LORA), lambda b: (b, 0, 0, 0)),
            pl.BlockSpec((1, HB_HEADS, ts, LANES), lambda b: (b, 0, 0, 0)),
            pl.BlockSpec((None, 1, past, KV_LORA), lambda b: (layer, b, 0, 0)),
            pl.BlockSpec((None, 1, ROPE_DIM, past), lambda b: (layer, b, 0, 0)),
            pl.BlockSpec((None, ts, KV_LORA), lambda b: (layer, b, 0)),
            pl.BlockSpec((ts, ROPE_DIM), lambda b: (b, 0)),
        ],
        out_specs=pl.BlockSpec((1, HB_HEADS, ts, KV_LORA), lambda b: (b, 0, 0, 0)),
        out_shape=jax.ShapeDtypeStruct(qabs.shape, BF16),
        compiler_params=_cparams(("parallel",)),
        name="attn_sample",
    )(qabs, qpe, past_ckv, past_kpe, ckv, kpe)


def _gates(logits, bias):
    tm = logits.shape[0]
    lane = lax.broadcasted_iota(jnp.int32, (tm, LANES), 1)
    pos = lane % EXPERTS_PER_GROUP
    valid = lane < N_EXPERTS
    scores = jax.nn.sigmoid(logits)
    sel = jnp.where(valid, scores + bias, -jnp.inf)

    others = []
    for r in range(1, EXPERTS_PER_GROUP):
        others.append(jnp.where(pos >= r, pltpu.roll(sel, r, 1),
                                pltpu.roll(sel, LANES - (EXPERTS_PER_GROUP - r), 1)))
    a, b, c, d = sel, others[0], others[1], others[2]
    hi1, lo1 = jnp.maximum(a, b), jnp.minimum(a, b)
    hi2, lo2 = jnp.maximum(c, d), jnp.minimum(c, d)
    gscore = jnp.maximum(hi1, hi2) + jnp.maximum(jnp.minimum(hi1, hi2), jnp.maximum(lo1, lo2))
    gmax = jnp.max(gscore, axis=-1, keepdims=True)
    group = (lane // EXPERTS_PER_GROUP).astype(F32)
    gidx = jnp.min(jnp.where(gscore == gmax, group, float(LANES)), axis=-1, keepdims=True)

    rank = jnp.zeros((tm, LANES), jnp.int32)
    for r, o in enumerate(others, start=1):
        ahead = (o > sel) | ((o == sel) & (pos >= r))
        rank = rank + ahead.astype(jnp.int32)
    chosen = (group == gidx) & (rank < 2) & valid
    w = jnp.where(chosen, scores, 0.0)
    w = w / jnp.sum(w, axis=-1, keepdims=True)
    return jnp.where(lane == GROUP_LANE, gidx, w)


def _merge_kernel(x_ref, o_ref, gs_ref, olat_ref, wuv_ref, wo_ref, hg_ref, mg_ref,
                  g1_ref, b1_ref, wr_ref, rb_ref, x1g_ref, *, alpha):
    _, _, tq, _ = olat_ref.shape
    tm = x_ref.shape[0]
    hg = hg_ref[...]
    d_a = HA_HEADS * HA_DV
    halves = [(i * tm // 2, (i + 1) * tm // 2) for i in range(2)]

    def lat(h, lo, hi):
        if tq >= hi - lo:
            return olat_ref[lo // tq, h, lo % tq:lo % tq + hi - lo]
        return jnp.concatenate([olat_ref[s, h] for s in range(lo // tq, hi // tq)], axis=0)

    ob = [[_dot(lat(h, lo, hi), wuv_ref[h]) for h in range(HB_HEADS)] for lo, hi in halves]
    oa = [jnp.concatenate([_rms(o_ref[h, lo:hi, :], hg) * gs_ref[lo:hi, h * HA_DV:(h + 1) * HA_DV]
                           for h in range(HA_HEADS)], axis=-1) for lo, hi in halves]
    ob = [_rms(jnp.concatenate(o, axis=-1), mg_ref[...]) for o in ob]
    mix = [_dot(a.astype(BF16), wo_ref[:d_a, :]) + _dot(o.astype(BF16), wo_ref[d_a:, :])
           for a, o in zip(oa, ob)]
    x1 = [_layer_norm(alpha * x_ref[lo:hi, :] + m, g1_ref[...], b1_ref[...])
          for (lo, hi), m in zip(halves, mix)]
    logits = [_dot(x.astype(BF16), wr_ref[...]) for x in x1]
    for (lo, hi), x, lg in zip(halves, x1, logits):
        for c in range(X_PIECES):
            x1g_ref[c, lo:hi, :] = x[:, c * SC_ROW:(c + 1) * SC_ROW]
        x1g_ref[X_PIECES, lo:hi, :LANES] = _gates(lg, rb_ref[...])
        x1g_ref[X_PIECES, lo:hi, LANES:] = jnp.zeros((hi - lo, SC_ROW - LANES), F32)


def _merge(x, o_raw, gs, olat, wuv, wo, hg, mg, g1, b1, wr, rb, *, tm, tq, alpha):
    n = x.shape[0]
    row = lambda w: pl.BlockSpec((tm, w), lambda i: (i, 0))
    return pl.pallas_call(
        functools.partial(_merge_kernel, alpha=alpha),
        grid=(n // tm,),
        in_specs=[row(D_MODEL), pl.BlockSpec((HA_HEADS, tm, HA_DV), lambda i: (0, i, 0)), row(HA_W),
                  pl.BlockSpec((tm // tq, HB_HEADS, tq, KV_LORA), lambda i: (i, 0, 0, 0)),
                  _full(wuv.shape), _full(wo.shape), _full(hg.shape), _full(mg.shape),
                  _full(g1.shape), _full(b1.shape), _full(wr.shape), _full(rb.shape)],
        out_specs=pl.BlockSpec((X_PIECES + 1, tm, SC_ROW), lambda i: (0, i, 0)),
        out_shape=jax.ShapeDtypeStruct((X_PIECES + 1, n, SC_ROW), F32),
        compiler_params=_cparams(("parallel",)),
        name="merge",
    )(x, o_raw, gs, olat, wuv, wo, hg, mg, g1, b1, wr, rb)


def _sc_scatter_rows(src, dst, m):
    parts, n, _ = src.shape
    windows = dst.shape[0] // SC_WINDOW
    src_windows = n // SC_WINDOW
    pieces = src.reshape(parts * n, SC_ROW)
    piece_dst = (jnp.arange(parts, dtype=jnp.int32)[:, None] * m + dst[None, :]).reshape(1, -1)
    mesh = plsc.VectorSubcoreMesh(core_axis_name="core", subcore_axis_name="subcore")

    @functools.partial(pl.kernel, out_type=jax.ShapeDtypeStruct((parts * m, SC_ROW), src.dtype),
                       mesh=mesh)
    def scatter(x_hbm, i_hbm, o_hbm):
        def body(x_vmem, i_vmem):
            pltpu.sync_copy(x_vmem, o_hbm.at[i_vmem.at[0]])

        pltpu.emit_pipeline(
            body,
            grid=(parts * windows,),
            in_specs=[pl.BlockSpec((SC_WINDOW, SC_ROW),
                                   lambda i: ((i // windows) * src_windows
                                              + (i % windows) % src_windows, 0)),
                      pl.BlockSpec((1, SC_WINDOW), lambda i: (0, i))],
            out_specs=[],
            core_axis_name=("core", "subcore"),
            dimension_semantics=(pltpu.PARALLEL,),
        )(x_hbm, i_hbm)

    return scatter(pieces, piece_dst).reshape(parts, m, SC_ROW)


def _sc_gather_rows(src, idx):
    parts, n, _ = src.shape
    m = idx.shape[0] * parts
    pieces = src.reshape(parts * n, SC_ROW)
    piece_idx = (jnp.arange(parts, dtype=jnp.int32)[:, None] * n + idx[None, :]).reshape(1, m)
    mesh = plsc.VectorSubcoreMesh(core_axis_name="core", subcore_axis_name="subcore")

    @functools.partial(pl.kernel, out_type=jax.ShapeDtypeStruct((m, SC_ROW), src.dtype), mesh=mesh)
    def gather(x_hbm, i_hbm, o_hbm):
        def body(i_vmem, o_vmem):
            pltpu.sync_copy(x_hbm.at[i_vmem.at[0]], o_vmem)

        pltpu.emit_pipeline(
            body,
            grid=(m // SC_WINDOW,),
            in_specs=[pl.BlockSpec((1, SC_WINDOW), lambda i: (0, i))],
            out_specs=[pl.BlockSpec((SC_WINDOW, SC_ROW), lambda i: (i, 0))],
            core_axis_name=("core", "subcore"),
            dimension_semantics=(pltpu.PARALLEL,),
        )(i_hbm, o_hbm)

    return gather(pieces, piece_idx).reshape(parts, idx.shape[0], SC_ROW)


def _moe_kernel(tg_ref, xs_ref, wg_ref, wu_ref, wd_ref, y_ref, acc_scr):
    tm = xs_ref.shape[1]
    per_step = wg_ref.shape[0]
    step = pl.program_id(1)
    first_expert = tg_ref[pl.program_id(0)] * EXPERTS_PER_GROUP + step * per_step
    xb = jnp.concatenate([xs_ref[c] for c in range(X_PIECES)], axis=1).astype(BF16)
    gates = xs_ref[X_PIECES, :, :LANES]
    lane = lax.broadcasted_iota(jnp.int32, (tm, LANES), 1)
    acc = jnp.zeros((tm, D_MODEL), F32)
    for k in range(per_step):
        hmid = (_silu(_dot(xb, wg_ref[k].astype(BF16))) * _dot(xb, wu_ref[k].astype(BF16)))
        gcol = jnp.sum(jnp.where(lane == first_expert + k, gates, 0.0), axis=-1, keepdims=True)
        acc = acc + gcol * _dot(hmid.astype(BF16), wd_ref[k].astype(BF16))

    @pl.when(step == 0)
    def _():
        acc_scr[...] = acc

    @pl.when(step > 0)
    def _():
        acc_scr[...] += acc

    @pl.when(step == pl.num_programs(1) - 1)
    def _():
        for c in range(D_MODEL // SC_ROW):
            y_ref[c] = acc_scr[:, c * SC_ROW:(c + 1) * SC_ROW]


def _moe_sorted(tile_group, xs, wg, wu, wd, layer, *, tm, per_step, weight_buffers):
    m = xs.shape[1]
    steps = EXPERTS_PER_GROUP // per_step
    mode = dict(pipeline_mode=pl.Buffered(weight_buffers))
    wspec = lambda shape: pl.BlockSpec((None, per_step) + shape,
                                       lambda i, k, tg: (layer, tg[i] * steps + k, 0, 0), **mode)
    return pl.pallas_call(
        _moe_kernel,
        grid_spec=pltpu.PrefetchScalarGridSpec(
            num_scalar_prefetch=1,
            grid=(m // tm, steps),
            in_specs=[pl.BlockSpec((X_PIECES + 1, tm, SC_ROW), lambda i, k, tg: (0, i, 0)),
                      wspec((D_MODEL, D_EXPERT)), wspec((D_MODEL, D_EXPERT)),
                      wspec((D_EXPERT, D_MODEL))],
            out_specs=pl.BlockSpec((D_MODEL // SC_ROW, tm, SC_ROW), lambda i, k, tg: (0, i, 0)),
            scratch_shapes=[pltpu.VMEM((tm, D_MODEL), F32)],
        ),
        out_shape=jax.ShapeDtypeStruct((D_MODEL // SC_ROW, m, SC_ROW), F32),
        compiler_params=_cparams(("arbitrary", "arbitrary")),
        name="moe",
    )(tile_group, xs, wg, wu, wd)


def _route(x1g, *, tm):
    n = x1g.shape[1]
    n_tiles = n // tm + N_GROUPS - 1
    m = n_tiles * tm
    gidx = x1g[X_PIECES, :, GROUP_LANE].astype(jnp.int32)
    onehot = (gidx[:, None] == jnp.arange(N_GROUPS, dtype=jnp.int32)[None, :]).astype(jnp.int32)
    counts = jnp.sum(onehot, axis=0)
    tiles = (counts + tm - 1) // tm
    tile_end = jnp.cumsum(tiles)
    tile_start = tile_end - tiles
    rank = jnp.sum((jnp.cumsum(onehot, axis=0) - onehot) * onehot, axis=1)
    pos = jnp.take(tile_start, gidx) * tm + rank
    pad_start = tile_start * tm + counts
    pad_end = (tile_end * tm).at[N_GROUPS - 1].set(m)
    pad_cum = jnp.cumsum(pad_end - pad_start)
    k = jnp.arange(m - n, dtype=jnp.int32)
    pg = jnp.sum(k[:, None] >= pad_cum[None, :], axis=1)
    pad_rows = jnp.take(pad_start, pg) + k - jnp.take(pad_cum - (pad_end - pad_start), pg)
    dst = jnp.concatenate([pos, pad_rows]).astype(jnp.int32)
    tile_group = jnp.sum(jnp.arange(n_tiles, dtype=jnp.int32)[:, None] >= tile_end[None, :], axis=1)
    return dst, jnp.minimum(tile_group, N_GROUPS - 1).astype(jnp.int32), m


def _post_kernel(x1_ref, f_ref, p_ref, g2_ref, b2_ref, wpg_ref, wp_ref, y_ref, *, alpha):
    tm = x1_ref.shape[1]
    halves = [(i * tm // 2, (i + 1) * tm // 2) for i in range(2)]
    emb = [_dot(p_ref[lo:hi, :].astype(BF16), wp_ref[...]) for lo, hi in halves]

    def rows(ref, lo, hi):
        return jnp.concatenate([ref[c, lo:hi, :] for c in range(X_PIECES)], axis=1)

    x2 = [_layer_norm(alpha * rows(x1_ref, lo, hi) + rows(f_ref, lo, hi), g2_ref[...], b2_ref[...])
          for lo, hi in halves]
    gate = [_dot(x.astype(BF16), wpg_ref[...]) for x in x2]
    for (lo, hi), x, g, e in zip(halves, x2, gate, emb):
        y_ref[lo:hi, :] = x + jax.nn.sigmoid(g) * e


def _post(x1g, ffn, p_all, layer, g2, b2, wpg, wp, *, tm, alpha):
    _, n, _ = ffn.shape
    row = lambda w: pl.BlockSpec((tm, w), lambda i: (i, 0))
    pieces = pl.BlockSpec((X_PIECES, tm, SC_ROW), lambda i: (0, i, 0))
    return pl.pallas_call(
        functools.partial(_post_kernel, alpha=alpha),
        grid=(n // tm,),
        in_specs=[pieces, pieces,
                  pl.BlockSpec((None, tm, PLE_DIM), lambda i: (layer, i, 0)),
                  _full(g2.shape), _full(b2.shape), _full(wpg.shape), _full(wp.shape)],
        out_specs=row(D_MODEL),
        out_shape=jax.ShapeDtypeStruct((n, D_MODEL), F32),
        compiler_params=_cparams(("parallel",)),
        name="post",
    )(x1g, ffn, p_all, g2, b2, wpg, wp)


def _rot_cols(w):
    half = ROPE_DIM // 2
    return jnp.concatenate([-w[..., half:], w[..., :half]], axis=-1)


def _prep_layer(l, w_in, w_qb, w_kvb, w_o, w_gate, w_up, w_down, w_ple, w_ple_gate):
    win = w_in[l]
    win_aug = jnp.concatenate([win, _rot_cols(win[:, COL_KPE:])], axis=-1).astype(BF16)
    wqb = w_qb[l].reshape(Q_LORA, HB_HEADS, NOPE_DIM + ROPE_DIM)
    wqb_aug = jnp.concatenate([wqb, _rot_cols(wqb[..., NOPE_DIM:])], axis=-1)
    wqb_aug = wqb_aug.reshape(Q_LORA, HB_HEADS * Q_HEAD_AUG).astype(BF16)
    wkvb = w_kvb[l].reshape(KV_LORA, HB_HEADS, NOPE_DIM + V_DIM)
    wuk_t = jnp.transpose(wkvb[..., :NOPE_DIM], (1, 2, 0)).astype(BF16)
    wuv = jnp.transpose(wkvb[..., NOPE_DIM:], (1, 0, 2)).astype(BF16)
    return dict(win=win_aug, wqb=wqb_aug, wuk=wuk_t, wuv=wuv, wo=w_o[l].astype(BF16),
                wg=w_gate, wu=w_up, wd=w_down,
                wp=w_ple[l].astype(BF16), wpg=w_ple_gate[l].astype(BF16))


def _rope_table(pos):
    inv = ROPE_THETA ** (-np.arange(0, ROPE_DIM, 2, dtype=np.float64) / ROPE_DIM)
    ang = pos.astype(np.float64)[:, None] * inv[None, :]
    cos, sin = np.cos(ang), np.sin(ang)
    return np.concatenate([cos, cos, sin, sin], axis=-1).astype(np.float32)


def _tiles(n, t, prompt):
    if prompt:
        tm = min(512, t)
        return dict(tm=tm, tq=tm, tk=tm, tb=tm, cb=min(128, t), bb=2, tmoe=min(512, n),
                    experts_per_step=EXPERTS_PER_GROUP, weight_buffers=1)
    return dict(tm=min(512, n), tq=t, tk=None, tb=t, cb=t, bb=2, tmoe=min(256, n),
                experts_per_step=EXPERTS_PER_GROUP, weight_buffers=2)


def _layer(x, p_all, layer, cs, s0_all, past, prev_ckv, lbp, prm, small, *, alpha, prompt):
    bsz, t, _ = x.shape
    n = bsz * t
    tl = _tiles(n, t, prompt)
    x2d = x.reshape(n, D_MODEL)
    tm, tb, cb = tl["tm"], tl["tb"], tl["cb"]
    outs = _inproj(x2d, cs, prm["win"], lbp, small["qg"], prm["wqb"], small["kvg"], prm["wuk"],
                   prev_ckv, tm=tm, tq=tl["tq"], cb=cb, prompt=prompt)
    qs, kin, logf, v, gs, ckv_all, kpe = outs[:7]
    r4 = lambda a: a.reshape(HA_HEADS, bsz, t, HA_DK)
    if prompt:
        qt, vt, kcat, ckvt = outs[7:]
        vt = vt.reshape(bsz, t // tb, HA_HEADS, tb // cb, HA_DV, cb)
    else:
        qabs, qpe = outs[7:]
        vt = jnp.transpose(r4(v), (1, 0, 3, 2)).astype(BF16)
        vt = vt.reshape(bsz, 1, HA_HEADS, 1, HA_DV, t)
    o_raw, s_new = _hgrn(r4(qs), r4(kin), r4(logf), r4(v), vt, s0_all, layer if not prompt else 0,
                         bb=tl["bb"], tb=tb, cb=cb)
    if prompt:
        olat = _attn_prompt(qt, kcat, ckvt, bsz=bsz, t=t, tq=tl["tq"], tk=tl["tk"])
    else:
        olat = _attn_sample(qabs, qpe, past[0], past[1], layer, ckv_all, kpe, ts=t)
    x1g = _merge(x2d, o_raw.reshape(HA_HEADS, n, HA_DV), gs, olat, prm["wuv"], prm["wo"],
                 small["hg"], small["mg"], small["g1"], small["b1"],
                 small["wr"], small["rb"], tm=min(2 * tm, n), tq=tl["tq"], alpha=alpha)
    tmoe = tl["tmoe"]
    dst, tile_group, m_sorted = _route(x1g, tm=tmoe)
    xs = _sc_scatter_rows(x1g, dst, m_sorted)
    ys = _moe_sorted(tile_group, xs, prm["wg"], prm["wu"], prm["wd"], layer, tm=tmoe,
                     per_step=tl["experts_per_step"], weight_buffers=tl["weight_buffers"])
    ffn = _sc_gather_rows(ys, dst[:n])
    y = _post(x1g, ffn, p_all.reshape(-1, n, PLE_DIM), layer, small["g2"], small["b2"],
              prm["wpg"], prm["wp"], tm=min(2 * tm, n), alpha=alpha)
    return y.reshape(bsz, t, D_MODEL), s_new, ckv_all, kpe.reshape(bsz, t, ROPE_DIM)


def kernel(x_prompt, x_sample, p_prompt, p_sample, state_hgrn, cache_ckv, cache_kpe, w_in,
           lb_logits, hgrn_norm_g, q_norm_g, w_qb, kv_norm_g, w_kvb, mla_norm_g, w_o,
           ln1_g, ln1_b, ln2_g, ln2_b, w_router, router_bias, w_gate, w_up, w_down,
           w_ple, w_ple_gate):
    depth = w_in.shape[0]
    alpha = (2 * depth) ** 0.25
    bp, tp, _ = x_prompt.shape
    bs, ts, _ = x_sample.shape
    past = cache_ckv.shape[2]

    sm = jax.nn.softmax(lb_logits.astype(F32), axis=0)
    lb_all = jnp.maximum(jnp.cumsum(sm, axis=0) - sm[0:1], 0.0)
    lbp_all = jnp.stack([jnp.log(lb_all), jnp.log1p(-lb_all), 1.0 - lb_all], axis=1)

    cs_p = jnp.asarray(_rope_table(np.arange(tp)))
    tm_s = _tiles(bs * ts, ts, False)["tm"]
    cs_s = jnp.asarray(np.tile(_rope_table(past + np.arange(ts)), (tm_s // ts, 1)))
    wr = jnp.pad(w_router, ((0, 0), (0, LANES - N_EXPERTS))).astype(BF16)
    rb = jnp.pad(router_bias.astype(F32), (0, LANES - N_EXPERTS)).reshape(1, LANES)
    s0_p = jnp.zeros((1, bp, HA_HEADS, HA_DK, HA_DV), F32)
    cache_kpe_t = jnp.swapaxes(cache_kpe, 2, 3)

    yp, ys = x_prompt, x_sample
    res = [[] for _ in range(4)]
    cp = cs_ = None
    for l in range(depth):
        prm = _prep_layer(l, w_in, w_qb, w_kvb, w_o, w_gate, w_up, w_down, w_ple, w_ple_gate)
        row = lambda a: a[l].reshape(1, -1).astype(F32)
        small = dict(qg=row(q_norm_g), kvg=row(kv_norm_g), hg=row(hgrn_norm_g), mg=row(mla_norm_g),
                     g1=row(ln1_g), b1=row(ln1_b), g2=row(ln2_g), b2=row(ln2_b), wr=wr, rb=rb)
        yp, sp, cp, kp = _layer(yp, p_prompt, l, cs_p, s0_p, None, cp, lbp_all[l], prm, small,
                                alpha=alpha, prompt=True)
        ys, ss, cs_, ks = _layer(ys, p_sample, l, cs_s, state_hgrn, (cache_ckv, cache_kpe_t), cs_,
                                 lbp_all[l], prm, small, alpha=alpha, prompt=False)
        for lst, a in zip(res, (sp, kp, ss, ks)):
            lst.append(a)
    sp, kp, ss, ks = (jnp.stack(a) for a in res)
    return (yp, ys, sp, cp.reshape(depth, bp, tp, KV_LORA), kp, ss,
            cs_.reshape(depth, bs, ts, KV_LORA), ks)
```

```python
import functools

import jax
import jax.numpy as jnp
import numpy as np
from jax import lax
from jax.experimental import pallas as pl
from jax.experimental.pallas import tpu as pltpu
from jax.experimental.pallas import tpu_sc as plsc

F32 = jnp.float32
BF16 = jnp.bfloat16

D_MODEL = 1024
HA_HEADS = 4
HA_DK = 128
HA_DV = 128
HB_HEADS = 4
Q_LORA = 384
KV_LORA = 256
NOPE_DIM = 128
ROPE_DIM = 64
V_DIM = 128
ROPE_THETA = 10000.0
MLA_SCALE = (NOPE_DIM + ROPE_DIM) ** -0.5
LOG2E = 1.4426950408889634
Q_SCALE = MLA_SCALE * LOG2E
CHUNK = 64
N_EXPERTS = 16
N_GROUPS = 4
EXPERTS_PER_GROUP = N_EXPERTS // N_GROUPS
D_EXPERT = 512
PLE_DIM = 256
NEG_INF = -1e30

HA_W = HA_HEADS * HA_DK
COL_CQ = 4 * HA_W
COL_CKV = COL_CQ + Q_LORA
COL_KPE = COL_CKV + KV_LORA
D_IN_AUG = COL_KPE + 2 * ROPE_DIM
Q_HEAD_AUG = NOPE_DIM + 2 * ROPE_DIM

LANES = 128
SUBLANES = 8
QK_DIM = KV_LORA + LANES
GROUP_LANE = N_EXPERTS
SC_WINDOW = 128
SC_ROW = 256
X_PIECES = D_MODEL // SC_ROW
SUB_BLOCK = 16
VMEM_LIMIT = 56 * 1024 * 1024


def _cparams(sem, vmem=VMEM_LIMIT):
    return pltpu.CompilerParams(dimension_semantics=sem, vmem_limit_bytes=vmem)


def _dot(a, b):
    return jnp.dot(a, b, preferred_element_type=F32)


def _dot_nt(a, b):
    return lax.dot_general(a, b, (((1,), (1,)), ((), ())), preferred_element_type=F32)


def _rms(x, g, eps=1e-6):
    return x * lax.rsqrt(jnp.mean(x * x, axis=-1, keepdims=True) + eps) * g


def _layer_norm(x, g, b, eps=1e-5):
    mu = jnp.mean(x, axis=-1, keepdims=True)
    xc = x - mu
    var = jnp.mean(xc * xc, axis=-1, keepdims=True)
    return xc * lax.rsqrt(var + eps) * g + b


def _silu(x):
    return x * jax.nn.sigmoid(x)


def _full(shape):
    nd = len(shape)
    return pl.BlockSpec(shape, lambda *_: (0,) * nd)


def _inproj_kernel(x_ref, cs_ref, win_ref, lb_ref, qg_ref, wqb_ref, kvg_ref, wuk_ref, prev_ref,
                   qs_ref, kin_ref, logf_ref, v_ref, gs_ref, ckv_ref, kpe_ref, *rest,
                   tq, cb, prompt, n_prev):
    tm = x_ref.shape[0]
    x = x_ref[...].astype(BF16)

    def sect(lo, hi):
        return _dot(x, win_ref[:, lo:hi])

    def put_heads(ref, val):
        for h in range(HA_HEADS):
            ref[h] = val[:, h * HA_DK:(h + 1) * HA_DK]

    cqn = _rms(sect(COL_CQ, COL_CKV), qg_ref[...]).astype(BF16)

    put_heads(qs_ref, _silu(sect(0, HA_W)))
    fa = sect(HA_W, 2 * HA_W)
    log_lb = lb_ref[0:1, :]
    log1m_lb = lb_ref[1:2, :]
    one_m_lb = lb_ref[2:3, :]
    e = jnp.exp(-jnp.abs(fa))
    r = 1.0 / (1.0 + e)
    c = log1m_lb + jnp.minimum(fa, 0.0) + jnp.log(r)
    put_heads(logf_ref, jnp.maximum(log_lb, c) + jnp.log(1.0 + jnp.exp(-jnp.abs(log_lb - c))))
    put_heads(kin_ref, one_m_lb * jnp.where(fa >= 0.0, e * r, r))
    v = sect(2 * HA_W, 3 * HA_W)
    put_heads(v_ref, v)
    gs_ref[...] = _silu(sect(3 * HA_W, 4 * HA_W))

    cs = cs_ref[...]

    def rope(t):
        prod = t * cs
        return prod + pltpu.roll(prod, ROPE_DIM, 1)

    ckv = _rms(sect(COL_CKV, COL_KPE), kvg_ref[...])
    for l in range(n_prev):
        ckv_ref[l] = prev_ref[l]
    ckv_ref[n_prev] = ckv
    kpe2 = rope(sect(COL_KPE, D_IN_AUG))
    kpe_ref[...] = kpe2[:, :ROPE_DIM]
    if prompt:
        qt_ref, vt_ref, kcat_ref, ckvt_ref = rest
        for h in range(HA_HEADS):
            for ci in range(tm // cb):
                vt_ref[0, h, ci] = v[ci * cb:(ci + 1) * cb, h * HA_DV:(h + 1) * HA_DV].T.astype(BF16)
        kcat_ref[:, :KV_LORA] = ckv.astype(BF16)
        kcat_ref[:, KV_LORA:] = kpe2.astype(BF16)
        ckvt_ref[0] = ckv.T.astype(BF16)
    else:
        qabs_ref, qpe_ref = rest

    lane = lax.broadcasted_iota(jnp.int32, (tm, LANES), 1)
    qhs = [_dot(cqn, wqb_ref[:, h * Q_HEAD_AUG:(h + 1) * Q_HEAD_AUG]) for h in range(HB_HEADS)]
    qabss = [_dot(qhs[h][:, :NOPE_DIM].astype(BF16), wuk_ref[h]) for h in range(HB_HEADS)]
    for h in range(HB_HEADS):
        qabs = qabss[h] * Q_SCALE
        qpe = jnp.where(lane < ROPE_DIM, rope(qhs[h][:, NOPE_DIM:]) * Q_SCALE, 0.0)
        if prompt:
            qabs_t = qabs.T.astype(BF16)
            qpe_t = qpe.T.astype(BF16)
            for s in range(tm // tq):
                qt_ref[s, :KV_LORA, h * tq:(h + 1) * tq] = qabs_t[:, s * tq:(s + 1) * tq]
                qt_ref[s, KV_LORA:, h * tq:(h + 1) * tq] = qpe_t[:, s * tq:(s + 1) * tq]
        else:
            for s in range(tm // tq):
                qabs_ref[s, h] = qabs[s * tq:(s + 1) * tq].astype(BF16)
                qpe_ref[s, h] = qpe[s * tq:(s + 1) * tq].astype(BF16)


def _inproj(x, cs, w_in, lbp, qg, wqb, kvg, wuk, prev_ckv, *, tm, tq, cb, prompt):
    n = x.shape[0]
    nt = n // tm
    n_prev = 0 if prev_ckv is None else prev_ckv.shape[0]
    if prev_ckv is None:
        prev_ckv = jnp.zeros((1, tm, KV_LORA), F32)
        prev_spec = _full(prev_ckv.shape)
    else:
        prev_spec = pl.BlockSpec((n_prev, tm, KV_LORA), lambda i: (0, i, 0))
    row = lambda w: pl.BlockSpec((tm, w), lambda i: (i, 0))
    heads = pl.BlockSpec((HA_HEADS, tm, HA_DK), lambda i: (0, i, 0))
    qblk = lambda w: pl.BlockSpec((tm // tq, HB_HEADS, tq, w), lambda i: (i, 0, 0, 0))
    head_major = jax.ShapeDtypeStruct((HA_HEADS, n, HA_DK), F32)
    out_shape = [
        head_major,
        head_major,
        head_major,
        head_major,
        jax.ShapeDtypeStruct((n, HA_W), F32),
        jax.ShapeDtypeStruct((n_prev + 1, n, KV_LORA), F32),
        jax.ShapeDtypeStruct((n, ROPE_DIM), F32),
    ]
    out_specs = [heads, heads, heads, heads, row(HA_W),
                 pl.BlockSpec((n_prev + 1, tm, KV_LORA), lambda i: (0, i, 0)), row(ROPE_DIM)]
    if prompt:
        out_shape += [jax.ShapeDtypeStruct((n // tq, QK_DIM, HB_HEADS * tq), BF16),
                      jax.ShapeDtypeStruct((nt, HA_HEADS, tm // cb, HA_DV, cb), BF16),
                      jax.ShapeDtypeStruct((n, QK_DIM), BF16),
                      jax.ShapeDtypeStruct((nt, KV_LORA, tm), BF16)]
        out_specs += [pl.BlockSpec((tm // tq, QK_DIM, HB_HEADS * tq), lambda i: (i, 0, 0)),
                      pl.BlockSpec((1, HA_HEADS, tm // cb, HA_DV, cb), lambda i: (i, 0, 0, 0, 0)),
                      row(QK_DIM),
                      pl.BlockSpec((1, KV_LORA, tm), lambda i: (i, 0, 0))]
    else:
        out_shape += [jax.ShapeDtypeStruct((n // tq, HB_HEADS, tq, KV_LORA), BF16),
                      jax.ShapeDtypeStruct((n // tq, HB_HEADS, tq, LANES), BF16)]
        out_specs += [qblk(KV_LORA), qblk(LANES)]
    return pl.pallas_call(
        functools.partial(_inproj_kernel, tq=tq, cb=cb, prompt=prompt, n_prev=n_prev),
        grid=(nt,),
        in_specs=[row(D_MODEL),
                  pl.BlockSpec((tm, LANES), lambda i: (i % (cs.shape[0] // tm), 0)),
                  _full(w_in.shape), _full(lbp.shape), _full(qg.shape),
                  _full(wqb.shape), _full(kvg.shape), _full(wuk.shape), prev_spec],
        out_specs=out_specs,
        out_shape=out_shape,
        compiler_params=_cparams(("parallel",)),
        name="inproj",
    )(x, cs, w_in, lbp, qg, wqb, kvg, wuk, prev_ckv)


def _hgrn_kernel(q_ref, k_ref, g_ref, v_ref, vt_ref, s0_ref, o_ref, sfin_ref, st_scr, b2_scr,
                 hide_ref, *, cb):
    _, bb, tb, _ = q_ref.shape
    nsub = cb // SUB_BLOCK
    ti = pl.program_id(1)

    @pl.when(ti == 0)
    def _():
        for b in range(bb):
            for h in range(HA_HEADS):
                st_scr[b, h] = s0_ref[b, h].T
        row8 = lax.broadcasted_iota(jnp.int32, (SUBLANES, LANES), 0)
        for s in range(SUBLANES):
            hide_ref[s] = jnp.where(row8 >= s, 0.0, NEG_INF)

    r_i = lax.broadcasted_iota(jnp.int32, (cb, cb), 0)
    c_i = lax.broadcasted_iota(jnp.int32, (cb, cb), 1)
    tril = (r_i >= c_i).astype(F32)

    chains = [(b, h) for b in range(bb) for h in range(HA_HEADS)]

    def matmul_part(ci):
        r0 = pl.multiple_of(ci * cb, cb)
        rows = pl.ds(r0, cb)

        def sub(ref, b, h, lo, size):
            return ref[h, b, pl.ds(pl.multiple_of(r0 + lo, SUB_BLOCK), size), :]

        for b, h in chains:
            bcum = jnp.dot(tril, g_ref[h, b, rows, :], precision=lax.Precision.HIGHEST,
                           preferred_element_type=F32)
            b2_scr[b * HA_HEADS + h] = bcum * LOG2E
        for b, h in chains:
            b2 = b2_scr[b * HA_HEADS + h]
            o_ref[h, b, rows, :] = _dot_nt((q_ref[h, b, rows, :] * jnp.exp2(b2)).astype(BF16),
                                           st_scr[b, h].astype(BF16))
        for j in range(nsub - 1):
            lo, hi = j * SUB_BLOCK, (j + 1) * SUB_BLOCK
            below = pl.ds(pl.multiple_of(r0 + hi, SUB_BLOCK), cb - hi)
            for b, h in chains:
                slot = b * HA_HEADS + h
                bnd = b2_scr[slot, hi - 1:hi, :]
                kd = (sub(k_ref, b, h, lo, SUB_BLOCK)
                      * jnp.exp2(bnd - b2_scr[slot, lo:hi, :])).astype(BF16)
                qe = (sub(q_ref, b, h, hi, cb - hi)
                      * jnp.exp2(b2_scr[slot, hi:, :] - bnd)).astype(BF16)
                a = _dot_nt(qe, kd).astype(BF16)
                o_ref[h, b, below, :] += _dot(a, sub(v_ref, b, h, lo, SUB_BLOCK).astype(BF16))
        for b, h in chains:
            slot = b * HA_HEADS + h
            b_last = b2_scr[slot, cb - 1:cb, :]
            kd_all = (k_ref[h, b, rows, :] * jnp.exp2(b_last - b2_scr[slot])).astype(BF16)
            st_scr[b, h] = st_scr[b, h] * jnp.exp2(b_last) + _dot(vt_ref[b, h, ci], kd_all)

    def pairwise_part(ci, b, h, j):
        slot = b * HA_HEADS + h
        r0 = pl.multiple_of(ci * cb, cb)
        lo = j * SUB_BLOCK
        mid = lo + SUBLANES
        top = pl.ds(pl.multiple_of(r0 + lo, SUBLANES), SUBLANES)
        bot = pl.ds(pl.multiple_of(r0 + mid, SUBLANES), SUBLANES)
        q_top, q_bot = q_ref[h, b, top, :], q_ref[h, b, bot, :]
        b_top, b_bot = b2_scr[slot, lo:mid, :], b2_scr[slot, mid:mid + SUBLANES, :]
        acc_top, acc_bot = o_ref[h, b, top, :], o_ref[h, b, bot, :]
        for s in range(SUB_BLOCK):
            row = pl.ds(r0 + lo + s, 1)
            bs = b2_scr[slot, lo + s:lo + s + 1, :]
            ks = k_ref[h, b, row, :]
            vs = v_ref[h, b, row, :]
            if s < SUBLANES:
                w = q_top * jnp.exp2(b_top - bs + hide_ref[s]) * ks
                acc_top = acc_top + jnp.sum(w, axis=-1, keepdims=True) * vs
                w = q_bot * jnp.exp2(b_bot - bs) * ks
            else:
                w = q_bot * jnp.exp2(b_bot - bs + hide_ref[s - SUBLANES]) * ks
            acc_bot = acc_bot + jnp.sum(w, axis=-1, keepdims=True) * vs
        o_ref[h, b, top, :] = acc_top
        o_ref[h, b, bot, :] = acc_bot

    def chunk(ci, carry):
        matmul_part(ci)
        for j in range(nsub):
            for b, h in chains:
                pairwise_part(ci, b, h, j)
        return carry

    lax.fori_loop(0, tb // cb, chunk, 0)

    @pl.when(ti == pl.num_programs(1) - 1)
    def _():
        for b in range(bb):
            for h in range(HA_HEADS):
                sfin_ref[b, h] = st_scr[b, h].T


def _hgrn(qs, kin, logf, v, vt, s0_all, layer, *, bb, tb, cb):
    _, bsz, t, _ = qs.shape
    blk = pl.BlockSpec((HA_HEADS, bb, tb, HA_DK), lambda i, j: (0, i, j, 0))
    st_blk = pl.BlockSpec((bb, HA_HEADS, HA_DK, HA_DV), lambda i, j: (i, 0, 0, 0))
    s0_blk = pl.BlockSpec((None, bb, HA_HEADS, HA_DK, HA_DV), lambda i, j: (layer, i, 0, 0, 0))
    vt_blk = pl.BlockSpec((bb, None, HA_HEADS, tb // cb, HA_DV, cb),
                          lambda i, j: (i, j, 0, 0, 0, 0))
    return pl.pallas_call(
        functools.partial(_hgrn_kernel, cb=cb),
        grid=(bsz // bb, t // tb),
        in_specs=[blk, blk, blk, blk, vt_blk, s0_blk],
        out_specs=[blk, st_blk],
        out_shape=[jax.ShapeDtypeStruct(qs.shape, F32),
                   jax.ShapeDtypeStruct(s0_all.shape[1:], F32)],
        scratch_shapes=[pltpu.VMEM((bb, HA_HEADS, HA_DV, HA_DK), F32),
                        pltpu.VMEM((bb * HA_HEADS, cb, HA_DK), F32),
                        pltpu.VMEM((SUBLANES, SUBLANES, LANES), F32)],
        compiler_params=_cparams(("parallel", "arbitrary")),
        name="hgrn",
    )(qs, kin, logf, v, vt, s0_all)


def _attn_prompt_kernel(qt_ref, kc_ref, kt_ref, o_ref, s0, s1, x0, x1, p0, p1, a0, a1, m_scr,
                        l_scr, acc_scr, *, tq, tk):
    qi = pl.program_id(1)
    jd = (qi * tq) // tk
    shift = CHUNK.bit_length() - 1
    head_cols = [slice(h * tq, (h + 1) * tq) for h in range(HB_HEADS)]

    def scores(j, s_dst, x_dst):
        keys = kc_ref[pl.ds(pl.multiple_of(j * tk, tk), tk), :]
        for cols in head_cols:
            s = _dot(keys, qt_ref[0, :, cols])
            s_dst[:, cols] = s
            x_dst[:, cols] = jnp.max(s, axis=0, keepdims=True)

    def softmax(j, s_src, x_src, p_dst, a_dst, masked):
        if masked:
            kpos = j * tk + lax.broadcasted_iota(jnp.int32, (tk, 1), 0)
            qpos = qi * tq + lax.broadcasted_iota(jnp.int32, (1, tq), 1)
            visible = (kpos >> shift) <= (qpos >> shift)
        for cols in head_cols:
            m_prev = m_scr[:, cols]
            if masked:
                s = jnp.where(visible, s_src[:, cols], NEG_INF)
                m_new = jnp.maximum(m_prev, jnp.max(s, axis=0, keepdims=True))
            else:
                s = s_src[:, cols]
                m_new = jnp.maximum(m_prev, x_src[:, cols])
            alpha = jnp.exp2(m_prev - m_new)
            p = jnp.exp2(s - m_new)
            l_scr[:, cols] = alpha * l_scr[:, cols] + jnp.sum(p, axis=0, keepdims=True)
            m_scr[:, cols] = m_new
            a_dst[:, cols] = alpha
            p_dst[:, cols] = p.astype(BF16)

    def values(j, p_src, a_src):
        for cols in head_cols:
            acc_scr[:, cols] = (a_src[:, cols] * acc_scr[:, cols]
                                + _dot(kt_ref[j], p_src[:, cols]))

    set0, set1 = (s0, x0, p0, a0), (s1, x1, p1, a1)

    def trip(j, cur, nxt):
        scores(j + 1, nxt[0], nxt[1])
        softmax(j, cur[0], cur[1], cur[2], cur[3], False)
        values(jnp.maximum(j - 1, 0), nxt[2], nxt[3])

    m_scr[...] = jnp.full(m_scr.shape, NEG_INF, F32)
    l_scr[...] = jnp.zeros(l_scr.shape, F32)
    acc_scr[...] = jnp.zeros(acc_scr.shape, F32)
    p1[...] = jnp.zeros(p1.shape, BF16)
    a1[...] = jnp.ones(a1.shape, F32)
    scores(0, s0, x0)

    def pair(jj, carry):
        trip(2 * jj, set0, set1)
        trip(2 * jj + 1, set1, set0)
        return carry

    lax.fori_loop(0, jd // 2, pair, 0)

    @pl.when(jd % 2 == 1)
    def _():
        trip(jd - 1, set0, set1)

    def drain(cur, prv):
        softmax(jd, cur[0], cur[1], cur[2], cur[3], True)
        values(jnp.maximum(jd - 1, 0), prv[2], prv[3])
        for h, cols in enumerate(head_cols):
            acc = (cur[3][:, cols] * acc_scr[:, cols] + _dot(kt_ref[jd], cur[2][:, cols]))
            o_ref[0, h] = (acc * (1.0 / l_scr[:, cols])).T.astype(BF16)

    pl.when(jd % 2 == 0)(functools.partial(drain, set0, set1))
    pl.when(jd % 2 == 1)(functools.partial(drain, set1, set0))


def _attn_prompt(qt, kcat, ckvt, *, bsz, t, tq, tk):
    nq = t // tq
    nk = t // tk
    cols = HB_HEADS * tq
    single = pl.Buffered(1)
    return pl.pallas_call(
        functools.partial(_attn_prompt_kernel, tq=tq, tk=tk),
        grid=(bsz, nq),
        in_specs=[
            pl.BlockSpec((1, QK_DIM, cols), lambda b, i: (b * nq + i, 0, 0)),
            pl.BlockSpec((t, QK_DIM), lambda b, i: (b, 0), pipeline_mode=single),
            pl.BlockSpec((nk, KV_LORA, tk), lambda b, i: (b, 0, 0), pipeline_mode=single),
        ],
        out_specs=pl.BlockSpec((1, HB_HEADS, tq, KV_LORA), lambda b, i: (b * nq + i, 0, 0, 0)),
        out_shape=jax.ShapeDtypeStruct((bsz * nq, HB_HEADS, tq, KV_LORA), BF16),
        scratch_shapes=[pltpu.VMEM((tk, cols), F32), pltpu.VMEM((tk, cols), F32),
                        pltpu.VMEM((1, cols), F32), pltpu.VMEM((1, cols), F32),
                        pltpu.VMEM((tk, cols), BF16), pltpu.VMEM((tk, cols), BF16),
                        pltpu.VMEM((1, cols), F32), pltpu.VMEM((1, cols), F32),
                        pltpu.VMEM((1, cols), F32), pltpu.VMEM((1, cols), F32),
                        pltpu.VMEM((KV_LORA, cols), F32)],
        compiler_params=_cparams(("parallel", "arbitrary")),
        name="attn_prompt",
    )(qt, kcat, ckvt)


def _attn_sample_kernel(qa_ref, qp_ref, pckv_ref, pkpe_ref, nckv_ref, nkpe_ref, o_ref):
    _, _, ts, _ = qa_ref.shape
    rows = HB_HEADS * ts
    qa = qa_ref[0].reshape(rows, KV_LORA)
    qp = qp_ref[0].reshape(rows, LANES)[:, :ROPE_DIM]
    pckv = pckv_ref[0].astype(BF16)
    nckv = nckv_ref[...].astype(BF16)
    s_past = _dot_nt(qa, pckv) + _dot(qp, pkpe_ref[0].astype(BF16))
    s_new = _dot_nt(qa, nckv) + _dot_nt(qp, nkpe_ref[...].astype(BF16))
    m = jnp.maximum(jnp.max(s_past, axis=-1, keepdims=True), jnp.max(s_new, axis=-1, keepdims=True))
    p_past = jnp.exp2(s_past - m)
    p_new = jnp.exp2(s_new - m)
    l = jnp.sum(p_past, axis=-1, keepdims=True) + jnp.sum(p_new, axis=-1, keepdims=True)
    o = (_dot(p_past.astype(BF16), pckv) + _dot(p_new.astype(BF16), nckv)) / l
    o_ref[0] = o.astype(BF16).reshape(HB_HEADS, ts, KV_LORA)


def _attn_sample(qabs, qpe, past_ckv, past_kpe, layer, ckv, kpe, *, ts):
    _, bsz, past, _ = past_ckv.shape
    return pl.pallas_call(
        _attn_sample_kernel,
        grid=(bsz,),
        in_specs=[
            pl.BlockSpec((1, HB_HEADS, ts, KV_LORA), lambda b: (b, 0, 0, 0)),
            pl.BlockSpec((1, HB_HEADS, ts, LANES), lambda b: (b, 0, 0, 0)),
            pl.BlockSpec((None, 1, past, KV_LORA), lambda b: (layer, b, 0, 0)),
            pl.BlockSpec((None, 1, ROPE_DIM, past), lambda b: (layer, b, 0, 0)),
            pl.BlockSpec((None, ts, KV_LORA), lambda b: (layer, b, 0)),
            pl.BlockSpec((ts, ROPE_DIM), lambda b: (b, 0)),
        ],
        out_specs=pl.BlockSpec((1, HB_HEADS, ts, KV_LORA), lambda b: (b, 0, 0, 0)),
        out_shape=jax.ShapeDtypeStruct(qabs.shape, BF16),
        compiler_params=_cparams(("parallel",)),
        name="attn_sample",
    )(qabs, qpe, past_ckv, past_kpe, ckv, kpe)


def _gates(logits, bias):
    tm = logits.shape[0]
    lane = lax.broadcasted_iota(jnp.int32, (tm, LANES), 1)
    pos = lane % EXPERTS_PER_GROUP
    valid = lane < N_EXPERTS
    scores = jax.nn.sigmoid(logits)
    sel = jnp.where(valid, scores + bias, -jnp.inf)

    others = []
    for r in range(1, EXPERTS_PER_GROUP):
        others.append(jnp.where(pos >= r, pltpu.roll(sel, r, 1),
                                pltpu.roll(sel, LANES - (EXPERTS_PER_GROUP - r), 1)))
    a, b, c, d = sel, others[0], others[1], others[2]
    hi1, lo1 = jnp.maximum(a, b), jnp.minimum(a, b)
    hi2, lo2 = jnp.maximum(c, d), jnp.minimum(c, d)
    gscore = jnp.maximum(hi1, hi2) + jnp.maximum(jnp.minimum(hi1, hi2), jnp.maximum(lo1, lo2))
    gmax = jnp.max(gscore, axis=-1, keepdims=True)
    group = (lane // EXPERTS_PER_GROUP).astype(F32)
    gidx = jnp.min(jnp.where(gscore == gmax, group, float(LANES)), axis=-1, keepdims=True)

    rank = jnp.zeros((tm, LANES), jnp.int32)
    for r, o in enumerate(others, start=1):
        ahead = (o > sel) | ((o == sel) & (pos >= r))
        rank = rank + ahead.astype(jnp.int32)
    chosen = (group == gidx) & (rank < 2) & valid
    w = jnp.where(chosen, scores, 0.0)
    w = w / jnp.sum(w, axis=-1, keepdims=True)
    return jnp.where(lane == GROUP_LANE, gidx, w)


def _merge_kernel(x_ref, o_ref, gs_ref, olat_ref, wuv_ref, wo_ref, hg_ref, mg_ref,
                  g1_ref, b1_ref, wr_ref, rb_ref, x1g_ref, *, alpha):
    _, _, tq, _ = olat_ref.shape
    tm = x_ref.shape[0]
    hg = hg_ref[...]
    d_a = HA_HEADS * HA_DV
    halves = [(i * tm // 2, (i + 1) * tm // 2) for i in range(2)]

    def lat(h, lo, hi):
        if tq >= hi - lo:
            return olat_ref[lo // tq, h, lo % tq:lo % tq + hi - lo]
        return jnp.concatenate([olat_ref[s, h] for s in range(lo // tq, hi // tq)], axis=0)

    ob = [[_dot(lat(h, lo, hi), wuv_ref[h]) for h in range(HB_HEADS)] for lo, hi in halves]
    oa = [jnp.concatenate([_rms(o_ref[h, lo:hi, :], hg) * gs_ref[lo:hi, h * HA_DV:(h + 1) * HA_DV]
                           for h in range(HA_HEADS)], axis=-1) for lo, hi in halves]
    ob = [_rms(jnp.concatenate(o, axis=-1), mg_ref[...]) for o in ob]
    mix = [_dot(a.astype(BF16), wo_ref[:d_a, :]) + _dot(o.astype(BF16), wo_ref[d_a:, :])
           for a, o in zip(oa, ob)]
    x1 = [_layer_norm(alpha * x_ref[lo:hi, :] + m, g1_ref[...], b1_ref[...])
          for (lo, hi), m in zip(halves, mix)]
    logits = [_dot(x.astype(BF16), wr_ref[...]) for x in x1]
    for (lo, hi), x, lg in zip(halves, x1, logits):
        for c in range(X_PIECES):
            x1g_ref[c, lo:hi, :] = x[:, c * SC_ROW:(c + 1) * SC_ROW]
        x1g_ref[X_PIECES, lo:hi, :LANES] = _gates(lg, rb_ref[...])
        x1g_ref[X_PIECES, lo:hi, LANES:] = jnp.zeros((hi - lo, SC_ROW - LANES), F32)


def _merge(x, o_raw, gs, olat, wuv, wo, hg, mg, g1, b1, wr, rb, *, tm, tq, alpha):
    n = x.shape[0]
    row = lambda w: pl.BlockSpec((tm, w), lambda i: (i, 0))
    return pl.pallas_call(
        functools.partial(_merge_kernel, alpha=alpha),
        grid=(n // tm,),
        in_specs=[row(D_MODEL), pl.BlockSpec((HA_HEADS, tm, HA_DV), lambda i: (0, i, 0)), row(HA_W),
                  pl.BlockSpec((tm // tq, HB_HEADS, tq, KV_LORA), lambda i: (i, 0, 0, 0)),
                  _full(wuv.shape), _full(wo.shape), _full(hg.shape), _full(mg.shape),
                  _full(g1.shape), _full(b1.shape), _full(wr.shape), _full(rb.shape)],
        out_specs=pl.BlockSpec((X_PIECES + 1, tm, SC_ROW), lambda i: (0, i, 0)),
        out_shape=jax.ShapeDtypeStruct((X_PIECES + 1, n, SC_ROW), F32),
        compiler_params=_cparams(("parallel",)),
        name="merge",
    )(x, o_raw, gs, olat, wuv, wo, hg, mg, g1, b1, wr, rb)


def _sc_scatter_rows(src, dst, m):
    parts, n, _ = src.shape
    windows = dst.shape[0] // SC_WINDOW
    src_windows = n // SC_WINDOW
    pieces = src.reshape(parts * n, SC_ROW)
    piece_dst = (jnp.arange(parts, dtype=jnp.int32)[:, None] * m + dst[None, :]).reshape(1, -1)
    mesh = plsc.VectorSubcoreMesh(core_axis_name="core", subcore_axis_name="subcore")

    @functools.partial(pl.kernel, out_type=jax.ShapeDtypeStruct((parts * m, SC_ROW), src.dtype),
                       mesh=mesh)
    def scatter(x_hbm, i_hbm, o_hbm):
        def body(x_vmem, i_vmem):
            pltpu.sync_copy(x_vmem, o_hbm.at[i_vmem.at[0]])

        pltpu.emit_pipeline(
            body,
            grid=(parts * windows,),
            in_specs=[pl.BlockSpec((SC_WINDOW, SC_ROW),
                                   lambda i: ((i // windows) * src_windows
                                              + (i % windows) % src_windows, 0)),
                      pl.BlockSpec((1, SC_WINDOW), lambda i: (0, i))],
            out_specs=[],
            core_axis_name=("core", "subcore"),
            dimension_semantics=(pltpu.PARALLEL,),
        )(x_hbm, i_hbm)

    return scatter(pieces, piece_dst).reshape(parts, m, SC_ROW)


def _sc_gather_rows(src, idx):
    parts, n, _ = src.shape
    m = idx.shape[0] * parts
    pieces = src.reshape(parts * n, SC_ROW)
    piece_idx = (jnp.arange(parts, dtype=jnp.int32)[:, None] * n + idx[None, :]).reshape(1, m)
    mesh = plsc.VectorSubcoreMesh(core_axis_name="core", subcore_axis_name="subcore")

    @functools.partial(pl.kernel, out_type=jax.ShapeDtypeStruct((m, SC_ROW), src.dtype), mesh=mesh)
    def gather(x_hbm, i_hbm, o_hbm):
        def body(i_vmem, o_vmem):
            pltpu.sync_copy(x_hbm.at[i_vmem.at[0]], o_vmem)

        pltpu.emit_pipeline(
            body,
            grid=(m // SC_WINDOW,),
            in_specs=[pl.BlockSpec((1, SC_WINDOW), lambda i: (0, i))],
            out_specs=[pl.BlockSpec((SC_WINDOW, SC_ROW), lambda i: (i, 0))],
            core_axis_name=("core", "subcore"),
            dimension_semantics=(pltpu.PARALLEL,),
        )(i_hbm, o_hbm)

    return gather(pieces, piece_idx).reshape(parts, idx.shape[0], SC_ROW)


def _moe_kernel(tg_ref, xs_ref, wg_ref, wu_ref, wd_ref, y_ref, acc_scr):
    tm = xs_ref.shape[1]
    per_step = wg_ref.shape[0]
    step = pl.program_id(1)
    first_expert = tg_ref[pl.program_id(0)] * EXPERTS_PER_GROUP + step * per_step
    xb = jnp.concatenate([xs_ref[c] for c in range(X_PIECES)], axis=1).astype(BF16)
    gates = xs_ref[X_PIECES, :, :LANES]
    lane = lax.broadcasted_iota(jnp.int32, (tm, LANES), 1)
    hidden = []
    for k in range(per_step):
        hmid = (_silu(_dot(xb, wg_ref[k].astype(BF16))) * _dot(xb, wu_ref[k].astype(BF16)))
        gcol = jnp.sum(jnp.where(lane == first_expert + k, gates, 0.0), axis=-1, keepdims=True)
        hidden.append((gcol * hmid).astype(BF16))
    w_down = wd_ref[...].reshape(per_step * D_EXPERT, D_MODEL).astype(BF16)
    acc = _dot(jnp.concatenate(hidden, axis=1), w_down)

    @pl.when(step == 0)
    def _():
        acc_scr[...] = acc

    @pl.when(step > 0)
    def _():
        acc_scr[...] += acc

    @pl.when(step == pl.num_programs(1) - 1)
    def _():
        for c in range(D_MODEL // SC_ROW):
            y_ref[c] = acc_scr[:, c * SC_ROW:(c + 1) * SC_ROW]


def _moe_sorted(tile_group, xs, wg, wu, wd, layer, *, tm, per_step, weight_buffers):
    m = xs.shape[1]
    steps = EXPERTS_PER_GROUP // per_step
    mode = dict(pipeline_mode=pl.Buffered(weight_buffers))
    wspec = lambda shape: pl.BlockSpec((None, per_step) + shape,
                                       lambda i, k, tg: (layer, tg[i] * steps + k, 0, 0), **mode)
    return pl.pallas_call(
        _moe_kernel,
        grid_spec=pltpu.PrefetchScalarGridSpec(
            num_scalar_prefetch=1,
            grid=(m // tm, steps),
            in_specs=[pl.BlockSpec((X_PIECES + 1, tm, SC_ROW), lambda i, k, tg: (0, i, 0)),
                      wspec((D_MODEL, D_EXPERT)), wspec((D_MODEL, D_EXPERT)),
                      wspec((D_EXPERT, D_MODEL))],
            out_specs=pl.BlockSpec((D_MODEL // SC_ROW, tm, SC_ROW), lambda i, k, tg: (0, i, 0)),
            scratch_shapes=[pltpu.VMEM((tm, D_MODEL), F32)],
        ),
        out_shape=jax.ShapeDtypeStruct((D_MODEL // SC_ROW, m, SC_ROW), F32),
        compiler_params=_cparams(("arbitrary", "arbitrary")),
        name="moe",
    )(tile_group, xs, wg, wu, wd)


def _route(x1g, *, tm):
    n = x1g.shape[1]
    n_tiles = n // tm + N_GROUPS - 1
    m = n_tiles * tm
    gidx = x1g[X_PIECES, :, GROUP_LANE].astype(jnp.int32)
    onehot = (gidx[:, None] == jnp.arange(N_GROUPS, dtype=jnp.int32)[None, :]).astype(jnp.int32)
    counts = jnp.sum(onehot, axis=0)
    tiles = (counts + tm - 1) // tm
    tile_end = jnp.cumsum(tiles)
    tile_start = tile_end - tiles
    rank = jnp.sum((jnp.cumsum(onehot, axis=0) - onehot) * onehot, axis=1)
    pos = jnp.take(tile_start, gidx) * tm + rank
    pad_start = tile_start * tm + counts
    pad_end = (tile_end * tm).at[N_GROUPS - 1].set(m)
    pad_cum = jnp.cumsum(pad_end - pad_start)
    k = jnp.arange(m - n, dtype=jnp.int32)
    pg = jnp.sum(k[:, None] >= pad_cum[None, :], axis=1)
    pad_rows = jnp.take(pad_start, pg) + k - jnp.take(pad_cum - (pad_end - pad_start), pg)
    dst = jnp.concatenate([pos, pad_rows]).astype(jnp.int32)
    tile_group = jnp.sum(jnp.arange(n_tiles, dtype=jnp.int32)[:, None] >= tile_end[None, :], axis=1)
    return dst, jnp.minimum(tile_group, N_GROUPS - 1).astype(jnp.int32), m


def _post_kernel(x1_ref, f_ref, p_ref, g2_ref, b2_ref, wpg_ref, wp_ref, y_ref, *, alpha):
    tm = x1_ref.shape[1]
    halves = [(i * tm // 2, (i + 1) * tm // 2) for i in range(2)]
    emb = [_dot(p_ref[lo:hi, :].astype(BF16), wp_ref[...]) for lo, hi in halves]

    def rows(ref, lo, hi):
        return jnp.concatenate([ref[c, lo:hi, :] for c in range(X_PIECES)], axis=1)

    x2 = [_layer_norm(alpha * rows(x1_ref, lo, hi) + rows(f_ref, lo, hi), g2_ref[...], b2_ref[...])
          for lo, hi in halves]
    gate = [_dot(x.astype(BF16), wpg_ref[...]) for x in x2]
    for (lo, hi), x, g, e in zip(halves, x2, gate, emb):
        y_ref[lo:hi, :] = x + jax.nn.sigmoid(g) * e


def _post(x1g, ffn, p_all, layer, g2, b2, wpg, wp, *, tm, alpha):
    _, n, _ = ffn.shape
    row = lambda w: pl.BlockSpec((tm, w), lambda i: (i, 0))
    pieces = pl.BlockSpec((X_PIECES, tm, SC_ROW), lambda i: (0, i, 0))
    return pl.pallas_call(
        functools.partial(_post_kernel, alpha=alpha),
        grid=(n // tm,),
        in_specs=[pieces, pieces,
                  pl.BlockSpec((None, tm, PLE_DIM), lambda i: (layer, i, 0)),
                  _full(g2.shape), _full(b2.shape), _full(wpg.shape), _full(wp.shape)],
        out_specs=row(D_MODEL),
        out_shape=jax.ShapeDtypeStruct((n, D_MODEL), F32),
        compiler_params=_cparams(("parallel",)),
        name="post",
    )(x1g, ffn, p_all, g2, b2, wpg, wp)


def _rot_cols(w):
    half = ROPE_DIM // 2
    return jnp.concatenate([-w[..., half:], w[..., :half]], axis=-1)


def _prep_layer(l, w_in, w_qb, w_kvb, w_o, w_gate, w_up, w_down, w_ple, w_ple_gate):
    win = w_in[l]
    win_aug = jnp.concatenate([win, _rot_cols(win[:, COL_KPE:])], axis=-1).astype(BF16)
    wqb = w_qb[l].reshape(Q_LORA, HB_HEADS, NOPE_DIM + ROPE_DIM)
    wqb_aug = jnp.concatenate([wqb, _rot_cols(wqb[..., NOPE_DIM:])], axis=-1)
    wqb_aug = wqb_aug.reshape(Q_LORA, HB_HEADS * Q_HEAD_AUG).astype(BF16)
    wkvb = w_kvb[l].reshape(KV_LORA, HB_HEADS, NOPE_DIM + V_DIM)
    wuk_t = jnp.transpose(wkvb[..., :NOPE_DIM], (1, 2, 0)).astype(BF16)
    wuv = jnp.transpose(wkvb[..., NOPE_DIM:], (1, 0, 2)).astype(BF16)
    return dict(win=win_aug, wqb=wqb_aug, wuk=wuk_t, wuv=wuv, wo=w_o[l].astype(BF16),
                wg=w_gate, wu=w_up, wd=w_down,
                wp=w_ple[l].astype(BF16), wpg=w_ple_gate[l].astype(BF16))


def _rope_table(pos):
    inv = ROPE_THETA ** (-np.arange(0, ROPE_DIM, 2, dtype=np.float64) / ROPE_DIM)
    ang = pos.astype(np.float64)[:, None] * inv[None, :]
    cos, sin = np.cos(ang), np.sin(ang)
    return np.concatenate([cos, cos, sin, sin], axis=-1).astype(np.float32)


def _tiles(n, t, prompt):
    if prompt:
        tm = min(512, t)
        return dict(tm=tm, tq=tm, tk=tm, tb=tm, cb=min(128, t), bb=2, tmoe=min(512, n),
                    experts_per_step=EXPERTS_PER_GROUP, weight_buffers=1)
    return dict(tm=min(512, n), tq=t, tk=None, tb=t, cb=t, bb=2, tmoe=min(256, n),
                experts_per_step=EXPERTS_PER_GROUP, weight_buffers=2)


def _layer(x, p_all, layer, cs, s0_all, past, prev_ckv, lbp, prm, small, *, alpha, prompt):
    bsz, t, _ = x.shape
    n = bsz * t
    tl = _tiles(n, t, prompt)
    x2d = x.reshape(n, D_MODEL)
    tm, tb, cb = tl["tm"], tl["tb"], tl["cb"]
    outs = _inproj(x2d, cs, prm["win"], lbp, small["qg"], prm["wqb"], small["kvg"], prm["wuk"],
                   prev_ckv, tm=tm, tq=tl["tq"], cb=cb, prompt=prompt)
    qs, kin, logf, v, gs, ckv_all, kpe = outs[:7]
    r4 = lambda a: a.reshape(HA_HEADS, bsz, t, HA_DK)
    if prompt:
        qt, vt, kcat, ckvt = outs[7:]
        vt = vt.reshape(bsz, t // tb, HA_HEADS, tb // cb, HA_DV, cb)
    else:
        qabs, qpe = outs[7:]
        vt = jnp.transpose(r4(v), (1, 0, 3, 2)).astype(BF16)
        vt = vt.reshape(bsz, 1, HA_HEADS, 1, HA_DV, t)
    o_raw, s_new = _hgrn(r4(qs), r4(kin), r4(logf), r4(v), vt, s0_all, layer if not prompt else 0,
                         bb=tl["bb"], tb=tb, cb=cb)
    if prompt:
        olat = _attn_prompt(qt, kcat, ckvt, bsz=bsz, t=t, tq=tl["tq"], tk=tl["tk"])
    else:
        olat = _attn_sample(qabs, qpe, past[0], past[1], layer, ckv_all, kpe, ts=t)
    x1g = _merge(x2d, o_raw.reshape(HA_HEADS, n, HA_DV), gs, olat, prm["wuv"], prm["wo"],
                 small["hg"], small["mg"], small["g1"], small["b1"],
                 small["wr"], small["rb"], tm=min(2 * tm, n), tq=tl["tq"], alpha=alpha)
    tmoe = tl["tmoe"]
    dst, tile_group, m_sorted = _route(x1g, tm=tmoe)
    xs = _sc_scatter_rows(x1g, dst, m_sorted)
    ys = _moe_sorted(tile_group, xs, prm["wg"], prm["wu"], prm["wd"], layer, tm=tmoe,
                     per_step=tl["experts_per_step"], weight_buffers=tl["weight_buffers"])
    ffn = _sc_gather_rows(ys, dst[:n])
    y = _post(x1g, ffn, p_all.reshape(-1, n, PLE_DIM), layer, small["g2"], small["b2"],
              prm["wpg"], prm["wp"], tm=min(2 * tm, n), alpha=alpha)
    return y.reshape(bsz, t, D_MODEL), s_new, ckv_all, kpe.reshape(bsz, t, ROPE_DIM)


def kernel(x_prompt, x_sample, p_prompt, p_sample, state_hgrn, cache_ckv, cache_kpe, w_in,
           lb_logits, hgrn_norm_g, q_norm_g, w_qb, kv_norm_g, w_kvb, mla_norm_g, w_o,
           ln1_g, ln1_b, ln2_g, ln2_b, w_router, router_bias, w_gate, w_up, w_down,
           w_ple, w_ple_gate):
    depth = w_in.shape[0]
    alpha = (2 * depth) ** 0.25
    bp, tp, _ = x_prompt.shape
    bs, ts, _ = x_sample.shape
    past = cache_ckv.shape[2]

    sm = jax.nn.softmax(lb_logits.astype(F32), axis=0)
    lb_all = jnp.maximum(jnp.cumsum(sm, axis=0) - sm[0:1], 0.0)
    lbp_all = jnp.stack([jnp.log(lb_all), jnp.log1p(-lb_all), 1.0 - lb_all], axis=1)

    cs_p = jnp.asarray(_rope_table(np.arange(tp)))
    tm_s = _tiles(bs * ts, ts, False)["tm"]
    cs_s = jnp.asarray(np.tile(_rope_table(past + np.arange(ts)), (tm_s // ts, 1)))
    wr = jnp.pad(w_router, ((0, 0), (0, LANES - N_EXPERTS))).astype(BF16)
    rb = jnp.pad(router_bias.astype(F32), (0, LANES - N_EXPERTS)).reshape(1, LANES)
    s0_p = jnp.zeros((1, bp, HA_HEADS, HA_DK, HA_DV), F32)
    cache_kpe_t = jnp.swapaxes(cache_kpe, 2, 3)

    yp, ys = x_prompt, x_sample
    res = [[] for _ in range(4)]
    cp = cs_ = None
    for l in range(depth):
        prm = _prep_layer(l, w_in, w_qb, w_kvb, w_o, w_gate, w_up, w_down, w_ple, w_ple_gate)
        row = lambda a: a[l].reshape(1, -1).astype(F32)
        small = dict(qg=row(q_norm_g), kvg=row(kv_norm_g), hg=row(hgrn_norm_g), mg=row(mla_norm_g),
                     g1=row(ln1_g), b1=row(ln1_b), g2=row(ln2_g), b2=row(ln2_b), wr=wr, rb=rb)
        yp, sp, cp, kp = _layer(yp, p_prompt, l, cs_p, s0_p, None, cp, lbp_all[l], prm, small,
                                alpha=alpha, prompt=True)
        ys, ss, cs_, ks = _layer(ys, p_sample, l, cs_s, state_hgrn, (cache_ckv, cache_kpe_t), cs_,
                                 lbp_all[l], prm, small, alpha=alpha, prompt=False)
        for lst, a in zip(res, (sp, kp, ss, ks)):
            lst.append(a)
    sp, kp, ss, ks = (jnp.stack(a) for a in res)
    return (yp, ys, sp, cp.reshape(depth, bp, tp, KV_LORA), kp, ss,
            cs_.reshape(depth, bs, ts, KV_LORA), ks)
```
